```python
import jax
import jax.numpy as jnp
from jax import lax
import numpy as np

D_MODEL = 1024
BATCH = 8
SEQ = 4096
DEPTH = 2

GRID_W = 64
CTX_LEN = 256

F32 = jnp.float32

SG_WIDTH = D_MODEL // 4
SG_GROUPS = 4
SG_GDIM = SG_WIDTH // SG_GROUPS
SG_CHUNK = 128
MLA_HEADS = 8
MLA_NOPE = 64
MLA_ROPE = 32
MLA_VDIM = 64
MLA_WIDTH = MLA_HEADS * MLA_VDIM
Q_LORA = 384
KV_LORA = 256
ROPE_AXIS = MLA_ROPE // 2
ROPE_BASE = 10000.0
ATTN_BLOCK = 128
MLA_SCALE = (MLA_NOPE + MLA_ROPE) ** -0.5
ML_HEADS = 4
ML_DIM = 64
ML_WIDTH = ML_HEADS * ML_DIM
ML_CHUNK = 64
ML_CONV = 5
N_SG = 2 * SG_WIDTH
N_MLA = Q_LORA + KV_LORA + MLA_ROPE
N_ML = 4 * ML_WIDTH + 4 * ML_HEADS
N_IN = N_SG + N_MLA + N_ML
MIX_WIDTH = SG_WIDTH + MLA_WIDTH + ML_WIDTH
N_EXPERTS = 16
EXPERT_FF = 1024
EC_CAPACITY = 2
ALPHA = (2 * DEPTH) ** 0.25
BETA = (8 * DEPTH) ** -0.25
LN_EPS = 1e-6

kernel_name = 'hybrid_diffusion_ec_moe_trunk'


def plain_norm(x):
    x32 = x.astype(F32)
    mu = jnp.mean(x32, -1, keepdims=True)
    var = jnp.mean(jnp.square(x32 - mu), -1, keepdims=True)
    return ((x32 - mu) * lax.rsqrt(var + LN_EPS)).astype(x.dtype)


def layer_norm(x, g, b):
    return plain_norm(x) * g + b


def rms_norm(x, g):
    x32 = x.astype(F32)
    y = x32 * lax.rsqrt(jnp.mean(jnp.square(x32), -1, keepdims=True) + LN_EPS)
    return y.astype(x.dtype) * g


def modulate(x, shift, scale):
    return plain_norm(x) * (1 + scale) + shift


def axial_rope_tables(rows):
    row = jnp.repeat(jnp.arange(rows, dtype=F32), GRID_W)
    col = jnp.tile(jnp.arange(GRID_W, dtype=F32), rows)
    inv = ROPE_BASE ** (-jnp.arange(ROPE_AXIS // 2, dtype=F32) * 2.0 / ROPE_AXIS)
    ang = jnp.concatenate([row[:, None] * inv, col[:, None] * inv], axis=-1)
    return jnp.cos(ang), jnp.sin(ang)


def apply_axial_rope(x, cos, sin):
    half = ROPE_AXIS // 2
    parts = []
    for a in range(2):
        z = x[..., a * ROPE_AXIS:(a + 1) * ROPE_AXIS]
        cs = cos[:, a * half:(a + 1) * half].astype(x.dtype)
        sn = sin[:, a * half:(a + 1) * half].astype(x.dtype)
        z1, z2 = z[..., :half], z[..., half:]
        parts += [z1 * cs - z2 * sn, z2 * cs + z1 * sn]
    return jnp.concatenate(parts, axis=-1)


def chunk_spatial_gating(z, ln_g, ln_b, w_s, b_s):
    B, T, _ = z.shape
    z = jax.nn.gelu(z)
    u, v = z[..., :SG_WIDTH], z[..., SG_WIDTH:]
    v = layer_norm(v, ln_g, ln_b).reshape(B, T // SG_CHUNK, SG_CHUNK, SG_GROUPS, SG_GDIM)
    mixed = jnp.einsum('gts,bnsgc->bntgc', w_s, v) + b_s.T[:, :, None]
    return u * mixed.reshape(B, T, SG_WIDTH)


def mla_q(z, q_norm_g, w_uq, rope):
    B, T, _ = z.shape
    c_q = rms_norm(z[..., :Q_LORA], q_norm_g)
    q = (c_q @ w_uq).reshape(B, T, MLA_HEADS, MLA_NOPE + MLA_ROPE).transpose(0, 2, 1, 3)
    q_nope, q_rope = q[..., :MLA_NOPE], q[..., MLA_NOPE:]
    if rope is not None:
        q_rope = apply_axial_rope(q_rope, rope[0], rope[1])
    return q_nope, q_rope


def mla_kv(z, kv_norm_g, w_ukv, rope):
    B, T, _ = z.shape
    c_kv = rms_norm(z[..., Q_LORA:Q_LORA + KV_LORA], kv_norm_g)
    k_rope = z[..., Q_LORA + KV_LORA:]
    kv = (c_kv @ w_ukv).reshape(B, T, MLA_HEADS, MLA_NOPE + MLA_VDIM).transpose(0, 2, 1, 3)
    k_nope, v = kv[..., :MLA_NOPE], kv[..., MLA_NOPE:]
    if rope is not None:
        k_rope = apply_axial_rope(k_rope, rope[0], rope[1])
    return k_nope, k_rope, v


def attend(q_nope, q_rope, k_nope, k_rope, v):
    s = jnp.einsum('bhqd,bhkd->bhqk', q_nope, k_nope) + jnp.einsum('bhqd,bkd->bhqk', q_rope, k_rope)
    p = jax.nn.softmax(s.astype(F32) * MLA_SCALE, axis=-1).astype(v.dtype)
    return jnp.einsum('bhqk,bhkd->bhqd', p, v)


def blocked_attention(q_nope, q_rope, k_nope, k_rope, v):
    B, H, T, _ = q_nope.shape
    nb = T // ATTN_BLOCK
    qn = q_nope.reshape(B, H, nb, ATTN_BLOCK, MLA_NOPE).transpose(2, 0, 1, 3, 4)
    qr = q_rope.reshape(B, H, nb, ATTN_BLOCK, MLA_ROPE).transpose(2, 0, 1, 3, 4)
    out = lax.map(lambda qs: attend(qs[0], qs[1], k_nope, k_rope, v), (qn, qr))
    return out.transpose(1, 2, 0, 3, 4).reshape(B, H, T, MLA_VDIM)


def heads_to_width(h):
    B, H, T, d = h.shape
    return h.transpose(0, 2, 1, 3).reshape(B, T, H * d)


def depthwise_conv(x, w, b):
    pad = (ML_CONV - 1) // 2
    y = lax.conv_general_dilated(x, w[:, None, :], window_strides=(1,), padding=[(pad, pad)],
                                 dimension_numbers=('NWC', 'WIO', 'NWC'),
                                 feature_group_count=x.shape[-1])
    return y + b


def mlstm_inputs(z, conv_w, conv_b, f_bias):
    B, T, _ = z.shape
    W = ML_WIDTH
    qk = jax.nn.silu(depthwise_conv(z[..., :2 * W], conv_w, conv_b))
    heads = lambda a: a.reshape(B, T, ML_HEADS, ML_DIM).transpose(0, 2, 1, 3).astype(F32)
    q = heads(qk[..., :W])
    k = heads(qk[..., W:]) * (ML_DIM ** -0.5)
    v = heads(z[..., 2 * W:3 * W])
    o = z[..., 3 * W:4 * W]
    gates = z[..., 4 * W:].astype(F32).reshape(B, T, 2, 2, ML_HEADS)
    log_i = gates[:, :, :, 0].transpose(2, 0, 3, 1)
    log_f = jax.nn.log_sigmoid(gates[:, :, :, 1] + f_bias.astype(F32)).transpose(2, 0, 3, 1)
    return q, k, v, o, log_i, log_f


def mlstm_chunkwise(q, k, v, log_i, log_f, state, with_output):
    B, H, T, d = k.shape
    L = ML_CHUNK
    nc = T // L
    k = k.reshape(B, H, nc, L, d)
    v = v.reshape(B, H, nc, L, d)
    li = log_i.reshape(B, H, nc, L)
    b = jnp.cumsum(log_f.reshape(B, H, nc, L), axis=-1)
    b_end = b[..., -1]
    g = b_end[..., None] - b + li
    m_loc = jnp.max(g, axis=-1)
    w = jnp.exp(g - m_loc[..., None])
    C_loc = jnp.einsum('bhnl,bhnli,bhnlj->bhnij', w, v, k)
    n_loc = jnp.einsum('bhnl,bhnlj->bhnj', w, k)

    def step(carry, inp):
        C, n, m = carry
        be, ml, Cl, nl = inp
        m_new = jnp.maximum(be + m, ml)
        a = jnp.exp(be + m - m_new)
        s = jnp.exp(ml - m_new)
        new = (a[..., None, None] * C + s[..., None, None] * Cl, a[..., None] * n + s[..., None] * nl, m_new)
        return new, (C, n, m)

    to_front = lambda a: jnp.moveaxis(a, 2, 0)
    final, starts = lax.scan(step, state, (to_front(b_end), to_front(m_loc), to_front(C_loc), to_front(n_loc)))
    if not with_output:
        return None, final
    C0, n0, m0 = [jnp.moveaxis(a, 0, 2) for a in starts]
    q = q.reshape(B, H, nc, L, d)
    lower = jnp.tril(jnp.ones((L, L), dtype=bool))
    dmat = jnp.where(lower, b[..., :, None] - b[..., None, :] + li[..., None, :], -jnp.inf)
    a_t = b + m0[..., None]
    m_t = jnp.maximum(a_t, jnp.max(dmat, axis=-1))
    p = jnp.exp(dmat - m_t[..., None]) * jnp.einsum('bhntd,bhnsd->bhnts', q, k)
    inter = jnp.exp(a_t - m_t)
    num = inter[..., None] * jnp.einsum('bhnij,bhntj->bhnti', C0, q) + jnp.einsum('bhnts,bhnsd->bhntd', p, v)
    den = inter * jnp.einsum('bhnj,bhntj->bhnt', n0, q) + jnp.sum(p, axis=-1)
    h = num / jnp.maximum(jnp.abs(den), jnp.exp(-m_t))[..., None]
    return h.reshape(B, H, T, d), final


def mlstm_out(h, o, norm_g):
    B, H, T, d = h.shape
    h = h.transpose(0, 2, 1, 3)
    mu = jnp.mean(h, -1, keepdims=True)
    var = jnp.mean(jnp.square(h - mu), -1, keepdims=True)
    hn = ((h - mu) * lax.rsqrt(var + LN_EPS)).reshape(B, T, H * d).astype(o.dtype) * norm_g
    return jax.nn.sigmoid(o) * hn


def _rev(a, flip):
    return jnp.flip(a, axis=2) if flip else a


def mlstm_bidirectional(zl, zc, conv_w, conv_b, f_bias, norm_g, ctx_output):
    ql, kl, vl, ol, lil, lfl = mlstm_inputs(zl, conv_w, conv_b, f_bias)
    qc, kc, vc, oc, lic, lfc = mlstm_inputs(zc, conv_w, conv_b, f_bias)
    B = kl.shape[0]
    zero = (jnp.zeros((B, ML_HEADS, ML_DIM, ML_DIM), F32), jnp.zeros((B, ML_HEADS, ML_DIM), F32),
            jnp.zeros((B, ML_HEADS), F32))
    hl_dirs, hc_dirs = [], []
    for d in range(2):
        fl = d == 1
        hc_d, st = mlstm_chunkwise(_rev(qc, fl), _rev(kc, fl), _rev(vc, fl), _rev(lic[d], fl),
                                   _rev(lfc[d], fl), zero, ctx_output)
        hl_d, _ = mlstm_chunkwise(_rev(ql, fl), _rev(kl, fl), _rev(vl, fl), _rev(lil[d], fl),
                                  _rev(lfl[d], fl), st, True)
        hl_dirs.append(_rev(hl_d, fl))
        if ctx_output:
            hc_dirs.append(_rev(hc_d, fl))
    yl = mlstm_out(hl_dirs[0] + hl_dirs[1], ol, norm_g)
    yc = mlstm_out(hc_dirs[0] + hc_dirs[1], oc, norm_g) if ctx_output else None
    return yl, yc


def expert_choice_moe(h, w_router, w_gate, w_up, w_down):
    B, T, D = h.shape
    cap = EC_CAPACITY * T // N_EXPERTS
    aff = jax.nn.softmax((h @ w_router).astype(F32), axis=-1)
    gate, idx = lax.top_k(aff.transpose(0, 2, 1), cap)
    xe = jax.vmap(lambda hb, ib: hb[ib])(h, idx)
    hid = jax.nn.silu(jnp.einsum('becd,edf->becf', xe, w_gate)) * jnp.einsum('becd,edf->becf', xe, w_up)
    ye = jnp.einsum('becf,efd->becd', hid, w_down) * gate[..., None].astype(h.dtype)
    flat = (idx + jnp.arange(B, dtype=idx.dtype)[:, None, None] * T).reshape(-1)
    out = jnp.zeros((B * T, D), h.dtype).at[flat].add(ye.reshape(-1, D))
    return out.reshape(B, T, D)


def token_mixers(zl, zc, p, rope, ctx_output):
    al, bl, ml = zl[..., :N_SG], zl[..., N_SG:N_SG + N_MLA], zl[..., N_SG + N_MLA:]
    ac, bc, mc = zc[..., :N_SG], zc[..., N_SG:N_SG + N_MLA], zc[..., N_SG + N_MLA:]
    ya_l = chunk_spatial_gating(al, p['sg_ln_g'], p['sg_ln_b'], p['sg_w'], p['sg_b'])
    kn_l, kr_l, v_l = mla_kv(bl, p['kv_norm_g'], p['w_ukv'], rope)
    kn_c, kr_c, v_c = mla_kv(bc, p['kv_norm_g'], p['w_ukv'], None)
    qn_l, qr_l = mla_q(bl, p['q_norm_g'], p['w_uq'], rope)
    yb_l = heads_to_width(blocked_attention(qn_l, qr_l, jnp.concatenate([kn_l, kn_c], axis=2),
                                            jnp.concatenate([kr_l, kr_c], axis=1),
                                            jnp.concatenate([v_l, v_c], axis=2)))
    yc_l, yc_c = mlstm_bidirectional(ml, mc, p['ml_conv_w'], p['ml_conv_b'], p['ml_f_bias'],
                                     p['ml_norm_g'], ctx_output)
    y_l = jnp.concatenate([ya_l, yb_l, yc_l], axis=-1) @ p['w_out']
    if not ctx_output:
        return y_l, None
    ya_c = chunk_spatial_gating(ac, p['sg_ln_g'], p['sg_ln_b'], p['sg_w'], p['sg_b'])
    qn_c, qr_c = mla_q(bc, p['q_norm_g'], p['w_uq'], None)
    yb_c = heads_to_width(attend(qn_c, qr_c, kn_c, kr_c, v_c))
    y_c = jnp.concatenate([ya_c, yb_c, yc_c], axis=-1) @ p['w_out']
    return y_l, y_c


def trunk_layer(xl, xc, c, c_ctx, rope, p, update_ctx):
    ada_l = jax.nn.silu(c) @ p['w_ada'] + p['b_ada']
    ada_c = jax.nn.silu(c_ctx) @ p['w_ada'] + p['b_ada']
    sh1_l, sc1_l, g1_l, sh2_l, sc2_l, g2_l = [a[:, None, :] for a in jnp.split(ada_l, 6, axis=-1)]
    sh1_c, sc1_c, g1_c, sh2_c, sc2_c, g2_c = jnp.split(ada_c, 6, axis=-1)
    zl = modulate(xl, sh1_l, sc1_l) @ p['w_in'] + p['b_in']
    zc = modulate(xc, sh1_c, sc1_c) @ p['w_in'] + p['b_in']
    y_l, y_c = token_mixers(zl, zc, p, rope, update_ctx)
    xl = layer_norm(ALPHA * xl + g1_l * y_l, p['ln1_g'], p['ln1_b'])
    f_l = expert_choice_moe(modulate(xl, sh2_l, sc2_l), p['w_router'], p['w_gate'], p['w_up'], p['w_down'])
    xl = layer_norm(ALPHA * xl + g2_l * f_l, p['ln2_g'], p['ln2_b'])
    if update_ctx:
        xc = layer_norm(ALPHA * xc + g1_c * y_c, p['ln1_g'], p['ln1_b'])
        f_c = expert_choice_moe(modulate(xc, sh2_c, sc2_c), p['w_router'], p['w_gate'], p['w_up'], p['w_down'])
        xc = layer_norm(ALPHA * xc + g2_c * f_c, p['ln2_g'], p['ln2_b'])
    return xl, xc


def setup_inputs(seed: int = 0) -> dict:
    key = jax.random.key(seed)
    ks = iter(jax.random.split(key, 40))
    nrm = lambda shape, s: jax.random.normal(next(ks), shape, F32) * s
    L, D = DEPTH, D_MODEL
    return {
        'x': nrm((BATCH, SEQ, D), 1.0),
        'c': nrm((BATCH, D), 1.0),
        'ctx': nrm((BATCH, CTX_LEN, D), 1.0),
        'c_ctx': nrm((D,), 1.0),
        'w_ada': nrm((L, D, 6 * D), 0.5 * D ** -0.5),
        'b_ada': nrm((L, 6 * D), 0.02),
        'w_in': nrm((L, D, N_IN), D ** -0.5),
        'b_in': nrm((L, N_IN), 0.02),
        'sg_ln_g': 1.0 + nrm((L, SG_WIDTH), 0.02),
        'sg_ln_b': nrm((L, SG_WIDTH), 0.02),
        'sg_w': nrm((L, SG_GROUPS, SG_CHUNK, SG_CHUNK), SG_CHUNK ** -0.5),
        'sg_b': 1.0 + nrm((L, SG_GROUPS, SG_CHUNK), 0.02),
        'q_norm_g': 1.0 + nrm((L, Q_LORA), 0.02),
        'kv_norm_g': 1.0 + nrm((L, KV_LORA), 0.02),
        'w_uq': nrm((L, Q_LORA, MLA_HEADS * (MLA_NOPE + MLA_ROPE)), Q_LORA ** -0.5),
        'w_ukv': nrm((L, KV_LORA, MLA_HEADS * (MLA_NOPE + MLA_VDIM)), KV_LORA ** -0.5),
        'ml_conv_w': nrm((L, ML_CONV, 2 * ML_WIDTH), ML_CONV ** -0.5),
        'ml_conv_b': nrm((L, 2 * ML_WIDTH), 0.02),
        'ml_f_bias': jax.random.uniform(next(ks), (L, 2, ML_HEADS), F32, 3.0, 6.0),
        'ml_norm_g': 1.0 + nrm((L, ML_WIDTH), 0.02),
        'w_out': nrm((L, MIX_WIDTH, D), BETA * MIX_WIDTH ** -0.5),
        'ln1_g': 1.0 + nrm((L, D), 0.02),
        'ln1_b': nrm((L, D), 0.02),
        'w_router': nrm((L, D, N_EXPERTS), D ** -0.5),
        'w_gate': nrm((L, N_EXPERTS, D, EXPERT_FF), D ** -0.5),
        'w_up': nrm((L, N_EXPERTS, D, EXPERT_FF), D ** -0.5),
        'w_down': nrm((L, N_EXPERTS, EXPERT_FF, D), BETA * EXPERT_FF ** -0.5),
        'ln2_g': 1.0 + nrm((L, D), 0.02),
        'ln2_b': nrm((L, D), 0.02),
    }


def reference(x, c, ctx, c_ctx, w_ada, b_ada, w_in, b_in, sg_ln_g, sg_ln_b, sg_w, sg_b,
              q_norm_g, kv_norm_g, w_uq, w_ukv, ml_conv_w, ml_conv_b, ml_f_bias, ml_norm_g,
              w_out, ln1_g, ln1_b, w_router, w_gate, w_up, w_down, ln2_g, ln2_b):
    rows = x.shape[1] // GRID_W
    rope = axial_rope_tables(rows)
    xl, xc = x, ctx
    for l in range(DEPTH):
        p = {
            'w_ada': w_ada[l], 'b_ada': b_ada[l], 'w_in': w_in[l], 'b_in': b_in[l],
            'sg_ln_g': sg_ln_g[l], 'sg_ln_b': sg_ln_b[l], 'sg_w': sg_w[l], 'sg_b': sg_b[l],
            'q_norm_g': q_norm_g[l], 'kv_norm_g': kv_norm_g[l], 'w_uq': w_uq[l], 'w_ukv': w_ukv[l],
            'ml_conv_w': ml_conv_w[l], 'ml_conv_b': ml_conv_b[l], 'ml_f_bias': ml_f_bias[l],
            'ml_norm_g': ml_norm_g[l], 'w_out': w_out[l], 'ln1_g': ln1_g[l], 'ln1_b': ln1_b[l],
            'w_router': w_router[l], 'w_gate': w_gate[l], 'w_up': w_up[l], 'w_down': w_down[l],
            'ln2_g': ln2_g[l], 'ln2_b': ln2_b[l],
        }
        xl, xc = trunk_layer(xl, xc, c, c_ctx, rope, p, l < DEPTH - 1)
    return xl
```

```python
import functools

import jax
import jax.numpy as jnp
from jax import lax
from jax.experimental import pallas as pl
from jax.experimental.pallas import tpu as pltpu

F32 = jnp.float32
BF16 = jnp.bfloat16
HIGHEST = lax.Precision.HIGHEST

D_MODEL = 1024
DEPTH = 2
GRID_W = 64
SG_WIDTH = 256
SG_GROUPS = 4
SG_GDIM = 64
SG_CHUNK = 128
MLA_HEADS = 8
MLA_NOPE = 64
MLA_ROPE = 32
MLA_VDIM = 64
Q_LORA = 384
KV_LORA = 256
ROPE_AXIS = 16
ROPE_BASE = 10000.0
MLA_SCALE = (MLA_NOPE + MLA_ROPE) ** -0.5
ML_HEADS = 4
ML_DIM = 64
ML_WIDTH = 256
ML_CONV = 5
N_EXPERTS = 16
EXPERT_FF = 1024
EC_CAPACITY = 2
ALPHA = (2 * DEPTH) ** 0.25
LN_EPS = 1e-6

LANE = 128
HEAD_PAD = 128
ML_CHUNK = 256
OFF_SG, OFF_CQ, OFF_CKV, OFF_KR, OFF_KRR, OFF_ML, OFF_G, N_CAT = 0, 512, 896, 1152, 1280, 1408, 2432, 2560
VMEM_LIMIT = 56 * 1024 * 1024


def _cparams(sem):
    return pltpu.CompilerParams(dimension_semantics=sem, vmem_limit_bytes=VMEM_LIMIT)


def _nt(a, b):
    return lax.dot_general(a, b, (((1,), (1,)), ((), ())), preferred_element_type=F32)


def _tn(a, b):
    return lax.dot_general(a, b, (((0,), (0,)), ((), ())), preferred_element_type=F32)


def _dot(a, b, precision=None):
    return jnp.dot(a, b, preferred_element_type=F32, precision=precision)


def _plain_norm(x):
    mu = jnp.mean(x, axis=-1, keepdims=True)
    xc = x - mu
    var = jnp.mean(xc * xc, axis=-1, keepdims=True)
    return xc * lax.rsqrt(var + LN_EPS)


def _rms(x):
    return x * lax.rsqrt(jnp.mean(x * x, axis=-1, keepdims=True) + LN_EPS)


def _silu(x):
    return x * (1.0 / (1.0 + jnp.exp(-x)))


def _gelu_tanh(x):
    return 0.5 * x * (1.0 + jnp.tanh(0.7978845608028654 * (x + 0.044715 * (x * x * x))))


def _log_sigmoid(x):
    return jnp.minimum(x, 0.0) - jnp.log(1.0 + jnp.exp(-jnp.abs(x)))


def _ada_kernel(c_ref, w_ref, b_ref, o_ref):
    o_ref[0] = _dot(_silu(c_ref[...]), w_ref[0], precision=HIGHEST) + b_ref[0]


def _ada(cc, w_ada, b_ada):
    L, D, N = w_ada.shape
    tn = 1536
    return pl.pallas_call(
        _ada_kernel,
        grid=(L, N // tn),
        in_specs=[pl.BlockSpec((16, D), lambda l, j: (0, 0)),
                  pl.BlockSpec((1, D, tn), lambda l, j: (l, 0, j)),
                  pl.BlockSpec((1, 1, tn), lambda l, j: (l, 0, j))],
        out_specs=pl.BlockSpec((1, 16, tn), lambda l, j: (l, 0, j)),
        out_shape=jax.ShapeDtypeStruct((L, 16, N), F32),
        compiler_params=_cparams(("parallel", "parallel")),
        name="ada",
    )(cc, w_ada, b_ada.reshape(L, 1, N))


def _inproj_kernel(x_ref, sh_ref, sc_ref, cos_ref, sin_ref, wcat_ref, bcat_ref, qg_ref, kvg_ref, wq_ref, wkv_ref,
                   sglg_ref, sglb_ref, sgw_ref, sgb_ref,
                   ya_ref, q_ref, k_ref, v_ref, zml_ref, g_ref):
    tm = x_ref.shape[1]
    h = _plain_norm(x_ref[0]) * (1.0 + sc_ref[0]) + sh_ref[0]
    z = _dot(h.astype(BF16), wcat_ref[...]) + bcat_ref[...]
    zs = _gelu_tanh(z[:, OFF_SG:OFF_SG + 2 * SG_WIDTH])
    u = zs[:, :SG_WIDTH]
    vn = _plain_norm(zs[:, SG_WIDTH:]) * sglg_ref[...] + sglb_ref[...]
    group = jnp.right_shift(lax.broadcasted_iota(jnp.int32, (SG_CHUNK, SG_WIDTH), 1), SG_GDIM.bit_length() - 1)
    for ch in range(tm // SG_CHUNK):
        rows = slice(ch * SG_CHUNK, (ch + 1) * SG_CHUNK)
        vc = vn[rows].astype(BF16)
        mixed = sgb_ref[...]
        for g in range(SG_GROUPS):
            mixed = mixed + jnp.where(group == g, _dot(sgw_ref[g], vc), 0.0)
        ya_ref[0, rows, :] = (u[rows] * mixed).astype(BF16)
    cosT = cos_ref[...]
    sinT = sin_ref[...]
    cqn = (_rms(z[:, OFF_CQ:OFF_CQ + Q_LORA]) * qg_ref[...]).astype(BF16)
    qq = _dot(cqn, wq_ref[...])
    ckvn = (_rms(z[:, OFF_CKV:OFF_CKV + KV_LORA]) * kvg_ref[...]).astype(BF16)
    kv = _dot(ckvn, wkv_ref[...])
    kr = z[:, OFF_KR:OFF_KR + HEAD_PAD] * cosT + z[:, OFF_KRR:OFF_KRR + HEAD_PAD] * sinT
    one_lane = jnp.where(lax.broadcasted_iota(jnp.int32, (1, HEAD_PAD), 1) == MLA_VDIM, 1.0, 0.0)
    hw = MLA_HEADS * HEAD_PAD
    for hd in range(MLA_HEADS):
        cols = slice(hd * HEAD_PAD, (hd + 1) * HEAD_PAD)
        rcols = slice(hw + hd * HEAD_PAD, hw + (hd + 1) * HEAD_PAD)
        q_ref[0, hd] = ((qq[:, cols] * cosT + qq[:, rcols] * sinT) * MLA_SCALE).astype(BF16)
        k_ref[0, hd] = (kv[:, cols] + kr).astype(BF16)
        v_ref[0, hd] = (kv[:, rcols] + one_lane).astype(BF16)
    zml_ref[0] = z[:, OFF_ML:OFF_ML + 4 * ML_WIDTH]
    g_ref[0] = z[:, OFF_G:OFF_G + LANE]


def _inproj(x, shift, scale, cosT, sinT, p, tm):
    B, T, D = x.shape
    const = lambda shape: pl.BlockSpec(shape, lambda b, i: (0,) * len(shape))
    heads = (B, MLA_HEADS, T, HEAD_PAD)
    return pl.pallas_call(
        _inproj_kernel,
        grid=(B, T // tm),
        in_specs=[pl.BlockSpec((1, tm, D), lambda b, i: (b, i, 0)),
                  pl.BlockSpec((1, 1, D), lambda b, i: (b, 0, 0)),
                  pl.BlockSpec((1, 1, D), lambda b, i: (b, 0, 0)),
                  pl.BlockSpec((tm, HEAD_PAD), lambda b, i: (i, 0)),
                  pl.BlockSpec((tm, HEAD_PAD), lambda b, i: (i, 0)),
                  const((D, N_CAT)), const((1, N_CAT)), const((1, Q_LORA)), const((1, KV_LORA)),
                  const((Q_LORA, 2 * MLA_HEADS * HEAD_PAD)), const((KV_LORA, 2 * MLA_HEADS * HEAD_PAD)),
                  const((1, SG_WIDTH)), const((1, SG_WIDTH)), const((SG_GROUPS, SG_CHUNK, SG_CHUNK)),
                  const((SG_CHUNK, SG_WIDTH))],
        out_specs=[pl.BlockSpec((1, tm, SG_WIDTH), lambda b, i: (b, i, 0)),
                   pl.BlockSpec((1, MLA_HEADS, tm, HEAD_PAD), lambda b, i: (b, 0, i, 0)),
                   pl.BlockSpec((1, MLA_HEADS, tm, HEAD_PAD), lambda b, i: (b, 0, i, 0)),
                   pl.BlockSpec((1, MLA_HEADS, tm, HEAD_PAD), lambda b, i: (b, 0, i, 0)),
                   pl.BlockSpec((1, tm, 4 * ML_WIDTH), lambda b, i: (b, i, 0)),
                   pl.BlockSpec((1, tm, LANE), lambda b, i: (b, i, 0))],
        out_shape=[jax.ShapeDtypeStruct((B, T, SG_WIDTH), BF16),
                   jax.ShapeDtypeStruct(heads, BF16), jax.ShapeDtypeStruct(heads, BF16),
                   jax.ShapeDtypeStruct(heads, BF16),
                   jax.ShapeDtypeStruct((B, T, 4 * ML_WIDTH), F32),
                   jax.ShapeDtypeStruct((B, T, LANE), F32)],
        compiler_params=_cparams(("parallel", "parallel")),
        name="inproj",
    )(x, shift, scale, cosT, sinT, p["wcat"], p["bcat"], p["q_norm_g"], p["kv_norm_g"], p["wq"], p["wkv"],
      p["sg_ln_g"], p["sg_ln_b"], p["sg_w"], p["sg_bias"])


def _attn_kernel(*refs, n_src, tk):
    q_ref = refs[0]
    kv_refs = refs[1:1 + 2 * n_src]
    o_ref = refs[1 + 2 * n_src]
    tq = q_ref.shape[2]
    outs = []
    for hh in range(2):
        q = q_ref[0, hh]
        m = jnp.full((tq, 1), -jnp.inf, F32)
        acc = jnp.zeros((tq, HEAD_PAD), F32)
        for s in range(n_src):
            k_ref, v_ref = kv_refs[2 * s], kv_refs[2 * s + 1]
            ck = min(tk, k_ref.shape[2])

            def body(c, carry, k_ref=k_ref, v_ref=v_ref, ck=ck):
                m, acc = carry
                rows = pl.ds(pl.multiple_of(c * ck, ck), ck)
                sc = _nt(q, k_ref[0, hh, rows, :])
                m_new = jnp.maximum(m, jnp.max(sc, axis=1, keepdims=True))
                pr = jnp.exp(sc - m_new)
                acc = acc * jnp.exp(m - m_new) + _dot(pr.astype(BF16), v_ref[0, hh, rows, :])
                return m_new, acc

            m, acc = lax.fori_loop(0, k_ref.shape[2] // ck, body, (m, acc))
        outs.append(acc / acc[:, MLA_VDIM:MLA_VDIM + 1])
    lane = lax.broadcasted_iota(jnp.int32, (tq, HEAD_PAD), 1)
    o_ref[0] = jnp.where(lane < MLA_VDIM, outs[0], pltpu.roll(outs[1], MLA_VDIM, axis=1)).astype(BF16)


def _attention(q, kvs, tq, tk=512):
    B, H, T, _ = q.shape
    in_specs = [pl.BlockSpec((1, 2, tq, HEAD_PAD), lambda b, hp, i: (b, hp, i, 0))]
    args = [q]
    for k, v in kvs:
        K = k.shape[2]
        in_specs += [pl.BlockSpec((1, 2, K, HEAD_PAD), lambda b, hp, i: (b, hp, 0, 0))] * 2
        args += [k, v]
    return pl.pallas_call(
        functools.partial(_attn_kernel, n_src=len(kvs), tk=tk),
        grid=(B, H // 2, T // tq),
        in_specs=in_specs,
        out_specs=pl.BlockSpec((1, tq, HEAD_PAD), lambda b, hp, i: (b, i, hp)),
        out_shape=jax.ShapeDtypeStruct((B, T, H * MLA_VDIM), BF16),
        compiler_params=_cparams(("parallel", "parallel", "parallel")),
        name="attention",
    )(*args)


def _mlstm_kernel(zp_ref, zc_ref, zn_ref, v_ref, g_ref, cw_ref, cb_ref, fb_ref, ci_ref, mi_ref,
                  h_ref, cs_ref, ms_ref, *, nc):
    L = zc_ref.shape[1]
    d = pl.program_id(1)
    j = pl.program_id(2)
    c = j + d * (nc - 1 - 2 * j)

    @pl.when(j == 0)
    def _():
        cs_ref[0, 0] = ci_ref[0, 0]
        ms_ref[0, 0] = mi_ref[0, 0]

    prev = jnp.where(c > 0, zp_ref[0], 0.0)
    nxt = jnp.where(c < nc - 1, zn_ref[0], 0.0)
    big = jnp.concatenate([prev, zc_ref[0], nxt], axis=0)
    n_big = L + 16
    pad = (ML_CONV - 1) // 2
    conv = cb_ref[...]
    for t in range(ML_CONV):
        s = t - pad
        shifted = big if s == 0 else pltpu.roll(big, (-s) % n_big, axis=0)
        conv = conv + shifted[8:8 + L] * cw_ref[t:t + 1, :]
    qk = _silu(conv)
    qv = qk[:, :ML_WIDTH]
    kvv = qk[:, ML_WIDTH:] * (ML_DIM ** -0.5)
    vv = v_ref[0]

    gl = g_ref[0]
    gl = jnp.where(d == 0, gl, pltpu.roll(gl, LANE - ML_HEADS, axis=1))
    li = gl
    fb = jnp.where(d == 0, fb_ref[0:1, :], fb_ref[1:2, :])
    lf = _log_sigmoid(pltpu.roll(gl, LANE - 2 * ML_HEADS, axis=1) + fb)
    row = lax.broadcasted_iota(jnp.int32, (L, L), 0)
    col = lax.broadcasted_iota(jnp.int32, (L, L), 1)
    causal = (row - col) * (1 - 2 * d) >= 0
    bc = _dot(jnp.where(causal, 1.0, 0.0), lf, precision=HIGHEST)
    b_end = jnp.where(d == 0, bc[L - 1:L, :], bc[0:1, :])
    gw = b_end - bc + li
    m_loc = jnp.max(gw, axis=0, keepdims=True)
    wt = jnp.exp(gw - m_loc)
    rt = (li - bc).T
    m_prev = ms_ref[0, 0]
    m_next = jnp.maximum(b_end + m_prev, m_loc)
    a_sc = jnp.exp(b_end + m_prev - m_next)
    s_sc = jnp.exp(m_loc - m_next)

    lane = lax.broadcasted_iota(jnp.int32, (L, HEAD_PAD), 1)
    lane_r = lax.broadcasted_iota(jnp.int32, (1, HEAD_PAD), 1)
    for blk in range(ML_HEADS // 2):
        cols = slice(blk * HEAD_PAD, (blk + 1) * HEAD_PAD)
        q_blk, k_blk, v_blk = qv[:, cols], kvv[:, cols], vv[:, cols]
        hh = []
        for half in range(2):
            hd = 2 * blk + half
            in_half = (lane >= ML_DIM) if half else (lane < ML_DIM)
            den_lane = 0 if half else ML_DIM
            qm = jnp.where(in_half, q_blk, 0.0).astype(BF16)
            km = jnp.where(in_half, k_blk, 0.0).astype(BF16)
            vaug = jnp.where(in_half, v_blk, jnp.where(lane == den_lane, 1.0, 0.0))
            b_col = bc[:, hd:hd + 1]
            dm = jnp.where(causal, b_col + rt[hd:hd + 1, :], -jnp.inf)
            a_col = b_col + m_prev[:, hd:hd + 1]
            m_t = jnp.maximum(a_col, jnp.max(dm, axis=1, keepdims=True))
            pm = jnp.exp(dm - m_t) * _nt(qm, km)
            inter = jnp.exp(a_col - m_t)
            c0 = cs_ref[0, 0, hd * HEAD_PAD:(hd + 1) * HEAD_PAD, :]
            nd = inter * _nt(qm, c0.astype(BF16)) + _dot(pm.astype(BF16), vaug.astype(BF16))
            den = nd[:, den_lane:den_lane + 1]
            hh.append(nd / jnp.maximum(jnp.abs(den), jnp.exp(-m_t)))
            wv = (wt[:, hd:hd + 1] * vaug).astype(BF16)
            c_loc = _tn(wv, km)
            cs_ref[0, 0, hd * HEAD_PAD:(hd + 1) * HEAD_PAD, :] = a_sc[:, hd:hd + 1] * c0 + s_sc[:, hd:hd + 1] * c_loc
        h_ref[0, 0, :, cols] = jnp.where(lane < ML_DIM, hh[0], hh[1])
    ms_ref[0, 0] = jnp.where(lane_r < ML_HEADS, m_next, m_prev)


def _mlstm(zml, gates, p, c_init, m_init):
    B, T, _ = zml.shape
    L = min(ML_CHUNK, T)
    nc = T // L
    hb = L // 8
    chunk = lambda b, d, j: j + d * (nc - 1 - 2 * j)
    const = lambda shape: pl.BlockSpec(shape, lambda b, d, j: (0,) * len(shape))
    state = pl.BlockSpec((1, 1, ML_HEADS * HEAD_PAD, HEAD_PAD), lambda b, d, j: (b, d, 0, 0))
    mstate = pl.BlockSpec((1, 1, 1, LANE), lambda b, d, j: (b, d, 0, 0))
    return pl.pallas_call(
        functools.partial(_mlstm_kernel, nc=nc),
        grid=(B, 2, nc),
        in_specs=[pl.BlockSpec((1, 8, 2 * ML_WIDTH), lambda b, d, j: (b, jnp.maximum(chunk(b, d, j) * hb - 1, 0), 0)),
                  pl.BlockSpec((1, L, 2 * ML_WIDTH), lambda b, d, j: (b, chunk(b, d, j), 0)),
                  pl.BlockSpec((1, 8, 2 * ML_WIDTH),
                               lambda b, d, j: (b, jnp.minimum((chunk(b, d, j) + 1) * hb, T // 8 - 1), 0)),
                  pl.BlockSpec((1, L, ML_WIDTH), lambda b, d, j: (b, chunk(b, d, j), 2)),
                  pl.BlockSpec((1, L, LANE), lambda b, d, j: (b, chunk(b, d, j), 0)),
                  const((8, 2 * ML_WIDTH)), const((1, 2 * ML_WIDTH)), const((2, LANE)),
                  state, mstate],
        out_specs=[pl.BlockSpec((1, 1, L, ML_WIDTH), lambda b, d, j: (d, b, chunk(b, d, j), 0)),
                   state, mstate],
        out_shape=[jax.ShapeDtypeStruct((2, B, T, ML_WIDTH), F32),
                   jax.ShapeDtypeStruct((B, 2, ML_HEADS * HEAD_PAD, HEAD_PAD), F32),
                   jax.ShapeDtypeStruct((B, 2, 1, LANE), F32)],
        compiler_params=_cparams(("parallel", "parallel", "arbitrary")),
        name="mlstm",
    )(zml, zml, zml, zml, gates, p["conv_w"], p["conv_b"], p["f_bias"], c_init, m_init)


def _outproj_kernel(x_ref, g1_ref, sh2_ref, sc2_ref, ya_ref, yb_ref, h0_ref, h1_ref, o_ref, mg_ref, avg_ref,
                    wo_ref, l1g_ref, l1b_ref, wr_ref, x1_ref, hm_ref, aff_ref):
    hs = h0_ref[0, 0] + h1_ref[0, 0]
    mu = _dot(hs, avg_ref[...], precision=HIGHEST)
    hc = hs - mu
    var = _dot(hc * hc, avg_ref[...], precision=HIGHEST)
    yc = (1.0 / (1.0 + jnp.exp(-o_ref[0]))) * (hc * lax.rsqrt(var + LN_EPS) * mg_ref[...])
    y = (_dot(ya_ref[0], wo_ref[0:SG_WIDTH, :])
         + _dot(yb_ref[0], wo_ref[SG_WIDTH:SG_WIDTH + MLA_HEADS * MLA_VDIM, :])
         + _dot(yc.astype(BF16), wo_ref[SG_WIDTH + MLA_HEADS * MLA_VDIM:, :]))
    x1 = _plain_norm(ALPHA * x_ref[0] + g1_ref[0] * y) * l1g_ref[...] + l1b_ref[...]
    x1_ref[0] = x1
    hm = _plain_norm(x1) * (1.0 + sc2_ref[0]) + sh2_ref[0]
    hm_ref[0] = hm.astype(BF16)
    logits = lax.dot_general(wr_ref[...], hm, (((1,), (1,)), ((), ())), preferred_element_type=F32,
                             precision=HIGHEST)
    ex = jnp.exp(logits - jnp.max(logits, axis=0, keepdims=True))
    aff_ref[0] = ex / jnp.sum(ex, axis=0, keepdims=True)


def _outproj(x, g1, sh2, sc2, ya, yb, hdirs, zml, p, tm):
    B, T, D = x.shape
    const = lambda shape: pl.BlockSpec(shape, lambda b, i: (0,) * len(shape))
    vec = pl.BlockSpec((1, 1, D), lambda b, i: (b, 0, 0))
    return pl.pallas_call(
        _outproj_kernel,
        grid=(B, T // tm),
        in_specs=[pl.BlockSpec((1, tm, D), lambda b, i: (b, i, 0)), vec, vec, vec,
                  pl.BlockSpec((1, tm, SG_WIDTH), lambda b, i: (b, i, 0)),
                  pl.BlockSpec((1, tm, MLA_HEADS * MLA_VDIM), lambda b, i: (b, i, 0)),
                  pl.BlockSpec((1, 1, tm, ML_WIDTH), lambda b, i: (0, b, i, 0)),
                  pl.BlockSpec((1, 1, tm, ML_WIDTH), lambda b, i: (1, b, i, 0)),
                  pl.BlockSpec((1, tm, ML_WIDTH), lambda b, i: (b, i, 3)),
                  const((1, ML_WIDTH)), const((ML_WIDTH, ML_WIDTH)), const((D, D)), const((1, D)), const((1, D)),
                  const((N_EXPERTS, D))],
        out_specs=[pl.BlockSpec((1, tm, D), lambda b, i: (b, i, 0)),
                   pl.BlockSpec((1, tm, D), lambda b, i: (b, i, 0)),
                   pl.BlockSpec((1, N_EXPERTS, tm), lambda b, i: (b, 0, i))],
        out_shape=[jax.ShapeDtypeStruct((B, T, D), F32), jax.ShapeDtypeStruct((B, T, D), BF16),
                   jax.ShapeDtypeStruct((B, N_EXPERTS, T), F32)],
        compiler_params=_cparams(("parallel", "parallel")),
        name="outproj",
    )(x, g1, sh2, sc2, ya, yb, hdirs, hdirs, zml, p["ml_norm_g"], p["head_avg"], p["w_out"], p["ln1_g"], p["ln1_b"],
      p["w_router_t"])


def _route_kernel(aff_ref, pos_ref, *, cap, tw):
    aff = aff_ref[0]
    E, T = aff.shape
    bits = pltpu.bitcast(aff, jnp.int32)
    count = lambda m: jnp.sum(jnp.where(m, 1.0, 0.0), axis=1, keepdims=True)
    thr = jnp.zeros((E, 1), jnp.int32)
    for bit in range(30, -1, -1):
        cand = thr | (1 << bit)
        thr = jnp.where(count(bits >= cand) >= cap, cand, thr)
    above = bits > thr
    tied = bits == thr
    need = cap - count(above)
    idx = lax.broadcasted_iota(jnp.int32, (E, T), 1)
    cut = jnp.zeros((E, 1), jnp.int32)
    for bit in range(T.bit_length() - 1, -1, -1):
        cand = cut | (1 << bit)
        ok = (cand <= T) & (count(tied & (idx < cand)) <= need)
        cut = jnp.where(ok, cand, cut)
    sel = above | (tied & (idx < cut))
    r = lax.broadcasted_iota(jnp.int32, (tw, tw), 0)
    cidx = lax.broadcasted_iota(jnp.int32, (tw, tw), 1)
    before = jnp.where(r < cidx, 1.0, 0.0).astype(BF16)
    offset = jnp.zeros((E, 1), F32)
    for blk in range(T // tw):
        cols = slice(blk * tw, (blk + 1) * tw)
        sb = jnp.where(sel[:, cols], 1.0, 0.0)
        slot = _dot(sb.astype(BF16), before) + offset
        pos_ref[0, :, cols] = jnp.where(sel[:, cols], slot, -1.0).astype(jnp.int32)
        offset = offset + jnp.sum(sb, axis=1, keepdims=True)


def _route(aff_t, cap):
    B, E, T = aff_t.shape
    return pl.pallas_call(
        functools.partial(_route_kernel, cap=cap, tw=min(512, T)),
        grid=(B,),
        in_specs=[pl.BlockSpec((1, E, T), lambda b: (b, 0, 0))],
        out_specs=pl.BlockSpec((1, E, T), lambda b: (b, 0, 0)),
        out_shape=jax.ShapeDtypeStruct((B, E, T), jnp.int32),
        compiler_params=_cparams(("parallel",)),
        name="route",
    )(aff_t)


def _expert_kernel(pos_ref, aff_ref, h_ref, wg_ref, wu_ref, wd_ref, y_ref, *, tw):
    cap = y_ref.shape[2]
    T = h_ref.shape[1]
    slot = lax.broadcasted_iota(jnp.int32, (cap, tw), 0)
    xe = jnp.zeros((cap, h_ref.shape[2]), F32)
    gate = jnp.zeros((cap, 1), F32)
    for blk in range(T // tw):
        cols = slice(blk * tw, (blk + 1) * tw)
        hit = pos_ref[0, 0, :, cols] == slot
        xe = xe + _dot(jnp.where(hit, 1.0, 0.0).astype(BF16), h_ref[0, cols, :])
        gate = gate + jnp.sum(jnp.where(hit, aff_ref[0, 0, :, cols], 0.0), axis=1, keepdims=True)
    xb = xe.astype(BF16)
    hid = _silu(_dot(xb, wg_ref[0])) * _dot(xb, wu_ref[0])
    y_ref[0, 0] = (_dot(hid.astype(BF16), wd_ref[0]) * gate).astype(BF16)


def _experts(pos, aff_t, hm, p, cap):
    B, T, D = hm.shape
    E = N_EXPERTS
    row = pl.BlockSpec((1, 1, 1, T), lambda b, e: (b, e, 0, 0))
    return pl.pallas_call(
        functools.partial(_expert_kernel, tw=min(512, T)),
        grid=(B, E),
        in_specs=[row, row,
                  pl.BlockSpec((1, T, D), lambda b, e: (b, 0, 0)),
                  pl.BlockSpec((1, D, EXPERT_FF), lambda b, e: (e, 0, 0)),
                  pl.BlockSpec((1, D, EXPERT_FF), lambda b, e: (e, 0, 0)),
                  pl.BlockSpec((1, EXPERT_FF, D), lambda b, e: (e, 0, 0))],
        out_specs=pl.BlockSpec((1, 1, cap, D), lambda b, e: (b, e, 0, 0)),
        out_shape=jax.ShapeDtypeStruct((B, E, cap, D), BF16),
        compiler_params=_cparams(("parallel", "arbitrary")),
        name="experts",
    )(pos.reshape(B, E, 1, T), aff_t.reshape(B, E, 1, T), hm, p["w_gate"], p["w_up"], p["w_down"])


def _combine_kernel(pos_ref, y_ref, x_ref, g2_ref, lg_ref, lb_ref, o_ref, acc_ref):
    e = pl.program_id(2)
    cap = y_ref.shape[2]
    tt = x_ref.shape[1]

    @pl.when(e == 0)
    def _():
        acc_ref[...] = jnp.zeros_like(acc_ref)

    slot = lax.broadcasted_iota(jnp.int32, (cap, tt), 0)
    onehot = jnp.where(pos_ref[0, 0] == slot, 1.0, 0.0).astype(BF16)
    acc_ref[...] += _tn(onehot, y_ref[0, 0])

    @pl.when(e == pl.num_programs(2) - 1)
    def _():
        o_ref[0] = _plain_norm(ALPHA * x_ref[0] + g2_ref[0] * acc_ref[...]) * lg_ref[...] + lb_ref[...]


def _combine(pos, ye, x1, g2, p, tt):
    B, T, D = x1.shape
    E, cap = ye.shape[1], ye.shape[2]
    const = lambda shape: pl.BlockSpec(shape, lambda b, i, e: (0,) * len(shape))
    return pl.pallas_call(
        _combine_kernel,
        grid=(B, T // tt, E),
        in_specs=[pl.BlockSpec((1, 1, 1, tt), lambda b, i, e: (b, e, 0, i)),
                  pl.BlockSpec((1, 1, cap, D), lambda b, i, e: (b, e, 0, 0)),
                  pl.BlockSpec((1, tt, D), lambda b, i, e: (b, i, 0)),
                  pl.BlockSpec((1, 1, D), lambda b, i, e: (b, 0, 0)),
                  const((1, D)), const((1, D))],
        out_specs=pl.BlockSpec((1, tt, D), lambda b, i, e: (b, i, 0)),
        out_shape=jax.ShapeDtypeStruct((B, T, D), F32),
        scratch_shapes=[pltpu.VMEM((tt, D), F32)],
        compiler_params=_cparams(("parallel", "parallel", "arbitrary")),
        name="combine",
    )(pos.reshape(B, E, 1, T), ye, x1, g2, p["ln2_g"], p["ln2_b"])


def _rot_cols(a):
    a4 = a.reshape(a.shape[:-1] + (2, 2, ROPE_AXIS // 2))
    return jnp.stack([-a4[..., 1, :], a4[..., 0, :]], axis=-2).reshape(a.shape)


def _pad_cols(a, before, total):
    pads = [(0, 0)] * (a.ndim - 1) + [(before, total - before - a.shape[-1])]
    return jnp.pad(a, pads)


def _prep_layer(l, w):
    D = D_MODEL
    wi, bi = w["w_in"][l], w["b_in"][l][None, :]
    n_sg, n_mla = 2 * SG_WIDTH, Q_LORA + KV_LORA + MLA_ROPE
    o_kr = n_sg + Q_LORA + KV_LORA
    o_ml = n_sg + n_mla
    o_g = o_ml + 4 * ML_WIDTH

    def cat(a):
        kr = a[:, o_kr:o_kr + MLA_ROPE]
        gates = a[:, o_g:o_g + 4 * ML_HEADS].reshape(-1, 2, 2, ML_HEADS).transpose(0, 2, 1, 3).reshape(-1, 4 * ML_HEADS)
        return jnp.concatenate([a[:, :o_kr], _pad_cols(kr, MLA_NOPE, HEAD_PAD), _pad_cols(_rot_cols(kr), MLA_NOPE, HEAD_PAD),
                                a[:, o_ml:o_g], _pad_cols(gates, 0, LANE)], axis=1)

    wq = w["w_uq"][l].reshape(Q_LORA, MLA_HEADS, MLA_NOPE + MLA_ROPE)
    wq_main = _pad_cols(wq, 0, HEAD_PAD).reshape(Q_LORA, -1)
    wq_rot = _pad_cols(_rot_cols(wq[..., MLA_NOPE:]), MLA_NOPE, HEAD_PAD).reshape(Q_LORA, -1)
    wkv = w["w_ukv"][l].reshape(KV_LORA, MLA_HEADS, MLA_NOPE + MLA_VDIM)
    wk = _pad_cols(wkv[..., :MLA_NOPE], 0, HEAD_PAD).reshape(KV_LORA, -1)
    wv = _pad_cols(wkv[..., MLA_NOPE:], 0, HEAD_PAD).reshape(KV_LORA, -1)
    head = jnp.arange(ML_WIDTH) // ML_DIM
    return {
        "wcat": cat(wi).astype(BF16), "bcat": cat(bi),
        "q_norm_g": w["q_norm_g"][l][None, :], "kv_norm_g": w["kv_norm_g"][l][None, :],
        "wq": jnp.concatenate([wq_main, wq_rot], axis=1).astype(BF16),
        "wkv": jnp.concatenate([wk, wv], axis=1).astype(BF16),
        "sg_ln_g": w["sg_ln_g"][l][None, :], "sg_ln_b": w["sg_ln_b"][l][None, :],
        "sg_w": w["sg_w"][l].astype(BF16),
        "sg_bias": jnp.repeat(w["sg_b"][l].T, SG_GDIM, axis=1),
        "conv_w": jnp.pad(w["ml_conv_w"][l], ((0, 8 - ML_CONV), (0, 0))), "conv_b": w["ml_conv_b"][l][None, :],
        "f_bias": _pad_cols(w["ml_f_bias"][l], 0, LANE),
        "ml_norm_g": w["ml_norm_g"][l][None, :],
        "head_avg": (head[:, None] == head[None, :]).astype(F32) / ML_DIM,
        "w_out": w["w_out"][l].astype(BF16),
        "ln1_g": w["ln1_g"][l][None, :], "ln1_b": w["ln1_b"][l][None, :],
        "w_router_t": w["w_router"][l].T,
        "w_gate": w["w_gate"][l].astype(BF16), "w_up": w["w_up"][l].astype(BF16), "w_down": w["w_down"][l].astype(BF16),
        "ln2_g": w["ln2_g"][l][None, :], "ln2_b": w["ln2_b"][l][None, :],
    }


def _rope_tables(T):
    rows = T // GRID_W
    row = jnp.repeat(jnp.arange(rows, dtype=F32), GRID_W)
    colv = jnp.tile(jnp.arange(GRID_W, dtype=F32), rows)
    inv = ROPE_BASE ** (-jnp.arange(ROPE_AXIS // 2, dtype=F32) * 2.0 / ROPE_AXIS)
    ang = jnp.concatenate([row[:, None] * inv, colv[:, None] * inv], axis=-1)
    half = ROPE_AXIS // 2
    spread = lambda t: jnp.concatenate([t[:, :half], t[:, :half], t[:, half:], t[:, half:]], axis=1)
    cosT = jnp.concatenate([jnp.ones((T, MLA_NOPE), F32), spread(jnp.cos(ang)), jnp.zeros((T, 32), F32)], axis=1)
    sinT = _pad_cols(spread(jnp.sin(ang)), MLA_NOPE, HEAD_PAD)
    return cosT, sinT


def _moe(x1, hm, aff_t, g2, p, tt):
    cap = EC_CAPACITY * x1.shape[1] // N_EXPERTS
    pos = _route(aff_t, cap)
    ye = _experts(pos, aff_t, hm, p, cap)
    return _combine(pos, ye, x1, g2, p, tt)


def _forward(x, c, ctx, c_ctx, w):
    B, T, D = x.shape
    Tc = ctx.shape[1]
    L = w["w_ada"].shape[0]
    cc = jnp.zeros((16, D), F32).at[:B].set(c).at[B].set(c_ctx)
    ada = _ada(cc, w["w_ada"], w["b_ada"])
    cos_l, sin_l = _rope_tables(T)
    cos_c = _pad_cols(jnp.ones((Tc, MLA_NOPE + MLA_ROPE), F32), 0, HEAD_PAD)
    sin_c = jnp.zeros((Tc, HEAD_PAD), F32)
    zero_c = jnp.zeros((B, 2, ML_HEADS * HEAD_PAD, HEAD_PAD), F32)
    zero_m = jnp.zeros((B, 2, 1, LANE), F32)
    tm = 256
    xl, xc = x, ctx
    for l in range(L):
        p = _prep_layer(l, w)
        mods_l = [ada[l, :B, i * D:(i + 1) * D][:, None, :] for i in range(6)]
        mods_c = [jnp.broadcast_to(ada[l, B, i * D:(i + 1) * D][None, None, :], (B, 1, D)) for i in range(6)]
        update_ctx = l < L - 1
        ya_c, q_c, k_c, v_c, zml_c, g_c = _inproj(xc, mods_c[0], mods_c[1], cos_c, sin_c, p, min(tm, Tc))
        ya_l, q_l, k_l, v_l, zml_l, g_l = _inproj(xl, mods_l[0], mods_l[1], cos_l, sin_l, p, tm)
        hd_c, cs_c, ms_c = _mlstm(zml_c, g_c, p, zero_c, zero_m)
        hd_l, _, _ = _mlstm(zml_l, g_l, p, cs_c, ms_c)
        yb_l = _attention(q_l, [(k_l, v_l), (k_c, v_c)], tq=256)
        x1_l, hm_l, aff_l = _outproj(xl, mods_l[2], mods_l[3], mods_l[4], ya_l, yb_l, hd_l, zml_l, p, tm)
        xl = _moe(x1_l, hm_l, aff_l, mods_l[5], p, tt=min(1024, T))
        if update_ctx:
            yb_c = _attention(q_c, [(k_c, v_c)], tq=min(256, Tc))
            x1_c, hm_c, aff_c = _outproj(xc, mods_c[2], mods_c[3], mods_c[4], ya_c, yb_c, hd_c, zml_c, p, min(tm, Tc))
            xc = _moe(x1_c, hm_c, aff_c, mods_c[5], p, tt=min(1024, Tc))
    return xl


def kernel(x, c, ctx, c_ctx, w_ada, b_ada, w_in, b_in, sg_ln_g, sg_ln_b, sg_w, sg_b, q_norm_g, kv_norm_g, w_uq, w_ukv,
           ml_conv_w, ml_conv_b, ml_f_bias, ml_norm_g, w_out, ln1_g, ln1_b, w_router, w_gate, w_up, w_down, ln2_g, ln2_b):
    w = dict(w_ada=w_ada, b_ada=b_ada, w_in=w_in, b_in=b_in, sg_ln_g=sg_ln_g, sg_ln_b=sg_ln_b, sg_w=sg_w, sg_b=sg_b,
             q_norm_g=q_norm_g, kv_norm_g=kv_norm_g, w_uq=w_uq, w_ukv=w_ukv, ml_conv_w=ml_conv_w, ml_conv_b=ml_conv_b,
             ml_f_bias=ml_f_bias, ml_norm_g=ml_norm_g, w_out=w_out, ln1_g=ln1_g, ln1_b=ln1_b, w_router=w_router,
             w_gate=w_gate, w_up=w_up, w_down=w_down, ln2_g=ln2_g, ln2_b=ln2_b)
    return _forward(x, c, ctx, c_ctx, w)
```

```python
import functools

import jax
import jax.numpy as jnp
from jax import lax
from jax.experimental import pallas as pl
from jax.experimental.pallas import tpu as pltpu

F32 = jnp.float32
BF16 = jnp.bfloat16
HIGHEST = lax.Precision.HIGHEST

D_MODEL = 1024
DEPTH = 2
GRID_W = 64
SG_WIDTH = 256
SG_GROUPS = 4
SG_GDIM = 64
SG_CHUNK = 128
MLA_HEADS = 8
MLA_NOPE = 64
MLA_ROPE = 32
MLA_VDIM = 64
Q_LORA = 384
KV_LORA = 256
ROPE_AXIS = 16
ROPE_BASE = 10000.0
MLA_SCALE = (MLA_NOPE + MLA_ROPE) ** -0.5
ML_HEADS = 4
ML_DIM = 64
ML_WIDTH = 256
ML_CONV = 5
N_EXPERTS = 16
EXPERT_FF = 1024
EC_CAPACITY = 2
ALPHA = (2 * DEPTH) ** 0.25
LN_EPS = 1e-6
LOG2E = 1.4426950408889634

LANE = 128
HEAD_PAD = 128
ML_CHUNK = 256
OFF_SG, OFF_CQ, OFF_CKV, OFF_KR, OFF_KRR, OFF_ML, OFF_G, N_CAT = 0, 512, 896, 1152, 1280, 1408, 2432, 2560
VMEM_LIMIT = 56 * 1024 * 1024
ATTN_UNROLL = 8


def _cparams(sem):
    return pltpu.CompilerParams(dimension_semantics=sem, vmem_limit_bytes=VMEM_LIMIT)


def _nt(a, b):
    return lax.dot_general(a, b, (((1,), (1,)), ((), ())), preferred_element_type=F32)


def _tn(a, b):
    return lax.dot_general(a, b, (((0,), (0,)), ((), ())), preferred_element_type=F32)


def _dot(a, b, precision=None):
    return jnp.dot(a, b, preferred_element_type=F32, precision=precision)


def _plain_norm(x):
    mu = jnp.mean(x, axis=-1, keepdims=True)
    xc = x - mu
    var = jnp.mean(xc * xc, axis=-1, keepdims=True)
    return xc * lax.rsqrt(var + LN_EPS)


def _rms(x):
    return x * lax.rsqrt(jnp.mean(x * x, axis=-1, keepdims=True) + LN_EPS)


def _silu(x):
    return x * (1.0 / (1.0 + jnp.exp(-x)))


def _gelu_tanh(x):
    return 0.5 * x * (1.0 + jnp.tanh(0.7978845608028654 * (x + 0.044715 * (x * x * x))))


def _log_sigmoid(x):
    return jnp.minimum(x, 0.0) - jnp.log(1.0 + jnp.exp(-jnp.abs(x)))


def _ada_kernel(c_ref, w_ref, b_ref, o_ref):
    o_ref[0] = _dot(_silu(c_ref[...]), w_ref[0], precision=HIGHEST) + b_ref[0]


def _ada(cc, w_ada, b_ada):
    L, D, N = w_ada.shape
    tn = 1536
    return pl.pallas_call(
        _ada_kernel,
        grid=(L, N // tn),
        in_specs=[pl.BlockSpec((16, D), lambda l, j: (0, 0)),
                  pl.BlockSpec((1, D, tn), lambda l, j: (l, 0, j)),
                  pl.BlockSpec((1, 1, tn), lambda l, j: (l, 0, j))],
        out_specs=pl.BlockSpec((1, 16, tn), lambda l, j: (l, 0, j)),
        out_shape=jax.ShapeDtypeStruct((L, 16, N), F32),
        compiler_params=_cparams(("parallel", "parallel")),
        name="ada",
    )(cc, w_ada, b_ada.reshape(L, 1, N))


def _inproj_kernel(x_ref, sh_ref, sc_ref, cos_ref, sin_ref, wcat_ref, bcat_ref, qg_ref, kvg_ref, wq_ref, wkv_ref,
                   sglg_ref, sglb_ref, sgw_ref, sgb_ref,
                   ya_ref, q_ref, k_ref, v_ref, zml_ref, g_ref):
    tm = x_ref.shape[1]
    h = _plain_norm(x_ref[0]) * (1.0 + sc_ref[0]) + sh_ref[0]
    z = _dot(h.astype(BF16), wcat_ref[...]) + bcat_ref[...]
    zs = _gelu_tanh(z[:, OFF_SG:OFF_SG + 2 * SG_WIDTH])
    u = zs[:, :SG_WIDTH]
    vn = _plain_norm(zs[:, SG_WIDTH:]) * sglg_ref[...] + sglb_ref[...]
    group = jnp.right_shift(lax.broadcasted_iota(jnp.int32, (SG_CHUNK, SG_WIDTH), 1), SG_GDIM.bit_length() - 1)
    for ch in range(tm // SG_CHUNK):
        rows = slice(ch * SG_CHUNK, (ch + 1) * SG_CHUNK)
        vc = vn[rows].astype(BF16)
        mixed = sgb_ref[...]
        for g in range(SG_GROUPS):
            mixed = mixed + jnp.where(group == g, _dot(sgw_ref[g], vc), 0.0)
        ya_ref[0, rows, :] = (u[rows] * mixed).astype(BF16)
    cosT = cos_ref[...]
    sinT = sin_ref[...]
    cqn = (_rms(z[:, OFF_CQ:OFF_CQ + Q_LORA]) * qg_ref[...]).astype(BF16)
    qq = _dot(cqn, wq_ref[...])
    ckvn = (_rms(z[:, OFF_CKV:OFF_CKV + KV_LORA]) * kvg_ref[...]).astype(BF16)
    kv = _dot(ckvn, wkv_ref[...])
    kr = z[:, OFF_KR:OFF_KR + HEAD_PAD] * cosT + z[:, OFF_KRR:OFF_KRR + HEAD_PAD] * sinT
    one_lane = jnp.where(lax.broadcasted_iota(jnp.int32, (1, HEAD_PAD), 1) == MLA_VDIM, 1.0, 0.0)
    hw = MLA_HEADS * HEAD_PAD
    for hd in range(MLA_HEADS):
        cols = slice(hd * HEAD_PAD, (hd + 1) * HEAD_PAD)
        rcols = slice(hw + hd * HEAD_PAD, hw + (hd + 1) * HEAD_PAD)
        q_ref[0, hd] = ((qq[:, cols] * cosT + qq[:, rcols] * sinT) * (MLA_SCALE * LOG2E)).astype(BF16)
        k_ref[0, hd] = (kv[:, cols] + kr).astype(BF16)
        v_ref[0, hd] = (kv[:, rcols] + one_lane).astype(BF16)
    zml_ref[0] = z[:, OFF_ML:OFF_ML + 4 * ML_WIDTH]
    g_ref[0] = z[:, OFF_G:OFF_G + LANE]


def _inproj(x, shift, scale, cosT, sinT, p, tm):
    B, T, D = x.shape
    const = lambda shape: pl.BlockSpec(shape, lambda b, i: (0,) * len(shape))
    heads = (B, MLA_HEADS, T, HEAD_PAD)
    return pl.pallas_call(
        _inproj_kernel,
        grid=(B, T // tm),
        in_specs=[pl.BlockSpec((1, tm, D), lambda b, i: (b, i, 0)),
                  pl.BlockSpec((1, 1, D), lambda b, i: (b, 0, 0)),
                  pl.BlockSpec((1, 1, D), lambda b, i: (b, 0, 0)),
                  pl.BlockSpec((tm, HEAD_PAD), lambda b, i: (i, 0)),
                  pl.BlockSpec((tm, HEAD_PAD), lambda b, i: (i, 0)),
                  const((D, N_CAT)), const((1, N_CAT)), const((1, Q_LORA)), const((1, KV_LORA)),
                  const((Q_LORA, 2 * MLA_HEADS * HEAD_PAD)), const((KV_LORA, 2 * MLA_HEADS * HEAD_PAD)),
                  const((1, SG_WIDTH)), const((1, SG_WIDTH)), const((SG_GROUPS, SG_CHUNK, SG_CHUNK)),
                  const((SG_CHUNK, SG_WIDTH))],
        out_specs=[pl.BlockSpec((1, tm, SG_WIDTH), lambda b, i: (b, i, 0)),
                   pl.BlockSpec((1, MLA_HEADS, tm, HEAD_PAD), lambda b, i: (b, 0, i, 0)),
                   pl.BlockSpec((1, MLA_HEADS, tm, HEAD_PAD), lambda b, i: (b, 0, i, 0)),
                   pl.BlockSpec((1, MLA_HEADS, tm, HEAD_PAD), lambda b, i: (b, 0, i, 0)),
                   pl.BlockSpec((1, tm, 4 * ML_WIDTH), lambda b, i: (b, i, 0)),
                   pl.BlockSpec((1, tm, LANE), lambda b, i: (b, i, 0))],
        out_shape=[jax.ShapeDtypeStruct((B, T, SG_WIDTH), BF16),
                   jax.ShapeDtypeStruct(heads, BF16), jax.ShapeDtypeStruct(heads, BF16),
                   jax.ShapeDtypeStruct(heads, BF16),
                   jax.ShapeDtypeStruct((B, T, 4 * ML_WIDTH), F32),
                   jax.ShapeDtypeStruct((B, T, LANE), F32)],
        compiler_params=_cparams(("parallel", "parallel")),
        name="inproj",
    )(x, shift, scale, cosT, sinT, p["wcat"], p["bcat"], p["q_norm_g"], p["kv_norm_g"], p["wq"], p["wkv"],
      p["sg_ln_g"], p["sg_ln_b"], p["sg_w"], p["sg_bias"])


def _attn_kernel(*refs, n_src, tk):
    q_ref = refs[0]
    kv_refs = refs[1:1 + 2 * n_src]
    o_ref = refs[1 + 2 * n_src]
    tq = q_ref.shape[2]
    qs = [q_ref[0, hh] for hh in range(2)]
    carry = (tuple(jnp.full((tq, 1), -jnp.inf, F32) for _ in range(2))
             + tuple(jnp.zeros((tq, HEAD_PAD), F32) for _ in range(2)))
    for s in range(n_src):
        k_ref, v_ref = kv_refs[2 * s], kv_refs[2 * s + 1]
        ck = min(tk, k_ref.shape[2])

        def body(c, carry, k_ref=k_ref, v_ref=v_ref, ck=ck):
            rows = pl.ds(pl.multiple_of(c * ck, ck), ck)
            ms, accs = [], []
            for hh in range(2):
                m, acc = carry[hh], carry[2 + hh]
                sc = _nt(qs[hh], k_ref[0, hh, rows, :])
                m_new = jnp.maximum(m, jnp.max(sc, axis=1, keepdims=True))
                pr = jnp.exp2(sc - m_new)
                accs.append(acc * jnp.exp2(m - m_new) + _dot(pr.astype(BF16), v_ref[0, hh, rows, :]))
                ms.append(m_new)
            return tuple(ms) + tuple(accs)

        n_chunks = k_ref.shape[2] // ck
        carry = lax.fori_loop(0, n_chunks, body, carry, unroll=min(ATTN_UNROLL, n_chunks))
    outs = [carry[2 + hh] / carry[2 + hh][:, MLA_VDIM:MLA_VDIM + 1] for hh in range(2)]
    lane = lax.broadcasted_iota(jnp.int32, (tq, HEAD_PAD), 1)
    o_ref[0] = jnp.where(lane < MLA_VDIM, outs[0], pltpu.roll(outs[1], MLA_VDIM, axis=1)).astype(BF16)


def _attention(q, kvs, tq, tk=512):
    B, H, T, _ = q.shape
    in_specs = [pl.BlockSpec((1, 2, tq, HEAD_PAD), lambda b, hp, i: (b, hp, i, 0))]
    args = [q]
    for k, v in kvs:
        K = k.shape[2]
        in_specs += [pl.BlockSpec((1, 2, K, HEAD_PAD), lambda b, hp, i: (b, hp, 0, 0))] * 2
        args += [k, v]
    return pl.pallas_call(
        functools.partial(_attn_kernel, n_src=len(kvs), tk=tk),
        grid=(B, H // 2, T // tq),
        in_specs=in_specs,
        out_specs=pl.BlockSpec((1, tq, HEAD_PAD), lambda b, hp, i: (b, i, hp)),
        out_shape=jax.ShapeDtypeStruct((B, T, H * MLA_VDIM), BF16),
        compiler_params=_cparams(("parallel", "parallel", "parallel")),
        name="attention",
    )(*args)


def _mlstm_kernel(zp_ref, zc_ref, zn_ref, v_ref, g_ref, cw_ref, cb_ref, fb_ref, ci_ref, mi_ref,
                  h_ref, cs_ref, ms_ref, *, nc):
    L = zc_ref.shape[1]
    d = pl.program_id(1)
    j = pl.program_id(2)
    c = j + d * (nc - 1 - 2 * j)

    @pl.when(j == 0)
    def _():
        cs_ref[0, 0] = ci_ref[0, 0]
        ms_ref[0, 0] = mi_ref[0, 0]

    prev = jnp.where(c > 0, zp_ref[0], 0.0)
    nxt = jnp.where(c < nc - 1, zn_ref[0], 0.0)
    big = jnp.concatenate([prev, zc_ref[0], nxt], axis=0)
    n_big = L + 16
    pad = (ML_CONV - 1) // 2
    conv = cb_ref[...]
    for t in range(ML_CONV):
        s = t - pad
        shifted = big if s == 0 else pltpu.roll(big, (-s) % n_big, axis=0)
        conv = conv + shifted[8:8 + L] * cw_ref[t:t + 1, :]
    qk = _silu(conv)
    qv = qk[:, :ML_WIDTH]
    kvv = qk[:, ML_WIDTH:] * (ML_DIM ** -0.5)
    vv = v_ref[0]

    gl = g_ref[0]
    gl = jnp.where(d == 0, gl, pltpu.roll(gl, LANE - ML_HEADS, axis=1))
    li = gl
    fb = jnp.where(d == 0, fb_ref[0:1, :], fb_ref[1:2, :])
    lf = _log_sigmoid(pltpu.roll(gl, LANE - 2 * ML_HEADS, axis=1) + fb)
    row = lax.broadcasted_iota(jnp.int32, (L, L), 0)
    col = lax.broadcasted_iota(jnp.int32, (L, L), 1)
    causal = (row - col) * (1 - 2 * d) >= 0
    bc = _dot(jnp.where(causal, 1.0, 0.0), lf, precision=HIGHEST)
    b_end = jnp.where(d == 0, bc[L - 1:L, :], bc[0:1, :])
    gw = b_end - bc + li
    m_loc = jnp.max(gw, axis=0, keepdims=True)
    wt = jnp.exp(gw - m_loc)
    rt = (li - bc).T
    m_prev = ms_ref[0, 0]
    m_next = jnp.maximum(b_end + m_prev, m_loc)
    a_sc = jnp.exp(b_end + m_prev - m_next)
    s_sc = jnp.exp(m_loc - m_next)

    lane = lax.broadcasted_iota(jnp.int32, (L, HEAD_PAD), 1)
    lane_r = lax.broadcasted_iota(jnp.int32, (1, HEAD_PAD), 1)
    for blk in range(ML_HEADS // 2):
        cols = slice(blk * HEAD_PAD, (blk + 1) * HEAD_PAD)
        q_blk, k_blk, v_blk = qv[:, cols], kvv[:, cols], vv[:, cols]
        hh = []
        for half in range(2):
            hd = 2 * blk + half
            in_half = (lane >= ML_DIM) if half else (lane < ML_DIM)
            den_lane = 0 if half else ML_DIM
            qm = jnp.where(in_half, q_blk, 0.0).astype(BF16)
            km = jnp.where(in_half, k_blk, 0.0).astype(BF16)
            vaug = jnp.where(in_half, v_blk, jnp.where(lane == den_lane, 1.0, 0.0))
            b_col = bc[:, hd:hd + 1]
            dm = jnp.where(causal, b_col + rt[hd:hd + 1, :], -jnp.inf)
            a_col = b_col + m_prev[:, hd:hd + 1]
            m_t = jnp.maximum(a_col, jnp.max(dm, axis=1, keepdims=True))
            pm = jnp.exp(dm - m_t) * _nt(qm, km)
            inter = jnp.exp(a_col - m_t)
            c0 = cs_ref[0, 0, hd * HEAD_PAD:(hd + 1) * HEAD_PAD, :]
            nd = inter * _nt(qm, c0.astype(BF16)) + _dot(pm.astype(BF16), vaug.astype(BF16))
            den = nd[:, den_lane:den_lane + 1]
            hh.append(nd / jnp.maximum(jnp.abs(den), jnp.exp(-m_t)))
            wv = (wt[:, hd:hd + 1] * vaug).astype(BF16)
            c_loc = _tn(wv, km)
            cs_ref[0, 0, hd * HEAD_PAD:(hd + 1) * HEAD_PAD, :] = a_sc[:, hd:hd + 1] * c0 + s_sc[:, hd:hd + 1] * c_loc
        h_ref[0, 0, :, cols] = jnp.where(lane < ML_DIM, hh[0], hh[1])
    ms_ref[0, 0] = jnp.where(lane_r < ML_HEADS, m_next, m_prev)


def _mlstm(zml, gates, p, c_init, m_init):
    B, T, _ = zml.shape
    L = min(ML_CHUNK, T)
    nc = T // L
    hb = L // 8
    chunk = lambda b, d, j: j + d * (nc - 1 - 2 * j)
    const = lambda shape: pl.BlockSpec(shape, lambda b, d, j: (0,) * len(shape))
    state = pl.BlockSpec((1, 1, ML_HEADS * HEAD_PAD, HEAD_PAD), lambda b, d, j: (b, d, 0, 0))
    mstate = pl.BlockSpec((1, 1, 1, LANE), lambda b, d, j: (b, d, 0, 0))
    return pl.pallas_call(
        functools.partial(_mlstm_kernel, nc=nc),
        grid=(B, 2, nc),
        in_specs=[pl.BlockSpec((1, 8, 2 * ML_WIDTH), lambda b, d, j: (b, jnp.maximum(chunk(b, d, j) * hb - 1, 0), 0)),
                  pl.BlockSpec((1, L, 2 * ML_WIDTH), lambda b, d, j: (b, chunk(b, d, j), 0)),
                  pl.BlockSpec((1, 8, 2 * ML_WIDTH),
                               lambda b, d, j: (b, jnp.minimum((chunk(b, d, j) + 1) * hb, T // 8 - 1), 0)),
                  pl.BlockSpec((1, L, ML_WIDTH), lambda b, d, j: (b, chunk(b, d, j), 2)),
                  pl.BlockSpec((1, L, LANE), lambda b, d, j: (b, chunk(b, d, j), 0)),
                  const((8, 2 * ML_WIDTH)), const((1, 2 * ML_WIDTH)), const((2, LANE)),
                  state, mstate],
        out_specs=[pl.BlockSpec((1, 1, L, ML_WIDTH), lambda b, d, j: (d, b, chunk(b, d, j), 0)),
                   state, mstate],
        out_shape=[jax.ShapeDtypeStruct((2, B, T, ML_WIDTH), F32),
                   jax.ShapeDtypeStruct((B, 2, ML_HEADS * HEAD_PAD, HEAD_PAD), F32),
                   jax.ShapeDtypeStruct((B, 2, 1, LANE), F32)],
        compiler_params=_cparams(("parallel", "parallel", "arbitrary")),
        name="mlstm",
    )(zml, zml, zml, zml, gates, p["conv_w"], p["conv_b"], p["f_bias"], c_init, m_init)


def _outproj_kernel(x_ref, g1_ref, sh2_ref, sc2_ref, ya_ref, yb_ref, h0_ref, h1_ref, o_ref, mg_ref, avg_ref,
                    wo_ref, l1g_ref, l1b_ref, wr_ref, x1_ref, hm_ref, aff_ref):
    hs = h0_ref[0, 0] + h1_ref[0, 0]
    mu = _dot(hs, avg_ref[...], precision=HIGHEST)
    hc = hs - mu
    var = _dot(hc * hc, avg_ref[...], precision=HIGHEST)
    yc = (1.0 / (1.0 + jnp.exp(-o_ref[0]))) * (hc * lax.rsqrt(var + LN_EPS) * mg_ref[...])
    y = (_dot(ya_ref[0], wo_ref[0:SG_WIDTH, :])
         + _dot(yb_ref[0], wo_ref[SG_WIDTH:SG_WIDTH + MLA_HEADS * MLA_VDIM, :])
         + _dot(yc.astype(BF16), wo_ref[SG_WIDTH + MLA_HEADS * MLA_VDIM:, :]))
    x1 = _plain_norm(ALPHA * x_ref[0] + g1_ref[0] * y) * l1g_ref[...] + l1b_ref[...]
    x1_ref[0] = x1
    hm = _plain_norm(x1) * (1.0 + sc2_ref[0]) + sh2_ref[0]
    hm_ref[0] = hm.astype(BF16)
    logits = lax.dot_general(wr_ref[...], hm, (((1,), (1,)), ((), ())), preferred_element_type=F32,
                             precision=HIGHEST)
    ex = jnp.exp(logits - jnp.max(logits, axis=0, keepdims=True))
    aff_ref[0] = ex / jnp.sum(ex, axis=0, keepdims=True)


def _outproj(x, g1, sh2, sc2, ya, yb, hdirs, zml, p, tm):
    B, T, D = x.shape
    const = lambda shape: pl.BlockSpec(shape, lambda b, i: (0,) * len(shape))
    vec = pl.BlockSpec((1, 1, D), lambda b, i: (b, 0, 0))
    return pl.pallas_call(
        _outproj_kernel,
        grid=(B, T // tm),
        in_specs=[pl.BlockSpec((1, tm, D), lambda b, i: (b, i, 0)), vec, vec, vec,
                  pl.BlockSpec((1, tm, SG_WIDTH), lambda b, i: (b, i, 0)),
                  pl.BlockSpec((1, tm, MLA_HEADS * MLA_VDIM), lambda b, i: (b, i, 0)),
                  pl.BlockSpec((1, 1, tm, ML_WIDTH), lambda b, i: (0, b, i, 0)),
                  pl.BlockSpec((1, 1, tm, ML_WIDTH), lambda b, i: (1, b, i, 0)),
                  pl.BlockSpec((1, tm, ML_WIDTH), lambda b, i: (b, i, 3)),
                  const((1, ML_WIDTH)), const((ML_WIDTH, ML_WIDTH)), const((D, D)), const((1, D)), const((1, D)),
                  const((N_EXPERTS, D))],
        out_specs=[pl.BlockSpec((1, tm, D), lambda b, i: (b, i, 0)),
                   pl.BlockSpec((1, tm, D), lambda b, i: (b, i, 0)),
                   pl.BlockSpec((1, N_EXPERTS, tm), lambda b, i: (b, 0, i))],
        out_shape=[jax.ShapeDtypeStruct((B, T, D), F32), jax.ShapeDtypeStruct((B, T, D), BF16),
                   jax.ShapeDtypeStruct((B, N_EXPERTS, T), F32)],
        compiler_params=_cparams(("parallel", "parallel")),
        name="outproj",
    )(x, g1, sh2, sc2, ya, yb, hdirs, hdirs, zml, p["ml_norm_g"], p["head_avg"], p["w_out"], p["ln1_g"], p["ln1_b"],
      p["w_router_t"])


def _route_kernel(aff_ref, pos_ref, *, cap, tw):
    aff = aff_ref[0]
    E, T = aff.shape
    bits = pltpu.bitcast(aff, jnp.int32)
    count = lambda m: jnp.sum(jnp.where(m, 1.0, 0.0), axis=1, keepdims=True)
    thr = jnp.zeros((E, 1), jnp.int32)
    for bit in range(30, -1, -1):
        cand = thr | (1 << bit)
        thr = jnp.where(count(bits >= cand) >= cap, cand, thr)
    above = bits > thr
    tied = bits == thr
    need = cap - count(above)
    idx = lax.broadcasted_iota(jnp.int32, (E, T), 1)
    cut = jnp.zeros((E, 1), jnp.int32)
    for bit in range(T.bit_length() - 1, -1, -1):
        cand = cut | (1 << bit)
        ok = (cand <= T) & (count(tied & (idx < cand)) <= need)
        cut = jnp.where(ok, cand, cut)
    sel = above | (tied & (idx < cut))
    r = lax.broadcasted_iota(jnp.int32, (tw, tw), 0)
    cidx = lax.broadcasted_iota(jnp.int32, (tw, tw), 1)
    before = jnp.where(r < cidx, 1.0, 0.0).astype(BF16)
    offset = jnp.zeros((E, 1), F32)
    for blk in range(T // tw):
        cols = slice(blk * tw, (blk + 1) * tw)
        sb = jnp.where(sel[:, cols], 1.0, 0.0)
        slot = _dot(sb.astype(BF16), before) + offset
        pos_ref[0, :, cols] = jnp.where(sel[:, cols], slot, -1.0).astype(jnp.int32)
        offset = offset + jnp.sum(sb, axis=1, keepdims=True)


def _route(aff_t, cap):
    B, E, T = aff_t.shape
    return pl.pallas_call(
        functools.partial(_route_kernel, cap=cap, tw=min(512, T)),
        grid=(B,),
        in_specs=[pl.BlockSpec((1, E, T), lambda b: (b, 0, 0))],
        out_specs=pl.BlockSpec((1, E, T), lambda b: (b, 0, 0)),
        out_shape=jax.ShapeDtypeStruct((B, E, T), jnp.int32),
        compiler_params=_cparams(("parallel",)),
        name="route",
    )(aff_t)


def _expert_kernel(pos_ref, aff_ref, h_ref, wg_ref, wu_ref, wd_ref, y_ref, *, tw):
    cap = y_ref.shape[2]
    T = h_ref.shape[1]
    slot = lax.broadcasted_iota(jnp.int32, (cap, tw), 0)
    xe = jnp.zeros((cap, h_ref.shape[2]), F32)
    gate = jnp.zeros((cap, 1), F32)
    for blk in range(T // tw):
        cols = slice(blk * tw, (blk + 1) * tw)
        hit = pos_ref[0, 0, :, cols] == slot
        xe = xe + _dot(jnp.where(hit, 1.0, 0.0).astype(BF16), h_ref[0, cols, :])
        gate = gate + jnp.sum(jnp.where(hit, aff_ref[0, 0, :, cols], 0.0), axis=1, keepdims=True)
    xb = xe.astype(BF16)
    hid = _silu(_dot(xb, wg_ref[0])) * _dot(xb, wu_ref[0])
    y_ref[0, 0] = (_dot(hid.astype(BF16), wd_ref[0]) * gate).astype(BF16)


def _experts(pos, aff_t, hm, p, cap):
    B, T, D = hm.shape
    E = N_EXPERTS
    row = pl.BlockSpec((1, 1, 1, T), lambda b, e: (b, e, 0, 0))
    return pl.pallas_call(
        functools.partial(_expert_kernel, tw=min(512, T)),
        grid=(B, E),
        in_specs=[row, row,
                  pl.BlockSpec((1, T, D), lambda b, e: (b, 0, 0)),
                  pl.BlockSpec((1, D, EXPERT_FF), lambda b, e: (e, 0, 0)),
                  pl.BlockSpec((1, D, EXPERT_FF), lambda b, e: (e, 0, 0)),
                  pl.BlockSpec((1, EXPERT_FF, D), lambda b, e: (e, 0, 0))],
        out_specs=pl.BlockSpec((1, 1, cap, D), lambda b, e: (b, e, 0, 0)),
        out_shape=jax.ShapeDtypeStruct((B, E, cap, D), BF16),
        compiler_params=_cparams(("parallel", "arbitrary")),
        name="experts",
    )(pos.reshape(B, E, 1, T), aff_t.reshape(B, E, 1, T), hm, p["w_gate"], p["w_up"], p["w_down"])


def _combine_kernel(pos_ref, y_ref, x_ref, g2_ref, lg_ref, lb_ref, o_ref, acc_ref):
    e = pl.program_id(2)
    cap = y_ref.shape[2]
    tt = x_ref.shape[1]

    @pl.when(e == 0)
    def _():
        acc_ref[...] = jnp.zeros_like(acc_ref)

    slot = lax.broadcasted_iota(jnp.int32, (cap, tt), 0)
    onehot = jnp.where(pos_ref[0, 0] == slot, 1.0, 0.0).astype(BF16)
    acc_ref[...] += _tn(onehot, y_ref[0, 0])

    @pl.when(e == pl.num_programs(2) - 1)
    def _():
        o_ref[0] = _plain_norm(ALPHA * x_ref[0] + g2_ref[0] * acc_ref[...]) * lg_ref[...] + lb_ref[...]


def _combine(pos, ye, x1, g2, p, tt):
    B, T, D = x1.shape
    E, cap = ye.shape[1], ye.shape[2]
    const = lambda shape: pl.BlockSpec(shape, lambda b, i, e: (0,) * len(shape))
    return pl.pallas_call(
        _combine_kernel,
        grid=(B, T // tt, E),
        in_specs=[pl.BlockSpec((1, 1, 1, tt), lambda b, i, e: (b, e, 0, i)),
                  pl.BlockSpec((1, 1, cap, D), lambda b, i, e: (b, e, 0, 0)),
                  pl.BlockSpec((1, tt, D), lambda b, i, e: (b, i, 0)),
                  pl.BlockSpec((1, 1, D), lambda b, i, e: (b, 0, 0)),
                  const((1, D)), const((1, D))],
        out_specs=pl.BlockSpec((1, tt, D), lambda b, i, e: (b, i, 0)),
        out_shape=jax.ShapeDtypeStruct((B, T, D), F32),
        scratch_shapes=[pltpu.VMEM((tt, D), F32)],
        compiler_params=_cparams(("parallel", "parallel", "arbitrary")),
        name="combine",
    )(pos.reshape(B, E, 1, T), ye, x1, g2, p["ln2_g"], p["ln2_b"])


def _rot_cols(a):
    a4 = a.reshape(a.shape[:-1] + (2, 2, ROPE_AXIS // 2))
    return jnp.stack([-a4[..., 1, :], a4[..., 0, :]], axis=-2).reshape(a.shape)


def _pad_cols(a, before, total):
    pads = [(0, 0)] * (a.ndim - 1) + [(before, total - before - a.shape[-1])]
    return jnp.pad(a, pads)


def _prep_layer(l, w):
    D = D_MODEL
    wi, bi = w["w_in"][l], w["b_in"][l][None, :]
    n_sg, n_mla = 2 * SG_WIDTH, Q_LORA + KV_LORA + MLA_ROPE
    o_kr = n_sg + Q_LORA + KV_LORA
    o_ml = n_sg + n_mla
    o_g = o_ml + 4 * ML_WIDTH

    def cat(a):
        kr = a[:, o_kr:o_kr + MLA_ROPE]
        gates = a[:, o_g:o_g + 4 * ML_HEADS].reshape(-1, 2, 2, ML_HEADS).transpose(0, 2, 1, 3).reshape(-1, 4 * ML_HEADS)
        return jnp.concatenate([a[:, :o_kr], _pad_cols(kr, MLA_NOPE, HEAD_PAD), _pad_cols(_rot_cols(kr), MLA_NOPE, HEAD_PAD),
                                a[:, o_ml:o_g], _pad_cols(gates, 0, LANE)], axis=1)

    wq = w["w_uq"][l].reshape(Q_LORA, MLA_HEADS, MLA_NOPE + MLA_ROPE)
    wq_main = _pad_cols(wq, 0, HEAD_PAD).reshape(Q_LORA, -1)
    wq_rot = _pad_cols(_rot_cols(wq[..., MLA_NOPE:]), MLA_NOPE, HEAD_PAD).reshape(Q_LORA, -1)
    wkv = w["w_ukv"][l].reshape(KV_LORA, MLA_HEADS, MLA_NOPE + MLA_VDIM)
    wk = _pad_cols(wkv[..., :MLA_NOPE], 0, HEAD_PAD).reshape(KV_LORA, -1)
    wv = _pad_cols(wkv[..., MLA_NOPE:], 0, HEAD_PAD).reshape(KV_LORA, -1)
    head = jnp.arange(ML_WIDTH) // ML_DIM
    return {
        "wcat": cat(wi).astype(BF16), "bcat": cat(bi),
        "q_norm_g": w["q_norm_g"][l][None, :], "kv_norm_g": w["kv_norm_g"][l][None, :],
        "wq": jnp.concatenate([wq_main, wq_rot], axis=1).astype(BF16),
        "wkv": jnp.concatenate([wk, wv], axis=1).astype(BF16),
        "sg_ln_g": w["sg_ln_g"][l][None, :], "sg_ln_b": w["sg_ln_b"][l][None, :],
        "sg_w": w["sg_w"][l].astype(BF16),
        "sg_bias": jnp.repeat(w["sg_b"][l].T, SG_GDIM, axis=1),
        "conv_w": jnp.pad(w["ml_conv_w"][l], ((0, 8 - ML_CONV), (0, 0))), "conv_b": w["ml_conv_b"][l][None, :],
        "f_bias": _pad_cols(w["ml_f_bias"][l], 0, LANE),
        "ml_norm_g": w["ml_norm_g"][l][None, :],
        "head_avg": (head[:, None] == head[None, :]).astype(F32) / ML_DIM,
        "w_out": w["w_out"][l].astype(BF16),
        "ln1_g": w["ln1_g"][l][None, :], "ln1_b": w["ln1_b"][l][None, :],
        "w_router_t": w["w_router"][l].T,
        "w_gate": w["w_gate"][l].astype(BF16), "w_up": w["w_up"][l].astype(BF16), "w_down": w["w_down"][l].astype(BF16),
        "ln2_g": w["ln2_g"][l][None, :], "ln2_b": w["ln2_b"][l][None, :],
    }


def _rope_tables(T):
    rows = T // GRID_W
    row = jnp.repeat(jnp.arange(rows, dtype=F32), GRID_W)
    colv = jnp.tile(jnp.arange(GRID_W, dtype=F32), rows)
    inv = ROPE_BASE ** (-jnp.arange(ROPE_AXIS // 2, dtype=F32) * 2.0 / ROPE_AXIS)
    ang = jnp.concatenate([row[:, None] * inv, colv[:, None] * inv], axis=-1)
    half = ROPE_AXIS // 2
    spread = lambda t: jnp.concatenate([t[:, :half], t[:, :half], t[:, half:], t[:, half:]], axis=1)
    cosT = jnp.concatenate([jnp.ones((T, MLA_NOPE), F32), spread(jnp.cos(ang)), jnp.zeros((T, 32), F32)], axis=1)
    sinT = _pad_cols(spread(jnp.sin(ang)), MLA_NOPE, HEAD_PAD)
    return cosT, sinT


def _moe(x1, hm, aff_t, g2, p, tt):
    cap = EC_CAPACITY * x1.shape[1] // N_EXPERTS
    pos = _route(aff_t, cap)
    ye = _experts(pos, aff_t, hm, p, cap)
    return _combine(pos, ye, x1, g2, p, tt)


def _forward(x, c, ctx, c_ctx, w):
    B, T, D = x.shape
    Tc = ctx.shape[1]
    L = w["w_ada"].shape[0]
    cc = jnp.zeros((16, D), F32).at[:B].set(c).at[B].set(c_ctx)
    ada = _ada(cc, w["w_ada"], w["b_ada"])
    cos_l, sin_l = _rope_tables(T)
    cos_c = _pad_cols(jnp.ones((Tc, MLA_NOPE + MLA_ROPE), F32), 0, HEAD_PAD)
    sin_c = jnp.zeros((Tc, HEAD_PAD), F32)
    zero_c = jnp.zeros((B, 2, ML_HEADS * HEAD_PAD, HEAD_PAD), F32)
    zero_m = jnp.zeros((B, 2, 1, LANE), F32)
    tm = 256
    xl, xc = x, ctx
    for l in range(L):
        p = _prep_layer(l, w)
        mods_l = [ada[l, :B, i * D:(i + 1) * D][:, None, :] for i in range(6)]
        mods_c = [jnp.broadcast_to(ada[l, B, i * D:(i + 1) * D][None, None, :], (B, 1, D)) for i in range(6)]
        update_ctx = l < L - 1
        ya_c, q_c, k_c, v_c, zml_c, g_c = _inproj(xc, mods_c[0], mods_c[1], cos_c, sin_c, p, min(tm, Tc))
        ya_l, q_l, k_l, v_l, zml_l, g_l = _inproj(xl, mods_l[0], mods_l[1], cos_l, sin_l, p, tm)
        hd_c, cs_c, ms_c = _mlstm(zml_c, g_c, p, zero_c, zero_m)
        hd_l, _, _ = _mlstm(zml_l, g_l, p, cs_c, ms_c)
        yb_l = _attention(q_l, [(k_l, v_l), (k_c, v_c)], tq=min(512, T))
        x1_l, hm_l, aff_l = _outproj(xl, mods_l[2], mods_l[3], mods_l[4], ya_l, yb_l, hd_l, zml_l, p, tm)
        xl = _moe(x1_l, hm_l, aff_l, mods_l[5], p, tt=min(1024, T))
        if update_ctx:
            yb_c = _attention(q_c, [(k_c, v_c)], tq=min(256, Tc))
            x1_c, hm_c, aff_c = _outproj(xc, mods_c[2], mods_c[3], mods_c[4], ya_c, yb_c, hd_c, zml_c, p, min(tm, Tc))
            xc = _moe(x1_c, hm_c, aff_c, mods_c[5], p, tt=min(1024, Tc))
    return xl


def kernel(x, c, ctx, c_ctx, w_ada, b_ada, w_in, b_in, sg_ln_g, sg_ln_b, sg_w, sg_b, q_norm_g, kv_norm_g, w_uq, w_ukv,
           ml_conv_w, ml_conv_b, ml_f_bias, ml_norm_g, w_out, ln1_g, ln1_b, w_router, w_gate, w_up, w_down, ln2_g, ln2_b):
    w = dict(w_ada=w_ada, b_ada=b_ada, w_in=w_in, b_in=b_in, sg_ln_g=sg_ln_g, sg_ln_b=sg_ln_b, sg_w=sg_w, sg_b=sg_b,
             q_norm_g=q_norm_g, kv_norm_g=kv_norm_g, w_uq=w_uq, w_ukv=w_ukv, ml_conv_w=ml_conv_w, ml_conv_b=ml_conv_b,
             ml_f_bias=ml_f_bias, ml_norm_g=ml_norm_g, w_out=w_out, ln1_g=ln1_g, ln1_b=ln1_b, w_router=w_router,
             w_gate=w_gate, w_up=w_up, w_down=w_down, ln2_g=ln2_g, ln2_b=ln2_b)
    return _forward(x, c, ctx, c_ctx, w)
```

```python
import functools

import jax
import jax.numpy as jnp
from jax import lax
from jax.experimental import pallas as pl
from jax.experimental.pallas import tpu as pltpu

F32 = jnp.float32
BF16 = jnp.bfloat16
HIGHEST = lax.Precision.HIGHEST

D_MODEL = 1024
DEPTH = 2
GRID_W = 64
SG_WIDTH = 256
SG_GROUPS = 4
SG_GDIM = 64
SG_CHUNK = 128
MLA_HEADS = 8
MLA_NOPE = 64
MLA_ROPE = 32
MLA_VDIM = 64
Q_LORA = 384
KV_LORA = 256
ROPE_AXIS = 16
ROPE_BASE = 10000.0
MLA_SCALE = (MLA_NOPE + MLA_ROPE) ** -0.5
ML_HEADS = 4
ML_DIM = 64
ML_WIDTH = 256
ML_CONV = 5
N_EXPERTS = 16
EXPERT_FF = 1024
EC_CAPACITY = 2
ALPHA = (2 * DEPTH) ** 0.25
LN_EPS = 1e-6
LOG2E = 1.4426950408889634

LANE = 128
HEAD_PAD = 128
ML_CHUNK = 256
OFF_SG, OFF_CQ, OFF_CKV, OFF_KR, OFF_KRR, OFF_ML, OFF_G, N_CAT = 0, 512, 896, 1152, 1280, 1408, 2432, 2560
VMEM_LIMIT = 56 * 1024 * 1024
QK_LANES = ML_HEADS * HEAD_PAD + ML_WIDTH
ATTN_UNROLL = 8


def _cparams(sem):
    return pltpu.CompilerParams(dimension_semantics=sem, vmem_limit_bytes=VMEM_LIMIT)


def _nt(a, b):
    return lax.dot_general(a, b, (((1,), (1,)), ((), ())), preferred_element_type=F32)


def _tn(a, b):
    return lax.dot_general(a, b, (((0,), (0,)), ((), ())), preferred_element_type=F32)


def _dot(a, b, precision=None):
    return jnp.dot(a, b, preferred_element_type=F32, precision=precision)


def _split(x):
    hi = x.astype(BF16)
    return hi, (x - hi.astype(F32)).astype(BF16)


def _dot2(x, w):
    hi, lo = _split(x)
    return _dot(hi, w) + _dot(lo, w)


def _plain_norm(x):
    mu = jnp.mean(x, axis=-1, keepdims=True)
    xc = x - mu
    var = jnp.mean(xc * xc, axis=-1, keepdims=True)
    return xc * lax.rsqrt(var + LN_EPS)


def _rms(x):
    return x * lax.rsqrt(jnp.mean(x * x, axis=-1, keepdims=True) + LN_EPS)


def _silu(x):
    return x * (1.0 / (1.0 + jnp.exp(-x)))


def _gelu_tanh(x):
    return 0.5 * x * (1.0 + jnp.tanh(0.7978845608028654 * (x + 0.044715 * (x * x * x))))


def _log_sigmoid(x):
    return jnp.minimum(x, 0.0) - jnp.log(1.0 + jnp.exp(-jnp.abs(x)))


def _ada_kernel(c_ref, w_ref, b_ref, o_ref):
    o_ref[0] = _dot(_silu(c_ref[...]), w_ref[0], precision=HIGHEST) + b_ref[0]


def _ada(cc, w_ada, b_ada):
    L, D, N = w_ada.shape
    tn = 1536
    return pl.pallas_call(
        _ada_kernel,
        grid=(L, N // tn),
        in_specs=[pl.BlockSpec((16, D), lambda l, j: (0, 0)),
                  pl.BlockSpec((1, D, tn), lambda l, j: (l, 0, j)),
                  pl.BlockSpec((1, 1, tn), lambda l, j: (l, 0, j))],
        out_specs=pl.BlockSpec((1, 16, tn), lambda l, j: (l, 0, j)),
        out_shape=jax.ShapeDtypeStruct((L, 16, N), F32),
        compiler_params=_cparams(("parallel", "parallel")),
        name="ada",
    )(cc, w_ada, b_ada.reshape(L, 1, N))


def _inproj_kernel(x_ref, sh_ref, sc_ref, cos_ref, sin_ref, wcat_ref, bcat_ref, qg_ref, kvg_ref, wq_ref, wkv_ref,
                   sglg_ref, sglb_ref, sgw_ref, sgb_ref,
                   ya_ref, q_ref, k_ref, v_ref, zml_ref, g_ref):
    tm = x_ref.shape[1]
    h = _plain_norm(x_ref[0]) * (1.0 + sc_ref[0]) + sh_ref[0]
    z = _dot(h.astype(BF16), wcat_ref[...]) + bcat_ref[...]
    zs = _gelu_tanh(z[:, OFF_SG:OFF_SG + 2 * SG_WIDTH])
    u = zs[:, :SG_WIDTH]
    vn = _plain_norm(zs[:, SG_WIDTH:]) * sglg_ref[...] + sglb_ref[...]
    group = jnp.right_shift(lax.broadcasted_iota(jnp.int32, (SG_CHUNK, SG_WIDTH), 1), SG_GDIM.bit_length() - 1)
    for ch in range(tm // SG_CHUNK):
        rows = slice(ch * SG_CHUNK, (ch + 1) * SG_CHUNK)
        vc = vn[rows].astype(BF16)
        mixed = sgb_ref[...]
        for g in range(SG_GROUPS):
            mixed = mixed + jnp.where(group == g, _dot(sgw_ref[g], vc), 0.0)
        ya_ref[0, rows, :] = (u[rows] * mixed).astype(BF16)
    cosT = cos_ref[...]
    sinT = sin_ref[...]
    cqn = (_rms(z[:, OFF_CQ:OFF_CQ + Q_LORA]) * qg_ref[...]).astype(BF16)
    qq = _dot(cqn, wq_ref[...])
    ckvn = (_rms(z[:, OFF_CKV:OFF_CKV + KV_LORA]) * kvg_ref[...]).astype(BF16)
    kv = _dot(ckvn, wkv_ref[...])
    kr = z[:, OFF_KR:OFF_KR + HEAD_PAD] * cosT + z[:, OFF_KRR:OFF_KRR + HEAD_PAD] * sinT
    one_lane = jnp.where(lax.broadcasted_iota(jnp.int32, (1, HEAD_PAD), 1) == MLA_VDIM, 1.0, 0.0)
    hw = MLA_HEADS * HEAD_PAD
    for hd in range(MLA_HEADS):
        cols = slice(hd * HEAD_PAD, (hd + 1) * HEAD_PAD)
        rcols = slice(hw + hd * HEAD_PAD, hw + (hd + 1) * HEAD_PAD)
        q_ref[0, hd] = ((qq[:, cols] * cosT + qq[:, rcols] * sinT) * (MLA_SCALE * LOG2E)).astype(BF16)
        k_ref[0, hd] = (kv[:, cols] + kr).astype(BF16)
        v_ref[0, hd] = (kv[:, rcols] + one_lane).astype(BF16)
    zml_ref[0] = z[:, OFF_ML:OFF_ML + 4 * ML_WIDTH]
    g_ref[0] = z[:, OFF_G:OFF_G + LANE]


def _inproj(x, shift, scale, cosT, sinT, p, tm):
    B, T, D = x.shape
    const = lambda shape: pl.BlockSpec(shape, lambda b, i: (0,) * len(shape))
    heads = (B, MLA_HEADS, T, HEAD_PAD)
    return pl.pallas_call(
        _inproj_kernel,
        grid=(B, T // tm),
        in_specs=[pl.BlockSpec((1, tm, D), lambda b, i: (b, i, 0)),
                  pl.BlockSpec((1, 1, D), lambda b, i: (b, 0, 0)),
                  pl.BlockSpec((1, 1, D), lambda b, i: (b, 0, 0)),
                  pl.BlockSpec((tm, HEAD_PAD), lambda b, i: (i, 0)),
                  pl.BlockSpec((tm, HEAD_PAD), lambda b, i: (i, 0)),
                  const((D, N_CAT)), const((1, N_CAT)), const((1, Q_LORA)), const((1, KV_LORA)),
                  const((Q_LORA, 2 * MLA_HEADS * HEAD_PAD)), const((KV_LORA, 2 * MLA_HEADS * HEAD_PAD)),
                  const((1, SG_WIDTH)), const((1, SG_WIDTH)), const((SG_GROUPS, SG_CHUNK, SG_CHUNK)),
                  const((SG_CHUNK, SG_WIDTH))],
        out_specs=[pl.BlockSpec((1, tm, SG_WIDTH), lambda b, i: (b, i, 0)),
                   pl.BlockSpec((1, MLA_HEADS, tm, HEAD_PAD), lambda b, i: (b, 0, i, 0)),
                   pl.BlockSpec((1, MLA_HEADS, tm, HEAD_PAD), lambda b, i: (b, 0, i, 0)),
                   pl.BlockSpec((1, MLA_HEADS, tm, HEAD_PAD), lambda b, i: (b, 0, i, 0)),
                   pl.BlockSpec((1, tm, 4 * ML_WIDTH), lambda b, i: (b, i, 0)),
                   pl.BlockSpec((1, tm, LANE), lambda b, i: (b, i, 0))],
        out_shape=[jax.ShapeDtypeStruct((B, T, SG_WIDTH), BF16),
                   jax.ShapeDtypeStruct(heads, BF16), jax.ShapeDtypeStruct(heads, BF16),
                   jax.ShapeDtypeStruct(heads, BF16),
                   jax.ShapeDtypeStruct((B, T, 4 * ML_WIDTH), F32),
                   jax.ShapeDtypeStruct((B, T, LANE), F32)],
        compiler_params=_cparams(("parallel", "parallel")),
        name="inproj",
    )(x, shift, scale, cosT, sinT, p["wcat"], p["bcat"], p["q_norm_g"], p["kv_norm_g"], p["wq"], p["wkv"],
      p["sg_ln_g"], p["sg_ln_b"], p["sg_w"], p["sg_bias"])


def _attn_kernel(*refs, n_src, tk):
    q_ref = refs[0]
    kv_refs = refs[1:1 + 2 * n_src]
    o_ref = refs[1 + 2 * n_src]
    tq = q_ref.shape[2]
    qs = [q_ref[0, hh] for hh in range(2)]
    carry = (tuple(jnp.full((tq, 1), -jnp.inf, F32) for _ in range(2))
             + tuple(jnp.zeros((tq, HEAD_PAD), F32) for _ in range(2)))
    for s in range(n_src):
        k_ref, v_ref = kv_refs[2 * s], kv_refs[2 * s + 1]
        ck = min(tk, k_ref.shape[2])

        def body(c, carry, k_ref=k_ref, v_ref=v_ref, ck=ck):
            rows = pl.ds(pl.multiple_of(c * ck, ck), ck)
            ms, accs = [], []
            for hh in range(2):
                m, acc = carry[hh], carry[2 + hh]
                sc = _nt(qs[hh], k_ref[0, hh, rows, :])
                m_new = jnp.maximum(m, jnp.max(sc, axis=1, keepdims=True))
                pr = jnp.exp2(sc - m_new)
                accs.append(acc * jnp.exp2(m - m_new) + _dot(pr.astype(BF16), v_ref[0, hh, rows, :]))
                ms.append(m_new)
            return tuple(ms) + tuple(accs)

        n_chunks = k_ref.shape[2] // ck
        carry = lax.fori_loop(0, n_chunks, body, carry, unroll=min(ATTN_UNROLL, n_chunks))
    outs = [carry[2 + hh] / carry[2 + hh][:, MLA_VDIM:MLA_VDIM + 1] for hh in range(2)]
    lane = lax.broadcasted_iota(jnp.int32, (tq, HEAD_PAD), 1)
    o_ref[0] = jnp.where(lane < MLA_VDIM, outs[0], pltpu.roll(outs[1], MLA_VDIM, axis=1)).astype(BF16)


def _attention(q, kvs, tq, tk=512):
    B, H, T, _ = q.shape
    in_specs = [pl.BlockSpec((1, 2, tq, HEAD_PAD), lambda b, hp, i: (b, hp, i, 0))]
    args = [q]
    for k, v in kvs:
        K = k.shape[2]
        in_specs += [pl.BlockSpec((1, 2, K, HEAD_PAD), lambda b, hp, i: (b, hp, 0, 0))] * 2
        args += [k, v]
    return pl.pallas_call(
        functools.partial(_attn_kernel, n_src=len(kvs), tk=tk),
        grid=(B, H // 2, T // tq),
        in_specs=in_specs,
        out_specs=pl.BlockSpec((1, tq, HEAD_PAD), lambda b, hp, i: (b, i, hp)),
        out_shape=jax.ShapeDtypeStruct((B, T, H * MLA_VDIM), BF16),
        compiler_params=_cparams(("parallel", "parallel", "parallel")),
        name="attention",
    )(*args)


def _qkconv_kernel(zp_ref, zc_ref, zn_ref, cw_ref, cb_ref, o_ref):
    tc = zc_ref.shape[1]
    i = pl.program_id(1)
    prev = jnp.where(i > 0, zp_ref[0], 0.0)
    nxt = jnp.where(i < pl.num_programs(1) - 1, zn_ref[0], 0.0)
    big = jnp.concatenate([prev, zc_ref[0], nxt], axis=0)
    n_big = tc + 16
    pad = (ML_CONV - 1) // 2
    conv = cb_ref[...]
    for t in range(ML_CONV):
        s = t - pad
        shifted = big if s == 0 else pltpu.roll(big, (-s) % n_big, axis=0)
        conv = conv + shifted[8:8 + tc] * cw_ref[t:t + 1, :]
    qk = _silu(conv)
    lane = lax.broadcasted_iota(jnp.int32, (1, HEAD_PAD), 1)
    for hd in range(ML_HEADS):
        q_blk = qk[:, (hd // 2) * HEAD_PAD:(hd // 2 + 1) * HEAD_PAD]
        in_half = (lane >= ML_DIM) if hd % 2 else (lane < ML_DIM)
        o_ref[0, :, hd * HEAD_PAD:(hd + 1) * HEAD_PAD] = jnp.where(in_half, q_blk, 0.0).astype(BF16)
    o_ref[0, :, ML_HEADS * HEAD_PAD:] = (qk[:, ML_WIDTH:] * (ML_DIM ** -0.5)).astype(BF16)


def _qkconv(zml, p, tc):
    B, T, _ = zml.shape
    hb = tc // 8
    const = lambda shape: pl.BlockSpec(shape, lambda b, i: (0,) * len(shape))
    return pl.pallas_call(
        _qkconv_kernel,
        grid=(B, T // tc),
        in_specs=[pl.BlockSpec((1, 8, 2 * ML_WIDTH), lambda b, i: (b, jnp.maximum(i * hb - 1, 0), 0)),
                  pl.BlockSpec((1, tc, 2 * ML_WIDTH), lambda b, i: (b, i, 0)),
                  pl.BlockSpec((1, 8, 2 * ML_WIDTH), lambda b, i: (b, jnp.minimum((i + 1) * hb, T // 8 - 1), 0)),
                  const((8, 2 * ML_WIDTH)), const((1, 2 * ML_WIDTH))],
        out_specs=pl.BlockSpec((1, tc, QK_LANES), lambda b, i: (b, i, 0)),
        out_shape=jax.ShapeDtypeStruct((B, T, QK_LANES), BF16),
        compiler_params=_cparams(("parallel", "parallel")),
        name="qkconv",
    )(zml, zml, zml, p["conv_w"], p["conv_b"])


def _mlstm_direction(d, qk, vv, gl, fb, c_prev, m_prev):
    L = qk.shape[0]
    li = gl
    lf = _log_sigmoid(pltpu.roll(gl, LANE - 2 * ML_HEADS, axis=1) + fb)
    row = lax.broadcasted_iota(jnp.int32, (L, L), 0)
    col = lax.broadcasted_iota(jnp.int32, (L, L), 1)
    causal = (col >= row) if d else (row >= col)
    tri = jnp.where(causal, 1.0, 0.0).astype(BF16)
    lf_hi, lf_lo = _split(lf)
    bc = _dot(tri, lf_hi) + _dot(tri, lf_lo)
    b_end = bc[0:1, :] if d else bc[L - 1:L, :]
    gw = b_end - bc + li
    m_loc = jnp.max(gw, axis=0, keepdims=True)
    wt = jnp.exp(gw - m_loc)
    rt = (li - bc).T
    m_next = jnp.maximum(b_end + m_prev, m_loc)
    a_sc = jnp.exp(b_end + m_prev - m_next)
    s_sc = jnp.exp(m_loc - m_next)

    lane = lax.broadcasted_iota(jnp.int32, (L, HEAD_PAD), 1)
    lane_r = lax.broadcasted_iota(jnp.int32, (1, HEAD_PAD), 1)
    h_blocks, c_next = [], []
    for blk in range(ML_HEADS // 2):
        cols = slice(blk * HEAD_PAD, (blk + 1) * HEAD_PAD)
        km = qk[:, (ML_HEADS + blk) * HEAD_PAD:(ML_HEADS + blk + 1) * HEAD_PAD]
        v_blk = vv[:, cols]
        hh = []
        for half in range(2):
            hd = 2 * blk + half
            gln = d * ML_HEADS + hd
            in_half = (lane >= ML_DIM) if half else (lane < ML_DIM)
            den_lane = 0 if half else ML_DIM
            qm = qk[:, hd * HEAD_PAD:(hd + 1) * HEAD_PAD]
            vaug = jnp.where(in_half, v_blk, jnp.where(lane == den_lane, 1.0, 0.0))
            b_col = bc[:, gln:gln + 1]
            dm = jnp.where(causal, b_col + rt[gln:gln + 1, :], -jnp.inf)
            a_col = b_col + m_prev[:, gln:gln + 1]
            m_t = jnp.maximum(a_col, jnp.max(dm, axis=1, keepdims=True))
            pm = jnp.exp(dm - m_t) * _nt(qm, km)
            inter = jnp.exp(a_col - m_t)
            c0 = c_prev[hd]
            nd = inter * _nt(qm, c0.astype(BF16)) + _dot(pm.astype(BF16), vaug.astype(BF16))
            den = nd[:, den_lane:den_lane + 1]
            hh.append(nd / jnp.maximum(jnp.abs(den), jnp.exp(-m_t)))
            wv = (wt[:, gln:gln + 1] * vaug).astype(BF16)
            c_loc = _tn(wv, km)
            c_next.append(a_sc[:, gln:gln + 1] * c0 + s_sc[:, gln:gln + 1] * c_loc)
        h_blocks.append(jnp.where(lane < ML_DIM, hh[0], hh[1]))
    mine = (lane_r >= d * ML_HEADS) & (lane_r < (d + 1) * ML_HEADS)
    return h_blocks, c_next, jnp.where(mine, m_next, m_prev)


def _mlstm_kernel(qkf_ref, vf_ref, gf_ref, qkb_ref, vb_ref, gb_ref, fb_ref, ci_ref, mi_ref,
                  hf_ref, hb_ref, cs_ref, ms_ref):
    @pl.when(pl.program_id(1) == 0)
    def _():
        cs_ref[...] = ci_ref[...]
        ms_ref[...] = mi_ref[...]

    rows = lambda hd: slice(hd * HEAD_PAD, (hd + 1) * HEAD_PAD)
    ins = ((qkf_ref, vf_ref, gf_ref), (qkb_ref, vb_ref, gb_ref))
    states = [([cs_ref[0, d, rows(hd), :] for hd in range(ML_HEADS)], ms_ref[0, d]) for d in range(2)]
    results = [_mlstm_direction(d, ins[d][0][0], ins[d][1][0], ins[d][2][0], fb_ref[...], *states[d])
               for d in range(2)]
    for d, h_ref in enumerate((hf_ref, hb_ref)):
        h_blocks, c_next, m_next = results[d]
        for blk, hb in enumerate(h_blocks):
            h_ref[0, :, blk * HEAD_PAD:(blk + 1) * HEAD_PAD] = hb
        for hd in range(ML_HEADS):
            cs_ref[0, d, rows(hd), :] = c_next[hd]
        ms_ref[0, d] = m_next


def _mlstm(qk, zml, gates, p, c_init, m_init):
    B, T, _ = zml.shape
    L = min(ML_CHUNK, T)
    nc = T // L
    const = lambda shape: pl.BlockSpec(shape, lambda b, j: (0,) * len(shape))
    state = pl.BlockSpec((1, 2, ML_HEADS * HEAD_PAD, HEAD_PAD), lambda b, j: (b, 0, 0, 0))
    mstate = pl.BlockSpec((1, 2, 1, LANE), lambda b, j: (b, 0, 0, 0))
    fwd = lambda b, j: (b, j, 0)
    bwd = lambda b, j: (b, nc - 1 - j, 0)
    return pl.pallas_call(
        _mlstm_kernel,
        grid=(B, nc),
        in_specs=[pl.BlockSpec((1, L, QK_LANES), fwd),
                  pl.BlockSpec((1, L, ML_WIDTH), lambda b, j: (b, j, 2)),
                  pl.BlockSpec((1, L, LANE), fwd),
                  pl.BlockSpec((1, L, QK_LANES), bwd),
                  pl.BlockSpec((1, L, ML_WIDTH), lambda b, j: (b, nc - 1 - j, 2)),
                  pl.BlockSpec((1, L, LANE), bwd),
                  const((1, LANE)), state, mstate],
        out_specs=[pl.BlockSpec((1, L, ML_WIDTH), fwd), pl.BlockSpec((1, L, ML_WIDTH), bwd), state, mstate],
        out_shape=[jax.ShapeDtypeStruct((B, T, ML_WIDTH), F32), jax.ShapeDtypeStruct((B, T, ML_WIDTH), F32),
                   jax.ShapeDtypeStruct((B, 2, ML_HEADS * HEAD_PAD, HEAD_PAD), F32),
                   jax.ShapeDtypeStruct((B, 2, 1, LANE), F32)],
        compiler_params=_cparams(("parallel", "arbitrary")),
        name="mlstm",
    )(qk, zml, gates, qk, zml, gates, p["f_bias"], c_init, m_init)


def _outproj_kernel(x_ref, g1_ref, sh2_ref, sc2_ref, ya_ref, yb_ref, h0_ref, h1_ref, o_ref, mg_ref, avg_ref,
                    wo_ref, l1g_ref, l1b_ref, wrh_ref, wrl_ref, x1_ref, hm_ref, aff_ref):
    hs = h0_ref[0] + h1_ref[0]
    mu = _dot2(hs, avg_ref[...])
    hc = hs - mu
    var = _dot2(hc * hc, avg_ref[...])
    yc =(1.0 / (1.0 + jnp.exp(-o_ref[0]))) * (hc * lax.rsqrt(var + LN_EPS) * mg_ref[...])
    y = (_dot(ya_ref[0], wo_ref[0:SG_WIDTH, :])
         + _dot(yb_ref[0], wo_ref[SG_WIDTH:SG_WIDTH + MLA_HEADS * MLA_VDIM, :])
         + _dot(yc.astype(BF16), wo_ref[SG_WIDTH + MLA_HEADS * MLA_VDIM:, :]))
    x1 = _plain_norm(ALPHA * x_ref[0] + g1_ref[0] * y) * l1g_ref[...] + l1b_ref[...]
    x1_ref[0] = x1
    hm = _plain_norm(x1) * (1.0 + sc2_ref[0]) + sh2_ref[0]
    hm_ref[0] = hm.astype(BF16)
    hi, lo = _split(hm)
    logits = _dot(hi, wrh_ref[...]) + (_dot(lo, wrh_ref[...]) + _dot(hi, wrl_ref[...]))
    logits = logits.T[:N_EXPERTS, :]
    ex =jnp.exp(logits - jnp.max(logits, axis=0, keepdims=True))
    aff_ref[0] = ex / jnp.sum(ex, axis=0, keepdims=True)


def _outproj(x, g1, sh2, sc2, ya, yb, h_fwd, h_bwd, zml, p, tm):
    B, T, D = x.shape
    const = lambda shape: pl.BlockSpec(shape, lambda b, i: (0,) * len(shape))
    vec = pl.BlockSpec((1, 1, D), lambda b, i: (b, 0, 0))
    return pl.pallas_call(
        _outproj_kernel,
        grid=(B, T // tm),
        in_specs=[pl.BlockSpec((1, tm, D), lambda b, i: (b, i, 0)), vec, vec, vec,
                  pl.BlockSpec((1, tm, SG_WIDTH), lambda b, i: (b, i, 0)),
                  pl.BlockSpec((1, tm, MLA_HEADS * MLA_VDIM), lambda b, i: (b, i, 0)),
                  pl.BlockSpec((1, tm, ML_WIDTH), lambda b, i: (b, i, 0)),
                  pl.BlockSpec((1, tm, ML_WIDTH), lambda b, i: (b, i, 0)),
                  pl.BlockSpec((1, tm, ML_WIDTH), lambda b, i: (b, i, 3)),
                  const((1, ML_WIDTH)), const((ML_WIDTH, ML_WIDTH)), const((D, D)), const((1, D)), const((1, D)),
                  const((D, LANE)), const((D, LANE))],
        out_specs=[pl.BlockSpec((1, tm, D), lambda b, i: (b, i, 0)),
                   pl.BlockSpec((1, tm, D), lambda b, i: (b, i, 0)),
                   pl.BlockSpec((1, N_EXPERTS, tm), lambda b, i: (b, 0, i))],
        out_shape=[jax.ShapeDtypeStruct((B, T, D), F32), jax.ShapeDtypeStruct((B, T, D), BF16),
                   jax.ShapeDtypeStruct((B, N_EXPERTS, T), F32)],
        compiler_params=_cparams(("parallel", "parallel")),
        name="outproj",
    )(x, g1, sh2, sc2, ya, yb, h_fwd, h_bwd, zml, p["ml_norm_g"], p["head_avg"], p["w_out"], p["ln1_g"], p["ln1_b"],
      p["w_router_hi"], p["w_router_lo"])


def _route_kernel(aff_ref, pos_ref, *, cap, tw):
    aff = aff_ref[0]
    E, T = aff.shape
    bits = pltpu.bitcast(aff, jnp.int32)
    count = lambda m: jnp.sum(jnp.where(m, 1.0, 0.0), axis=1, keepdims=True)
    thr = jnp.zeros((E, 1), jnp.int32)
    for bit in range(30, -1, -1):
        cand = thr | (1 << bit)
        thr = jnp.where(count(bits >= cand) >= cap, cand, thr)
    above = bits > thr
    tied = bits == thr
    need = cap - count(above)
    idx = lax.broadcasted_iota(jnp.int32, (E, T), 1)
    cut = jnp.zeros((E, 1), jnp.int32)
    for bit in range(T.bit_length() - 1, -1, -1):
        cand = cut | (1 << bit)
        ok = (cand <= T) & (count(tied & (idx < cand)) <= need)
        cut = jnp.where(ok, cand, cut)
    sel = above | (tied & (idx < cut))
    r = lax.broadcasted_iota(jnp.int32, (tw, tw), 0)
    cidx = lax.broadcasted_iota(jnp.int32, (tw, tw), 1)
    before = jnp.where(r < cidx, 1.0, 0.0).astype(BF16)
    offset = jnp.zeros((E, 1), F32)
    for blk in range(T // tw):
        cols = slice(blk * tw, (blk + 1) * tw)
        sb = jnp.where(sel[:, cols], 1.0, 0.0)
        slot = _dot(sb.astype(BF16), before) + offset
        pos_ref[0, :, cols] = jnp.where(sel[:, cols], slot, -1.0).astype(jnp.int32)
        offset = offset + jnp.sum(sb, axis=1, keepdims=True)


def _route(aff_t, cap):
    B, E, T = aff_t.shape
    return pl.pallas_call(
        functools.partial(_route_kernel, cap=cap, tw=min(512, T)),
        grid=(B,),
        in_specs=[pl.BlockSpec((1, E, T), lambda b: (b, 0, 0))],
        out_specs=pl.BlockSpec((1, E, T), lambda b: (b, 0, 0)),
        out_shape=jax.ShapeDtypeStruct((B, E, T), jnp.int32),
        compiler_params=_cparams(("parallel",)),
        name="route",
    )(aff_t)


def _expert_kernel(pos_ref, aff_ref, h_ref, wg_ref, wu_ref, wd_ref, y_ref, *, tw):
    cap = y_ref.shape[2]
    T = h_ref.shape[1]
    slot = lax.broadcasted_iota(jnp.int32, (cap, tw), 0)
    xe = jnp.zeros((cap, h_ref.shape[2]), F32)
    gate = jnp.zeros((cap, 1), F32)
    for blk in range(T // tw):
        cols = slice(blk * tw, (blk + 1) * tw)
        hit = pos_ref[0, 0, :, cols] == slot
        xe = xe + _dot(jnp.where(hit, 1.0, 0.0).astype(BF16), h_ref[0, cols, :])
        gate = gate + jnp.sum(jnp.where(hit, aff_ref[0, 0, :, cols], 0.0), axis=1, keepdims=True)
    xb = xe.astype(BF16)
    hid = _silu(_dot(xb, wg_ref[0])) * _dot(xb, wu_ref[0])
    y_ref[0, 0] = (_dot(hid.astype(BF16), wd_ref[0]) * gate).astype(BF16)


def _experts(pos, aff_t, hm, p, cap):
    B, T, D = hm.shape
    E = N_EXPERTS
    expert_major = T * D < 3 * D * EXPERT_FF
    be = (lambda i, j: (j, i)) if expert_major else (lambda i, j: (i, j))
    row = pl.BlockSpec((1, 1, 1, T), lambda i, j: be(i, j) + (0, 0))
    wspec = lambda shape: pl.BlockSpec(shape, lambda i, j: (be(i, j)[1], 0, 0))
    return pl.pallas_call(
        functools.partial(_expert_kernel, tw=min(512, T)),
        grid=(E, B) if expert_major else (B, E),
        in_specs=[row, row,
                  pl.BlockSpec((1, T, D), lambda i, j: (be(i, j)[0], 0, 0)),
                  wspec((1, D, EXPERT_FF)), wspec((1, D, EXPERT_FF)), wspec((1, EXPERT_FF, D))],
        out_specs=pl.BlockSpec((1, 1, cap, D), lambda i, j: be(i, j) + (0, 0)),
        out_shape=jax.ShapeDtypeStruct((B, E, cap, D), BF16),
        compiler_params=_cparams(("parallel", "arbitrary")),
        name="experts",
    )(pos.reshape(B, E, 1, T), aff_t.reshape(B, E, 1, T), hm, p["w_gate"], p["w_up"], p["w_down"])


def _combine_kernel(pos_ref, y_ref, x_ref, g2_ref, lg_ref, lb_ref, o_ref, acc_ref):
    e = pl.program_id(2)
    cap = y_ref.shape[2]
    tt = x_ref.shape[1]

    @pl.when(e == 0)
    def _():
        acc_ref[...] = jnp.zeros_like(acc_ref)

    slot = lax.broadcasted_iota(jnp.int32, (cap, tt), 0)
    onehot = jnp.where(pos_ref[0, 0] == slot, 1.0, 0.0).astype(BF16)
    acc_ref[...] += _tn(onehot, y_ref[0, 0])

    @pl.when(e == pl.num_programs(2) - 1)
    def _():
        o_ref[0] = _plain_norm(ALPHA * x_ref[0] + g2_ref[0] * acc_ref[...]) * lg_ref[...] + lb_ref[...]


def _combine(pos, ye, x1, g2, p, tt):
    B, T, D = x1.shape
    E, cap = ye.shape[1], ye.shape[2]
    const = lambda shape: pl.BlockSpec(shape, lambda b, i, e: (0,) * len(shape))
    return pl.pallas_call(
        _combine_kernel,
        grid=(B, T // tt, E),
        in_specs=[pl.BlockSpec((1, 1, 1, tt), lambda b, i, e: (b, e, 0, i)),
                  pl.BlockSpec((1, 1, cap, D), lambda b, i, e: (b, e, 0, 0)),
                  pl.BlockSpec((1, tt, D), lambda b, i, e: (b, i, 0)),
                  pl.BlockSpec((1, 1, D), lambda b, i, e: (b, 0, 0)),
                  const((1, D)), const((1, D))],
        out_specs=pl.BlockSpec((1, tt, D), lambda b, i, e: (b, i, 0)),
        out_shape=jax.ShapeDtypeStruct((B, T, D), F32),
        scratch_shapes=[pltpu.VMEM((tt, D), F32)],
        compiler_params=_cparams(("parallel", "parallel", "arbitrary")),
        name="combine",
    )(pos.reshape(B, E, 1, T), ye, x1, g2, p["ln2_g"], p["ln2_b"])


def _rot_cols(a):
    a4 = a.reshape(a.shape[:-1] + (2, 2, ROPE_AXIS // 2))
    return jnp.stack([-a4[..., 1, :], a4[..., 0, :]], axis=-2).reshape(a.shape)


def _pad_cols(a, before, total):
    pads = [(0, 0)] * (a.ndim - 1) + [(before, total - before - a.shape[-1])]
    return jnp.pad(a, pads)


def _prep_layer(l, w):
    D = D_MODEL
    wi, bi = w["w_in"][l], w["b_in"][l][None, :]
    n_sg, n_mla = 2 * SG_WIDTH, Q_LORA + KV_LORA + MLA_ROPE
    o_kr = n_sg + Q_LORA + KV_LORA
    o_ml = n_sg + n_mla
    o_g = o_ml + 4 * ML_WIDTH

    def cat(a):
        kr = a[:, o_kr:o_kr + MLA_ROPE]
        gates = a[:, o_g:o_g + 4 * ML_HEADS].reshape(-1, 2, 2, ML_HEADS).transpose(0, 2, 1, 3).reshape(-1, 4 * ML_HEADS)
        return jnp.concatenate([a[:, :o_kr], _pad_cols(kr, MLA_NOPE, HEAD_PAD), _pad_cols(_rot_cols(kr), MLA_NOPE, HEAD_PAD),
                                a[:, o_ml:o_g], _pad_cols(gates, 0, LANE)], axis=1)

    wq = w["w_uq"][l].reshape(Q_LORA, MLA_HEADS, MLA_NOPE + MLA_ROPE)
    wq_main = _pad_cols(wq, 0, HEAD_PAD).reshape(Q_LORA, -1)
    wq_rot = _pad_cols(_rot_cols(wq[..., MLA_NOPE:]), MLA_NOPE, HEAD_PAD).reshape(Q_LORA, -1)
    wkv = w["w_ukv"][l].reshape(KV_LORA, MLA_HEADS, MLA_NOPE + MLA_VDIM)
    wk = _pad_cols(wkv[..., :MLA_NOPE], 0, HEAD_PAD).reshape(KV_LORA, -1)
    wv = _pad_cols(wkv[..., MLA_NOPE:], 0, HEAD_PAD).reshape(KV_LORA, -1)
    head = jnp.arange(ML_WIDTH) // ML_DIM
    wr = _pad_cols(w["w_router"][l], 0, LANE)
    wr_hi = wr.astype(BF16)
    return {
        "wcat": cat(wi).astype(BF16), "bcat": cat(bi),
        "q_norm_g": w["q_norm_g"][l][None, :], "kv_norm_g": w["kv_norm_g"][l][None, :],
        "wq": jnp.concatenate([wq_main, wq_rot], axis=1).astype(BF16),
        "wkv": jnp.concatenate([wk, wv], axis=1).astype(BF16),
        "sg_ln_g": w["sg_ln_g"][l][None, :], "sg_ln_b": w["sg_ln_b"][l][None, :],
        "sg_w": w["sg_w"][l].astype(BF16),
        "sg_bias": jnp.repeat(w["sg_b"][l].T, SG_GDIM, axis=1),
        "conv_w": jnp.pad(w["ml_conv_w"][l], ((0, 8 - ML_CONV), (0, 0))), "conv_b": w["ml_conv_b"][l][None, :],
        "f_bias": _pad_cols(w["ml_f_bias"][l].reshape(1, 2 * ML_HEADS), 0, LANE),
        "ml_norm_g": w["ml_norm_g"][l][None, :],
        "head_avg": ((head[:, None] == head[None, :]).astype(F32) / ML_DIM).astype(BF16),
        "w_out": w["w_out"][l].astype(BF16),
        "ln1_g": w["ln1_g"][l][None, :], "ln1_b": w["ln1_b"][l][None, :],
        "w_router_hi": wr_hi, "w_router_lo": (wr - wr_hi.astype(F32)).astype(BF16),
        "w_gate": w["w_gate"][l].astype(BF16), "w_up": w["w_up"][l].astype(BF16), "w_down": w["w_down"][l].astype(BF16),
        "ln2_g": w["ln2_g"][l][None, :], "ln2_b": w["ln2_b"][l][None, :],
    }


def _rope_tables(T):
    rows = T // GRID_W
    row = jnp.repeat(jnp.arange(rows, dtype=F32), GRID_W)
    colv = jnp.tile(jnp.arange(GRID_W, dtype=F32), rows)
    inv = ROPE_BASE ** (-jnp.arange(ROPE_AXIS // 2, dtype=F32) * 2.0 / ROPE_AXIS)
    ang = jnp.concatenate([row[:, None] * inv, colv[:, None] * inv], axis=-1)
    half = ROPE_AXIS // 2
    spread = lambda t: jnp.concatenate([t[:, :half], t[:, :half], t[:, half:], t[:, half:]], axis=1)
    cosT = jnp.concatenate([jnp.ones((T, MLA_NOPE), F32), spread(jnp.cos(ang)), jnp.zeros((T, 32), F32)], axis=1)
    sinT = _pad_cols(spread(jnp.sin(ang)), MLA_NOPE, HEAD_PAD)
    return cosT, sinT


def _moe(x1, hm, aff_t, g2, p, tt):
    cap = EC_CAPACITY * x1.shape[1] // N_EXPERTS
    pos = _route(aff_t, cap)
    ye = _experts(pos, aff_t, hm, p, cap)
    return _combine(pos, ye, x1, g2, p, tt)


def _forward(x, c, ctx, c_ctx, w):
    B, T, D = x.shape
    Tc = ctx.shape[1]
    L = w["w_ada"].shape[0]
    cc = jnp.zeros((16, D), F32).at[:B].set(c).at[B].set(c_ctx)
    ada = _ada(cc, w["w_ada"], w["b_ada"])
    cos_l, sin_l = _rope_tables(T)
    cos_c = _pad_cols(jnp.ones((Tc, MLA_NOPE + MLA_ROPE), F32), 0, HEAD_PAD)
    sin_c = jnp.zeros((Tc, HEAD_PAD), F32)
    zero_c = jnp.zeros((B, 2, ML_HEADS * HEAD_PAD, HEAD_PAD), F32)
    zero_m = jnp.zeros((B, 2, 1, LANE), F32)
    tm = min(512, T)
    xl, xc = x, ctx
    for l in range(L):
        p = _prep_layer(l, w)
        mods_l = [ada[l, :B, i * D:(i + 1) * D][:, None, :] for i in range(6)]
        mods_c = [jnp.broadcast_to(ada[l, B, i * D:(i + 1) * D][None, None, :], (B, 1, D)) for i in range(6)]
        update_ctx = l < L - 1
        ya_c, q_c, k_c, v_c, zml_c, g_c = _inproj(xc, mods_c[0], mods_c[1], cos_c, sin_c, p, min(tm, Tc))
        ya_l, q_l, k_l, v_l, zml_l, g_l = _inproj(xl, mods_l[0], mods_l[1], cos_l, sin_l, p, tm)
        hf_c, hb_c, cs_c, ms_c = _mlstm(_qkconv(zml_c, p, min(512, Tc)), zml_c, g_c, p, zero_c, zero_m)
        hf_l, hb_l, _, _ = _mlstm(_qkconv(zml_l, p, min(512, T)), zml_l, g_l, p, cs_c, ms_c)
        yb_l = _attention(q_l, [(k_l, v_l), (k_c, v_c)], tq=min(512, T))
        x1_l, hm_l, aff_l = _outproj(xl, mods_l[2], mods_l[3], mods_l[4], ya_l, yb_l, hf_l, hb_l, zml_l, p, tm)
        xl = _moe(x1_l, hm_l, aff_l, mods_l[5], p, tt=min(1024, T))
        if update_ctx:
            yb_c = _attention(q_c, [(k_c, v_c)], tq=min(256, Tc))
            x1_c, hm_c, aff_c = _outproj(xc, mods_c[2], mods_c[3], mods_c[4], ya_c, yb_c, hf_c, hb_c, zml_c, p,
                                         min(tm, Tc))
            xc = _moe(x1_c, hm_c, aff_c, mods_c[5], p, tt=min(1024, Tc))
    return xl


def kernel(x, c, ctx, c_ctx, w_ada, b_ada, w_in, b_in, sg_ln_g, sg_ln_b, sg_w, sg_b, q_norm_g, kv_norm_g, w_uq, w_ukv,
           ml_conv_w, ml_conv_b, ml_f_bias, ml_norm_g, w_out, ln1_g, ln1_b, w_router, w_gate, w_up, w_down, ln2_g, ln2_b):
    w = dict(w_ada=w_ada, b_ada=b_ada, w_in=w_in, b_in=b_in, sg_ln_g=sg_ln_g, sg_ln_b=sg_ln_b, sg_w=sg_w, sg_b=sg_b,
             q_norm_g=q_norm_g, kv_norm_g=kv_norm_g, w_uq=w_uq, w_ukv=w_ukv, ml_conv_w=ml_conv_w, ml_conv_b=ml_conv_b,
             ml_f_bias=ml_f_bias, ml_norm_g=ml_norm_g, w_out=w_out, ln1_g=ln1_g, ln1_b=ln1_b, w_router=w_router,
             w_gate=w_gate, w_up=w_up, w_down=w_down, ln2_g=ln2_g, ln2_b=ln2_b)
    return _forward(x, c, ctx, c_ctx, w)
```

```python
import functools

import jax
import jax.numpy as jnp
from jax import lax
from jax.experimental import pallas as pl
from jax.experimental.pallas import tpu as pltpu

F32 = jnp.float32
BF16 = jnp.bfloat16
HIGHEST = lax.Precision.HIGHEST

D_MODEL = 1024
DEPTH = 2
GRID_W = 64
SG_WIDTH = 256
SG_GROUPS = 4
SG_GDIM = 64
SG_CHUNK = 128
MLA_HEADS = 8
MLA_NOPE = 64
MLA_ROPE = 32
MLA_VDIM = 64
Q_LORA = 384
KV_LORA = 256
ROPE_AXIS = 16
ROPE_BASE = 10000.0
MLA_SCALE = (MLA_NOPE + MLA_ROPE) ** -0.5
ML_HEADS = 4
ML_DIM = 64
ML_WIDTH = 256
ML_CONV = 5
N_EXPERTS = 16
EXPERT_FF = 1024
EC_CAPACITY = 2
ALPHA = (2 * DEPTH) ** 0.25
LN_EPS = 1e-6
LOG2E = 1.4426950408889634

LANE = 128
HEAD_PAD = 128
ML_CHUNK = 256
OFF_SG, OFF_CQ, OFF_CKV, OFF_KR, OFF_KRR, OFF_ML, OFF_G, N_CAT = 0, 512, 896, 1152, 1280, 1408, 2432, 2560
VMEM_LIMIT = 56 * 1024 * 1024
ROUTE_TILE = 256
GATHER_ROWS = 128
GATHER_TILES = 6
COMBINE_WIN = 128
QK_LANES =ML_HEADS * HEAD_PAD + ML_WIDTH
ATTN_UNROLL = 8


def _cparams(sem):
    return pltpu.CompilerParams(dimension_semantics=sem, vmem_limit_bytes=VMEM_LIMIT)


def _nt(a, b):
    return lax.dot_general(a, b, (((1,), (1,)), ((), ())), preferred_element_type=F32)


def _tn(a, b):
    return lax.dot_general(a, b, (((0,), (0,)), ((), ())), preferred_element_type=F32)


def _dot(a, b, precision=None):
    return jnp.dot(a, b, preferred_element_type=F32, precision=precision)


def _split(x):
    hi = x.astype(BF16)
    return hi, (x - hi.astype(F32)).astype(BF16)


def _dot2(x, w):
    hi, lo = _split(x)
    return _dot(hi, w) + _dot(lo, w)


def _plain_norm(x):
    mu = jnp.mean(x, axis=-1, keepdims=True)
    xc = x - mu
    var = jnp.mean(xc * xc, axis=-1, keepdims=True)
    return xc * lax.rsqrt(var + LN_EPS)


def _rms(x):
    return x * lax.rsqrt(jnp.mean(x * x, axis=-1, keepdims=True) + LN_EPS)


def _silu(x):
    return x * (1.0 / (1.0 + jnp.exp(-x)))


def _gelu_tanh(x):
    return 0.5 * x * (1.0 + jnp.tanh(0.7978845608028654 * (x + 0.044715 * (x * x * x))))


def _log_sigmoid(x):
    return jnp.minimum(x, 0.0) - jnp.log(1.0 + jnp.exp(-jnp.abs(x)))


def _ada_kernel(c_ref, w_ref, b_ref, o_ref):
    o_ref[0] = _dot(_silu(c_ref[...]), w_ref[0], precision=HIGHEST) + b_ref[0]


def _ada(cc, w_ada, b_ada):
    L, D, N = w_ada.shape
    tn = 1536
    return pl.pallas_call(
        _ada_kernel,
        grid=(L, N // tn),
        in_specs=[pl.BlockSpec((16, D), lambda l, j: (0, 0)),
                  pl.BlockSpec((1, D, tn), lambda l, j: (l, 0, j)),
                  pl.BlockSpec((1, 1, tn), lambda l, j: (l, 0, j))],
        out_specs=pl.BlockSpec((1, 16, tn), lambda l, j: (l, 0, j)),
        out_shape=jax.ShapeDtypeStruct((L, 16, N), F32),
        compiler_params=_cparams(("parallel", "parallel")),
        name="ada",
    )(cc, w_ada, b_ada.reshape(L, 1, N))


def _inproj_kernel(x_ref, sh_ref, sc_ref, cos_ref, sin_ref, wcat_ref, bcat_ref, qg_ref, kvg_ref, wq_ref, wkv_ref,
                   sglg_ref, sglb_ref, sgw_ref, sgb_ref,
                   ya_ref, q_ref, k_ref, v_ref, zml_ref, g_ref):
    tm = x_ref.shape[1]
    h = _plain_norm(x_ref[0]) * (1.0 + sc_ref[0]) + sh_ref[0]
    z = _dot(h.astype(BF16), wcat_ref[...]) + bcat_ref[...]
    zs = _gelu_tanh(z[:, OFF_SG:OFF_SG + 2 * SG_WIDTH])
    u = zs[:, :SG_WIDTH]
    vn = _plain_norm(zs[:, SG_WIDTH:]) * sglg_ref[...] + sglb_ref[...]
    group = jnp.right_shift(lax.broadcasted_iota(jnp.int32, (SG_CHUNK, SG_WIDTH), 1), SG_GDIM.bit_length() - 1)
    for ch in range(tm // SG_CHUNK):
        rows = slice(ch * SG_CHUNK, (ch + 1) * SG_CHUNK)
        vc = vn[rows].astype(BF16)
        mixed = sgb_ref[...]
        for g in range(SG_GROUPS):
            mixed = mixed + jnp.where(group == g, _dot(sgw_ref[g], vc), 0.0)
        ya_ref[0, rows, :] = (u[rows] * mixed).astype(BF16)
    cosT = cos_ref[...]
    sinT = sin_ref[...]
    cqn = (_rms(z[:, OFF_CQ:OFF_CQ + Q_LORA]) * qg_ref[...]).astype(BF16)
    qq = _dot(cqn, wq_ref[...])
    ckvn = (_rms(z[:, OFF_CKV:OFF_CKV + KV_LORA]) * kvg_ref[...]).astype(BF16)
    kv = _dot(ckvn, wkv_ref[...])
    kr = z[:, OFF_KR:OFF_KR + HEAD_PAD] * cosT + z[:, OFF_KRR:OFF_KRR + HEAD_PAD] * sinT
    one_lane = jnp.where(lax.broadcasted_iota(jnp.int32, (1, HEAD_PAD), 1) == MLA_VDIM, 1.0, 0.0)
    hw = MLA_HEADS * HEAD_PAD
    for hd in range(MLA_HEADS):
        cols = slice(hd * HEAD_PAD, (hd + 1) * HEAD_PAD)
        rcols = slice(hw + hd * HEAD_PAD, hw + (hd + 1) * HEAD_PAD)
        q_ref[0, hd] = ((qq[:, cols] * cosT + qq[:, rcols] * sinT) * (MLA_SCALE * LOG2E)).astype(BF16)
        k_ref[0, hd] = (kv[:, cols] + kr).astype(BF16)
        v_ref[0, hd] = (kv[:, rcols] + one_lane).astype(BF16)
    zml_ref[0] = z[:, OFF_ML:OFF_ML + 4 * ML_WIDTH]
    g_ref[0] = z[:, OFF_G:OFF_G + LANE]


def _inproj(x, shift, scale, cosT, sinT, p, tm):
    B, T, D = x.shape
    const = lambda shape: pl.BlockSpec(shape, lambda b, i: (0,) * len(shape))
    heads = (B, MLA_HEADS, T, HEAD_PAD)
    return pl.pallas_call(
        _inproj_kernel,
        grid=(B, T // tm),
        in_specs=[pl.BlockSpec((1, tm, D), lambda b, i: (b, i, 0)),
                  pl.BlockSpec((1, 1, D), lambda b, i: (b, 0, 0)),
                  pl.BlockSpec((1, 1, D), lambda b, i: (b, 0, 0)),
                  pl.BlockSpec((tm, HEAD_PAD), lambda b, i: (i, 0)),
                  pl.BlockSpec((tm, HEAD_PAD), lambda b, i: (i, 0)),
                  const((D, N_CAT)), const((1, N_CAT)), const((1, Q_LORA)), const((1, KV_LORA)),
                  const((Q_LORA, 2 * MLA_HEADS * HEAD_PAD)), const((KV_LORA, 2 * MLA_HEADS * HEAD_PAD)),
                  const((1, SG_WIDTH)), const((1, SG_WIDTH)), const((SG_GROUPS, SG_CHUNK, SG_CHUNK)),
                  const((SG_CHUNK, SG_WIDTH))],
        out_specs=[pl.BlockSpec((1, tm, SG_WIDTH), lambda b, i: (b, i, 0)),
                   pl.BlockSpec((1, MLA_HEADS, tm, HEAD_PAD), lambda b, i: (b, 0, i, 0)),
                   pl.BlockSpec((1, MLA_HEADS, tm, HEAD_PAD), lambda b, i: (b, 0, i, 0)),
                   pl.BlockSpec((1, MLA_HEADS, tm, HEAD_PAD), lambda b, i: (b, 0, i, 0)),
                   pl.BlockSpec((1, tm, 4 * ML_WIDTH), lambda b, i: (b, i, 0)),
                   pl.BlockSpec((1, tm, LANE), lambda b, i: (b, i, 0))],
        out_shape=[jax.ShapeDtypeStruct((B, T, SG_WIDTH), BF16),
                   jax.ShapeDtypeStruct(heads, BF16), jax.ShapeDtypeStruct(heads, BF16),
                   jax.ShapeDtypeStruct(heads, BF16),
                   jax.ShapeDtypeStruct((B, T, 4 * ML_WIDTH), F32),
                   jax.ShapeDtypeStruct((B, T, LANE), F32)],
        compiler_params=_cparams(("parallel", "parallel")),
        name="inproj",
    )(x, shift, scale, cosT, sinT, p["wcat"], p["bcat"], p["q_norm_g"], p["kv_norm_g"], p["wq"], p["wkv"],
      p["sg_ln_g"], p["sg_ln_b"], p["sg_w"], p["sg_bias"])


def _attn_kernel(*refs, n_src, tk):
    q_ref = refs[0]
    kv_refs = refs[1:1 + 2 * n_src]
    o_ref = refs[1 + 2 * n_src]
    tq = q_ref.shape[2]
    qs = [q_ref[0, hh] for hh in range(2)]
    carry = (tuple(jnp.full((tq, 1), -jnp.inf, F32) for _ in range(2))
             + tuple(jnp.zeros((tq, HEAD_PAD), F32) for _ in range(2)))
    for s in range(n_src):
        k_ref, v_ref = kv_refs[2 * s], kv_refs[2 * s + 1]
        ck = min(tk, k_ref.shape[2])

        def body(c, carry, k_ref=k_ref, v_ref=v_ref, ck=ck):
            rows = pl.ds(pl.multiple_of(c * ck, ck), ck)
            ms, accs = [], []
            for hh in range(2):
                m, acc = carry[hh], carry[2 + hh]
                sc = _nt(qs[hh], k_ref[0, hh, rows, :])
                m_new = jnp.maximum(m, jnp.max(sc, axis=1, keepdims=True))
                pr = jnp.exp2(sc - m_new)
                accs.append(acc * jnp.exp2(m - m_new) + _dot(pr.astype(BF16), v_ref[0, hh, rows, :]))
                ms.append(m_new)
            return tuple(ms) + tuple(accs)

        n_chunks = k_ref.shape[2] // ck
        carry = lax.fori_loop(0, n_chunks, body, carry, unroll=min(ATTN_UNROLL, n_chunks))
    outs = [carry[2 + hh] / carry[2 + hh][:, MLA_VDIM:MLA_VDIM + 1] for hh in range(2)]
    lane = lax.broadcasted_iota(jnp.int32, (tq, HEAD_PAD), 1)
    o_ref[0] = jnp.where(lane < MLA_VDIM, outs[0], pltpu.roll(outs[1], MLA_VDIM, axis=1)).astype(BF16)


def _attention(q, kvs, tq, tk=512):
    B, H, T, _ = q.shape
    in_specs = [pl.BlockSpec((1, 2, tq, HEAD_PAD), lambda b, hp, i: (b, hp, i, 0))]
    args = [q]
    for k, v in kvs:
        K = k.shape[2]
        in_specs += [pl.BlockSpec((1, 2, K, HEAD_PAD), lambda b, hp, i: (b, hp, 0, 0))] * 2
        args += [k, v]
    return pl.pallas_call(
        functools.partial(_attn_kernel, n_src=len(kvs), tk=tk),
        grid=(B, H // 2, T // tq),
        in_specs=in_specs,
        out_specs=pl.BlockSpec((1, tq, HEAD_PAD), lambda b, hp, i: (b, i, hp)),
        out_shape=jax.ShapeDtypeStruct((B, T, H * MLA_VDIM), BF16),
        compiler_params=_cparams(("parallel", "parallel", "parallel")),
        name="attention",
    )(*args)


def _qkconv_kernel(zp_ref, zc_ref, zn_ref, cw_ref, cb_ref, o_ref):
    tc = zc_ref.shape[1]
    i = pl.program_id(1)
    prev = jnp.where(i > 0, zp_ref[0], 0.0)
    nxt = jnp.where(i < pl.num_programs(1) - 1, zn_ref[0], 0.0)
    big = jnp.concatenate([prev, zc_ref[0], nxt], axis=0)
    n_big = tc + 16
    pad = (ML_CONV - 1) // 2
    conv = cb_ref[...]
    for t in range(ML_CONV):
        s = t - pad
        shifted = big if s == 0 else pltpu.roll(big, (-s) % n_big, axis=0)
        conv = conv + shifted[8:8 + tc] * cw_ref[t:t + 1, :]
    qk = _silu(conv)
    lane = lax.broadcasted_iota(jnp.int32, (1, HEAD_PAD), 1)
    for hd in range(ML_HEADS):
        q_blk = qk[:, (hd // 2) * HEAD_PAD:(hd // 2 + 1) * HEAD_PAD]
        in_half = (lane >= ML_DIM) if hd % 2 else (lane < ML_DIM)
        o_ref[0, :, hd * HEAD_PAD:(hd + 1) * HEAD_PAD] = jnp.where(in_half, q_blk, 0.0).astype(BF16)
    o_ref[0, :, ML_HEADS * HEAD_PAD:] = (qk[:, ML_WIDTH:] * (ML_DIM ** -0.5)).astype(BF16)


def _qkconv(zml, p, tc):
    B, T, _ = zml.shape
    hb = tc // 8
    const = lambda shape: pl.BlockSpec(shape, lambda b, i: (0,) * len(shape))
    return pl.pallas_call(
        _qkconv_kernel,
        grid=(B, T // tc),
        in_specs=[pl.BlockSpec((1, 8, 2 * ML_WIDTH), lambda b, i: (b, jnp.maximum(i * hb - 1, 0), 0)),
                  pl.BlockSpec((1, tc, 2 * ML_WIDTH), lambda b, i: (b, i, 0)),
                  pl.BlockSpec((1, 8, 2 * ML_WIDTH), lambda b, i: (b, jnp.minimum((i + 1) * hb, T // 8 - 1), 0)),
                  const((8, 2 * ML_WIDTH)), const((1, 2 * ML_WIDTH))],
        out_specs=pl.BlockSpec((1, tc, QK_LANES), lambda b, i: (b, i, 0)),
        out_shape=jax.ShapeDtypeStruct((B, T, QK_LANES), BF16),
        compiler_params=_cparams(("parallel", "parallel")),
        name="qkconv",
    )(zml, zml, zml, p["conv_w"], p["conv_b"])


def _mlstm_direction(d, qk, vv, gl, fb, c_prev, m_prev):
    L = qk.shape[0]
    li = gl
    lf = _log_sigmoid(pltpu.roll(gl, LANE - 2 * ML_HEADS, axis=1) + fb)
    row = lax.broadcasted_iota(jnp.int32, (L, L), 0)
    col = lax.broadcasted_iota(jnp.int32, (L, L), 1)
    causal = (col >= row) if d else (row >= col)
    tri = jnp.where(causal, 1.0, 0.0).astype(BF16)
    lf_hi, lf_lo = _split(lf)
    bc = _dot(tri, lf_hi) + _dot(tri, lf_lo)
    b_end = bc[0:1, :] if d else bc[L - 1:L, :]
    gw = b_end - bc + li
    m_loc = jnp.max(gw, axis=0, keepdims=True)
    wt = jnp.exp(gw - m_loc)
    rt = (li - bc).T
    m_next = jnp.maximum(b_end + m_prev, m_loc)
    a_sc = jnp.exp(b_end + m_prev - m_next)
    s_sc = jnp.exp(m_loc - m_next)

    lane = lax.broadcasted_iota(jnp.int32, (L, HEAD_PAD), 1)
    lane_r = lax.broadcasted_iota(jnp.int32, (1, HEAD_PAD), 1)
    h_blocks, c_next = [], []
    for blk in range(ML_HEADS // 2):
        cols = slice(blk * HEAD_PAD, (blk + 1) * HEAD_PAD)
        km = qk[:, (ML_HEADS + blk) * HEAD_PAD:(ML_HEADS + blk + 1) * HEAD_PAD]
        v_blk = vv[:, cols]
        hh = []
        for half in range(2):
            hd = 2 * blk + half
            gln = d * ML_HEADS + hd
            in_half = (lane >= ML_DIM) if half else (lane < ML_DIM)
            den_lane = 0 if half else ML_DIM
            qm = qk[:, hd * HEAD_PAD:(hd + 1) * HEAD_PAD]
            vaug = jnp.where(in_half, v_blk, jnp.where(lane == den_lane, 1.0, 0.0))
            b_col = bc[:, gln:gln + 1]
            dm = jnp.where(causal, b_col + rt[gln:gln + 1, :], -jnp.inf)
            a_col = b_col + m_prev[:, gln:gln + 1]
            m_t = jnp.maximum(a_col, jnp.max(dm, axis=1, keepdims=True))
            pm = jnp.exp(dm - m_t) * _nt(qm, km)
            inter = jnp.exp(a_col - m_t)
            c0 = c_prev[hd]
            nd = inter * _nt(qm, c0.astype(BF16)) + _dot(pm.astype(BF16), vaug.astype(BF16))
            den = nd[:, den_lane:den_lane + 1]
            hh.append(nd / jnp.maximum(jnp.abs(den), jnp.exp(-m_t)))
            wv = (wt[:, gln:gln + 1] * vaug).astype(BF16)
            c_loc = _tn(wv, km)
            c_next.append(a_sc[:, gln:gln + 1] * c0 + s_sc[:, gln:gln + 1] * c_loc)
        h_blocks.append(jnp.where(lane < ML_DIM, hh[0], hh[1]))
    mine = (lane_r >= d * ML_HEADS) & (lane_r < (d + 1) * ML_HEADS)
    return h_blocks, c_next, jnp.where(mine, m_next, m_prev)


def _mlstm_kernel(qkf_ref, vf_ref, gf_ref, qkb_ref, vb_ref, gb_ref, fb_ref, ci_ref, mi_ref,
                  hf_ref, hb_ref, cs_ref, ms_ref):
    @pl.when(pl.program_id(1) == 0)
    def _():
        cs_ref[...] = ci_ref[...]
        ms_ref[...] = mi_ref[...]

    rows = lambda hd: slice(hd * HEAD_PAD, (hd + 1) * HEAD_PAD)
    ins = ((qkf_ref, vf_ref, gf_ref), (qkb_ref, vb_ref, gb_ref))
    states = [([cs_ref[0, d, rows(hd), :] for hd in range(ML_HEADS)], ms_ref[0, d]) for d in range(2)]
    results = [_mlstm_direction(d, ins[d][0][0], ins[d][1][0], ins[d][2][0], fb_ref[...], *states[d])
               for d in range(2)]
    for d, h_ref in enumerate((hf_ref, hb_ref)):
        h_blocks, c_next, m_next = results[d]
        for blk, hb in enumerate(h_blocks):
            h_ref[0, :, blk * HEAD_PAD:(blk + 1) * HEAD_PAD] = hb
        for hd in range(ML_HEADS):
            cs_ref[0, d, rows(hd), :] = c_next[hd]
        ms_ref[0, d] = m_next


def _mlstm(qk, zml, gates, p, c_init, m_init):
    B, T, _ = zml.shape
    L = min(ML_CHUNK, T)
    nc = T // L
    const = lambda shape: pl.BlockSpec(shape, lambda b, j: (0,) * len(shape))
    state = pl.BlockSpec((1, 2, ML_HEADS * HEAD_PAD, HEAD_PAD), lambda b, j: (b, 0, 0, 0))
    mstate = pl.BlockSpec((1, 2, 1, LANE), lambda b, j: (b, 0, 0, 0))
    fwd = lambda b, j: (b, j, 0)
    bwd = lambda b, j: (b, nc - 1 - j, 0)
    return pl.pallas_call(
        _mlstm_kernel,
        grid=(B, nc),
        in_specs=[pl.BlockSpec((1, L, QK_LANES), fwd),
                  pl.BlockSpec((1, L, ML_WIDTH), lambda b, j: (b, j, 2)),
                  pl.BlockSpec((1, L, LANE), fwd),
                  pl.BlockSpec((1, L, QK_LANES), bwd),
                  pl.BlockSpec((1, L, ML_WIDTH), lambda b, j: (b, nc - 1 - j, 2)),
                  pl.BlockSpec((1, L, LANE), bwd),
                  const((1, LANE)), state, mstate],
        out_specs=[pl.BlockSpec((1, L, ML_WIDTH), fwd), pl.BlockSpec((1, L, ML_WIDTH), bwd), state, mstate],
        out_shape=[jax.ShapeDtypeStruct((B, T, ML_WIDTH), F32), jax.ShapeDtypeStruct((B, T, ML_WIDTH), F32),
                   jax.ShapeDtypeStruct((B, 2, ML_HEADS * HEAD_PAD, HEAD_PAD), F32),
                   jax.ShapeDtypeStruct((B, 2, 1, LANE), F32)],
        compiler_params=_cparams(("parallel", "arbitrary")),
        name="mlstm",
    )(qk, zml, gates, qk, zml, gates, p["f_bias"], c_init, m_init)


def _outproj_kernel(x_ref, g1_ref, sh2_ref, sc2_ref, ya_ref, yb_ref, h0_ref, h1_ref, o_ref, mg_ref, avg_ref,
                    wo_ref, l1g_ref, l1b_ref, wrh_ref, wrl_ref, x1_ref, hm_ref, aff_ref):
    hs = h0_ref[0] + h1_ref[0]
    mu = _dot2(hs, avg_ref[...])
    hc = hs - mu
    var = _dot2(hc * hc, avg_ref[...])
    yc =(1.0 / (1.0 + jnp.exp(-o_ref[0]))) * (hc * lax.rsqrt(var + LN_EPS) * mg_ref[...])
    y = (_dot(ya_ref[0], wo_ref[0:SG_WIDTH, :])
         + _dot(yb_ref[0], wo_ref[SG_WIDTH:SG_WIDTH + MLA_HEADS * MLA_VDIM, :])
         + _dot(yc.astype(BF16), wo_ref[SG_WIDTH + MLA_HEADS * MLA_VDIM:, :]))
    x1 = _plain_norm(ALPHA * x_ref[0] + g1_ref[0] * y) * l1g_ref[...] + l1b_ref[...]
    x1_ref[0] = x1
    hm = _plain_norm(x1) * (1.0 + sc2_ref[0]) + sh2_ref[0]
    hm_ref[0] = hm.astype(BF16)
    hi, lo = _split(hm)
    logits = _dot(hi, wrh_ref[...]) + (_dot(lo, wrh_ref[...]) + _dot(hi, wrl_ref[...]))
    logits = logits.T[:N_EXPERTS, :]
    ex =jnp.exp(logits - jnp.max(logits, axis=0, keepdims=True))
    aff_ref[0] = ex / jnp.sum(ex, axis=0, keepdims=True)


def _outproj(x, g1, sh2, sc2, ya, yb, h_fwd, h_bwd, zml, p, tm):
    B, T, D = x.shape
    const = lambda shape: pl.BlockSpec(shape, lambda b, i: (0,) * len(shape))
    vec = pl.BlockSpec((1, 1, D), lambda b, i: (b, 0, 0))
    return pl.pallas_call(
        _outproj_kernel,
        grid=(B, T // tm),
        in_specs=[pl.BlockSpec((1, tm, D), lambda b, i: (b, i, 0)), vec, vec, vec,
                  pl.BlockSpec((1, tm, SG_WIDTH), lambda b, i: (b, i, 0)),
                  pl.BlockSpec((1, tm, MLA_HEADS * MLA_VDIM), lambda b, i: (b, i, 0)),
                  pl.BlockSpec((1, tm, ML_WIDTH), lambda b, i: (b, i, 0)),
                  pl.BlockSpec((1, tm, ML_WIDTH), lambda b, i: (b, i, 0)),
                  pl.BlockSpec((1, tm, ML_WIDTH), lambda b, i: (b, i, 3)),
                  const((1, ML_WIDTH)), const((ML_WIDTH, ML_WIDTH)), const((D, D)), const((1, D)), const((1, D)),
                  const((D, LANE)), const((D, LANE))],
        out_specs=[pl.BlockSpec((1, tm, D), lambda b, i: (b, i, 0)),
                   pl.BlockSpec((1, tm, D), lambda b, i: (b, i, 0)),
                   pl.BlockSpec((1, N_EXPERTS, tm), lambda b, i: (b, 0, i))],
        out_shape=[jax.ShapeDtypeStruct((B, T, D), F32), jax.ShapeDtypeStruct((B, T, D), BF16),
                   jax.ShapeDtypeStruct((B, N_EXPERTS, T), F32)],
        compiler_params=_cparams(("parallel", "parallel")),
        name="outproj",
    )(x, g1, sh2, sc2, ya, yb, h_fwd, h_bwd, zml, p["ml_norm_g"], p["head_avg"], p["w_out"], p["ln1_g"], p["ln1_b"],
      p["w_router_hi"], p["w_router_lo"])


def _route_kernel(aff_ref, pos_ref, cnt_ref, *, cap, tw):
    aff = aff_ref[0]
    E, T = aff.shape
    bits = pltpu.bitcast(aff, jnp.int32)
    count = lambda m: jnp.sum(jnp.where(m, 1.0, 0.0), axis=1, keepdims=True)
    thr = jnp.zeros((E, 1), jnp.int32)
    for bit in range(30, -1, -1):
        cand = thr | (1 << bit)
        thr = jnp.where(count(bits >= cand) >= cap, cand, thr)
    above = bits > thr
    tied = bits == thr
    need = cap - count(above)
    idx = lax.broadcasted_iota(jnp.int32, (E, T), 1)
    cut = jnp.zeros((E, 1), jnp.int32)
    for bit in range(T.bit_length() - 1, -1, -1):
        cand = cut | (1 << bit)
        ok = (cand <= T) & (count(tied & (idx < cand)) <= need)
        cut = jnp.where(ok, cand, cut)
    sel = above | (tied & (idx < cut))
    r = lax.broadcasted_iota(jnp.int32, (tw, tw), 0)
    cidx = lax.broadcasted_iota(jnp.int32, (tw, tw), 1)
    before = jnp.where(r < cidx, 1.0, 0.0).astype(BF16)
    offset = jnp.zeros((E, 1), F32)
    lane = lax.broadcasted_iota(jnp.int32, (E, LANE), 1)
    table = jnp.zeros((E, LANE), F32)
    for blk in range(T // tw):
        cols = slice(blk * tw, (blk + 1) * tw)
        sb = jnp.where(sel[:, cols], 1.0, 0.0)
        slot = _dot(sb.astype(BF16), before) + offset
        pos_ref[0, :, cols] = jnp.where(sel[:, cols], slot, -1.0).astype(jnp.int32)
        offset = offset + jnp.sum(sb, axis=1, keepdims=True)
        table = jnp.where(lane == blk + 1, offset, table)
    cnt_ref[0] = table.astype(jnp.int32)


def _route(aff_t, cap):
    B, E, T = aff_t.shape
    nt = T // ROUTE_TILE
    pos, cnt = pl.pallas_call(
        functools.partial(_route_kernel, cap=cap, tw=ROUTE_TILE),
        grid=(B,),
        in_specs=[pl.BlockSpec((1, E, T), lambda b: (b, 0, 0))],
        out_specs=[pl.BlockSpec((1, E, T), lambda b: (b, 0, 0)), pl.BlockSpec((1, E, LANE), lambda b: (b, 0, 0))],
        out_shape=[jax.ShapeDtypeStruct((B, E, T), jnp.int32), jax.ShapeDtypeStruct((B, E, LANE), jnp.int32)],
        compiler_params=_cparams(("parallel",)),
        name="route",
    )(aff_t)
    return pos, cnt[:, :, :nt + 1].reshape(-1)


def _expert_kernel(cnt_ref, pos_ref, aff_ref, h_ref, wg_ref, wu_ref, wd_ref, y_ref, xe_ref, gate_ref, *,
                   n_win, expert_major):
    cap = y_ref.shape[2]
    nt = pos_ref.shape[2]
    rb = min(GATHER_ROWS, cap)
    i0, i1 = pl.program_id(0), pl.program_id(1)
    b, e = (i1, i0) if expert_major else (i0, i1)
    base = (b * N_EXPERTS + e) * (nt + 1)
    slot_iota = lax.broadcasted_iota(jnp.int32, (rb, ROUTE_TILE), 0)

    def gather_tile(i, s0):
        hit = pos_ref[0, 0, pl.ds(i, 1), :] == slot_iota + s0
        rows = pl.ds(pl.multiple_of(i * ROUTE_TILE, ROUTE_TILE), ROUTE_TILE)
        xe = _dot(jnp.where(hit, 1.0, 0.0).astype(BF16), h_ref[0, rows, :])
        gate = jnp.sum(jnp.where(hit, aff_ref[0, 0, pl.ds(i, 1), :], 0.0), axis=1, keepdims=True)
        return xe, gate

    for r in range(cap // rb):
        s0 = r * rb
        first = sum((cnt_ref[base + i + 1] <= s0).astype(jnp.int32) for i in range(nt))
        end = sum((cnt_ref[base + i] < s0 + rb).astype(jnp.int32) for i in range(nt))
        start = jnp.minimum(first, nt - n_win)
        parts = [gather_tile(start + w, s0) for w in range(n_win)]
        rows = slice(s0, s0 + rb)
        xe_ref[rows, :] = sum(pt[0] for pt in parts)
        gate_ref[rows, :] = sum(pt[1] for pt in parts)

        def overflow(i, carry, s0=s0, rows=rows):
            xe, gate = gather_tile(i, s0)
            xe_ref[rows, :] += xe
            gate_ref[rows, :] += gate
            return carry

        lax.fori_loop(start + n_win, end, overflow, 0)
    xb = xe_ref[...].astype(BF16)
    hid = _silu(_dot(xb, wg_ref[0])) * _dot(xb, wu_ref[0])
    y_ref[0, 0] = (_dot(hid.astype(BF16), wd_ref[0]) * gate_ref[...]).astype(BF16)


def _experts(pos, cnt, aff_t, hm, p, cap):
    B, T, D = hm.shape
    E = N_EXPERTS
    nt = T // ROUTE_TILE
    expert_major = T * D < 3 * D * EXPERT_FF
    be = (lambda i, j: (j, i)) if expert_major else (lambda i, j: (i, j))
    row = pl.BlockSpec((1, 1, nt, ROUTE_TILE), lambda i, j, c: be(i, j) + (0, 0))
    wspec = lambda shape: pl.BlockSpec(shape, lambda i, j, c: (be(i, j)[1], 0, 0))
    return pl.pallas_call(
        functools.partial(_expert_kernel, n_win=min(GATHER_TILES, nt), expert_major=expert_major),
        grid_spec=pltpu.PrefetchScalarGridSpec(
            num_scalar_prefetch=1,
            grid=(E, B) if expert_major else (B, E),
            in_specs=[row, row,
                      pl.BlockSpec((1, T, D), lambda i, j, c: (be(i, j)[0], 0, 0)),
                      wspec((1, D, EXPERT_FF)), wspec((1, D, EXPERT_FF)), wspec((1, EXPERT_FF, D))],
            out_specs=pl.BlockSpec((1, 1, cap, D), lambda i, j, c: be(i, j) + (0, 0)),
            scratch_shapes=[pltpu.VMEM((cap, D), F32), pltpu.VMEM((cap, 1), F32)]),
        out_shape=jax.ShapeDtypeStruct((B, E, cap, D), BF16),
        compiler_params=_cparams(("parallel", "arbitrary")),
        name="experts",
    )(cnt, pos.reshape(B, E, nt, ROUTE_TILE), aff_t.reshape(B, E, nt, ROUTE_TILE), hm,
      p["w_gate"], p["w_up"], p["w_down"])


def _combine_kernel(cnt_ref, pos_ref, y_ref, x_ref, g2_ref, lg_ref, lb_ref, o_ref, ycat_ref, acc_ref, *, nt):
    E, cap = y_ref.shape[1], y_ref.shape[2]
    tt = x_ref.shape[1]
    win = min(COMBINE_WIN, cap)
    per = tt // ROUTE_TILE
    b, i = pl.program_id(0), pl.program_id(1)
    slot_iota = lax.broadcasted_iota(jnp.int32, (win, tt), 0)

    def window(e, w):
        start = pl.multiple_of(jnp.minimum(w, cap - win), 16)
        slot = slot_iota + start
        hit = (pos_ref[0, e:e + 1, :] == slot) & (slot >= w)
        return jnp.where(hit, 1.0, 0.0).astype(BF16), y_ref[0, e, pl.ds(start, win), :]

    onehots, ranges = [], []
    for e in range(E):
        base = (b * E + e) * (nt + 1) + i * per
        lo, hi = cnt_ref[base], cnt_ref[base + per]
        w0 = jnp.bitwise_and(lo, ~15)
        onehot, rows = window(e, w0)
        ycat_ref[e * win:(e + 1) * win, :] = rows
        onehots.append(onehot)
        ranges.append((w0, hi))
    acc_ref[...] = _tn(jnp.concatenate(onehots, axis=0), ycat_ref[...])
    for e in range(E):
        w0, hi = ranges[e]

        def more(w, e=e):
            onehot, rows = window(e, w)
            acc_ref[...] += _tn(onehot, rows)
            return w + win

        lax.while_loop(lambda w, hi=hi: w < hi, more, w0 + win)
    o_ref[0] = _plain_norm(ALPHA * x_ref[0] + g2_ref[0] * acc_ref[...]) * lg_ref[...] + lb_ref[...]


def _combine(pos, cnt, ye, x1, g2, p, tt):
    B, T, D = x1.shape
    E, cap = ye.shape[1], ye.shape[2]
    win = min(COMBINE_WIN, cap)
    const = lambda shape: pl.BlockSpec(shape, lambda b, i, c: (0,) * len(shape))
    return pl.pallas_call(
        functools.partial(_combine_kernel, nt=T // ROUTE_TILE),
        grid_spec=pltpu.PrefetchScalarGridSpec(
            num_scalar_prefetch=1,
            grid=(B, T // tt),
            in_specs=[pl.BlockSpec((1, E, tt), lambda b, i, c: (b, 0, i)),
                      pl.BlockSpec((1, E, cap, D), lambda b, i, c: (b, 0, 0, 0)),
                      pl.BlockSpec((1, tt, D), lambda b, i, c: (b, i, 0)),
                      pl.BlockSpec((1, 1, D), lambda b, i, c: (b, 0, 0)),
                      const((1, D)), const((1, D))],
            out_specs=pl.BlockSpec((1, tt, D), lambda b, i, c: (b, i, 0)),
            scratch_shapes=[pltpu.VMEM((E * win, D), BF16), pltpu.VMEM((tt, D), F32)]),
        out_shape=jax.ShapeDtypeStruct((B, T, D), F32),
        compiler_params=_cparams(("parallel", "arbitrary")),
        name="combine",
    )(cnt, pos, ye, x1, g2, p["ln2_g"], p["ln2_b"])


def _rot_cols(a):
    a4 = a.reshape(a.shape[:-1] + (2, 2, ROPE_AXIS // 2))
    return jnp.stack([-a4[..., 1, :], a4[..., 0, :]], axis=-2).reshape(a.shape)


def _pad_cols(a, before, total):
    pads = [(0, 0)] * (a.ndim - 1) + [(before, total - before - a.shape[-1])]
    return jnp.pad(a, pads)


def _prep_layer(l, w):
    D = D_MODEL
    wi, bi = w["w_in"][l], w["b_in"][l][None, :]
    n_sg, n_mla = 2 * SG_WIDTH, Q_LORA + KV_LORA + MLA_ROPE
    o_kr = n_sg + Q_LORA + KV_LORA
    o_ml = n_sg + n_mla
    o_g = o_ml + 4 * ML_WIDTH

    def cat(a):
        kr = a[:, o_kr:o_kr + MLA_ROPE]
        gates = a[:, o_g:o_g + 4 * ML_HEADS].reshape(-1, 2, 2, ML_HEADS).transpose(0, 2, 1, 3).reshape(-1, 4 * ML_HEADS)
        return jnp.concatenate([a[:, :o_kr], _pad_cols(kr, MLA_NOPE, HEAD_PAD), _pad_cols(_rot_cols(kr), MLA_NOPE, HEAD_PAD),
                                a[:, o_ml:o_g], _pad_cols(gates, 0, LANE)], axis=1)

    wq = w["w_uq"][l].reshape(Q_LORA, MLA_HEADS, MLA_NOPE + MLA_ROPE)
    wq_main = _pad_cols(wq, 0, HEAD_PAD).reshape(Q_LORA, -1)
    wq_rot = _pad_cols(_rot_cols(wq[..., MLA_NOPE:]), MLA_NOPE, HEAD_PAD).reshape(Q_LORA, -1)
    wkv = w["w_ukv"][l].reshape(KV_LORA, MLA_HEADS, MLA_NOPE + MLA_VDIM)
    wk = _pad_cols(wkv[..., :MLA_NOPE], 0, HEAD_PAD).reshape(KV_LORA, -1)
    wv = _pad_cols(wkv[..., MLA_NOPE:], 0, HEAD_PAD).reshape(KV_LORA, -1)
    head = jnp.arange(ML_WIDTH) // ML_DIM
    wr = _pad_cols(w["w_router"][l], 0, LANE)
    wr_hi = wr.astype(BF16)
    return {
        "wcat": cat(wi).astype(BF16), "bcat": cat(bi),
        "q_norm_g": w["q_norm_g"][l][None, :], "kv_norm_g": w["kv_norm_g"][l][None, :],
        "wq": jnp.concatenate([wq_main, wq_rot], axis=1).astype(BF16),
        "wkv": jnp.concatenate([wk, wv], axis=1).astype(BF16),
        "sg_ln_g": w["sg_ln_g"][l][None, :], "sg_ln_b": w["sg_ln_b"][l][None, :],
        "sg_w": w["sg_w"][l].astype(BF16),
        "sg_bias": jnp.repeat(w["sg_b"][l].T, SG_GDIM, axis=1),
        "conv_w": jnp.pad(w["ml_conv_w"][l], ((0, 8 - ML_CONV), (0, 0))), "conv_b": w["ml_conv_b"][l][None, :],
        "f_bias": _pad_cols(w["ml_f_bias"][l].reshape(1, 2 * ML_HEADS), 0, LANE),
        "ml_norm_g": w["ml_norm_g"][l][None, :],
        "head_avg": ((head[:, None] == head[None, :]).astype(F32) / ML_DIM).astype(BF16),
        "w_out": w["w_out"][l].astype(BF16),
        "ln1_g": w["ln1_g"][l][None, :], "ln1_b": w["ln1_b"][l][None, :],
        "w_router_hi": wr_hi, "w_router_lo": (wr - wr_hi.astype(F32)).astype(BF16),
        "w_gate": w["w_gate"][l].astype(BF16), "w_up": w["w_up"][l].astype(BF16), "w_down": w["w_down"][l].astype(BF16),
        "ln2_g": w["ln2_g"][l][None, :], "ln2_b": w["ln2_b"][l][None, :],
    }


def _rope_tables(T):
    rows = T // GRID_W
    row = jnp.repeat(jnp.arange(rows, dtype=F32), GRID_W)
    colv = jnp.tile(jnp.arange(GRID_W, dtype=F32), rows)
    inv = ROPE_BASE ** (-jnp.arange(ROPE_AXIS // 2, dtype=F32) * 2.0 / ROPE_AXIS)
    ang = jnp.concatenate([row[:, None] * inv, colv[:, None] * inv], axis=-1)
    half = ROPE_AXIS // 2
    spread = lambda t: jnp.concatenate([t[:, :half], t[:, :half], t[:, half:], t[:, half:]], axis=1)
    cosT = jnp.concatenate([jnp.ones((T, MLA_NOPE), F32), spread(jnp.cos(ang)), jnp.zeros((T, 32), F32)], axis=1)
    sinT = _pad_cols(spread(jnp.sin(ang)), MLA_NOPE, HEAD_PAD)
    return cosT, sinT


def _moe(x1, hm, aff_t, g2, p, tt):
    cap = EC_CAPACITY * x1.shape[1] // N_EXPERTS
    pos, cnt = _route(aff_t, cap)
    ye = _experts(pos, cnt, aff_t, hm, p, cap)
    return _combine(pos, cnt, ye, x1, g2, p, tt)


def _forward(x, c, ctx, c_ctx, w):
    B, T, D = x.shape
    Tc = ctx.shape[1]
    L = w["w_ada"].shape[0]
    cc = jnp.zeros((16, D), F32).at[:B].set(c).at[B].set(c_ctx)
    ada = _ada(cc, w["w_ada"], w["b_ada"])
    cos_l, sin_l = _rope_tables(T)
    cos_c = _pad_cols(jnp.ones((Tc, MLA_NOPE + MLA_ROPE), F32), 0, HEAD_PAD)
    sin_c = jnp.zeros((Tc, HEAD_PAD), F32)
    zero_c = jnp.zeros((B, 2, ML_HEADS * HEAD_PAD, HEAD_PAD), F32)
    zero_m = jnp.zeros((B, 2, 1, LANE), F32)
    tm = min(512, T)
    xl, xc = x, ctx
    for l in range(L):
        p = _prep_layer(l, w)
        mods_l = [ada[l, :B, i * D:(i + 1) * D][:, None, :] for i in range(6)]
        mods_c = [jnp.broadcast_to(ada[l, B, i * D:(i + 1) * D][None, None, :], (B, 1, D)) for i in range(6)]
        update_ctx = l < L - 1
        ya_c, q_c, k_c, v_c, zml_c, g_c = _inproj(xc, mods_c[0], mods_c[1], cos_c, sin_c, p, min(tm, Tc))
        ya_l, q_l, k_l, v_l, zml_l, g_l = _inproj(xl, mods_l[0], mods_l[1], cos_l, sin_l, p, tm)
        hf_c, hb_c, cs_c, ms_c = _mlstm(_qkconv(zml_c, p, min(512, Tc)), zml_c, g_c, p, zero_c, zero_m)
        hf_l, hb_l, _, _ = _mlstm(_qkconv(zml_l, p, min(512, T)), zml_l, g_l, p, cs_c, ms_c)
        yb_l = _attention(q_l, [(k_l, v_l), (k_c, v_c)], tq=min(512, T))
        x1_l, hm_l, aff_l = _outproj(xl, mods_l[2], mods_l[3], mods_l[4], ya_l, yb_l, hf_l, hb_l, zml_l, p, tm)
        xl = _moe(x1_l, hm_l, aff_l, mods_l[5], p, tt=min(512, T))
        if update_ctx:
            yb_c = _attention(q_c, [(k_c, v_c)], tq=min(256, Tc))
            x1_c, hm_c, aff_c = _outproj(xc, mods_c[2], mods_c[3], mods_c[4], ya_c, yb_c, hf_c, hb_c, zml_c, p,
                                         min(tm, Tc))
            xc = _moe(x1_c, hm_c, aff_c, mods_c[5], p, tt=min(512, Tc))
    return xl


def kernel(x, c, ctx, c_ctx, w_ada, b_ada, w_in, b_in, sg_ln_g, sg_ln_b, sg_w, sg_b, q_norm_g, kv_norm_g, w_uq, w_ukv,
           ml_conv_w, ml_conv_b, ml_f_bias, ml_norm_g, w_out, ln1_g, ln1_b, w_router, w_gate, w_up, w_down, ln2_g, ln2_b):
    w = dict(w_ada=w_ada, b_ada=b_ada, w_in=w_in, b_in=b_in, sg_ln_g=sg_ln_g, sg_ln_b=sg_ln_b, sg_w=sg_w, sg_b=sg_b,
             q_norm_g=q_norm_g, kv_norm_g=kv_norm_g, w_uq=w_uq, w_ukv=w_ukv, ml_conv_w=ml_conv_w, ml_conv_b=ml_conv_b,
             ml_f_bias=ml_f_bias, ml_norm_g=ml_norm_g, w_out=w_out, ln1_g=ln1_g, ln1_b=ln1_b, w_router=w_router,
             w_gate=w_gate, w_up=w_up, w_down=w_down, ln2_g=ln2_g, ln2_b=ln2_b)
    return _forward(x, c, ctx, c_ctx, w)
```

```python
import functools

import jax
import jax.numpy as jnp
from jax import lax
from jax.experimental import pallas as pl
from jax.experimental.pallas import tpu as pltpu

F32 = jnp.float32
BF16 = jnp.bfloat16
HIGHEST = lax.Precision.HIGHEST

D_MODEL = 1024
DEPTH = 2
GRID_W = 64
SG_WIDTH = 256
SG_GROUPS = 4
SG_GDIM = 64
SG_CHUNK = 128
MLA_HEADS = 8
MLA_NOPE = 64
MLA_ROPE = 32
MLA_VDIM = 64
Q_LORA = 384
KV_LORA = 256
ROPE_AXIS = 16
ROPE_BASE = 10000.0
MLA_SCALE = (MLA_NOPE + MLA_ROPE) ** -0.5
ML_HEADS = 4
ML_DIM = 64
ML_WIDTH = 256
ML_CONV = 5
N_EXPERTS = 16
EXPERT_FF = 1024
EC_CAPACITY = 2
ALPHA = (2 * DEPTH) ** 0.25
LN_EPS = 1e-6
LOG2E = 1.4426950408889634

LANE = 128
HEAD_PAD = 128
ML_CHUNK = 256
OFF_SG, OFF_CQ, OFF_CKV, OFF_KR, OFF_KRR, OFF_ML, OFF_G, N_CAT = 0, 512, 896, 1152, 1280, 1408, 2432, 2560
VMEM_LIMIT = 56 * 1024 * 1024
ROUTE_TILE = 256
GATHER_ROWS = 128
GATHER_TILES = 6
COMBINE_WIN = 128
QK_LANES =ML_HEADS * HEAD_PAD + ML_WIDTH
ATTN_HEADS = 2
ATTN_KEYS = 2048
ATTN_UNROLL = 8


def _cparams(sem):
    return pltpu.CompilerParams(dimension_semantics=sem, vmem_limit_bytes=VMEM_LIMIT)


def _nt(a, b):
    return lax.dot_general(a, b, (((1,), (1,)), ((), ())), preferred_element_type=F32)


def _tn(a, b):
    return lax.dot_general(a, b, (((0,), (0,)), ((), ())), preferred_element_type=F32)


def _dot(a, b, precision=None):
    return jnp.dot(a, b, preferred_element_type=F32, precision=precision)


def _split(x):
    hi = x.astype(BF16)
    return hi, (x - hi.astype(F32)).astype(BF16)


def _dot2(x, w):
    hi, lo = _split(x)
    return _dot(hi, w) + _dot(lo, w)


def _plain_norm(x):
    mu = jnp.mean(x, axis=-1, keepdims=True)
    xc = x - mu
    var = jnp.mean(xc * xc, axis=-1, keepdims=True)
    return xc * lax.rsqrt(var + LN_EPS)


def _rms(x):
    return x * lax.rsqrt(jnp.mean(x * x, axis=-1, keepdims=True) + LN_EPS)


def _silu(x):
    return x * (1.0 / (1.0 + jnp.exp(-x)))


def _gelu_tanh(x):
    return 0.5 * x * (1.0 + jnp.tanh(0.7978845608028654 * (x + 0.044715 * (x * x * x))))


def _log_sigmoid(x):
    return jnp.minimum(x, 0.0) - jnp.log(1.0 + jnp.exp(-jnp.abs(x)))


def _ada_kernel(c_ref, w_ref, b_ref, o_ref):
    o_ref[0] = _dot(_silu(c_ref[...]), w_ref[0], precision=HIGHEST) + b_ref[0]


def _ada(cc, w_ada, b_ada):
    L, D, N = w_ada.shape
    tn = 1536
    return pl.pallas_call(
        _ada_kernel,
        grid=(L, N // tn),
        in_specs=[pl.BlockSpec((16, D), lambda l, j: (0, 0)),
                  pl.BlockSpec((1, D, tn), lambda l, j: (l, 0, j)),
                  pl.BlockSpec((1, 1, tn), lambda l, j: (l, 0, j))],
        out_specs=pl.BlockSpec((1, 16, tn), lambda l, j: (l, 0, j)),
        out_shape=jax.ShapeDtypeStruct((L, 16, N), F32),
        compiler_params=_cparams(("parallel", "parallel")),
        name="ada",
    )(cc, w_ada, b_ada.reshape(L, 1, N))


def _inproj_kernel(x_ref, sh_ref, sc_ref, cos_ref, sin_ref, wcat_ref, bcat_ref, qg_ref, kvg_ref, wq_ref, wkv_ref,
                   sglg_ref, sglb_ref, sgw_ref, sgb_ref,
                   ya_ref, q_ref, k_ref, v_ref, zml_ref, g_ref):
    tm = x_ref.shape[1]
    h = _plain_norm(x_ref[0]) * (1.0 + sc_ref[0]) + sh_ref[0]
    z = _dot(h.astype(BF16), wcat_ref[...]) + bcat_ref[...]
    zs = _gelu_tanh(z[:, OFF_SG:OFF_SG + 2 * SG_WIDTH])
    u = zs[:, :SG_WIDTH]
    vn = _plain_norm(zs[:, SG_WIDTH:]) * sglg_ref[...] + sglb_ref[...]
    group = jnp.right_shift(lax.broadcasted_iota(jnp.int32, (SG_CHUNK, SG_WIDTH), 1), SG_GDIM.bit_length() - 1)
    for ch in range(tm // SG_CHUNK):
        rows = slice(ch * SG_CHUNK, (ch + 1) * SG_CHUNK)
        vc = vn[rows].astype(BF16)
        mixed = sgb_ref[...]
        for g in range(SG_GROUPS):
            mixed = mixed + jnp.where(group == g, _dot(sgw_ref[g], vc), 0.0)
        ya_ref[0, rows, :] = (u[rows] * mixed).astype(BF16)
    cosT = cos_ref[...]
    sinT = sin_ref[...]
    cqn = (_rms(z[:, OFF_CQ:OFF_CQ + Q_LORA]) * qg_ref[...]).astype(BF16)
    qq = _dot(cqn, wq_ref[...])
    ckvn = (_rms(z[:, OFF_CKV:OFF_CKV + KV_LORA]) * kvg_ref[...]).astype(BF16)
    kv = _dot(ckvn, wkv_ref[...])
    kr = z[:, OFF_KR:OFF_KR + HEAD_PAD] * cosT + z[:, OFF_KRR:OFF_KRR + HEAD_PAD] * sinT
    one_lane = jnp.where(lax.broadcasted_iota(jnp.int32, (1, HEAD_PAD), 1) == MLA_VDIM, 1.0, 0.0)
    hw = MLA_HEADS * HEAD_PAD
    for hd in range(MLA_HEADS):
        cols = slice(hd * HEAD_PAD, (hd + 1) * HEAD_PAD)
        rcols = slice(hw + hd * HEAD_PAD, hw + (hd + 1) * HEAD_PAD)
        q_ref[0, hd] = ((qq[:, cols] * cosT + qq[:, rcols] * sinT) * (MLA_SCALE * LOG2E)).astype(BF16)
        k_ref[0, hd] = (kv[:, cols] + kr).astype(BF16)
        v_ref[0, hd] = (kv[:, rcols] + one_lane).astype(BF16)
    zml_ref[0] = z[:, OFF_ML:OFF_ML + 4 * ML_WIDTH]
    g_ref[0] = z[:, OFF_G:OFF_G + LANE]


def _inproj(x, shift, scale, cosT, sinT, p, tm):
    B, T, D = x.shape
    const = lambda shape: pl.BlockSpec(shape, lambda b, i: (0,) * len(shape))
    heads = (B, MLA_HEADS, T, HEAD_PAD)
    return pl.pallas_call(
        _inproj_kernel,
        grid=(B, T // tm),
        in_specs=[pl.BlockSpec((1, tm, D), lambda b, i: (b, i, 0)),
                  pl.BlockSpec((1, 1, D), lambda b, i: (b, 0, 0)),
                  pl.BlockSpec((1, 1, D), lambda b, i: (b, 0, 0)),
                  pl.BlockSpec((tm, HEAD_PAD), lambda b, i: (i, 0)),
                  pl.BlockSpec((tm, HEAD_PAD), lambda b, i: (i, 0)),
                  const((D, N_CAT)), const((1, N_CAT)), const((1, Q_LORA)), const((1, KV_LORA)),
                  const((Q_LORA, 2 * MLA_HEADS * HEAD_PAD)), const((KV_LORA, 2 * MLA_HEADS * HEAD_PAD)),
                  const((1, SG_WIDTH)), const((1, SG_WIDTH)), const((SG_GROUPS, SG_CHUNK, SG_CHUNK)),
                  const((SG_CHUNK, SG_WIDTH))],
        out_specs=[pl.BlockSpec((1, tm, SG_WIDTH), lambda b, i: (b, i, 0)),
                   pl.BlockSpec((1, MLA_HEADS, tm, HEAD_PAD), lambda b, i: (b, 0, i, 0)),
                   pl.BlockSpec((1, MLA_HEADS, tm, HEAD_PAD), lambda b, i: (b, 0, i, 0)),
                   pl.BlockSpec((1, MLA_HEADS, tm, HEAD_PAD), lambda b, i: (b, 0, i, 0)),
                   pl.BlockSpec((1, tm, 4 * ML_WIDTH), lambda b, i: (b, i, 0)),
                   pl.BlockSpec((1, tm, LANE), lambda b, i: (b, i, 0))],
        out_shape=[jax.ShapeDtypeStruct((B, T, SG_WIDTH), BF16),
                   jax.ShapeDtypeStruct(heads, BF16), jax.ShapeDtypeStruct(heads, BF16),
                   jax.ShapeDtypeStruct(heads, BF16),
                   jax.ShapeDtypeStruct((B, T, 4 * ML_WIDTH), F32),
                   jax.ShapeDtypeStruct((B, T, LANE), F32)],
        compiler_params=_cparams(("parallel", "parallel")),
        name="inproj",
    )(x, shift, scale, cosT, sinT, p["wcat"], p["bcat"], p["q_norm_g"], p["kv_norm_g"], p["wq"], p["wkv"],
      p["sg_ln_g"], p["sg_ln_b"], p["sg_w"], p["sg_bias"])


def _attn_kernel(*refs, n_src, tk):
    q_ref = refs[0]
    kv_refs = refs[1:1 + 2 * n_src]
    o_ref = refs[1 + 2 * n_src]
    nh, tq = q_ref.shape[1], q_ref.shape[2]
    qs = [q_ref[0, hh] for hh in range(nh)]
    carry = (tuple(jnp.full((tq, 1), -jnp.inf, F32) for _ in range(nh))
             + tuple(jnp.zeros((tq, HEAD_PAD), F32) for _ in range(nh)))
    for s in range(n_src):
        k_ref, v_ref = kv_refs[2 * s], kv_refs[2 * s + 1]
        ck = min(tk, k_ref.shape[2])

        def body(c, carry, k_ref=k_ref, v_ref=v_ref, ck=ck):
            rows = pl.ds(pl.multiple_of(c * ck, ck), ck)
            ms, accs = [], []
            for hh in range(nh):
                m, acc = carry[hh], carry[nh + hh]
                sc = _nt(qs[hh], k_ref[0, hh, rows, :])
                m_new = jnp.maximum(m, jnp.max(sc, axis=1, keepdims=True))
                pr = jnp.exp2(sc - m_new)
                accs.append(acc * jnp.exp2(m - m_new) + _dot(pr.astype(BF16), v_ref[0, hh, rows, :]))
                ms.append(m_new)
            return tuple(ms) + tuple(accs)

        n_chunks = k_ref.shape[2] // ck
        carry = lax.fori_loop(0, n_chunks, body, carry, unroll=min(ATTN_UNROLL, n_chunks))
    outs = [carry[nh + hh] / carry[nh + hh][:, MLA_VDIM:MLA_VDIM + 1] for hh in range(nh)]
    lane = lax.broadcasted_iota(jnp.int32, (tq, HEAD_PAD), 1)
    for pair in range(nh // 2):
        both = jnp.where(lane < MLA_VDIM, outs[2 * pair], pltpu.roll(outs[2 * pair + 1], MLA_VDIM, axis=1))
        o_ref[0, :, pair * HEAD_PAD:(pair + 1) * HEAD_PAD] = both.astype(BF16)


def _attention(q, kvs, tq, tk=ATTN_KEYS, nh=ATTN_HEADS):
    B, H, T, _ = q.shape
    in_specs = [pl.BlockSpec((1, nh, tq, HEAD_PAD), lambda b, hp, i: (b, hp, i, 0))]
    args = [q]
    for k, v in kvs:
        K = k.shape[2]
        in_specs += [pl.BlockSpec((1, nh, K, HEAD_PAD), lambda b, hp, i: (b, hp, 0, 0))] * 2
        args += [k, v]
    return pl.pallas_call(
        functools.partial(_attn_kernel, n_src=len(kvs), tk=tk),
        grid=(B, H // nh, T // tq),
        in_specs=in_specs,
        out_specs=pl.BlockSpec((1, tq, nh * MLA_VDIM), lambda b, hp, i: (b, i, hp)),
        out_shape=jax.ShapeDtypeStruct((B, T, H * MLA_VDIM), BF16),
        compiler_params=_cparams(("parallel", "parallel", "parallel")),
        name="attention",
    )(*args)


def _qkconv_kernel(zp_ref, zc_ref, zn_ref, cw_ref, cb_ref, o_ref):
    tc = zc_ref.shape[1]
    i = pl.program_id(1)
    prev = jnp.where(i > 0, zp_ref[0], 0.0)
    nxt = jnp.where(i < pl.num_programs(1) - 1, zn_ref[0], 0.0)
    big = jnp.concatenate([prev, zc_ref[0], nxt], axis=0)
    n_big = tc + 16
    pad = (ML_CONV - 1) // 2
    conv = cb_ref[...]
    for t in range(ML_CONV):
        s = t - pad
        shifted = big if s == 0 else pltpu.roll(big, (-s) % n_big, axis=0)
        conv = conv + shifted[8:8 + tc] * cw_ref[t:t + 1, :]
    qk = _silu(conv)
    lane = lax.broadcasted_iota(jnp.int32, (1, HEAD_PAD), 1)
    for hd in range(ML_HEADS):
        q_blk = qk[:, (hd // 2) * HEAD_PAD:(hd // 2 + 1) * HEAD_PAD]
        in_half = (lane >= ML_DIM) if hd % 2 else (lane < ML_DIM)
        o_ref[0, :, hd * HEAD_PAD:(hd + 1) * HEAD_PAD] = jnp.where(in_half, q_blk, 0.0).astype(BF16)
    o_ref[0, :, ML_HEADS * HEAD_PAD:] = (qk[:, ML_WIDTH:] * (ML_DIM ** -0.5)).astype(BF16)


def _qkconv(zml, p, tc):
    B, T, _ = zml.shape
    hb = tc // 8
    const = lambda shape: pl.BlockSpec(shape, lambda b, i: (0,) * len(shape))
    return pl.pallas_call(
        _qkconv_kernel,
        grid=(B, T // tc),
        in_specs=[pl.BlockSpec((1, 8, 2 * ML_WIDTH), lambda b, i: (b, jnp.maximum(i * hb - 1, 0), 0)),
                  pl.BlockSpec((1, tc, 2 * ML_WIDTH), lambda b, i: (b, i, 0)),
                  pl.BlockSpec((1, 8, 2 * ML_WIDTH), lambda b, i: (b, jnp.minimum((i + 1) * hb, T // 8 - 1), 0)),
                  const((8, 2 * ML_WIDTH)), const((1, 2 * ML_WIDTH))],
        out_specs=pl.BlockSpec((1, tc, QK_LANES), lambda b, i: (b, i, 0)),
        out_shape=jax.ShapeDtypeStruct((B, T, QK_LANES), BF16),
        compiler_params=_cparams(("parallel", "parallel")),
        name="qkconv",
    )(zml, zml, zml, p["conv_w"], p["conv_b"])


def _mlstm_direction(d, qk, vv, gl, fb, c_prev, m_prev):
    L = qk.shape[0]
    lf = _log_sigmoid(pltpu.roll(gl, LANE - 2 * ML_HEADS, axis=1) + fb)
    ks = lax.broadcasted_iota(jnp.int32, (L, L), 0)
    qt = lax.broadcasted_iota(jnp.int32, (L, L), 1)
    visible = (ks >= qt) if d else (ks <= qt)
    tri = jnp.where((qt >= ks) if d else (qt <= ks), 1.0, 0.0).astype(BF16)
    lf_hi, lf_lo = _split(lf)
    bc = _dot(tri, lf_hi) + _dot(tri, lf_lo)
    r_cols = gl - bc
    b8 = bc.T[0:2 * ML_HEADS]
    li8 = gl.T[0:2 * ML_HEADS]
    b_end = b8[:, 0:1] if d else b8[:, L - 1:L]
    gw = b_end - b8 + li8
    m_loc = jnp.max(gw, axis=1, keepdims=True)
    w8 = jnp.exp(gw - m_loc)
    m_next = jnp.maximum(b_end + m_prev, m_loc)
    a_sc = jnp.exp(b_end + m_prev - m_next)
    s_sc = jnp.exp(m_loc - m_next)
    a8 = b8 + m_prev

    srow = lax.broadcasted_iota(jnp.int32, (HEAD_PAD, L), 0)
    h_blocks, c_next = [], []
    for blk in range(ML_HEADS // 2):
        km = qk[:, (ML_HEADS + blk) * HEAD_PAD:(ML_HEADS + blk + 1) * HEAD_PAD]
        v_t = vv[:, blk * HEAD_PAD:(blk + 1) * HEAD_PAD].T
        hh = []
        for half in range(2):
            hd = 2 * blk + half
            g = d * ML_HEADS + hd
            in_half = (srow >= ML_DIM) if half else (srow < ML_DIM)
            den_row = 0 if half else ML_DIM
            qm = qk[:, hd * HEAD_PAD:(hd + 1) * HEAD_PAD]
            vaug_t = jnp.where(in_half, v_t, jnp.where(srow == den_row, 1.0, 0.0))
            dm = jnp.where(visible, r_cols[:, g:g + 1] + b8[g:g + 1, :], -jnp.inf)
            a_row = a8[g:g + 1, :]
            m_t = jnp.maximum(a_row, jnp.max(dm, axis=0, keepdims=True))
            pm = jnp.exp(dm - m_t) * _nt(km, qm)
            c0 = c_prev[hd]
            nd = (jnp.exp(a_row - m_t) * _nt(c0.astype(BF16), qm)
                  + _dot(vaug_t.astype(BF16), pm.astype(BF16)))
            den = nd[den_row:den_row + 1, :]
            hh.append(nd / jnp.maximum(jnp.abs(den), jnp.exp(-m_t)))
            c_loc = _dot((vaug_t * w8[g:g + 1, :]).astype(BF16), km)
            c_next.append(a_sc[g:g + 1, 0:HEAD_PAD] * c0 + s_sc[g:g + 1, 0:HEAD_PAD] * c_loc)
        h_blocks.append(jnp.where(srow < ML_DIM, hh[0], hh[1]).T)
    grow = lax.broadcasted_iota(jnp.int32, (2 * ML_HEADS, L), 0)
    mine = (grow >= d * ML_HEADS) & (grow < (d + 1) * ML_HEADS)
    return h_blocks, c_next, jnp.where(mine, m_next, m_prev)


def _mlstm_kernel(qkf_ref, vf_ref, gf_ref, qkb_ref, vb_ref, gb_ref, fb_ref, ci_ref, mi_ref,
                  hf_ref, hb_ref, cs_ref, ms_ref):
    @pl.when(pl.program_id(1) == 0)
    def _():
        cs_ref[...] = ci_ref[...]
        ms_ref[...] = mi_ref[...]

    rows = lambda hd: slice(hd * HEAD_PAD, (hd + 1) * HEAD_PAD)
    ins = ((qkf_ref, vf_ref, gf_ref), (qkb_ref, vb_ref, gb_ref))
    states = [([cs_ref[0, d, rows(hd), :] for hd in range(ML_HEADS)], ms_ref[0, d]) for d in range(2)]
    results = [_mlstm_direction(d, ins[d][0][0], ins[d][1][0], ins[d][2][0], fb_ref[...], *states[d])
               for d in range(2)]
    for d, h_ref in enumerate((hf_ref, hb_ref)):
        h_blocks, c_next, m_next = results[d]
        for blk, hb in enumerate(h_blocks):
            h_ref[0, :, blk * HEAD_PAD:(blk + 1) * HEAD_PAD] = hb
        for hd in range(ML_HEADS):
            cs_ref[0, d, rows(hd), :] = c_next[hd]
        ms_ref[0, d] = m_next


def _mlstm(qk, zml, gates, p, c_init, m_init):
    B, T, _ = zml.shape
    L = min(ML_CHUNK, T)
    nc = T // L
    const = lambda shape: pl.BlockSpec(shape, lambda b, j: (0,) * len(shape))
    state = pl.BlockSpec((1, 2, ML_HEADS * HEAD_PAD, HEAD_PAD), lambda b, j: (b, 0, 0, 0))
    mstate = pl.BlockSpec((1, 2, 2 * ML_HEADS, L), lambda b, j: (b, 0, 0, 0))
    fwd = lambda b, j: (b, j, 0)
    bwd = lambda b, j: (b, nc - 1 - j, 0)
    return pl.pallas_call(
        _mlstm_kernel,
        grid=(B, nc),
        in_specs=[pl.BlockSpec((1, L, QK_LANES), fwd),
                  pl.BlockSpec((1, L, ML_WIDTH), lambda b, j: (b, j, 2)),
                  pl.BlockSpec((1, L, LANE), fwd),
                  pl.BlockSpec((1, L, QK_LANES), bwd),
                  pl.BlockSpec((1, L, ML_WIDTH), lambda b, j: (b, nc - 1 - j, 2)),
                  pl.BlockSpec((1, L, LANE), bwd),
                  const((1, LANE)), state, mstate],
        out_specs=[pl.BlockSpec((1, L, ML_WIDTH), fwd), pl.BlockSpec((1, L, ML_WIDTH), bwd), state, mstate],
        out_shape=[jax.ShapeDtypeStruct((B, T, ML_WIDTH), F32), jax.ShapeDtypeStruct((B, T, ML_WIDTH), F32),
                   jax.ShapeDtypeStruct((B, 2, ML_HEADS * HEAD_PAD, HEAD_PAD), F32),
                   jax.ShapeDtypeStruct((B, 2, 2 * ML_HEADS, L), F32)],
        compiler_params=_cparams(("parallel", "arbitrary")),
        name="mlstm",
    )(qk, zml, gates, qk, zml, gates, p["f_bias"], c_init, m_init)


def _outproj_kernel(x_ref, g1_ref, sh2_ref, sc2_ref, ya_ref, yb_ref, h0_ref, h1_ref, o_ref, mg_ref, avg_ref,
                    wo_ref, l1g_ref, l1b_ref, wrh_ref, wrl_ref, x1_ref, hm_ref, aff_ref):
    hs = h0_ref[0] + h1_ref[0]
    mu = _dot2(hs, avg_ref[...])
    hc = hs - mu
    var = _dot2(hc * hc, avg_ref[...])
    yc =(1.0 / (1.0 + jnp.exp(-o_ref[0]))) * (hc * lax.rsqrt(var + LN_EPS) * mg_ref[...])
    y = (_dot(ya_ref[0], wo_ref[0:SG_WIDTH, :])
         + _dot(yb_ref[0], wo_ref[SG_WIDTH:SG_WIDTH + MLA_HEADS * MLA_VDIM, :])
         + _dot(yc.astype(BF16), wo_ref[SG_WIDTH + MLA_HEADS * MLA_VDIM:, :]))
    x1 = _plain_norm(ALPHA * x_ref[0] + g1_ref[0] * y) * l1g_ref[...] + l1b_ref[...]
    x1_ref[0] = x1
    hm = _plain_norm(x1) * (1.0 + sc2_ref[0]) + sh2_ref[0]
    hm_ref[0] = hm.astype(BF16)
    hi, lo = _split(hm)
    logits = _dot(hi, wrh_ref[...]) + (_dot(lo, wrh_ref[...]) + _dot(hi, wrl_ref[...]))
    logits = logits.T[:N_EXPERTS, :]
    ex =jnp.exp(logits - jnp.max(logits, axis=0, keepdims=True))
    aff_ref[0] = ex / jnp.sum(ex, axis=0, keepdims=True)


def _outproj(x, g1, sh2, sc2, ya, yb, h_fwd, h_bwd, zml, p, tm):
    B, T, D = x.shape
    const = lambda shape: pl.BlockSpec(shape, lambda b, i: (0,) * len(shape))
    vec = pl.BlockSpec((1, 1, D), lambda b, i: (b, 0, 0))
    return pl.pallas_call(
        _outproj_kernel,
        grid=(B, T // tm),
        in_specs=[pl.BlockSpec((1, tm, D), lambda b, i: (b, i, 0)), vec, vec, vec,
                  pl.BlockSpec((1, tm, SG_WIDTH), lambda b, i: (b, i, 0)),
                  pl.BlockSpec((1, tm, MLA_HEADS * MLA_VDIM), lambda b, i: (b, i, 0)),
                  pl.BlockSpec((1, tm, ML_WIDTH), lambda b, i: (b, i, 0)),
                  pl.BlockSpec((1, tm, ML_WIDTH), lambda b, i: (b, i, 0)),
                  pl.BlockSpec((1, tm, ML_WIDTH), lambda b, i: (b, i, 3)),
                  const((1, ML_WIDTH)), const((ML_WIDTH, ML_WIDTH)), const((D, D)), const((1, D)), const((1, D)),
                  const((D, LANE)), const((D, LANE))],
        out_specs=[pl.BlockSpec((1, tm, D), lambda b, i: (b, i, 0)),
                   pl.BlockSpec((1, tm, D), lambda b, i: (b, i, 0)),
                   pl.BlockSpec((1, N_EXPERTS, tm), lambda b, i: (b, 0, i))],
        out_shape=[jax.ShapeDtypeStruct((B, T, D), F32), jax.ShapeDtypeStruct((B, T, D), BF16),
                   jax.ShapeDtypeStruct((B, N_EXPERTS, T), F32)],
        compiler_params=_cparams(("parallel", "parallel")),
        name="outproj",
    )(x, g1, sh2, sc2, ya, yb, h_fwd, h_bwd, zml, p["ml_norm_g"], p["head_avg"], p["w_out"], p["ln1_g"], p["ln1_b"],
      p["w_router_hi"], p["w_router_lo"])


def _route_kernel(aff_ref, pos_ref, cnt_ref, *, cap, tw):
    aff = aff_ref[0]
    E, T = aff.shape
    bits = pltpu.bitcast(aff, jnp.int32)
    count = lambda m: jnp.sum(jnp.where(m, 1.0, 0.0), axis=1, keepdims=True)
    thr = jnp.zeros((E, 1), jnp.int32)
    for bit in range(30, -1, -1):
        cand = thr | (1 << bit)
        thr = jnp.where(count(bits >= cand) >= cap, cand, thr)
    above = bits > thr
    tied = bits == thr
    need = cap - count(above)
    idx = lax.broadcasted_iota(jnp.int32, (E, T), 1)
    cut = jnp.zeros((E, 1), jnp.int32)
    for bit in range(T.bit_length() - 1, -1, -1):
        cand = cut | (1 << bit)
        ok = (cand <= T) & (count(tied & (idx < cand)) <= need)
        cut = jnp.where(ok, cand, cut)
    sel = above | (tied & (idx < cut))
    r = lax.broadcasted_iota(jnp.int32, (tw, tw), 0)
    cidx = lax.broadcasted_iota(jnp.int32, (tw, tw), 1)
    before = jnp.where(r < cidx, 1.0, 0.0).astype(BF16)
    offset = jnp.zeros((E, 1), F32)
    lane = lax.broadcasted_iota(jnp.int32, (E, LANE), 1)
    table = jnp.zeros((E, LANE), F32)
    for blk in range(T // tw):
        cols = slice(blk * tw, (blk + 1) * tw)
        sb = jnp.where(sel[:, cols], 1.0, 0.0)
        slot = _dot(sb.astype(BF16), before) + offset
        pos_ref[0, :, cols] = jnp.where(sel[:, cols], slot, -1.0).astype(jnp.int32)
        offset = offset + jnp.sum(sb, axis=1, keepdims=True)
        table = jnp.where(lane == blk + 1, offset, table)
    cnt_ref[0] = table.astype(jnp.int32)


def _route(aff_t, cap):
    B, E, T = aff_t.shape
    nt = T // ROUTE_TILE
    pos, cnt = pl.pallas_call(
        functools.partial(_route_kernel, cap=cap, tw=ROUTE_TILE),
        grid=(B,),
        in_specs=[pl.BlockSpec((1, E, T), lambda b: (b, 0, 0))],
        out_specs=[pl.BlockSpec((1, E, T), lambda b: (b, 0, 0)), pl.BlockSpec((1, E, LANE), lambda b: (b, 0, 0))],
        out_shape=[jax.ShapeDtypeStruct((B, E, T), jnp.int32), jax.ShapeDtypeStruct((B, E, LANE), jnp.int32)],
        compiler_params=_cparams(("parallel",)),
        name="route",
    )(aff_t)
    return pos, cnt[:, :, :nt + 1].reshape(-1)


def _expert_kernel(cnt_ref, pos_ref, aff_ref, h_ref, wg_ref, wu_ref, wd_ref, y_ref, xe_ref, gate_ref, *,
                   n_win, expert_major):
    cap = y_ref.shape[2]
    nt = pos_ref.shape[2]
    rb = min(GATHER_ROWS, cap)
    i0, i1 = pl.program_id(0), pl.program_id(1)
    b, e = (i1, i0) if expert_major else (i0, i1)
    base = (b * N_EXPERTS + e) * (nt + 1)
    slot_iota = lax.broadcasted_iota(jnp.int32, (rb, ROUTE_TILE), 0)

    def gather_tile(i, s0):
        hit = pos_ref[0, 0, pl.ds(i, 1), :] == slot_iota + s0
        rows = pl.ds(pl.multiple_of(i * ROUTE_TILE, ROUTE_TILE), ROUTE_TILE)
        xe = _dot(jnp.where(hit, 1.0, 0.0).astype(BF16), h_ref[0, rows, :])
        gate = jnp.sum(jnp.where(hit, aff_ref[0, 0, pl.ds(i, 1), :], 0.0), axis=1, keepdims=True)
        return xe, gate

    for r in range(cap // rb):
        s0 = r * rb
        first = sum((cnt_ref[base + i + 1] <= s0).astype(jnp.int32) for i in range(nt))
        end = sum((cnt_ref[base + i] < s0 + rb).astype(jnp.int32) for i in range(nt))
        start = jnp.minimum(first, nt - n_win)
        parts = [gather_tile(start + w, s0) for w in range(n_win)]
        rows = slice(s0, s0 + rb)
        xe_ref[rows, :] = sum(pt[0] for pt in parts)
        gate_ref[rows, :] = sum(pt[1] for pt in parts)

        def overflow(i, carry, s0=s0, rows=rows):
            xe, gate = gather_tile(i, s0)
            xe_ref[rows, :] += xe
            gate_ref[rows, :] += gate
            return carry

        lax.fori_loop(start + n_win, end, overflow, 0)
    xb = xe_ref[...].astype(BF16)
    hid = _silu(_dot(xb, wg_ref[0])) * _dot(xb, wu_ref[0])
    y_ref[0, 0] = (_dot(hid.astype(BF16), wd_ref[0]) * gate_ref[...]).astype(BF16)


def _experts(pos, cnt, aff_t, hm, p, cap):
    B, T, D = hm.shape
    E = N_EXPERTS
    nt = T // ROUTE_TILE
    expert_major = T * D < 3 * D * EXPERT_FF
    be = (lambda i, j: (j, i)) if expert_major else (lambda i, j: (i, j))
    row = pl.BlockSpec((1, 1, nt, ROUTE_TILE), lambda i, j, c: be(i, j) + (0, 0))
    wspec = lambda shape: pl.BlockSpec(shape, lambda i, j, c: (be(i, j)[1], 0, 0))
    return pl.pallas_call(
        functools.partial(_expert_kernel, n_win=min(GATHER_TILES, nt), expert_major=expert_major),
        grid_spec=pltpu.PrefetchScalarGridSpec(
            num_scalar_prefetch=1,
            grid=(E, B) if expert_major else (B, E),
            in_specs=[row, row,
                      pl.BlockSpec((1, T, D), lambda i, j, c: (be(i, j)[0], 0, 0)),
                      wspec((1, D, EXPERT_FF)), wspec((1, D, EXPERT_FF)), wspec((1, EXPERT_FF, D))],
            out_specs=pl.BlockSpec((1, 1, cap, D), lambda i, j, c: be(i, j) + (0, 0)),
            scratch_shapes=[pltpu.VMEM((cap, D), F32), pltpu.VMEM((cap, 1), F32)]),
        out_shape=jax.ShapeDtypeStruct((B, E, cap, D), BF16),
        compiler_params=_cparams(("parallel", "arbitrary")),
        name="experts",
    )(cnt, pos.reshape(B, E, nt, ROUTE_TILE), aff_t.reshape(B, E, nt, ROUTE_TILE), hm,
      p["w_gate"], p["w_up"], p["w_down"])


def _combine_kernel(cnt_ref, pos_ref, y_ref, x_ref, g2_ref, lg_ref, lb_ref, o_ref, ycat_ref, acc_ref, *, nt):
    E, cap = y_ref.shape[1], y_ref.shape[2]
    tt = x_ref.shape[1]
    win = min(COMBINE_WIN, cap)
    per = tt // ROUTE_TILE
    b, i = pl.program_id(0), pl.program_id(1)
    slot_iota = lax.broadcasted_iota(jnp.int32, (win, tt), 0)

    def window(e, w):
        start = pl.multiple_of(jnp.minimum(w, cap - win), 16)
        slot = slot_iota + start
        hit = (pos_ref[0, e:e + 1, :] == slot) & (slot >= w)
        return jnp.where(hit, 1.0, 0.0).astype(BF16), y_ref[0, e, pl.ds(start, win), :]

    onehots, ranges = [], []
    for e in range(E):
        base = (b * E + e) * (nt + 1) + i * per
        lo, hi = cnt_ref[base], cnt_ref[base + per]
        w0 = jnp.bitwise_and(lo, ~15)
        onehot, rows = window(e, w0)
        ycat_ref[e * win:(e + 1) * win, :] = rows
        onehots.append(onehot)
        ranges.append((w0, hi))
    acc_ref[...] = _tn(jnp.concatenate(onehots, axis=0), ycat_ref[...])
    for e in range(E):
        w0, hi = ranges[e]

        def more(w, e=e):
            onehot, rows = window(e, w)
            acc_ref[...] += _tn(onehot, rows)
            return w + win

        lax.while_loop(lambda w, hi=hi: w < hi, more, w0 + win)
    o_ref[0] = _plain_norm(ALPHA * x_ref[0] + g2_ref[0] * acc_ref[...]) * lg_ref[...] + lb_ref[...]


def _combine(pos, cnt, ye, x1, g2, p, tt):
    B, T, D = x1.shape
    E, cap = ye.shape[1], ye.shape[2]
    win = min(COMBINE_WIN, cap)
    const = lambda shape: pl.BlockSpec(shape, lambda b, i, c: (0,) * len(shape))
    return pl.pallas_call(
        functools.partial(_combine_kernel, nt=T // ROUTE_TILE),
        grid_spec=pltpu.PrefetchScalarGridSpec(
            num_scalar_prefetch=1,
            grid=(B, T // tt),
            in_specs=[pl.BlockSpec((1, E, tt), lambda b, i, c: (b, 0, i)),
                      pl.BlockSpec((1, E, cap, D), lambda b, i, c: (b, 0, 0, 0)),
                      pl.BlockSpec((1, tt, D), lambda b, i, c: (b, i, 0)),
                      pl.BlockSpec((1, 1, D), lambda b, i, c: (b, 0, 0)),
                      const((1, D)), const((1, D))],
            out_specs=pl.BlockSpec((1, tt, D), lambda b, i, c: (b, i, 0)),
            scratch_shapes=[pltpu.VMEM((E * win, D), BF16), pltpu.VMEM((tt, D), F32)]),
        out_shape=jax.ShapeDtypeStruct((B, T, D), F32),
        compiler_params=_cparams(("parallel", "arbitrary")),
        name="combine",
    )(cnt, pos, ye, x1, g2, p["ln2_g"], p["ln2_b"])


def _rot_cols(a):
    a4 = a.reshape(a.shape[:-1] + (2, 2, ROPE_AXIS // 2))
    return jnp.stack([-a4[..., 1, :], a4[..., 0, :]], axis=-2).reshape(a.shape)


def _pad_cols(a, before, total):
    pads = [(0, 0)] * (a.ndim - 1) + [(before, total - before - a.shape[-1])]
    return jnp.pad(a, pads)


def _prep_layer(l, w):
    D = D_MODEL
    wi, bi = w["w_in"][l], w["b_in"][l][None, :]
    n_sg, n_mla = 2 * SG_WIDTH, Q_LORA + KV_LORA + MLA_ROPE
    o_kr = n_sg + Q_LORA + KV_LORA
    o_ml = n_sg + n_mla
    o_g = o_ml + 4 * ML_WIDTH

    def cat(a):
        kr = a[:, o_kr:o_kr + MLA_ROPE]
        gates = a[:, o_g:o_g + 4 * ML_HEADS].reshape(-1, 2, 2, ML_HEADS).transpose(0, 2, 1, 3).reshape(-1, 4 * ML_HEADS)
        return jnp.concatenate([a[:, :o_kr], _pad_cols(kr, MLA_NOPE, HEAD_PAD), _pad_cols(_rot_cols(kr), MLA_NOPE, HEAD_PAD),
                                a[:, o_ml:o_g], _pad_cols(gates, 0, LANE)], axis=1)

    wq = w["w_uq"][l].reshape(Q_LORA, MLA_HEADS, MLA_NOPE + MLA_ROPE)
    wq_main = _pad_cols(wq, 0, HEAD_PAD).reshape(Q_LORA, -1)
    wq_rot = _pad_cols(_rot_cols(wq[..., MLA_NOPE:]), MLA_NOPE, HEAD_PAD).reshape(Q_LORA, -1)
    wkv = w["w_ukv"][l].reshape(KV_LORA, MLA_HEADS, MLA_NOPE + MLA_VDIM)
    wk = _pad_cols(wkv[..., :MLA_NOPE], 0, HEAD_PAD).reshape(KV_LORA, -1)
    wv = _pad_cols(wkv[..., MLA_NOPE:], 0, HEAD_PAD).reshape(KV_LORA, -1)
    head = jnp.arange(ML_WIDTH) // ML_DIM
    wr = _pad_cols(w["w_router"][l], 0, LANE)
    wr_hi = wr.astype(BF16)
    return {
        "wcat": cat(wi).astype(BF16), "bcat": cat(bi),
        "q_norm_g": w["q_norm_g"][l][None, :], "kv_norm_g": w["kv_norm_g"][l][None, :],
        "wq": jnp.concatenate([wq_main, wq_rot], axis=1).astype(BF16),
        "wkv": jnp.concatenate([wk, wv], axis=1).astype(BF16),
        "sg_ln_g": w["sg_ln_g"][l][None, :], "sg_ln_b": w["sg_ln_b"][l][None, :],
        "sg_w": w["sg_w"][l].astype(BF16),
        "sg_bias": jnp.repeat(w["sg_b"][l].T, SG_GDIM, axis=1),
        "conv_w": jnp.pad(w["ml_conv_w"][l], ((0, 8 - ML_CONV), (0, 0))), "conv_b": w["ml_conv_b"][l][None, :],
        "f_bias": _pad_cols(w["ml_f_bias"][l].reshape(1, 2 * ML_HEADS), 0, LANE),
        "ml_norm_g": w["ml_norm_g"][l][None, :],
        "head_avg": ((head[:, None] == head[None, :]).astype(F32) / ML_DIM).astype(BF16),
        "w_out": w["w_out"][l].astype(BF16),
        "ln1_g": w["ln1_g"][l][None, :], "ln1_b": w["ln1_b"][l][None, :],
        "w_router_hi": wr_hi, "w_router_lo": (wr - wr_hi.astype(F32)).astype(BF16),
        "w_gate": w["w_gate"][l].astype(BF16), "w_up": w["w_up"][l].astype(BF16), "w_down": w["w_down"][l].astype(BF16),
        "ln2_g": w["ln2_g"][l][None, :], "ln2_b": w["ln2_b"][l][None, :],
    }


def _rope_tables(T):
    rows = T // GRID_W
    row = jnp.repeat(jnp.arange(rows, dtype=F32), GRID_W)
    colv = jnp.tile(jnp.arange(GRID_W, dtype=F32), rows)
    inv = ROPE_BASE ** (-jnp.arange(ROPE_AXIS // 2, dtype=F32) * 2.0 / ROPE_AXIS)
    ang = jnp.concatenate([row[:, None] * inv, colv[:, None] * inv], axis=-1)
    half = ROPE_AXIS // 2
    spread = lambda t: jnp.concatenate([t[:, :half], t[:, :half], t[:, half:], t[:, half:]], axis=1)
    cosT = jnp.concatenate([jnp.ones((T, MLA_NOPE), F32), spread(jnp.cos(ang)), jnp.zeros((T, 32), F32)], axis=1)
    sinT = _pad_cols(spread(jnp.sin(ang)), MLA_NOPE, HEAD_PAD)
    return cosT, sinT


def _moe(x1, hm, aff_t, g2, p, tt):
    cap = EC_CAPACITY * x1.shape[1] // N_EXPERTS
    pos, cnt = _route(aff_t, cap)
    ye = _experts(pos, cnt, aff_t, hm, p, cap)
    return _combine(pos, cnt, ye, x1, g2, p, tt)


def _forward(x, c, ctx, c_ctx, w):
    B, T, D = x.shape
    Tc = ctx.shape[1]
    L = w["w_ada"].shape[0]
    cc = jnp.zeros((16, D), F32).at[:B].set(c).at[B].set(c_ctx)
    ada = _ada(cc, w["w_ada"], w["b_ada"])
    cos_l, sin_l = _rope_tables(T)
    cos_c = _pad_cols(jnp.ones((Tc, MLA_NOPE + MLA_ROPE), F32), 0, HEAD_PAD)
    sin_c = jnp.zeros((Tc, HEAD_PAD), F32)
    zero_c = jnp.zeros((B, 2, ML_HEADS * HEAD_PAD, HEAD_PAD), F32)
    zero_m = jnp.zeros((B, 2, 2 * ML_HEADS, min(ML_CHUNK, Tc)), F32)
    tm = min(512, T)
    xl, xc = x, ctx
    for l in range(L):
        p = _prep_layer(l, w)
        mods_l = [ada[l, :B, i * D:(i + 1) * D][:, None, :] for i in range(6)]
        mods_c = [jnp.broadcast_to(ada[l, B, i * D:(i + 1) * D][None, None, :], (B, 1, D)) for i in range(6)]
        update_ctx = l < L - 1
        ya_c, q_c, k_c, v_c, zml_c, g_c = _inproj(xc, mods_c[0], mods_c[1], cos_c, sin_c, p, min(tm, Tc))
        ya_l, q_l, k_l, v_l, zml_l, g_l = _inproj(xl, mods_l[0], mods_l[1], cos_l, sin_l, p, tm)
        hf_c, hb_c, cs_c, ms_c = _mlstm(_qkconv(zml_c, p, min(512, Tc)), zml_c, g_c, p, zero_c, zero_m)
        hf_l, hb_l, _, _ = _mlstm(_qkconv(zml_l, p, min(512, T)), zml_l, g_l, p, cs_c, ms_c)
        yb_l = _attention(q_l, [(k_l, v_l), (k_c, v_c)], tq=min(512, T))
        x1_l, hm_l, aff_l = _outproj(xl, mods_l[2], mods_l[3], mods_l[4], ya_l, yb_l, hf_l, hb_l, zml_l, p, tm)
        xl = _moe(x1_l, hm_l, aff_l, mods_l[5], p, tt=min(512, T))
        if update_ctx:
            yb_c = _attention(q_c, [(k_c, v_c)], tq=min(256, Tc))
            x1_c, hm_c, aff_c = _outproj(xc, mods_c[2], mods_c[3], mods_c[4], ya_c, yb_c, hf_c, hb_c, zml_c, p,
                                         min(tm, Tc))
            xc = _moe(x1_c, hm_c, aff_c, mods_c[5], p, tt=min(512, Tc))
    return xl


def kernel(x, c, ctx, c_ctx, w_ada, b_ada, w_in, b_in, sg_ln_g, sg_ln_b, sg_w, sg_b, q_norm_g, kv_norm_g, w_uq, w_ukv,
           ml_conv_w, ml_conv_b, ml_f_bias, ml_norm_g, w_out, ln1_g, ln1_b, w_router, w_gate, w_up, w_down, ln2_g, ln2_b):
    w = dict(w_ada=w_ada, b_ada=b_ada, w_in=w_in, b_in=b_in, sg_ln_g=sg_ln_g, sg_ln_b=sg_ln_b, sg_w=sg_w, sg_b=sg_b,
             q_norm_g=q_norm_g, kv_norm_g=kv_norm_g, w_uq=w_uq, w_ukv=w_ukv, ml_conv_w=ml_conv_w, ml_conv_b=ml_conv_b,
             ml_f_bias=ml_f_bias, ml_norm_g=ml_norm_g, w_out=w_out, ln1_g=ln1_g, ln1_b=ln1_b, w_router=w_router,
             w_gate=w_gate, w_up=w_up, w_down=w_down, ln2_g=ln2_g, ln2_b=ln2_b)
    return _forward(x, c, ctx, c_ctx, w)
```

```python
import functools

import jax
import jax.numpy as jnp
from jax import lax
from jax.experimental import pallas as pl
from jax.experimental.pallas import tpu as pltpu

F32 = jnp.float32
BF16 = jnp.bfloat16
HIGHEST = lax.Precision.HIGHEST

D_MODEL = 1024
DEPTH = 2
GRID_W = 64
SG_WIDTH = 256
SG_GROUPS = 4
SG_GDIM = 64
SG_CHUNK = 128
MLA_HEADS = 8
MLA_NOPE = 64
MLA_ROPE = 32
MLA_VDIM = 64
Q_LORA = 384
KV_LORA = 256
ROPE_AXIS = 16
ROPE_BASE = 10000.0
MLA_SCALE = (MLA_NOPE + MLA_ROPE) ** -0.5
ML_HEADS = 4
ML_DIM = 64
ML_WIDTH = 256
ML_CONV = 5
N_EXPERTS = 16
EXPERT_FF = 1024
EC_CAPACITY = 2
ALPHA = (2 * DEPTH) ** 0.25
LN_EPS = 1e-6
LOG2E = 1.4426950408889634

LANE = 128
HEAD_PAD = 128
ML_CHUNK = 256
OFF_SG, OFF_CQ, OFF_CKV, OFF_KR, OFF_KRR, OFF_ML, OFF_G, N_CAT = 0, 512, 896, 1152, 1280, 1408, 2432, 2560
VMEM_LIMIT = 56 * 1024 * 1024
ROUTE_TILE = 256
GATHER_ROWS = 128
GATHER_TILES = 6
COMBINE_WIN = 128
QK_LANES =ML_HEADS * HEAD_PAD + ML_WIDTH
ATTN_HEADS = 2
ATTN_KEYS = 2048


def _cparams(sem):
    return pltpu.CompilerParams(dimension_semantics=sem, vmem_limit_bytes=VMEM_LIMIT)


def _nt(a, b):
    return lax.dot_general(a, b, (((1,), (1,)), ((), ())), preferred_element_type=F32)


def _tn(a, b):
    return lax.dot_general(a, b, (((0,), (0,)), ((), ())), preferred_element_type=F32)


def _dot(a, b, precision=None):
    return jnp.dot(a, b, preferred_element_type=F32, precision=precision)


def _split(x):
    hi = x.astype(BF16)
    return hi, (x - hi.astype(F32)).astype(BF16)


def _dot2(x, w):
    hi, lo = _split(x)
    return _dot(jnp.concatenate([hi, lo], axis=1), jnp.concatenate([w, w], axis=0))


def _plain_norm(x):
    mu = jnp.mean(x, axis=-1, keepdims=True)
    xc = x - mu
    var = jnp.mean(xc * xc, axis=-1, keepdims=True)
    return xc * lax.rsqrt(var + LN_EPS)


def _rms(x):
    return x * lax.rsqrt(jnp.mean(x * x, axis=-1, keepdims=True) + LN_EPS)


def _silu(x):
    return x * (1.0 / (1.0 + jnp.exp(-x)))


def _gelu_tanh(x):
    return 0.5 * x * (1.0 + jnp.tanh(0.7978845608028654 * (x + 0.044715 * (x * x * x))))


def _log_sigmoid(x):
    return jnp.minimum(x, 0.0) - jnp.log(1.0 + jnp.exp(-jnp.abs(x)))


def _ada_kernel(c_ref, w_ref, b_ref, o_ref):
    o_ref[0] = _dot(_silu(c_ref[...]), w_ref[0], precision=HIGHEST) + b_ref[0]


def _ada(cc, w_ada, b_ada):
    L, D, N = w_ada.shape
    tn = 1536
    return pl.pallas_call(
        _ada_kernel,
        grid=(L, N // tn),
        in_specs=[pl.BlockSpec((16, D), lambda l, j: (0, 0)),
                  pl.BlockSpec((1, D, tn), lambda l, j: (l, 0, j)),
                  pl.BlockSpec((1, 1, tn), lambda l, j: (l, 0, j))],
        out_specs=pl.BlockSpec((1, 16, tn), lambda l, j: (l, 0, j)),
        out_shape=jax.ShapeDtypeStruct((L, 16, N), F32),
        compiler_params=_cparams(("parallel", "parallel")),
        name="ada",
    )(cc, w_ada, b_ada.reshape(L, 1, N))


def _inproj_kernel(x_ref, sh_ref, sc_ref, cos_ref, sin_ref, wcat_ref, bcat_ref, qg_ref, kvg_ref, wq_ref, wkv_ref,
                   sglg_ref, sglb_ref, sgw_ref, sgb_ref,
                   ya_ref, q_ref, k_ref, v_ref, zml_ref, g_ref):
    tm = x_ref.shape[1]
    h = _plain_norm(x_ref[0]) * (1.0 + sc_ref[0]) + sh_ref[0]
    z = _dot(h.astype(BF16), wcat_ref[...]) + bcat_ref[...]
    zs = _gelu_tanh(z[:, OFF_SG:OFF_SG + 2 * SG_WIDTH])
    u = zs[:, :SG_WIDTH]
    vn = _plain_norm(zs[:, SG_WIDTH:]) * sglg_ref[...] + sglb_ref[...]
    group = jnp.right_shift(lax.broadcasted_iota(jnp.int32, (SG_CHUNK, SG_WIDTH), 1), SG_GDIM.bit_length() - 1)
    for ch in range(tm // SG_CHUNK):
        rows = slice(ch * SG_CHUNK, (ch + 1) * SG_CHUNK)
        vc = vn[rows].astype(BF16)
        mixed = sgb_ref[...]
        for g in range(SG_GROUPS):
            mixed = mixed + jnp.where(group == g, _dot(sgw_ref[g], vc), 0.0)
        ya_ref[0, rows, :] = (u[rows] * mixed).astype(BF16)
    cosT = cos_ref[...]
    sinT = sin_ref[...]
    cqn = (_rms(z[:, OFF_CQ:OFF_CQ + Q_LORA]) * qg_ref[...]).astype(BF16)
    qq = _dot(cqn, wq_ref[...])
    ckvn = (_rms(z[:, OFF_CKV:OFF_CKV + KV_LORA]) * kvg_ref[...]).astype(BF16)
    kv = _dot(ckvn, wkv_ref[...])
    kr = z[:, OFF_KR:OFF_KR + HEAD_PAD] * cosT + z[:, OFF_KRR:OFF_KRR + HEAD_PAD] * sinT
    one_lane = jnp.where(lax.broadcasted_iota(jnp.int32, (1, HEAD_PAD), 1) == MLA_VDIM, 1.0, 0.0)
    hw = MLA_HEADS * HEAD_PAD
    for hd in range(MLA_HEADS):
        cols = slice(hd * HEAD_PAD, (hd + 1) * HEAD_PAD)
        rcols = slice(hw + hd * HEAD_PAD, hw + (hd + 1) * HEAD_PAD)
        q_ref[0, hd] = ((qq[:, cols] * cosT + qq[:, rcols] * sinT) * (MLA_SCALE * LOG2E)).astype(BF16)
        k_ref[0, hd] = (kv[:, cols] + kr).astype(BF16)
        v_ref[0, hd] = (kv[:, rcols] + one_lane).astype(BF16)
    zml_ref[0] = z[:, OFF_ML:OFF_ML + 4 * ML_WIDTH]
    g_ref[0] = z[:, OFF_G:OFF_G + LANE]


def _inproj(x, shift, scale, cosT, sinT, p, tm):
    B, T, D = x.shape
    const = lambda shape: pl.BlockSpec(shape, lambda b, i: (0,) * len(shape))
    heads = (B, MLA_HEADS, T, HEAD_PAD)
    return pl.pallas_call(
        _inproj_kernel,
        grid=(B, T // tm),
        in_specs=[pl.BlockSpec((1, tm, D), lambda b, i: (b, i, 0)),
                  pl.BlockSpec((1, 1, D), lambda b, i: (b, 0, 0)),
                  pl.BlockSpec((1, 1, D), lambda b, i: (b, 0, 0)),
                  pl.BlockSpec((tm, HEAD_PAD), lambda b, i: (i, 0)),
                  pl.BlockSpec((tm, HEAD_PAD), lambda b, i: (i, 0)),
                  const((D, N_CAT)), const((1, N_CAT)), const((1, Q_LORA)), const((1, KV_LORA)),
                  const((Q_LORA, 2 * MLA_HEADS * HEAD_PAD)), const((KV_LORA, 2 * MLA_HEADS * HEAD_PAD)),
                  const((1, SG_WIDTH)), const((1, SG_WIDTH)), const((SG_GROUPS, SG_CHUNK, SG_CHUNK)),
                  const((SG_CHUNK, SG_WIDTH))],
        out_specs=[pl.BlockSpec((1, tm, SG_WIDTH), lambda b, i: (b, i, 0)),
                   pl.BlockSpec((1, MLA_HEADS, tm, HEAD_PAD), lambda b, i: (b, 0, i, 0)),
                   pl.BlockSpec((1, MLA_HEADS, tm, HEAD_PAD), lambda b, i: (b, 0, i, 0)),
                   pl.BlockSpec((1, MLA_HEADS, tm, HEAD_PAD), lambda b, i: (b, 0, i, 0)),
                   pl.BlockSpec((1, tm, 4 * ML_WIDTH), lambda b, i: (b, i, 0)),
                   pl.BlockSpec((1, tm, LANE), lambda b, i: (b, i, 0))],
        out_shape=[jax.ShapeDtypeStruct((B, T, SG_WIDTH), BF16),
                   jax.ShapeDtypeStruct(heads, BF16), jax.ShapeDtypeStruct(heads, BF16),
                   jax.ShapeDtypeStruct(heads, BF16),
                   jax.ShapeDtypeStruct((B, T, 4 * ML_WIDTH), F32),
                   jax.ShapeDtypeStruct((B, T, LANE), F32)],
        compiler_params=_cparams(("parallel", "parallel")),
        name="inproj",
    )(x, shift, scale, cosT, sinT, p["wcat"], p["bcat"], p["q_norm_g"], p["kv_norm_g"], p["wq"], p["wkv"],
      p["sg_ln_g"], p["sg_ln_b"], p["sg_w"], p["sg_bias"])


def _attn_kernel(*refs, n_src, tk):
    q_ref = refs[0]
    kv_refs = refs[1:1 + 2 * n_src]
    o_ref = refs[1 + 2 * n_src]
    nh, tq = q_ref.shape[1], q_ref.shape[2]
    qs = [q_ref[0, hh] for hh in range(nh)]
    ms = [jnp.full((tq, 1), -jnp.inf, F32) for _ in range(nh)]
    accs = [jnp.zeros((tq, HEAD_PAD), F32) for _ in range(nh)]
    for s in range(n_src):
        k_ref, v_ref = kv_refs[2 * s], kv_refs[2 * s + 1]
        ck = min(tk, k_ref.shape[2])
        for c in range(k_ref.shape[2] // ck):
            rows = slice(c * ck, (c + 1) * ck)
            for hh in range(nh):
                sc = _nt(qs[hh], k_ref[0, hh, rows, :])
                m_new = jnp.maximum(ms[hh], jnp.max(sc, axis=1, keepdims=True))
                pr = jnp.exp2(sc - m_new)
                accs[hh] = accs[hh] * jnp.exp2(ms[hh] - m_new) + _dot(pr.astype(BF16), v_ref[0, hh, rows, :])
                ms[hh] = m_new
    outs = [accs[hh] / accs[hh][:, MLA_VDIM:MLA_VDIM + 1] for hh in range(nh)]
    lane = lax.broadcasted_iota(jnp.int32, (tq, HEAD_PAD), 1)
    for pair in range(nh // 2):
        both = jnp.where(lane < MLA_VDIM, outs[2 * pair], pltpu.roll(outs[2 * pair + 1], MLA_VDIM, axis=1))
        o_ref[0, :, pair * HEAD_PAD:(pair + 1) * HEAD_PAD] = both.astype(BF16)


def _attention(q, kvs, tq, tk=ATTN_KEYS, nh=ATTN_HEADS):
    B, H, T, _ = q.shape
    in_specs = [pl.BlockSpec((1, nh, tq, HEAD_PAD), lambda b, hp, i: (b, hp, i, 0))]
    args = [q]
    for k, v in kvs:
        K = k.shape[2]
        in_specs += [pl.BlockSpec((1, nh, K, HEAD_PAD), lambda b, hp, i: (b, hp, 0, 0))] * 2
        args += [k, v]
    return pl.pallas_call(
        functools.partial(_attn_kernel, n_src=len(kvs), tk=tk),
        grid=(B, H // nh, T // tq),
        in_specs=in_specs,
        out_specs=pl.BlockSpec((1, tq, nh * MLA_VDIM), lambda b, hp, i: (b, i, hp)),
        out_shape=jax.ShapeDtypeStruct((B, T, H * MLA_VDIM), BF16),
        compiler_params=_cparams(("parallel", "parallel", "parallel")),
        name="attention",
    )(*args)


def _qkconv_kernel(zp_ref, zc_ref, zn_ref, cw_ref, cb_ref, o_ref):
    tc = zc_ref.shape[1]
    i = pl.program_id(1)
    prev = jnp.where(i > 0, zp_ref[0], 0.0)
    nxt = jnp.where(i < pl.num_programs(1) - 1, zn_ref[0], 0.0)
    big = jnp.concatenate([prev, zc_ref[0], nxt], axis=0)
    n_big = tc + 16
    pad = (ML_CONV - 1) // 2
    conv = cb_ref[...]
    for t in range(ML_CONV):
        s = t - pad
        shifted = big if s == 0 else pltpu.roll(big, (-s) % n_big, axis=0)
        conv = conv + shifted[8:8 + tc] * cw_ref[t:t + 1, :]
    qk = _silu(conv)
    lane = lax.broadcasted_iota(jnp.int32, (1, HEAD_PAD), 1)
    for hd in range(ML_HEADS):
        q_blk = qk[:, (hd // 2) * HEAD_PAD:(hd // 2 + 1) * HEAD_PAD]
        in_half = (lane >= ML_DIM) if hd % 2 else (lane < ML_DIM)
        o_ref[0, :, hd * HEAD_PAD:(hd + 1) * HEAD_PAD] = jnp.where(in_half, q_blk, 0.0).astype(BF16)
    o_ref[0, :, ML_HEADS * HEAD_PAD:] = (qk[:, ML_WIDTH:] * (ML_DIM ** -0.5)).astype(BF16)


def _qkconv(zml, p, tc):
    B, T, _ = zml.shape
    hb = tc // 8
    const = lambda shape: pl.BlockSpec(shape, lambda b, i: (0,) * len(shape))
    return pl.pallas_call(
        _qkconv_kernel,
        grid=(B, T // tc),
        in_specs=[pl.BlockSpec((1, 8, 2 * ML_WIDTH), lambda b, i: (b, jnp.maximum(i * hb - 1, 0), 0)),
                  pl.BlockSpec((1, tc, 2 * ML_WIDTH), lambda b, i: (b, i, 0)),
                  pl.BlockSpec((1, 8, 2 * ML_WIDTH), lambda b, i: (b, jnp.minimum((i + 1) * hb, T // 8 - 1), 0)),
                  const((8, 2 * ML_WIDTH)), const((1, 2 * ML_WIDTH))],
        out_specs=pl.BlockSpec((1, tc, QK_LANES), lambda b, i: (b, i, 0)),
        out_shape=jax.ShapeDtypeStruct((B, T, QK_LANES), BF16),
        compiler_params=_cparams(("parallel", "parallel")),
        name="qkconv",
    )(zml, zml, zml, p["conv_w"], p["conv_b"])


def _mlstm_direction(d, qk, vv, gl, fb, c_prev, m_prev):
    L = qk.shape[0]
    lf = _log_sigmoid(pltpu.roll(gl, LANE - 2 * ML_HEADS, axis=1) + fb)
    ks = lax.broadcasted_iota(jnp.int32, (L, L), 0)
    qt = lax.broadcasted_iota(jnp.int32, (L, L), 1)
    visible = (ks >= qt) if d else (ks <= qt)
    tri = jnp.where((qt >= ks) if d else (qt <= ks), 1.0, 0.0).astype(BF16)
    lf_hi, lf_lo = _split(lf)
    bc = _dot(tri, lf_hi) + _dot(tri, lf_lo)
    r_cols = gl - bc
    b8 = bc.T[0:2 * ML_HEADS]
    li8 = gl.T[0:2 * ML_HEADS]
    b_end = b8[:, 0:1] if d else b8[:, L - 1:L]
    gw = b_end - b8 + li8
    m_loc = jnp.max(gw, axis=1, keepdims=True)
    w8 = jnp.exp(gw - m_loc)
    m_next = jnp.maximum(b_end + m_prev, m_loc)
    a_sc = jnp.exp(b_end + m_prev - m_next)
    s_sc = jnp.exp(m_loc - m_next)
    a8 = b8 + m_prev

    srow = lax.broadcasted_iota(jnp.int32, (HEAD_PAD, L), 0)
    h_blocks, c_next = [], []
    for blk in range(ML_HEADS // 2):
        km = qk[:, (ML_HEADS + blk) * HEAD_PAD:(ML_HEADS + blk + 1) * HEAD_PAD]
        v_t = vv[:, blk * HEAD_PAD:(blk + 1) * HEAD_PAD].T
        hh = []
        for half in range(2):
            hd = 2 * blk + half
            g = d * ML_HEADS + hd
            in_half = (srow >= ML_DIM) if half else (srow < ML_DIM)
            den_row = 0 if half else ML_DIM
            qm = qk[:, hd * HEAD_PAD:(hd + 1) * HEAD_PAD]
            vaug_t = jnp.where(in_half, v_t, jnp.where(srow == den_row, 1.0, 0.0))
            dm = jnp.where(visible, r_cols[:, g:g + 1] + b8[g:g + 1, :], -jnp.inf)
            a_row = a8[g:g + 1, :]
            m_t = jnp.maximum(a_row, jnp.max(dm, axis=0, keepdims=True))
            pm = jnp.exp(dm - m_t) * _nt(km, qm)
            c0 = c_prev[hd]
            nd = (jnp.exp(a_row - m_t) * _nt(c0.astype(BF16), qm)
                  + _dot(vaug_t.astype(BF16), pm.astype(BF16)))
            den = nd[den_row:den_row + 1, :]
            hh.append(nd / jnp.maximum(jnp.abs(den), jnp.exp(-m_t)))
            c_loc = _dot((vaug_t * w8[g:g + 1, :]).astype(BF16), km)
            c_next.append(a_sc[g:g + 1, 0:HEAD_PAD] * c0 + s_sc[g:g + 1, 0:HEAD_PAD] * c_loc)
        h_blocks.append(jnp.where(srow < ML_DIM, hh[0], hh[1]).T)
    grow = lax.broadcasted_iota(jnp.int32, (2 * ML_HEADS, L), 0)
    mine = (grow >= d * ML_HEADS) & (grow < (d + 1) * ML_HEADS)
    return h_blocks, c_next, jnp.where(mine, m_next, m_prev)


def _mlstm_kernel(qkf_ref, vf_ref, gf_ref, qkb_ref, vb_ref, gb_ref, fb_ref, ci_ref, mi_ref,
                  hf_ref, hb_ref, cs_ref, ms_ref):
    @pl.when(pl.program_id(1) == 0)
    def _():
        cs_ref[...] = ci_ref[...]
        ms_ref[...] = mi_ref[...]

    rows = lambda hd: slice(hd * HEAD_PAD, (hd + 1) * HEAD_PAD)
    ins = ((qkf_ref, vf_ref, gf_ref), (qkb_ref, vb_ref, gb_ref))
    states = [([cs_ref[0, d, rows(hd), :] for hd in range(ML_HEADS)], ms_ref[0, d]) for d in range(2)]
    results = [_mlstm_direction(d, ins[d][0][0], ins[d][1][0], ins[d][2][0], fb_ref[...], *states[d])
               for d in range(2)]
    for d, h_ref in enumerate((hf_ref, hb_ref)):
        h_blocks, c_next, m_next = results[d]
        for blk, hb in enumerate(h_blocks):
            h_ref[0, :, blk * HEAD_PAD:(blk + 1) * HEAD_PAD] = hb
        for hd in range(ML_HEADS):
            cs_ref[0, d, rows(hd), :] = c_next[hd]
        ms_ref[0, d] = m_next


def _mlstm(qk, zml, gates, p, c_init, m_init):
    B, T, _ = zml.shape
    L = min(ML_CHUNK, T)
    nc = T // L
    const = lambda shape: pl.BlockSpec(shape, lambda b, j: (0,) * len(shape))
    state = pl.BlockSpec((1, 2, ML_HEADS * HEAD_PAD, HEAD_PAD), lambda b, j: (b, 0, 0, 0))
    mstate = pl.BlockSpec((1, 2, 2 * ML_HEADS, L), lambda b, j: (b, 0, 0, 0))
    fwd = lambda b, j: (b, j, 0)
    bwd = lambda b, j: (b, nc - 1 - j, 0)
    return pl.pallas_call(
        _mlstm_kernel,
        grid=(B, nc),
        in_specs=[pl.BlockSpec((1, L, QK_LANES), fwd),
                  pl.BlockSpec((1, L, ML_WIDTH), lambda b, j: (b, j, 2)),
                  pl.BlockSpec((1, L, LANE), fwd),
                  pl.BlockSpec((1, L, QK_LANES), bwd),
                  pl.BlockSpec((1, L, ML_WIDTH), lambda b, j: (b, nc - 1 - j, 2)),
                  pl.BlockSpec((1, L, LANE), bwd),
                  const((1, LANE)), state, mstate],
        out_specs=[pl.BlockSpec((1, L, ML_WIDTH), fwd), pl.BlockSpec((1, L, ML_WIDTH), bwd), state, mstate],
        out_shape=[jax.ShapeDtypeStruct((B, T, ML_WIDTH), F32), jax.ShapeDtypeStruct((B, T, ML_WIDTH), F32),
                   jax.ShapeDtypeStruct((B, 2, ML_HEADS * HEAD_PAD, HEAD_PAD), F32),
                   jax.ShapeDtypeStruct((B, 2, 2 * ML_HEADS, L), F32)],
        compiler_params=_cparams(("parallel", "arbitrary")),
        name="mlstm",
    )(qk, zml, gates, qk, zml, gates, p["f_bias"], c_init, m_init)


def _outproj_kernel(x_ref, g1_ref, sh2_ref, sc2_ref, ya_ref, yb_ref, h0_ref, h1_ref, o_ref, mg_ref, avg_ref,
                    wo_ref, l1g_ref, l1b_ref, wr_ref, x1_ref, hm_ref, aff_ref):
    hs = h0_ref[0] + h1_ref[0]
    mu = _dot2(hs, avg_ref[...])
    hc = hs - mu
    var = _dot2(hc * hc, avg_ref[...])
    yc = (1.0 / (1.0 + jnp.exp(-o_ref[0]))) * (hc * lax.rsqrt(var + LN_EPS) * mg_ref[...])
    y = _dot(jnp.concatenate([ya_ref[0], yb_ref[0], yc.astype(BF16)], axis=1), wo_ref[...])
    x1 = _plain_norm(ALPHA * x_ref[0] + g1_ref[0] * y) * l1g_ref[...] + l1b_ref[...]
    x1_ref[0] = x1
    hm = _plain_norm(x1) * (1.0 + sc2_ref[0]) + sh2_ref[0]
    hm_ref[0] = hm.astype(BF16)
    hi, lo = _split(hm)
    logits = _dot(jnp.concatenate([hi, lo, hi], axis=1), wr_ref[...])
    logits = logits.T[:N_EXPERTS, :]
    ex = jnp.exp(logits - jnp.max(logits, axis=0, keepdims=True))
    aff_ref[0] = ex / jnp.sum(ex, axis=0, keepdims=True)


def _outproj(x, g1, sh2, sc2, ya, yb, h_fwd, h_bwd, zml, p, tm):
    B, T, D = x.shape
    const = lambda shape: pl.BlockSpec(shape, lambda b, i: (0,) * len(shape))
    vec = pl.BlockSpec((1, 1, D), lambda b, i: (b, 0, 0))
    return pl.pallas_call(
        _outproj_kernel,
        grid=(B, T // tm),
        in_specs=[pl.BlockSpec((1, tm, D), lambda b, i: (b, i, 0)), vec, vec, vec,
                  pl.BlockSpec((1, tm, SG_WIDTH), lambda b, i: (b, i, 0)),
                  pl.BlockSpec((1, tm, MLA_HEADS * MLA_VDIM), lambda b, i: (b, i, 0)),
                  pl.BlockSpec((1, tm, ML_WIDTH), lambda b, i: (b, i, 0)),
                  pl.BlockSpec((1, tm, ML_WIDTH), lambda b, i: (b, i, 0)),
                  pl.BlockSpec((1, tm, ML_WIDTH), lambda b, i: (b, i, 3)),
                  const((1, ML_WIDTH)), const((ML_WIDTH, ML_WIDTH)), const((D, D)), const((1, D)), const((1, D)),
                  const((3 * D, LANE))],
        out_specs=[pl.BlockSpec((1, tm, D), lambda b, i: (b, i, 0)),
                   pl.BlockSpec((1, tm, D), lambda b, i: (b, i, 0)),
                   pl.BlockSpec((1, N_EXPERTS, tm), lambda b, i: (b, 0, i))],
        out_shape=[jax.ShapeDtypeStruct((B, T, D), F32), jax.ShapeDtypeStruct((B, T, D), BF16),
                   jax.ShapeDtypeStruct((B, N_EXPERTS, T), F32)],
        compiler_params=_cparams(("parallel", "parallel")),
        name="outproj",
    )(x, g1, sh2, sc2, ya, yb, h_fwd, h_bwd, zml, p["ml_norm_g"], p["head_avg"], p["w_out"], p["ln1_g"], p["ln1_b"],
      p["w_router3"])


def _route_kernel(aff_ref, pos_ref, cnt_ref, *, cap, tw):
    aff = aff_ref[0]
    E, T = aff.shape
    bits = pltpu.bitcast(aff, jnp.int32)
    count = lambda m: jnp.sum(jnp.where(m, 1.0, 0.0), axis=1, keepdims=True)
    thr = jnp.zeros((E, 1), jnp.int32)
    for bit in range(30, -1, -1):
        cand = thr | (1 << bit)
        thr = jnp.where(count(bits >= cand) >= cap, cand, thr)
    above = bits > thr
    tied = bits == thr
    need = cap - count(above)
    idx = lax.broadcasted_iota(jnp.int32, (E, T), 1)
    cut = jnp.zeros((E, 1), jnp.int32)
    for bit in range(T.bit_length() - 1, -1, -1):
        cand = cut | (1 << bit)
        ok = (cand <= T) & (count(tied & (idx < cand)) <= need)
        cut = jnp.where(ok, cand, cut)
    sel = above | (tied & (idx < cut))
    r = lax.broadcasted_iota(jnp.int32, (tw, tw), 0)
    cidx = lax.broadcasted_iota(jnp.int32, (tw, tw), 1)
    before = jnp.where(r < cidx, 1.0, 0.0).astype(BF16)
    offset = jnp.zeros((E, 1), F32)
    lane = lax.broadcasted_iota(jnp.int32, (E, LANE), 1)
    table = jnp.zeros((E, LANE), F32)
    for blk in range(T // tw):
        cols = slice(blk * tw, (blk + 1) * tw)
        sb = jnp.where(sel[:, cols], 1.0, 0.0)
        slot = _dot(sb.astype(BF16), before) + offset
        pos_ref[0, :, cols] = jnp.where(sel[:, cols], slot, -1.0).astype(jnp.int32)
        offset = offset + jnp.sum(sb, axis=1, keepdims=True)
        table = jnp.where(lane == blk + 1, offset, table)
    cnt_ref[0] = table.astype(jnp.int32)


def _route(aff_t, cap):
    B, E, T = aff_t.shape
    nt = T // ROUTE_TILE
    pos, cnt = pl.pallas_call(
        functools.partial(_route_kernel, cap=cap, tw=ROUTE_TILE),
        grid=(B,),
        in_specs=[pl.BlockSpec((1, E, T), lambda b: (b, 0, 0))],
        out_specs=[pl.BlockSpec((1, E, T), lambda b: (b, 0, 0)), pl.BlockSpec((1, E, LANE), lambda b: (b, 0, 0))],
        out_shape=[jax.ShapeDtypeStruct((B, E, T), jnp.int32), jax.ShapeDtypeStruct((B, E, LANE), jnp.int32)],
        compiler_params=_cparams(("parallel",)),
        name="route",
    )(aff_t)
    return pos, cnt[:, :, :nt + 1].reshape(-1)


def _expert_kernel(cnt_ref, pos_ref, aff_ref, h_ref, wg_ref, wu_ref, wd_ref, y_ref, xe_ref, gate_ref,
                   wgb_ref, wub_ref, wdb_ref, *, n_win):
    cap = y_ref.shape[2]
    nt = pos_ref.shape[2]
    rb = min(GATHER_ROWS, cap)
    e, b = pl.program_id(0), pl.program_id(1)

    @pl.when(b == 0)
    def _():
        wgb_ref[...] = wg_ref[0, 0].astype(BF16)
        wub_ref[...] = wu_ref[0, 0].astype(BF16)
        wdb_ref[...] = wd_ref[0, 0].astype(BF16)

    base = (b * N_EXPERTS + e) * (nt + 1)
    slot_iota = lax.broadcasted_iota(jnp.int32, (rb, ROUTE_TILE), 0)

    def gather_tile(i, s0):
        hit = pos_ref[0, 0, pl.ds(i, 1), :] == slot_iota + s0
        rows = pl.ds(pl.multiple_of(i * ROUTE_TILE, ROUTE_TILE), ROUTE_TILE)
        xe = _dot(jnp.where(hit, 1.0, 0.0).astype(BF16), h_ref[0, rows, :])
        gate = jnp.sum(jnp.where(hit, aff_ref[0, 0, pl.ds(i, 1), :], 0.0), axis=1, keepdims=True)
        return xe, gate

    for r in range(cap // rb):
        s0 = r * rb
        first = sum((cnt_ref[base + i + 1] <= s0).astype(jnp.int32) for i in range(nt))
        end = sum((cnt_ref[base + i] < s0 + rb).astype(jnp.int32) for i in range(nt))
        start = jnp.minimum(first, nt - n_win)
        parts = [gather_tile(start + w, s0) for w in range(n_win)]
        rows = slice(s0, s0 + rb)
        xe_ref[rows, :] = sum(pt[0] for pt in parts)
        gate_ref[rows, :] = sum(pt[1] for pt in parts)

        def overflow(i, carry, s0=s0, rows=rows):
            xe, gate = gather_tile(i, s0)
            xe_ref[rows, :] += xe
            gate_ref[rows, :] += gate
            return carry

        lax.fori_loop(start + n_win, end, overflow, 0)
    xb = xe_ref[...].astype(BF16)
    hid = _silu(_dot(xb, wgb_ref[...])) * _dot(xb, wub_ref[...])
    y_ref[0, 0] = (_dot(hid.astype(BF16), wdb_ref[...]) * gate_ref[...]).astype(BF16)


def _experts(pos, cnt, aff_t, hm, p, cap):
    B, T, D = hm.shape
    E = N_EXPERTS
    nt = T // ROUTE_TILE
    layer = p["layer"]
    row = pl.BlockSpec((1, 1, nt, ROUTE_TILE), lambda e, b, c: (b, e, 0, 0))
    wspec = lambda shape: pl.BlockSpec((1, 1) + shape, lambda e, b, c: (layer, e, 0, 0), pipeline_mode=pl.Buffered(1))
    return pl.pallas_call(
        functools.partial(_expert_kernel, n_win=min(GATHER_TILES, nt)),
        grid_spec=pltpu.PrefetchScalarGridSpec(
            num_scalar_prefetch=1,
            grid=(E, B),
            in_specs=[row, row,
                      pl.BlockSpec((1, T, D), lambda e, b, c: (b, 0, 0)),
                      wspec((D, EXPERT_FF)), wspec((D, EXPERT_FF)), wspec((EXPERT_FF, D))],
            out_specs=pl.BlockSpec((1, 1, cap, D), lambda e, b, c: (b, e, 0, 0)),
            scratch_shapes=[pltpu.VMEM((cap, D), F32), pltpu.VMEM((cap, 1), F32),
                            pltpu.VMEM((D, EXPERT_FF), BF16), pltpu.VMEM((D, EXPERT_FF), BF16),
                            pltpu.VMEM((EXPERT_FF, D), BF16)]),
        out_shape=jax.ShapeDtypeStruct((B, E, cap, D), BF16),
        compiler_params=_cparams(("parallel", "arbitrary")),
        name="experts",
    )(cnt, pos.reshape(B, E, nt, ROUTE_TILE), aff_t.reshape(B, E, nt, ROUTE_TILE), hm,
      p["w_gate"], p["w_up"], p["w_down"])


def _combine_kernel(cnt_ref, pos_ref, y_ref, x_ref, g2_ref, lg_ref, lb_ref, o_ref, ycat_ref, acc_ref, *, nt):
    E, cap = y_ref.shape[1], y_ref.shape[2]
    tt = x_ref.shape[1]
    win = min(COMBINE_WIN, cap)
    per = tt // ROUTE_TILE
    b, i = pl.program_id(0), pl.program_id(1)
    slot_iota = lax.broadcasted_iota(jnp.int32, (win, tt), 0)

    def window(e, w):
        start = pl.multiple_of(jnp.minimum(w, cap - win), 16)
        slot = slot_iota + start
        hit = (pos_ref[0, e:e + 1, :] == slot) & (slot >= w)
        return jnp.where(hit, 1.0, 0.0).astype(BF16), y_ref[0, e, pl.ds(start, win), :]

    onehots, ranges = [], []
    for e in range(E):
        base = (b * E + e) * (nt + 1) + i * per
        lo, hi = cnt_ref[base], cnt_ref[base + per]
        w0 = jnp.bitwise_and(lo, ~15)
        onehot, rows = window(e, w0)
        ycat_ref[e * win:(e + 1) * win, :] = rows
        onehots.append(onehot)
        ranges.append((w0, hi))
    acc_ref[...] = _tn(jnp.concatenate(onehots, axis=0), ycat_ref[...])
    for e in range(E):
        w0, hi = ranges[e]

        def more(w, e=e):
            onehot, rows = window(e, w)
            acc_ref[...] += _tn(onehot, rows)
            return w + win

        lax.while_loop(lambda w, hi=hi: w < hi, more, w0 + win)
    o_ref[0] = _plain_norm(ALPHA * x_ref[0] + g2_ref[0] * acc_ref[...]) * lg_ref[...] + lb_ref[...]


def _combine(pos, cnt, ye, x1, g2, p, tt):
    B, T, D = x1.shape
    E, cap = ye.shape[1], ye.shape[2]
    win = min(COMBINE_WIN, cap)
    const = lambda shape: pl.BlockSpec(shape, lambda b, i, c: (0,) * len(shape))
    return pl.pallas_call(
        functools.partial(_combine_kernel, nt=T // ROUTE_TILE),
        grid_spec=pltpu.PrefetchScalarGridSpec(
            num_scalar_prefetch=1,
            grid=(B, T // tt),
            in_specs=[pl.BlockSpec((1, E, tt), lambda b, i, c: (b, 0, i)),
                      pl.BlockSpec((1, E, cap, D), lambda b, i, c: (b, 0, 0, 0)),
                      pl.BlockSpec((1, tt, D), lambda b, i, c: (b, i, 0)),
                      pl.BlockSpec((1, 1, D), lambda b, i, c: (b, 0, 0)),
                      const((1, D)), const((1, D))],
            out_specs=pl.BlockSpec((1, tt, D), lambda b, i, c: (b, i, 0)),
            scratch_shapes=[pltpu.VMEM((E * win, D), BF16), pltpu.VMEM((tt, D), F32)]),
        out_shape=jax.ShapeDtypeStruct((B, T, D), F32),
        compiler_params=_cparams(("parallel", "arbitrary")),
        name="combine",
    )(cnt, pos, ye, x1, g2, p["ln2_g"], p["ln2_b"])


def _rot_cols(a):
    a4 = a.reshape(a.shape[:-1] + (2, 2, ROPE_AXIS // 2))
    return jnp.stack([-a4[..., 1, :], a4[..., 0, :]], axis=-2).reshape(a.shape)


def _pad_cols(a, before, total):
    pads = [(0, 0)] * (a.ndim - 1) + [(before, total - before - a.shape[-1])]
    return jnp.pad(a, pads)


def _prep_layer(l, w):
    D = D_MODEL
    wi, bi = w["w_in"][l], w["b_in"][l][None, :]
    n_sg, n_mla = 2 * SG_WIDTH, Q_LORA + KV_LORA + MLA_ROPE
    o_kr = n_sg + Q_LORA + KV_LORA
    o_ml = n_sg + n_mla
    o_g = o_ml + 4 * ML_WIDTH

    def cat(a):
        kr = a[:, o_kr:o_kr + MLA_ROPE]
        gates = a[:, o_g:o_g + 4 * ML_HEADS].reshape(-1, 2, 2, ML_HEADS).transpose(0, 2, 1, 3).reshape(-1, 4 * ML_HEADS)
        return jnp.concatenate([a[:, :o_kr], _pad_cols(kr, MLA_NOPE, HEAD_PAD), _pad_cols(_rot_cols(kr), MLA_NOPE, HEAD_PAD),
                                a[:, o_ml:o_g], _pad_cols(gates, 0, LANE)], axis=1)

    wq = w["w_uq"][l].reshape(Q_LORA, MLA_HEADS, MLA_NOPE + MLA_ROPE)
    wq_main = _pad_cols(wq, 0, HEAD_PAD).reshape(Q_LORA, -1)
    wq_rot = _pad_cols(_rot_cols(wq[..., MLA_NOPE:]), MLA_NOPE, HEAD_PAD).reshape(Q_LORA, -1)
    wkv = w["w_ukv"][l].reshape(KV_LORA, MLA_HEADS, MLA_NOPE + MLA_VDIM)
    wk = _pad_cols(wkv[..., :MLA_NOPE], 0, HEAD_PAD).reshape(KV_LORA, -1)
    wv = _pad_cols(wkv[..., MLA_NOPE:], 0, HEAD_PAD).reshape(KV_LORA, -1)
    head = jnp.arange(ML_WIDTH) // ML_DIM
    wr = _pad_cols(w["w_router"][l], 0, LANE)
    wr_hi = wr.astype(BF16)
    return {
        "wcat": cat(wi).astype(BF16), "bcat": cat(bi),
        "q_norm_g": w["q_norm_g"][l][None, :], "kv_norm_g": w["kv_norm_g"][l][None, :],
        "wq": jnp.concatenate([wq_main, wq_rot], axis=1).astype(BF16),
        "wkv": jnp.concatenate([wk, wv], axis=1).astype(BF16),
        "sg_ln_g": w["sg_ln_g"][l][None, :], "sg_ln_b": w["sg_ln_b"][l][None, :],
        "sg_w": w["sg_w"][l].astype(BF16),
        "sg_bias": jnp.repeat(w["sg_b"][l].T, SG_GDIM, axis=1),
        "conv_w": jnp.pad(w["ml_conv_w"][l], ((0, 8 - ML_CONV), (0, 0))), "conv_b": w["ml_conv_b"][l][None, :],
        "f_bias": _pad_cols(w["ml_f_bias"][l].reshape(1, 2 * ML_HEADS), 0, LANE),
        "ml_norm_g": w["ml_norm_g"][l][None, :],
        "head_avg": ((head[:, None] == head[None, :]).astype(F32) / ML_DIM).astype(BF16),
        "w_out": w["w_out"][l].astype(BF16),
        "ln1_g": w["ln1_g"][l][None, :], "ln1_b": w["ln1_b"][l][None, :],
        "w_router3": jnp.concatenate([wr_hi, wr_hi, (wr - wr_hi.astype(F32)).astype(BF16)], axis=0),
        "layer": l, "w_gate": w["w_gate"], "w_up": w["w_up"], "w_down": w["w_down"],
        "ln2_g": w["ln2_g"][l][None, :], "ln2_b": w["ln2_b"][l][None, :],
    }


def _rope_tables(T):
    rows = T // GRID_W
    row = jnp.repeat(jnp.arange(rows, dtype=F32), GRID_W)
    colv = jnp.tile(jnp.arange(GRID_W, dtype=F32), rows)
    inv = ROPE_BASE ** (-jnp.arange(ROPE_AXIS // 2, dtype=F32) * 2.0 / ROPE_AXIS)
    ang = jnp.concatenate([row[:, None] * inv, colv[:, None] * inv], axis=-1)
    half = ROPE_AXIS // 2
    spread = lambda t: jnp.concatenate([t[:, :half], t[:, :half], t[:, half:], t[:, half:]], axis=1)
    cosT = jnp.concatenate([jnp.ones((T, MLA_NOPE), F32), spread(jnp.cos(ang)), jnp.zeros((T, 32), F32)], axis=1)
    sinT = _pad_cols(spread(jnp.sin(ang)), MLA_NOPE, HEAD_PAD)
    return cosT, sinT


def _moe(x1, hm, aff_t, g2, p, tt):
    cap = EC_CAPACITY * x1.shape[1] // N_EXPERTS
    pos, cnt = _route(aff_t, cap)
    ye = _experts(pos, cnt, aff_t, hm, p, cap)
    return _combine(pos, cnt, ye, x1, g2, p, tt)


def _forward(x, c, ctx, c_ctx, w):
    B, T, D = x.shape
    Tc = ctx.shape[1]
    L = w["w_ada"].shape[0]
    cc = jnp.zeros((16, D), F32).at[:B].set(c).at[B].set(c_ctx)
    ada = _ada(cc, w["w_ada"], w["b_ada"])
    cos_l, sin_l = _rope_tables(T)
    cos_c = _pad_cols(jnp.ones((Tc, MLA_NOPE + MLA_ROPE), F32), 0, HEAD_PAD)
    sin_c = jnp.zeros((Tc, HEAD_PAD), F32)
    zero_c = jnp.zeros((B, 2, ML_HEADS * HEAD_PAD, HEAD_PAD), F32)
    zero_m = jnp.zeros((B, 2, 2 * ML_HEADS, min(ML_CHUNK, Tc)), F32)
    tm = min(512, T)
    xl, xc = x, ctx
    for l in range(L):
        p = _prep_layer(l, w)
        mods_l = [ada[l, :B, i * D:(i + 1) * D][:, None, :] for i in range(6)]
        mods_c = [jnp.broadcast_to(ada[l, B, i * D:(i + 1) * D][None, None, :], (B, 1, D)) for i in range(6)]
        update_ctx = l < L - 1
        ya_c, q_c, k_c, v_c, zml_c, g_c = _inproj(xc, mods_c[0], mods_c[1], cos_c, sin_c, p, min(tm, Tc))
        ya_l, q_l, k_l, v_l, zml_l, g_l = _inproj(xl, mods_l[0], mods_l[1], cos_l, sin_l, p, tm)
        hf_c, hb_c, cs_c, ms_c = _mlstm(_qkconv(zml_c, p, min(512, Tc)), zml_c, g_c, p, zero_c, zero_m)
        hf_l, hb_l, _, _ = _mlstm(_qkconv(zml_l, p, min(512, T)), zml_l, g_l, p, cs_c, ms_c)
        yb_l = _attention(q_l, [(k_l, v_l), (k_c, v_c)], tq=min(512, T))
        x1_l, hm_l, aff_l = _outproj(xl, mods_l[2], mods_l[3], mods_l[4], ya_l, yb_l, hf_l, hb_l, zml_l, p, tm)
        xl = _moe(x1_l, hm_l, aff_l, mods_l[5], p, tt=min(512, T))
        if update_ctx:
            yb_c = _attention(q_c, [(k_c, v_c)], tq=min(256, Tc))
            x1_c, hm_c, aff_c = _outproj(xc, mods_c[2], mods_c[3], mods_c[4], ya_c, yb_c, hf_c, hb_c, zml_c, p,
                                         min(tm, Tc))
            xc = _moe(x1_c, hm_c, aff_c, mods_c[5], p, tt=min(512, Tc))
    return xl


def kernel(x, c, ctx, c_ctx, w_ada, b_ada, w_in, b_in, sg_ln_g, sg_ln_b, sg_w, sg_b, q_norm_g, kv_norm_g, w_uq, w_ukv,
           ml_conv_w, ml_conv_b, ml_f_bias, ml_norm_g, w_out, ln1_g, ln1_b, w_router, w_gate, w_up, w_down, ln2_g, ln2_b):
    w = dict(w_ada=w_ada, b_ada=b_ada, w_in=w_in, b_in=b_in, sg_ln_g=sg_ln_g, sg_ln_b=sg_ln_b, sg_w=sg_w, sg_b=sg_b,
             q_norm_g=q_norm_g, kv_norm_g=kv_norm_g, w_uq=w_uq, w_ukv=w_ukv, ml_conv_w=ml_conv_w, ml_conv_b=ml_conv_b,
             ml_f_bias=ml_f_bias, ml_norm_g=ml_norm_g, w_out=w_out, ln1_g=ln1_g, ln1_b=ln1_b, w_router=w_router,
             w_gate=w_gate, w_up=w_up, w_down=w_down, ln2_g=ln2_g, ln2_b=ln2_b)
    return _forward(x, c, ctx, c_ctx, w)
```

```python
import functools

import jax
import jax.numpy as jnp
from jax import lax
from jax.experimental import pallas as pl
from jax.experimental.pallas import tpu as pltpu

F32 = jnp.float32
BF16 = jnp.bfloat16
HIGHEST = lax.Precision.HIGHEST

D_MODEL = 1024
DEPTH = 2
GRID_W = 64
SG_WIDTH = 256
SG_GROUPS = 4
SG_GDIM = 64
SG_CHUNK = 128
MLA_HEADS = 8
MLA_NOPE = 64
MLA_ROPE = 32
MLA_VDIM = 64
Q_LORA = 384
KV_LORA = 256
ROPE_AXIS = 16
ROPE_BASE = 10000.0
MLA_SCALE = (MLA_NOPE + MLA_ROPE) ** -0.5
ML_HEADS = 4
ML_DIM = 64
ML_WIDTH = 256
ML_CONV = 5
N_EXPERTS = 16
EXPERT_FF = 1024
EC_CAPACITY = 2
ALPHA = (2 * DEPTH) ** 0.25
LN_EPS = 1e-6
LOG2E = 1.4426950408889634

LANE = 128
HEAD_PAD = 128
ML_CHUNK = 256
OFF_SG, OFF_CQ, OFF_CKV, OFF_KR, OFF_KRR, OFF_ML, OFF_G, N_CAT = 0, 512, 896, 1152, 1280, 1408, 2432, 2560
VMEM_LIMIT = 56 * 1024 * 1024
ROUTE_TILE = 256
EXPERT_ROWS = 512
GATHER_ROWS = 128
GATHER_TILES = 6
COMBINE_WIN = 128
QK_LANES =ML_HEADS * HEAD_PAD + ML_WIDTH
ATTN_HEADS = 2
ATTN_KEYS = 2048


def _cparams(sem):
    return pltpu.CompilerParams(dimension_semantics=sem, vmem_limit_bytes=VMEM_LIMIT)


def _nt(a, b):
    return lax.dot_general(a, b, (((1,), (1,)), ((), ())), preferred_element_type=F32)


def _tn(a, b):
    return lax.dot_general(a, b, (((0,), (0,)), ((), ())), preferred_element_type=F32)


def _dot(a, b, precision=None):
    return jnp.dot(a, b, preferred_element_type=F32, precision=precision)


def _split(x):
    hi = x.astype(BF16)
    return hi, (x - hi.astype(F32)).astype(BF16)


def _dot2(x, w):
    hi, lo = _split(x)
    return _dot(jnp.concatenate([hi, lo], axis=1), jnp.concatenate([w, w], axis=0))


def _plain_norm(x):
    mu = jnp.mean(x, axis=-1, keepdims=True)
    xc = x - mu
    var = jnp.mean(xc * xc, axis=-1, keepdims=True)
    return xc * lax.rsqrt(var + LN_EPS)


def _rms(x):
    return x * lax.rsqrt(jnp.mean(x * x, axis=-1, keepdims=True) + LN_EPS)


def _silu(x):
    return x * (1.0 / (1.0 + jnp.exp(-x)))


def _gelu_tanh(x):
    return 0.5 * x * (1.0 + jnp.tanh(0.7978845608028654 * (x + 0.044715 * (x * x * x))))


def _log_sigmoid(x):
    return jnp.minimum(x, 0.0) - jnp.log(1.0 + jnp.exp(-jnp.abs(x)))


def _ada_kernel(c_ref, w_ref, b_ref, o_ref):
    o_ref[0] = _dot(_silu(c_ref[...]), w_ref[0], precision=HIGHEST) + b_ref[0]


def _ada(cc, w_ada, b_ada):
    L, D, N = w_ada.shape
    tn = 1536
    return pl.pallas_call(
        _ada_kernel,
        grid=(L, N // tn),
        in_specs=[pl.BlockSpec((16, D), lambda l, j: (0, 0)),
                  pl.BlockSpec((1, D, tn), lambda l, j: (l, 0, j)),
                  pl.BlockSpec((1, 1, tn), lambda l, j: (l, 0, j))],
        out_specs=pl.BlockSpec((1, 16, tn), lambda l, j: (l, 0, j)),
        out_shape=jax.ShapeDtypeStruct((L, 16, N), F32),
        compiler_params=_cparams(("parallel", "parallel")),
        name="ada",
    )(cc, w_ada, b_ada.reshape(L, 1, N))


def _inproj_kernel(x_ref, sh_ref, sc_ref, cos_ref, sin_ref, wcat_ref, bcat_ref, qg_ref, kvg_ref, wq_ref, wkv_ref,
                   sglg_ref, sglb_ref, sgw_ref, sgb_ref,
                   ya_ref, q_ref, k_ref, v_ref, zml_ref, g_ref):
    tm = x_ref.shape[1]
    h = _plain_norm(x_ref[0]) * (1.0 + sc_ref[0]) + sh_ref[0]
    z = _dot(h.astype(BF16), wcat_ref[...]) + bcat_ref[...]
    zs = _gelu_tanh(z[:, OFF_SG:OFF_SG + 2 * SG_WIDTH])
    u = zs[:, :SG_WIDTH]
    vn = _plain_norm(zs[:, SG_WIDTH:]) * sglg_ref[...] + sglb_ref[...]
    group = jnp.right_shift(lax.broadcasted_iota(jnp.int32, (SG_CHUNK, SG_WIDTH), 1), SG_GDIM.bit_length() - 1)
    for ch in range(tm // SG_CHUNK):
        rows = slice(ch * SG_CHUNK, (ch + 1) * SG_CHUNK)
        vc = vn[rows].astype(BF16)
        mixed = sgb_ref[...]
        for g in range(SG_GROUPS):
            mixed = mixed + jnp.where(group == g, _dot(sgw_ref[g], vc), 0.0)
        ya_ref[0, rows, :] = (u[rows] * mixed).astype(BF16)
    cosT = cos_ref[...]
    sinT = sin_ref[...]
    cqn = (_rms(z[:, OFF_CQ:OFF_CQ + Q_LORA]) * qg_ref[...]).astype(BF16)
    qq = _dot(cqn, wq_ref[...])
    ckvn = (_rms(z[:, OFF_CKV:OFF_CKV + KV_LORA]) * kvg_ref[...]).astype(BF16)
    kv = _dot(ckvn, wkv_ref[...])
    kr = z[:, OFF_KR:OFF_KR + HEAD_PAD] * cosT + z[:, OFF_KRR:OFF_KRR + HEAD_PAD] * sinT
    one_lane = jnp.where(lax.broadcasted_iota(jnp.int32, (1, HEAD_PAD), 1) == MLA_VDIM, 1.0, 0.0)
    hw = MLA_HEADS * HEAD_PAD
    for hd in range(MLA_HEADS):
        cols = slice(hd * HEAD_PAD, (hd + 1) * HEAD_PAD)
        rcols = slice(hw + hd * HEAD_PAD, hw + (hd + 1) * HEAD_PAD)
        q_ref[0, hd] = ((qq[:, cols] * cosT + qq[:, rcols] * sinT) * (MLA_SCALE * LOG2E)).astype(BF16)
        k_ref[0, hd] = (kv[:, cols] + kr).astype(BF16)
        v_ref[0, hd] = (kv[:, rcols] + one_lane).astype(BF16)
    zml_ref[0] = z[:, OFF_ML:OFF_ML + 4 * ML_WIDTH]
    g_ref[0] = z[:, OFF_G:OFF_G + LANE]


def _inproj(x, shift, scale, cosT, sinT, p, tm):
    B, T, D = x.shape
    const = lambda shape: pl.BlockSpec(shape, lambda b, i: (0,) * len(shape))
    heads = (B, MLA_HEADS, T, HEAD_PAD)
    return pl.pallas_call(
        _inproj_kernel,
        grid=(B, T // tm),
        in_specs=[pl.BlockSpec((1, tm, D), lambda b, i: (b, i, 0)),
                  pl.BlockSpec((1, 1, D), lambda b, i: (b, 0, 0)),
                  pl.BlockSpec((1, 1, D), lambda b, i: (b, 0, 0)),
                  pl.BlockSpec((tm, HEAD_PAD), lambda b, i: (i, 0)),
                  pl.BlockSpec((tm, HEAD_PAD), lambda b, i: (i, 0)),
                  const((D, N_CAT)), const((1, N_CAT)), const((1, Q_LORA)), const((1, KV_LORA)),
                  const((Q_LORA, 2 * MLA_HEADS * HEAD_PAD)), const((KV_LORA, 2 * MLA_HEADS * HEAD_PAD)),
                  const((1, SG_WIDTH)), const((1, SG_WIDTH)), const((SG_GROUPS, SG_CHUNK, SG_CHUNK)),
                  const((SG_CHUNK, SG_WIDTH))],
        out_specs=[pl.BlockSpec((1, tm, SG_WIDTH), lambda b, i: (b, i, 0)),
                   pl.BlockSpec((1, MLA_HEADS, tm, HEAD_PAD), lambda b, i: (b, 0, i, 0)),
                   pl.BlockSpec((1, MLA_HEADS, tm, HEAD_PAD), lambda b, i: (b, 0, i, 0)),
                   pl.BlockSpec((1, MLA_HEADS, tm, HEAD_PAD), lambda b, i: (b, 0, i, 0)),
                   pl.BlockSpec((1, tm, 4 * ML_WIDTH), lambda b, i: (b, i, 0)),
                   pl.BlockSpec((1, tm, LANE), lambda b, i: (b, i, 0))],
        out_shape=[jax.ShapeDtypeStruct((B, T, SG_WIDTH), BF16),
                   jax.ShapeDtypeStruct(heads, BF16), jax.ShapeDtypeStruct(heads, BF16),
                   jax.ShapeDtypeStruct(heads, BF16),
                   jax.ShapeDtypeStruct((B, T, 4 * ML_WIDTH), F32),
                   jax.ShapeDtypeStruct((B, T, LANE), F32)],
        compiler_params=_cparams(("parallel", "parallel")),
        name="inproj",
    )(x, shift, scale, cosT, sinT, p["wcat"], p["bcat"], p["q_norm_g"], p["kv_norm_g"], p["wq"], p["wkv"],
      p["sg_ln_g"], p["sg_ln_b"], p["sg_w"], p["sg_bias"])


def _attn_kernel(*refs, n_src, tk):
    q_ref = refs[0]
    kv_refs = refs[1:1 + 2 * n_src]
    o_ref = refs[1 + 2 * n_src]
    nh, tq = q_ref.shape[1], q_ref.shape[2]
    qs = [q_ref[0, hh] for hh in range(nh)]
    ms = [jnp.full((tq, 1), -jnp.inf, F32) for _ in range(nh)]
    accs = [jnp.zeros((tq, HEAD_PAD), F32) for _ in range(nh)]
    for s in range(n_src):
        k_ref, v_ref = kv_refs[2 * s], kv_refs[2 * s + 1]
        ck = min(tk, k_ref.shape[2])
        for c in range(k_ref.shape[2] // ck):
            rows = slice(c * ck, (c + 1) * ck)
            for hh in range(nh):
                sc = _nt(qs[hh], k_ref[0, hh, rows, :])
                m_new = jnp.maximum(ms[hh], jnp.max(sc, axis=1, keepdims=True))
                pr = jnp.exp2(sc - m_new)
                accs[hh] = accs[hh] * jnp.exp2(ms[hh] - m_new) + _dot(pr.astype(BF16), v_ref[0, hh, rows, :])
                ms[hh] = m_new
    outs = [accs[hh] / accs[hh][:, MLA_VDIM:MLA_VDIM + 1] for hh in range(nh)]
    lane = lax.broadcasted_iota(jnp.int32, (tq, HEAD_PAD), 1)
    for pair in range(nh // 2):
        both = jnp.where(lane < MLA_VDIM, outs[2 * pair], pltpu.roll(outs[2 * pair + 1], MLA_VDIM, axis=1))
        o_ref[0, :, pair * HEAD_PAD:(pair + 1) * HEAD_PAD] = both.astype(BF16)


def _attention(q, kvs, tq, tk=ATTN_KEYS, nh=ATTN_HEADS):
    B, H, T, _ = q.shape
    in_specs = [pl.BlockSpec((1, nh, tq, HEAD_PAD), lambda b, hp, i: (b, hp, i, 0))]
    args = [q]
    for k, v in kvs:
        K = k.shape[2]
        in_specs += [pl.BlockSpec((1, nh, K, HEAD_PAD), lambda b, hp, i: (b, hp, 0, 0))] * 2
        args += [k, v]
    return pl.pallas_call(
        functools.partial(_attn_kernel, n_src=len(kvs), tk=tk),
        grid=(B, H // nh, T // tq),
        in_specs=in_specs,
        out_specs=pl.BlockSpec((1, tq, nh * MLA_VDIM), lambda b, hp, i: (b, i, hp)),
        out_shape=jax.ShapeDtypeStruct((B, T, H * MLA_VDIM), BF16),
        compiler_params=_cparams(("parallel", "parallel", "parallel")),
        name="attention",
    )(*args)


def _qkconv_kernel(zp_ref, zc_ref, zn_ref, cw_ref, cb_ref, o_ref):
    tc = zc_ref.shape[1]
    i = pl.program_id(1)
    prev = jnp.where(i > 0, zp_ref[0], 0.0)
    nxt = jnp.where(i < pl.num_programs(1) - 1, zn_ref[0], 0.0)
    big = jnp.concatenate([prev, zc_ref[0], nxt], axis=0)
    n_big = tc + 16
    pad = (ML_CONV - 1) // 2
    conv = cb_ref[...]
    for t in range(ML_CONV):
        s = t - pad
        shifted = big if s == 0 else pltpu.roll(big, (-s) % n_big, axis=0)
        conv = conv + shifted[8:8 + tc] * cw_ref[t:t + 1, :]
    qk = _silu(conv)
    lane = lax.broadcasted_iota(jnp.int32, (1, HEAD_PAD), 1)
    for hd in range(ML_HEADS):
        q_blk = qk[:, (hd // 2) * HEAD_PAD:(hd // 2 + 1) * HEAD_PAD]
        in_half = (lane >= ML_DIM) if hd % 2 else (lane < ML_DIM)
        o_ref[0, :, hd * HEAD_PAD:(hd + 1) * HEAD_PAD] = jnp.where(in_half, q_blk, 0.0).astype(BF16)
    o_ref[0, :, ML_HEADS * HEAD_PAD:] = (qk[:, ML_WIDTH:] * (ML_DIM ** -0.5)).astype(BF16)


def _qkconv(zml, p, tc):
    B, T, _ = zml.shape
    hb = tc // 8
    const = lambda shape: pl.BlockSpec(shape, lambda b, i: (0,) * len(shape))
    return pl.pallas_call(
        _qkconv_kernel,
        grid=(B, T // tc),
        in_specs=[pl.BlockSpec((1, 8, 2 * ML_WIDTH), lambda b, i: (b, jnp.maximum(i * hb - 1, 0), 0)),
                  pl.BlockSpec((1, tc, 2 * ML_WIDTH), lambda b, i: (b, i, 0)),
                  pl.BlockSpec((1, 8, 2 * ML_WIDTH), lambda b, i: (b, jnp.minimum((i + 1) * hb, T // 8 - 1), 0)),
                  const((8, 2 * ML_WIDTH)), const((1, 2 * ML_WIDTH))],
        out_specs=pl.BlockSpec((1, tc, QK_LANES), lambda b, i: (b, i, 0)),
        out_shape=jax.ShapeDtypeStruct((B, T, QK_LANES), BF16),
        compiler_params=_cparams(("parallel", "parallel")),
        name="qkconv",
    )(zml, zml, zml, p["conv_w"], p["conv_b"])


def _mlstm_direction(d, qk, vv, gl, fb, c_prev, m_prev):
    L = qk.shape[0]
    lf = _log_sigmoid(pltpu.roll(gl, LANE - 2 * ML_HEADS, axis=1) + fb)
    ks = lax.broadcasted_iota(jnp.int32, (L, L), 0)
    qt = lax.broadcasted_iota(jnp.int32, (L, L), 1)
    visible = (ks >= qt) if d else (ks <= qt)
    tri = jnp.where((qt >= ks) if d else (qt <= ks), 1.0, 0.0).astype(BF16)
    lf_hi, lf_lo = _split(lf)
    bc = _dot(tri, lf_hi) + _dot(tri, lf_lo)
    r_cols = gl - bc
    b8 = bc.T[0:2 * ML_HEADS]
    li8 = gl.T[0:2 * ML_HEADS]
    b_end = b8[:, 0:1] if d else b8[:, L - 1:L]
    gw = b_end - b8 + li8
    m_loc = jnp.max(gw, axis=1, keepdims=True)
    w8 = jnp.exp(gw - m_loc)
    m_next = jnp.maximum(b_end + m_prev, m_loc)
    a_sc = jnp.exp(b_end + m_prev - m_next)
    s_sc = jnp.exp(m_loc - m_next)
    a8 = b8 + m_prev

    srow = lax.broadcasted_iota(jnp.int32, (HEAD_PAD, L), 0)
    h_blocks, c_next = [], []
    for blk in range(ML_HEADS // 2):
        km = qk[:, (ML_HEADS + blk) * HEAD_PAD:(ML_HEADS + blk + 1) * HEAD_PAD]
        v_t = vv[:, blk * HEAD_PAD:(blk + 1) * HEAD_PAD].T
        hh = []
        for half in range(2):
            hd = 2 * blk + half
            g = d * ML_HEADS + hd
            in_half = (srow >= ML_DIM) if half else (srow < ML_DIM)
            den_row = 0 if half else ML_DIM
            qm = qk[:, hd * HEAD_PAD:(hd + 1) * HEAD_PAD]
            vaug_t = jnp.where(in_half, v_t, jnp.where(srow == den_row, 1.0, 0.0))
            dm = jnp.where(visible, r_cols[:, g:g + 1] + b8[g:g + 1, :], -jnp.inf)
            a_row = a8[g:g + 1, :]
            m_t = jnp.maximum(a_row, jnp.max(dm, axis=0, keepdims=True))
            pm = jnp.exp(dm - m_t) * _nt(km, qm)
            c0 = c_prev[hd]
            nd = (jnp.exp(a_row - m_t) * _nt(c0.astype(BF16), qm)
                  + _dot(vaug_t.astype(BF16), pm.astype(BF16)))
            den = nd[den_row:den_row + 1, :]
            hh.append(nd / jnp.maximum(jnp.abs(den), jnp.exp(-m_t)))
            c_loc = _dot((vaug_t * w8[g:g + 1, :]).astype(BF16), km)
            c_next.append(a_sc[g:g + 1, 0:HEAD_PAD] * c0 + s_sc[g:g + 1, 0:HEAD_PAD] * c_loc)
        h_blocks.append(jnp.where(srow < ML_DIM, hh[0], hh[1]).T)
    grow = lax.broadcasted_iota(jnp.int32, (2 * ML_HEADS, L), 0)
    mine = (grow >= d * ML_HEADS) & (grow < (d + 1) * ML_HEADS)
    return h_blocks, c_next, jnp.where(mine, m_next, m_prev)


def _mlstm_kernel(qkf_ref, vf_ref, gf_ref, qkb_ref, vb_ref, gb_ref, fb_ref, ci_ref, mi_ref,
                  hf_ref, hb_ref, cs_ref, ms_ref):
    @pl.when(pl.program_id(1) == 0)
    def _():
        cs_ref[...] = ci_ref[...]
        ms_ref[...] = mi_ref[...]

    rows = lambda hd: slice(hd * HEAD_PAD, (hd + 1) * HEAD_PAD)
    ins = ((qkf_ref, vf_ref, gf_ref), (qkb_ref, vb_ref, gb_ref))
    states = [([cs_ref[0, d, rows(hd), :] for hd in range(ML_HEADS)], ms_ref[0, d]) for d in range(2)]
    results = [_mlstm_direction(d, ins[d][0][0], ins[d][1][0], ins[d][2][0], fb_ref[...], *states[d])
               for d in range(2)]
    for d, h_ref in enumerate((hf_ref, hb_ref)):
        h_blocks, c_next, m_next = results[d]
        for blk, hb in enumerate(h_blocks):
            h_ref[0, :, blk * HEAD_PAD:(blk + 1) * HEAD_PAD] = hb
        for hd in range(ML_HEADS):
            cs_ref[0, d, rows(hd), :] = c_next[hd]
        ms_ref[0, d] = m_next


def _mlstm(qk, zml, gates, p, c_init, m_init):
    B, T, _ = zml.shape
    L = min(ML_CHUNK, T)
    nc = T // L
    const = lambda shape: pl.BlockSpec(shape, lambda b, j: (0,) * len(shape))
    state = pl.BlockSpec((1, 2, ML_HEADS * HEAD_PAD, HEAD_PAD), lambda b, j: (b, 0, 0, 0))
    mstate = pl.BlockSpec((1, 2, 2 * ML_HEADS, L), lambda b, j: (b, 0, 0, 0))
    fwd = lambda b, j: (b, j, 0)
    bwd = lambda b, j: (b, nc - 1 - j, 0)
    return pl.pallas_call(
        _mlstm_kernel,
        grid=(B, nc),
        in_specs=[pl.BlockSpec((1, L, QK_LANES), fwd),
                  pl.BlockSpec((1, L, ML_WIDTH), lambda b, j: (b, j, 2)),
                  pl.BlockSpec((1, L, LANE), fwd),
                  pl.BlockSpec((1, L, QK_LANES), bwd),
                  pl.BlockSpec((1, L, ML_WIDTH), lambda b, j: (b, nc - 1 - j, 2)),
                  pl.BlockSpec((1, L, LANE), bwd),
                  const((1, LANE)), state, mstate],
        out_specs=[pl.BlockSpec((1, L, ML_WIDTH), fwd), pl.BlockSpec((1, L, ML_WIDTH), bwd), state, mstate],
        out_shape=[jax.ShapeDtypeStruct((B, T, ML_WIDTH), F32), jax.ShapeDtypeStruct((B, T, ML_WIDTH), F32),
                   jax.ShapeDtypeStruct((B, 2, ML_HEADS * HEAD_PAD, HEAD_PAD), F32),
                   jax.ShapeDtypeStruct((B, 2, 2 * ML_HEADS, L), F32)],
        compiler_params=_cparams(("parallel", "arbitrary")),
        name="mlstm",
    )(qk, zml, gates, qk, zml, gates, p["f_bias"], c_init, m_init)


def _outproj_kernel(x_ref, g1_ref, sh2_ref, sc2_ref, ya_ref, yb_ref, h0_ref, h1_ref, o_ref, mg_ref, avg_ref,
                    wo_ref, l1g_ref, l1b_ref, wr_ref, x1_ref, hm_ref, aff_ref):
    hs = h0_ref[0] + h1_ref[0]
    mu = _dot2(hs, avg_ref[...])
    hc = hs - mu
    var = _dot2(hc * hc, avg_ref[...])
    yc = (1.0 / (1.0 + jnp.exp(-o_ref[0]))) * (hc * lax.rsqrt(var + LN_EPS) * mg_ref[...])
    y = _dot(jnp.concatenate([ya_ref[0], yb_ref[0], yc.astype(BF16)], axis=1), wo_ref[...])
    x1 = _plain_norm(ALPHA * x_ref[0] + g1_ref[0] * y) * l1g_ref[...] + l1b_ref[...]
    x1_ref[0] = x1
    hm = _plain_norm(x1) * (1.0 + sc2_ref[0]) + sh2_ref[0]
    hm_ref[0] = hm.astype(BF16)
    hi, lo = _split(hm)
    parts = _dot(jnp.concatenate([hi, lo], axis=1), wr_ref[...])
    logits = (parts[:, :LANE] + parts[:, LANE:]).T[:N_EXPERTS, :]
    ex = jnp.exp(logits - jnp.max(logits, axis=0, keepdims=True))
    aff_ref[0] = ex / jnp.sum(ex, axis=0, keepdims=True)


def _outproj(x, g1, sh2, sc2, ya, yb, h_fwd, h_bwd, zml, p, tm):
    B, T, D = x.shape
    const = lambda shape: pl.BlockSpec(shape, lambda b, i: (0,) * len(shape))
    vec = pl.BlockSpec((1, 1, D), lambda b, i: (b, 0, 0))
    return pl.pallas_call(
        _outproj_kernel,
        grid=(B, T // tm),
        in_specs=[pl.BlockSpec((1, tm, D), lambda b, i: (b, i, 0)), vec, vec, vec,
                  pl.BlockSpec((1, tm, SG_WIDTH), lambda b, i: (b, i, 0)),
                  pl.BlockSpec((1, tm, MLA_HEADS * MLA_VDIM), lambda b, i: (b, i, 0)),
                  pl.BlockSpec((1, tm, ML_WIDTH), lambda b, i: (b, i, 0)),
                  pl.BlockSpec((1, tm, ML_WIDTH), lambda b, i: (b, i, 0)),
                  pl.BlockSpec((1, tm, ML_WIDTH), lambda b, i: (b, i, 3)),
                  const((1, ML_WIDTH)), const((ML_WIDTH, ML_WIDTH)), const((D, D)), const((1, D)), const((1, D)),
                  const((2 * D, 2 * LANE))],
        out_specs=[pl.BlockSpec((1, tm, D), lambda b, i: (b, i, 0)),
                   pl.BlockSpec((1, tm, D), lambda b, i: (b, i, 0)),
                   pl.BlockSpec((1, N_EXPERTS, tm), lambda b, i: (b, 0, i))],
        out_shape=[jax.ShapeDtypeStruct((B, T, D), F32), jax.ShapeDtypeStruct((B, T, D), BF16),
                   jax.ShapeDtypeStruct((B, N_EXPERTS, T), F32)],
        compiler_params=_cparams(("parallel", "parallel")),
        name="outproj",
    )(x, g1, sh2, sc2, ya, yb, h_fwd, h_bwd, zml, p["ml_norm_g"], p["head_avg"], p["w_out"], p["ln1_g"], p["ln1_b"],
      p["w_router3"])


def _route_kernel(aff_ref, pos_ref, cnt_ref, *, cap, tw):
    aff = aff_ref[0]
    E, T = aff.shape
    bits = pltpu.bitcast(aff, jnp.int32)
    count = lambda m: jnp.sum(jnp.where(m, 1.0, 0.0), axis=1, keepdims=True)
    thr = jnp.zeros((E, 1), jnp.int32)
    for bit in range(30, -1, -1):
        cand = thr | (1 << bit)
        thr = jnp.where(count(bits >= cand) >= cap, cand, thr)
    above = bits > thr
    tied = bits == thr
    need = cap - count(above)
    idx = lax.broadcasted_iota(jnp.int32, (E, T), 1)
    cut = jnp.zeros((E, 1), jnp.int32)
    for bit in range(T.bit_length() - 1, -1, -1):
        cand = cut | (1 << bit)
        ok = (cand <= T) & (count(tied & (idx < cand)) <= need)
        cut = jnp.where(ok, cand, cut)
    sel = above | (tied & (idx < cut))
    r = lax.broadcasted_iota(jnp.int32, (tw, tw), 0)
    cidx = lax.broadcasted_iota(jnp.int32, (tw, tw), 1)
    before = jnp.where(r < cidx, 1.0, 0.0).astype(BF16)
    offset = jnp.zeros((E, 1), F32)
    lane = lax.broadcasted_iota(jnp.int32, (E, LANE), 1)
    table = jnp.zeros((E, LANE), F32)
    for blk in range(T // tw):
        cols = slice(blk * tw, (blk + 1) * tw)
        sb = jnp.where(sel[:, cols], 1.0, 0.0)
        slot = _dot(sb.astype(BF16), before) + offset
        pos_ref[0, :, cols] = jnp.where(sel[:, cols], slot, -1.0).astype(jnp.int32)
        offset = offset + jnp.sum(sb, axis=1, keepdims=True)
        table = jnp.where(lane == blk + 1, offset, table)
    cnt_ref[0] = table.astype(jnp.int32)


def _route(aff_t, cap):
    B, E, T = aff_t.shape
    nt = T // ROUTE_TILE
    pos, cnt = pl.pallas_call(
        functools.partial(_route_kernel, cap=cap, tw=ROUTE_TILE),
        grid=(B,),
        in_specs=[pl.BlockSpec((1, E, T), lambda b: (b, 0, 0))],
        out_specs=[pl.BlockSpec((1, E, T), lambda b: (b, 0, 0)), pl.BlockSpec((1, E, LANE), lambda b: (b, 0, 0))],
        out_shape=[jax.ShapeDtypeStruct((B, E, T), jnp.int32), jax.ShapeDtypeStruct((B, E, LANE), jnp.int32)],
        compiler_params=_cparams(("parallel",)),
        name="route",
    )(aff_t)
    return pos, cnt[:, :, :nt + 1].reshape(-1)


def _expert_kernel(cnt_ref, pos_ref, aff_ref, h_ref, wg_ref, wu_ref, wd_ref, y_ref, xe_ref, gate_ref,
                   wgb_ref, wub_ref, wdb_ref, *, n_win):
    nb, cap = y_ref.shape[0], y_ref.shape[2]
    nt = pos_ref.shape[2]
    rb = min(GATHER_ROWS, cap)
    e, j = pl.program_id(0), pl.program_id(1)

    @pl.when(j == 0)
    def _():
        wgb_ref[...] = wg_ref[0, 0].astype(BF16)
        wub_ref[...] = wu_ref[0, 0].astype(BF16)
        wdb_ref[...] = wd_ref[0, 0].astype(BF16)

    slot_iota = lax.broadcasted_iota(jnp.int32, (rb, ROUTE_TILE), 0)

    def gather_tile(bb, i, s0):
        hit = pos_ref[bb, 0, pl.ds(i, 1), :] == slot_iota + s0
        rows = pl.ds(pl.multiple_of(i * ROUTE_TILE, ROUTE_TILE), ROUTE_TILE)
        xe = _dot(jnp.where(hit, 1.0, 0.0).astype(BF16), h_ref[bb, rows, :])
        gate = jnp.sum(jnp.where(hit, aff_ref[bb, 0, pl.ds(i, 1), :], 0.0), axis=1, keepdims=True)
        return xe, gate

    for bb in range(nb):
        base = ((j * nb + bb) * N_EXPERTS + e) * (nt + 1)
        for r in range(cap // rb):
            s0 = r * rb
            first = sum((cnt_ref[base + i + 1] <= s0).astype(jnp.int32) for i in range(nt))
            end = sum((cnt_ref[base + i] < s0 + rb).astype(jnp.int32) for i in range(nt))
            start = jnp.minimum(first, nt - n_win)
            parts = [gather_tile(bb, start + w, s0) for w in range(n_win)]
            rows = slice(bb * cap + s0, bb * cap + s0 + rb)
            xe_ref[rows, :] = sum(pt[0] for pt in parts)
            gate_ref[rows, :] = sum(pt[1] for pt in parts)

            def overflow(i, carry, bb=bb, s0=s0, rows=rows):
                xe, gate = gather_tile(bb, i, s0)
                xe_ref[rows, :] += xe
                gate_ref[rows, :] += gate
                return carry

            lax.fori_loop(start + n_win, end, overflow, 0)
    xb = xe_ref[...].astype(BF16)
    hid = _silu(_dot(xb, wgb_ref[...])) * _dot(xb, wub_ref[...])
    y = (_dot(hid.astype(BF16), wdb_ref[...]) * gate_ref[...]).astype(BF16)
    for bb in range(nb):
        y_ref[bb, 0] = y[bb * cap:(bb + 1) * cap]


def _experts(pos, cnt, aff_t, hm, p, cap):
    B, T, D = hm.shape
    E = N_EXPERTS
    nt = T // ROUTE_TILE
    layer = p["layer"]
    nb = max(1, min(B, EXPERT_ROWS // cap))
    row = pl.BlockSpec((nb, 1, nt, ROUTE_TILE), lambda e, j, c: (j, e, 0, 0))
    wspec = lambda shape: pl.BlockSpec((1, 1) + shape, lambda e, j, c: (layer, e, 0, 0))
    return pl.pallas_call(
        functools.partial(_expert_kernel, n_win=min(GATHER_TILES, nt)),
        grid_spec=pltpu.PrefetchScalarGridSpec(
            num_scalar_prefetch=1,
            grid=(E, B // nb),
            in_specs=[row, row,
                      pl.BlockSpec((nb, T, D), lambda e, j, c: (j, 0, 0)),
                      wspec((D, EXPERT_FF)), wspec((D, EXPERT_FF)), wspec((EXPERT_FF, D))],
            out_specs=pl.BlockSpec((nb, 1, cap, D), lambda e, j, c: (j, e, 0, 0)),
            scratch_shapes=[pltpu.VMEM((nb * cap, D), F32), pltpu.VMEM((nb * cap, 1), F32),
                            pltpu.VMEM((D, EXPERT_FF), BF16), pltpu.VMEM((D, EXPERT_FF), BF16),
                            pltpu.VMEM((EXPERT_FF, D), BF16)]),
        out_shape=jax.ShapeDtypeStruct((B, E, cap, D), BF16),
        compiler_params=_cparams(("parallel", "arbitrary")),
        name="experts",
    )(cnt, pos.reshape(B, E, nt, ROUTE_TILE), aff_t.reshape(B, E, nt, ROUTE_TILE), hm,
      p["w_gate"], p["w_up"], p["w_down"])


def _combine_kernel(cnt_ref, pos_ref, y_ref, x_ref, g2_ref, lg_ref, lb_ref, o_ref, ycat_ref, acc_ref, *, nt):
    E, cap = y_ref.shape[1], y_ref.shape[2]
    tt = x_ref.shape[1]
    win = min(COMBINE_WIN, cap)
    per = tt // ROUTE_TILE
    b, i = pl.program_id(0), pl.program_id(1)
    slot_iota = lax.broadcasted_iota(jnp.int32, (win, tt), 0)

    def window(e, w):
        start = pl.multiple_of(jnp.minimum(w, cap - win), 16)
        slot = slot_iota + start
        hit = (pos_ref[0, e:e + 1, :] == slot) & (slot >= w)
        return jnp.where(hit, 1.0, 0.0).astype(BF16), y_ref[0, e, pl.ds(start, win), :]

    onehots, ranges = [], []
    for e in range(E):
        base = (b * E + e) * (nt + 1) + i * per
        lo, hi = cnt_ref[base], cnt_ref[base + per]
        w0 = jnp.bitwise_and(lo, ~15)
        onehot, rows = window(e, w0)
        ycat_ref[e * win:(e + 1) * win, :] = rows
        onehots.append(onehot)
        ranges.append((w0, hi))
    acc_ref[...] = _tn(jnp.concatenate(onehots, axis=0), ycat_ref[...])
    for e in range(E):
        w0, hi = ranges[e]

        def more(w, e=e):
            onehot, rows = window(e, w)
            acc_ref[...] += _tn(onehot, rows)
            return w + win

        lax.while_loop(lambda w, hi=hi: w < hi, more, w0 + win)
    o_ref[0] = _plain_norm(ALPHA * x_ref[0] + g2_ref[0] * acc_ref[...]) * lg_ref[...] + lb_ref[...]


def _combine(pos, cnt, ye, x1, g2, p, tt):
    B, T, D = x1.shape
    E, cap = ye.shape[1], ye.shape[2]
    win = min(COMBINE_WIN, cap)
    const = lambda shape: pl.BlockSpec(shape, lambda b, i, c: (0,) * len(shape))
    return pl.pallas_call(
        functools.partial(_combine_kernel, nt=T // ROUTE_TILE),
        grid_spec=pltpu.PrefetchScalarGridSpec(
            num_scalar_prefetch=1,
            grid=(B, T // tt),
            in_specs=[pl.BlockSpec((1, E, tt), lambda b, i, c: (b, 0, i)),
                      pl.BlockSpec((1, E, cap, D), lambda b, i, c: (b, 0, 0, 0)),
                      pl.BlockSpec((1, tt, D), lambda b, i, c: (b, i, 0)),
                      pl.BlockSpec((1, 1, D), lambda b, i, c: (b, 0, 0)),
                      const((1, D)), const((1, D))],
            out_specs=pl.BlockSpec((1, tt, D), lambda b, i, c: (b, i, 0)),
            scratch_shapes=[pltpu.VMEM((E * win, D), BF16), pltpu.VMEM((tt, D), F32)]),
        out_shape=jax.ShapeDtypeStruct((B, T, D), F32),
        compiler_params=_cparams(("parallel", "arbitrary")),
        name="combine",
    )(cnt, pos, ye, x1, g2, p["ln2_g"], p["ln2_b"])


def _rot_cols(a):
    a4 = a.reshape(a.shape[:-1] + (2, 2, ROPE_AXIS // 2))
    return jnp.stack([-a4[..., 1, :], a4[..., 0, :]], axis=-2).reshape(a.shape)


def _pad_cols(a, before, total):
    pads = [(0, 0)] * (a.ndim - 1) + [(before, total - before - a.shape[-1])]
    return jnp.pad(a, pads)


def _prep_layer(l, w):
    D = D_MODEL
    wi, bi = w["w_in"][l], w["b_in"][l][None, :]
    n_sg, n_mla = 2 * SG_WIDTH, Q_LORA + KV_LORA + MLA_ROPE
    o_kr = n_sg + Q_LORA + KV_LORA
    o_ml = n_sg + n_mla
    o_g = o_ml + 4 * ML_WIDTH

    def cat(a):
        kr = a[:, o_kr:o_kr + MLA_ROPE]
        gates = a[:, o_g:o_g + 4 * ML_HEADS].reshape(-1, 2, 2, ML_HEADS).transpose(0, 2, 1, 3).reshape(-1, 4 * ML_HEADS)
        return jnp.concatenate([a[:, :o_kr], _pad_cols(kr, MLA_NOPE, HEAD_PAD), _pad_cols(_rot_cols(kr), MLA_NOPE, HEAD_PAD),
                                a[:, o_ml:o_g], _pad_cols(gates, 0, LANE)], axis=1)

    wq = w["w_uq"][l].reshape(Q_LORA, MLA_HEADS, MLA_NOPE + MLA_ROPE)
    wq_main = _pad_cols(wq, 0, HEAD_PAD).reshape(Q_LORA, -1)
    wq_rot = _pad_cols(_rot_cols(wq[..., MLA_NOPE:]), MLA_NOPE, HEAD_PAD).reshape(Q_LORA, -1)
    wkv = w["w_ukv"][l].reshape(KV_LORA, MLA_HEADS, MLA_NOPE + MLA_VDIM)
    wk = _pad_cols(wkv[..., :MLA_NOPE], 0, HEAD_PAD).reshape(KV_LORA, -1)
    wv = _pad_cols(wkv[..., MLA_NOPE:], 0, HEAD_PAD).reshape(KV_LORA, -1)
    head = jnp.arange(ML_WIDTH) // ML_DIM
    wr = _pad_cols(w["w_router"][l], 0, LANE)
    wr_hi = wr.astype(BF16)
    return {
        "wcat": cat(wi).astype(BF16), "bcat": cat(bi),
        "q_norm_g": w["q_norm_g"][l][None, :], "kv_norm_g": w["kv_norm_g"][l][None, :],
        "wq": jnp.concatenate([wq_main, wq_rot], axis=1).astype(BF16),
        "wkv": jnp.concatenate([wk, wv], axis=1).astype(BF16),
        "sg_ln_g": w["sg_ln_g"][l][None, :], "sg_ln_b": w["sg_ln_b"][l][None, :],
        "sg_w": w["sg_w"][l].astype(BF16),
        "sg_bias": jnp.repeat(w["sg_b"][l].T, SG_GDIM, axis=1),
        "conv_w": jnp.pad(w["ml_conv_w"][l], ((0, 8 - ML_CONV), (0, 0))), "conv_b": w["ml_conv_b"][l][None, :],
        "f_bias": _pad_cols(w["ml_f_bias"][l].reshape(1, 2 * ML_HEADS), 0, LANE),
        "ml_norm_g": w["ml_norm_g"][l][None, :],
        "head_avg": ((head[:, None] == head[None, :]).astype(F32) / ML_DIM).astype(BF16),
        "w_out": w["w_out"][l].astype(BF16),
        "ln1_g": w["ln1_g"][l][None, :], "ln1_b": w["ln1_b"][l][None, :],
        "w_router3": jnp.concatenate([jnp.concatenate([wr_hi, (wr - wr_hi.astype(F32)).astype(BF16)], axis=1),
                                      jnp.concatenate([wr_hi, jnp.zeros_like(wr_hi)], axis=1)], axis=0),
        "layer": l, "w_gate": w["w_gate"], "w_up": w["w_up"], "w_down": w["w_down"],
        "ln2_g": w["ln2_g"][l][None, :], "ln2_b": w["ln2_b"][l][None, :],
    }


def _rope_tables(T):
    rows = T // GRID_W
    row = jnp.repeat(jnp.arange(rows, dtype=F32), GRID_W)
    colv = jnp.tile(jnp.arange(GRID_W, dtype=F32), rows)
    inv = ROPE_BASE ** (-jnp.arange(ROPE_AXIS // 2, dtype=F32) * 2.0 / ROPE_AXIS)
    ang = jnp.concatenate([row[:, None] * inv, colv[:, None] * inv], axis=-1)
    half = ROPE_AXIS // 2
    spread = lambda t: jnp.concatenate([t[:, :half], t[:, :half], t[:, half:], t[:, half:]], axis=1)
    cosT = jnp.concatenate([jnp.ones((T, MLA_NOPE), F32), spread(jnp.cos(ang)), jnp.zeros((T, 32), F32)], axis=1)
    sinT = _pad_cols(spread(jnp.sin(ang)), MLA_NOPE, HEAD_PAD)
    return cosT, sinT


def _moe(x1, hm, aff_t, g2, p, tt):
    cap = EC_CAPACITY * x1.shape[1] // N_EXPERTS
    pos, cnt = _route(aff_t, cap)
    ye = _experts(pos, cnt, aff_t, hm, p, cap)
    return _combine(pos, cnt, ye, x1, g2, p, tt)


def _forward(x, c, ctx, c_ctx, w):
    B, T, D = x.shape
    Tc = ctx.shape[1]
    L = w["w_ada"].shape[0]
    cc = jnp.zeros((16, D), F32).at[:B].set(c).at[B].set(c_ctx)
    ada = _ada(cc, w["w_ada"], w["b_ada"])
    cos_l, sin_l = _rope_tables(T)
    cos_c = _pad_cols(jnp.ones((Tc, MLA_NOPE + MLA_ROPE), F32), 0, HEAD_PAD)
    sin_c = jnp.zeros((Tc, HEAD_PAD), F32)
    zero_c = jnp.zeros((B, 2, ML_HEADS * HEAD_PAD, HEAD_PAD), F32)
    zero_m = jnp.zeros((B, 2, 2 * ML_HEADS, min(ML_CHUNK, Tc)), F32)
    tm = min(512, T)
    xl, xc = x, ctx
    for l in range(L):
        p = _prep_layer(l, w)
        mods_l = [ada[l, :B, i * D:(i + 1) * D][:, None, :] for i in range(6)]
        mods_c = [jnp.broadcast_to(ada[l, B, i * D:(i + 1) * D][None, None, :], (B, 1, D)) for i in range(6)]
        update_ctx = l < L - 1
        ya_c, q_c, k_c, v_c, zml_c, g_c = _inproj(xc, mods_c[0], mods_c[1], cos_c, sin_c, p, min(tm, Tc))
        ya_l, q_l, k_l, v_l, zml_l, g_l = _inproj(xl, mods_l[0], mods_l[1], cos_l, sin_l, p, tm)
        hf_c, hb_c, cs_c, ms_c = _mlstm(_qkconv(zml_c, p, min(512, Tc)), zml_c, g_c, p, zero_c, zero_m)
        hf_l, hb_l, _, _ = _mlstm(_qkconv(zml_l, p, min(512, T)), zml_l, g_l, p, cs_c, ms_c)
        yb_l = _attention(q_l, [(k_l, v_l), (k_c, v_c)], tq=min(512, T))
        x1_l, hm_l, aff_l = _outproj(xl, mods_l[2], mods_l[3], mods_l[4], ya_l, yb_l, hf_l, hb_l, zml_l, p, tm)
        xl = _moe(x1_l, hm_l, aff_l, mods_l[5], p, tt=min(512, T))
        if update_ctx:
            yb_c = _attention(q_c, [(k_c, v_c)], tq=min(256, Tc))
            x1_c, hm_c, aff_c = _outproj(xc, mods_c[2], mods_c[3], mods_c[4], ya_c, yb_c, hf_c, hb_c, zml_c, p,
                                         min(tm, Tc))
            xc = _moe(x1_c, hm_c, aff_c, mods_c[5], p, tt=min(512, Tc))
    return xl


def kernel(x, c, ctx, c_ctx, w_ada, b_ada, w_in, b_in, sg_ln_g, sg_ln_b, sg_w, sg_b, q_norm_g, kv_norm_g, w_uq, w_ukv,
           ml_conv_w, ml_conv_b, ml_f_bias, ml_norm_g, w_out, ln1_g, ln1_b, w_router, w_gate, w_up, w_down, ln2_g, ln2_b):
    w = dict(w_ada=w_ada, b_ada=b_ada, w_in=w_in, b_in=b_in, sg_ln_g=sg_ln_g, sg_ln_b=sg_ln_b, sg_w=sg_w, sg_b=sg_b,
             q_norm_g=q_norm_g, kv_norm_g=kv_norm_g, w_uq=w_uq, w_ukv=w_ukv, ml_conv_w=ml_conv_w, ml_conv_b=ml_conv_b,
             ml_f_bias=ml_f_bias, ml_norm_g=ml_norm_g, w_out=w_out, ln1_g=ln1_g, ln1_b=ln1_b, w_router=w_router,
             w_gate=w_gate, w_up=w_up, w_down=w_down, ln2_g=ln2_g, ln2_b=ln2_b)
    return _forward(x, c, ctx, c_ctx, w)
```

```python
import functools

import jax
import jax.numpy as jnp
from jax import lax
from jax.experimental import pallas as pl
from jax.experimental.pallas import tpu as pltpu

F32 = jnp.float32
BF16 = jnp.bfloat16
HIGHEST = lax.Precision.HIGHEST

D_MODEL = 1024
DEPTH = 2
GRID_W = 64
SG_WIDTH = 256
SG_GROUPS = 4
SG_GDIM = 64
SG_CHUNK = 128
MLA_HEADS = 8
MLA_NOPE = 64
MLA_ROPE = 32
MLA_VDIM = 64
Q_LORA = 384
KV_LORA = 256
ROPE_AXIS = 16
ROPE_BASE = 10000.0
MLA_SCALE = (MLA_NOPE + MLA_ROPE) ** -0.5
ML_HEADS = 4
ML_DIM = 64
ML_WIDTH = 256
ML_CONV = 5
N_EXPERTS = 16
EXPERT_FF = 1024
EC_CAPACITY = 2
ALPHA = (2 * DEPTH) ** 0.25
LN_EPS = 1e-6
LOG2E = 1.4426950408889634

LANE = 128
HEAD_PAD = 128
ML_CHUNK = 256
ML_BLOCK = 512
OFF_SG, OFF_CQ, OFF_CKV, OFF_KR, OFF_ML, OFF_G, N_CAT = 0, 512, 896, 1152, 1280, 2304, 2432
VMEM_LIMIT = 56 * 1024 * 1024
ROUTE_TILE = 256
EXPERT_ROWS = 512
GATHER_ROWS = 128
GATHER_TILES = 6
COMBINE_WIN = 128
QK_LANES =ML_HEADS * HEAD_PAD + ML_WIDTH
ATTN_HEADS = 2
ATTN_KEYS = 4096


def _cparams(sem):
    return pltpu.CompilerParams(dimension_semantics=sem, vmem_limit_bytes=VMEM_LIMIT)


def _nt(a, b):
    return lax.dot_general(a, b, (((1,), (1,)), ((), ())), preferred_element_type=F32)


def _tn(a, b):
    return lax.dot_general(a, b, (((0,), (0,)), ((), ())), preferred_element_type=F32)


def _dot(a, b, precision=None):
    return jnp.dot(a, b, preferred_element_type=F32, precision=precision)


def _split(x):
    hi = x.astype(BF16)
    return hi, (x - hi.astype(F32)).astype(BF16)


def _dot2(x, w):
    hi, lo = _split(x)
    return _dot(jnp.concatenate([hi, lo], axis=1), jnp.concatenate([w, w], axis=0))


def _plain_norm(x):
    mu = jnp.mean(x, axis=-1, keepdims=True)
    xc = x - mu
    var = jnp.mean(xc * xc, axis=-1, keepdims=True)
    return xc * lax.rsqrt(var + LN_EPS)


def _rms(x):
    return x * lax.rsqrt(jnp.mean(x * x, axis=-1, keepdims=True) + LN_EPS)


def _silu(x):
    return x * (1.0 / (1.0 + jnp.exp(-x)))


def _gelu_tanh(x):
    return 0.5 * x * (1.0 + jnp.tanh(0.7978845608028654 * (x + 0.044715 * (x * x * x))))


def _log_sigmoid(x):
    return jnp.minimum(x, 0.0) - jnp.log(1.0 + jnp.exp(-jnp.abs(x)))


def _ada_kernel(c_ref, w_ref, b_ref, o_ref):
    o_ref[0] = _dot(_silu(c_ref[...]), w_ref[0], precision=HIGHEST) + b_ref[0]


def _ada(cc, w_ada, b_ada):
    L, D, N = w_ada.shape
    tn = 1536
    return pl.pallas_call(
        _ada_kernel,
        grid=(L, N // tn),
        in_specs=[pl.BlockSpec((16, D), lambda l, j: (0, 0)),
                  pl.BlockSpec((1, D, tn), lambda l, j: (l, 0, j)),
                  pl.BlockSpec((1, 1, tn), lambda l, j: (l, 0, j))],
        out_specs=pl.BlockSpec((1, 16, tn), lambda l, j: (l, 0, j)),
        out_shape=jax.ShapeDtypeStruct((L, 16, N), F32),
        compiler_params=_cparams(("parallel", "parallel")),
        name="ada",
    )(cc, w_ada, b_ada.reshape(L, 1, N))


def _inproj_kernel(x_ref, sh_ref, sc_ref, cos_ref, sin_ref, wcat_ref, bcat_ref, qg_ref, kvg_ref, wq_ref, wkv_ref,
                   sglg_ref, sglb_ref, sgw_ref, sgb_ref,
                   ya_ref, q_ref, k_ref, v_ref, zml_ref, g_ref):
    tm = x_ref.shape[1]
    h = _plain_norm(x_ref[0]) * (1.0 + sc_ref[0]) + sh_ref[0]
    z = _dot(h.astype(BF16), wcat_ref[...]) + bcat_ref[...]
    zs = _gelu_tanh(z[:, OFF_SG:OFF_SG + 2 * SG_WIDTH])
    u = zs[:, :SG_WIDTH]
    vn = _plain_norm(zs[:, SG_WIDTH:]) * sglg_ref[...] + sglb_ref[...]
    group = jnp.right_shift(lax.broadcasted_iota(jnp.int32, (SG_CHUNK, SG_WIDTH), 1), SG_GDIM.bit_length() - 1)
    for ch in range(tm // SG_CHUNK):
        rows = slice(ch * SG_CHUNK, (ch + 1) * SG_CHUNK)
        vc = vn[rows].astype(BF16)
        mixed = sgb_ref[...]
        for g in range(SG_GROUPS):
            mixed = mixed + jnp.where(group == g, _dot(sgw_ref[g], vc), 0.0)
        ya_ref[0, rows, :] = (u[rows] * mixed).astype(BF16)
    cosT = cos_ref[...]
    sinS = sin_ref[...]
    lane = lax.broadcasted_iota(jnp.int32, (1, HEAD_PAD), 1)
    first = jnp.bitwise_and(lane, ROPE_AXIS // 2) == 0

    def rope(t):
        partner = jnp.where(first, pltpu.roll(t, HEAD_PAD - ROPE_AXIS // 2, axis=1), pltpu.roll(t, ROPE_AXIS // 2, axis=1))
        return t * cosT + partner * sinS

    cqn = (_rms(z[:, OFF_CQ:OFF_CQ + Q_LORA]) * qg_ref[...]).astype(BF16)
    qq = _dot(cqn, wq_ref[...])
    ckvn = (_rms(z[:, OFF_CKV:OFF_CKV + KV_LORA]) * kvg_ref[...]).astype(BF16)
    kv = _dot(ckvn, wkv_ref[...])
    kr = rope(z[:, OFF_KR:OFF_KR + HEAD_PAD])
    one_lane = jnp.where(lane == MLA_VDIM, 1.0, 0.0)
    hw = MLA_HEADS * HEAD_PAD
    for hd in range(MLA_HEADS):
        cols = slice(hd * HEAD_PAD, (hd + 1) * HEAD_PAD)
        rcols = slice(hw + hd * HEAD_PAD, hw + (hd + 1) * HEAD_PAD)
        q_ref[0, hd] = (rope(qq[:, cols]) * (MLA_SCALE * LOG2E)).astype(BF16)
        k_ref[0, hd] = (kv[:, cols] + kr).astype(BF16)
        v_ref[0, hd] = (kv[:, rcols] + one_lane).astype(BF16)
    zml_ref[0] = z[:, OFF_ML:OFF_ML + 4 * ML_WIDTH]
    g_ref[0] = z[:, OFF_G:OFF_G + LANE]


def _inproj(x, shift, scale, cosT, sinT, p, tm):
    B, T, D = x.shape
    const = lambda shape: pl.BlockSpec(shape, lambda b, i: (0,) * len(shape))
    heads = (B, MLA_HEADS, T, HEAD_PAD)
    return pl.pallas_call(
        _inproj_kernel,
        grid=(B, T // tm),
        in_specs=[pl.BlockSpec((1, tm, D), lambda b, i: (b, i, 0)),
                  pl.BlockSpec((1, 1, D), lambda b, i: (b, 0, 0)),
                  pl.BlockSpec((1, 1, D), lambda b, i: (b, 0, 0)),
                  pl.BlockSpec((tm, HEAD_PAD), lambda b, i: (i, 0)),
                  pl.BlockSpec((tm, HEAD_PAD), lambda b, i: (i, 0)),
                  const((D, N_CAT)), const((1, N_CAT)), const((1, Q_LORA)), const((1, KV_LORA)),
                  const((Q_LORA, MLA_HEADS * HEAD_PAD)), const((KV_LORA, 2 * MLA_HEADS * HEAD_PAD)),
                  const((1, SG_WIDTH)), const((1, SG_WIDTH)), const((SG_GROUPS, SG_CHUNK, SG_CHUNK)),
                  const((SG_CHUNK, SG_WIDTH))],
        out_specs=[pl.BlockSpec((1, tm, SG_WIDTH), lambda b, i: (b, i, 0)),
                   pl.BlockSpec((1, MLA_HEADS, tm, HEAD_PAD), lambda b, i: (b, 0, i, 0)),
                   pl.BlockSpec((1, MLA_HEADS, tm, HEAD_PAD), lambda b, i: (b, 0, i, 0)),
                   pl.BlockSpec((1, MLA_HEADS, tm, HEAD_PAD), lambda b, i: (b, 0, i, 0)),
                   pl.BlockSpec((1, tm, 4 * ML_WIDTH), lambda b, i: (b, i, 0)),
                   pl.BlockSpec((1, tm, LANE), lambda b, i: (b, i, 0))],
        out_shape=[jax.ShapeDtypeStruct((B, T, SG_WIDTH), BF16),
                   jax.ShapeDtypeStruct(heads, BF16), jax.ShapeDtypeStruct(heads, BF16),
                   jax.ShapeDtypeStruct(heads, BF16),
                   jax.ShapeDtypeStruct((B, T, 4 * ML_WIDTH), F32),
                   jax.ShapeDtypeStruct((B, T, LANE), F32)],
        compiler_params=_cparams(("parallel", "parallel")),
        name="inproj",
    )(x, shift, scale, cosT, sinT, p["wcat"], p["bcat"], p["q_norm_g"], p["kv_norm_g"], p["wq"], p["wkv"],
      p["sg_ln_g"], p["sg_ln_b"], p["sg_w"], p["sg_bias"])


def _attn_kernel(*refs, n_src, tk):
    q_ref = refs[0]
    kv_refs = refs[1:1 + 2 * n_src]
    o_ref = refs[1 + 2 * n_src]
    nh, tq = q_ref.shape[1], q_ref.shape[2]
    qs = [q_ref[0, hh] for hh in range(nh)]
    ms = [jnp.full((tq, 1), -jnp.inf, F32) for _ in range(nh)]
    accs = [jnp.zeros((tq, HEAD_PAD), F32) for _ in range(nh)]
    for s in range(n_src):
        k_ref, v_ref = kv_refs[2 * s], kv_refs[2 * s + 1]
        ck = min(tk, k_ref.shape[2])
        for c in range(k_ref.shape[2] // ck):
            rows = slice(c * ck, (c + 1) * ck)
            for hh in range(nh):
                sc = _nt(qs[hh], k_ref[0, hh, rows, :])
                m_new = jnp.maximum(ms[hh], jnp.max(sc, axis=1, keepdims=True))
                pr = jnp.exp2(sc - m_new)
                accs[hh] = accs[hh] * jnp.exp2(ms[hh] - m_new) + _dot(pr.astype(BF16), v_ref[0, hh, rows, :])
                ms[hh] = m_new
    outs = [accs[hh] / accs[hh][:, MLA_VDIM:MLA_VDIM + 1] for hh in range(nh)]
    lane = lax.broadcasted_iota(jnp.int32, (tq, HEAD_PAD), 1)
    for pair in range(nh // 2):
        both = jnp.where(lane < MLA_VDIM, outs[2 * pair], pltpu.roll(outs[2 * pair + 1], MLA_VDIM, axis=1))
        o_ref[0, :, pair * HEAD_PAD:(pair + 1) * HEAD_PAD] = both.astype(BF16)


def _attention(q, kvs, tq, tk=ATTN_KEYS, nh=ATTN_HEADS):
    B, H, T, _ = q.shape
    in_specs = [pl.BlockSpec((1, nh, tq, HEAD_PAD), lambda b, hp, i: (b, hp, i, 0))]
    args = [q]
    for k, v in kvs:
        K = k.shape[2]
        in_specs += [pl.BlockSpec((1, nh, K, HEAD_PAD), lambda b, hp, i: (b, hp, 0, 0))] * 2
        args += [k, v]
    return pl.pallas_call(
        functools.partial(_attn_kernel, n_src=len(kvs), tk=tk),
        grid=(B, H // nh, T // tq),
        in_specs=in_specs,
        out_specs=pl.BlockSpec((1, tq, nh * MLA_VDIM), lambda b, hp, i: (b, i, hp)),
        out_shape=jax.ShapeDtypeStruct((B, T, H * MLA_VDIM), BF16),
        compiler_params=_cparams(("parallel", "parallel", "parallel")),
        name="attention",
    )(*args)


def _qkconv_kernel(zp_ref, zc_ref, zn_ref, cw_ref, cb_ref, o_ref):
    tc = zc_ref.shape[1]
    i = pl.program_id(1)
    prev = jnp.where(i > 0, zp_ref[0], 0.0)
    nxt = jnp.where(i < pl.num_programs(1) - 1, zn_ref[0], 0.0)
    big = jnp.concatenate([prev, zc_ref[0], nxt], axis=0)
    n_big = tc + 16
    pad = (ML_CONV - 1) // 2
    conv = cb_ref[...]
    for t in range(ML_CONV):
        s = t - pad
        shifted = big if s == 0 else pltpu.roll(big, (-s) % n_big, axis=0)
        conv = conv + shifted[8:8 + tc] * cw_ref[t:t + 1, :]
    qk = _silu(conv)
    lane = lax.broadcasted_iota(jnp.int32, (1, HEAD_PAD), 1)
    for hd in range(ML_HEADS):
        q_blk = qk[:, (hd // 2) * HEAD_PAD:(hd // 2 + 1) * HEAD_PAD]
        in_half = (lane >= ML_DIM) if hd % 2 else (lane < ML_DIM)
        o_ref[0, :, hd * HEAD_PAD:(hd + 1) * HEAD_PAD] = jnp.where(in_half, q_blk, 0.0).astype(BF16)
    o_ref[0, :, ML_HEADS * HEAD_PAD:] = (qk[:, ML_WIDTH:] * (ML_DIM ** -0.5)).astype(BF16)


def _qkconv(zml, p, tc):
    B, T, _ = zml.shape
    hb = tc // 8
    const = lambda shape: pl.BlockSpec(shape, lambda b, i: (0,) * len(shape))
    return pl.pallas_call(
        _qkconv_kernel,
        grid=(B, T // tc),
        in_specs=[pl.BlockSpec((1, 8, 2 * ML_WIDTH), lambda b, i: (b, jnp.maximum(i * hb - 1, 0), 0)),
                  pl.BlockSpec((1, tc, 2 * ML_WIDTH), lambda b, i: (b, i, 0)),
                  pl.BlockSpec((1, 8, 2 * ML_WIDTH), lambda b, i: (b, jnp.minimum((i + 1) * hb, T // 8 - 1), 0)),
                  const((8, 2 * ML_WIDTH)), const((1, 2 * ML_WIDTH))],
        out_specs=pl.BlockSpec((1, tc, QK_LANES), lambda b, i: (b, i, 0)),
        out_shape=jax.ShapeDtypeStruct((B, T, QK_LANES), BF16),
        compiler_params=_cparams(("parallel", "parallel")),
        name="qkconv",
    )(zml, zml, zml, p["conv_w"], p["conv_b"])


def _mlstm_direction(d, qk, vv, gl, fb, c_prev, m_prev):
    L = qk.shape[0]
    lf = _log_sigmoid(pltpu.roll(gl, LANE - 2 * ML_HEADS, axis=1) + fb)
    ks = lax.broadcasted_iota(jnp.int32, (L, L), 0)
    qt = lax.broadcasted_iota(jnp.int32, (L, L), 1)
    visible = (ks >= qt) if d else (ks <= qt)
    tri = jnp.where((qt >= ks) if d else (qt <= ks), 1.0, 0.0).astype(BF16)
    lf_hi, lf_lo = _split(lf)
    bc = _dot(tri, lf_hi) + _dot(tri, lf_lo)
    r_cols = gl - bc
    b8 = bc.T[0:2 * ML_HEADS]
    li8 = gl.T[0:2 * ML_HEADS]
    b_end = b8[:, 0:1] if d else b8[:, L - 1:L]
    gw = b_end - b8 + li8
    m_loc = jnp.max(gw, axis=1, keepdims=True)
    w8 = jnp.exp(gw - m_loc)
    m_next = jnp.maximum(b_end + m_prev, m_loc)
    a_sc = jnp.exp(b_end + m_prev - m_next)
    s_sc = jnp.exp(m_loc - m_next)
    a8 = b8 + m_prev

    srow = lax.broadcasted_iota(jnp.int32, (HEAD_PAD, L), 0)
    h_blocks, c_next = [], []
    for blk in range(ML_HEADS // 2):
        km = qk[:, (ML_HEADS + blk) * HEAD_PAD:(ML_HEADS + blk + 1) * HEAD_PAD]
        v_t = vv[:, blk * HEAD_PAD:(blk + 1) * HEAD_PAD].T
        hh = []
        for half in range(2):
            hd = 2 * blk + half
            g = d * ML_HEADS + hd
            in_half = (srow >= ML_DIM) if half else (srow < ML_DIM)
            den_row = 0 if half else ML_DIM
            qm = qk[:, hd * HEAD_PAD:(hd + 1) * HEAD_PAD]
            vaug_t = jnp.where(in_half, v_t, jnp.where(srow == den_row, 1.0, 0.0))
            dm = jnp.where(visible, r_cols[:, g:g + 1] + b8[g:g + 1, :], -jnp.inf)
            a_row = a8[g:g + 1, :]
            m_t = jnp.maximum(a_row, jnp.max(dm, axis=0, keepdims=True))
            pm = jnp.exp(dm - m_t) * _nt(km, qm)
            c0 = c_prev[hd]
            nd = (jnp.exp(a_row - m_t) * _nt(c0.astype(BF16), qm)
                  + _dot(vaug_t.astype(BF16), pm.astype(BF16)))
            den = nd[den_row:den_row + 1, :]
            hh.append(nd / jnp.maximum(jnp.abs(den), jnp.exp(-m_t)))
            c_loc = _dot((vaug_t * w8[g:g + 1, :]).astype(BF16), km)
            c_next.append(a_sc[g:g + 1, 0:HEAD_PAD] * c0 + s_sc[g:g + 1, 0:HEAD_PAD] * c_loc)
        h_blocks.append(jnp.where(srow < ML_DIM, hh[0], hh[1]).T)
    grow = lax.broadcasted_iota(jnp.int32, (2 * ML_HEADS, L), 0)
    mine = (grow >= d * ML_HEADS) & (grow < (d + 1) * ML_HEADS)
    return h_blocks, c_next, jnp.where(mine, m_next, m_prev)


def _mlstm_kernel(qkf_ref, vf_ref, gf_ref, qkb_ref, vb_ref, gb_ref, fb_ref, ci_ref, mi_ref,
                  hf_ref, hb_ref, cs_ref, ms_ref):
    @pl.when(pl.program_id(1) == 0)
    def _():
        cs_ref[...] = ci_ref[...]
        ms_ref[...] = mi_ref[...]

    rows = lambda hd: slice(hd * HEAD_PAD, (hd + 1) * HEAD_PAD)
    ins = ((qkf_ref, vf_ref, gf_ref), (qkb_ref, vb_ref, gb_ref))
    outs = (hf_ref, hb_ref)
    L = ms_ref.shape[3]
    n_sub = qkf_ref.shape[1] // L
    states = [([cs_ref[0, d, rows(hd), :] for hd in range(ML_HEADS)], ms_ref[0, d]) for d in range(2)]
    for d in range(2):
        qk_ref, v_ref, g_ref = ins[d]
        c_state, m_state = states[d]
        for s in (range(n_sub - 1, -1, -1) if d else range(n_sub)):
            tok = slice(s * L, (s + 1) * L)
            h_blocks, c_state, m_state = _mlstm_direction(d, qk_ref[0, tok, :], v_ref[0, tok, :], g_ref[0, tok, :],
                                                          fb_ref[...], c_state, m_state)
            for blk, hb in enumerate(h_blocks):
                outs[d][0, tok, blk * HEAD_PAD:(blk + 1) * HEAD_PAD] = hb
        states[d] = (c_state, m_state)
    for d in range(2):
        for hd in range(ML_HEADS):
            cs_ref[0, d, rows(hd), :] = states[d][0][hd]
        ms_ref[0, d] = states[d][1]


def _mlstm(qk, zml, gates, p, c_init, m_init):
    B, T, _ = zml.shape
    chunk = min(ML_CHUNK, T)
    L = min(ML_BLOCK, T)
    nc = T // L
    const = lambda shape: pl.BlockSpec(shape, lambda b, j: (0,) * len(shape))
    state = pl.BlockSpec((1, 2, ML_HEADS * HEAD_PAD, HEAD_PAD), lambda b, j: (b, 0, 0, 0))
    mstate = pl.BlockSpec((1, 2, 2 * ML_HEADS, chunk), lambda b, j: (b, 0, 0, 0))
    fwd = lambda b, j: (b, j, 0)
    bwd = lambda b, j: (b, nc - 1 - j, 0)
    return pl.pallas_call(
        _mlstm_kernel,
        grid=(B, nc),
        in_specs=[pl.BlockSpec((1, L, QK_LANES), fwd),
                  pl.BlockSpec((1, L, ML_WIDTH), lambda b, j: (b, j, 2)),
                  pl.BlockSpec((1, L, LANE), fwd),
                  pl.BlockSpec((1, L, QK_LANES), bwd),
                  pl.BlockSpec((1, L, ML_WIDTH), lambda b, j: (b, nc - 1 - j, 2)),
                  pl.BlockSpec((1, L, LANE), bwd),
                  const((1, LANE)), state, mstate],
        out_specs=[pl.BlockSpec((1, L, ML_WIDTH), fwd), pl.BlockSpec((1, L, ML_WIDTH), bwd), state, mstate],
        out_shape=[jax.ShapeDtypeStruct((B, T, ML_WIDTH), F32), jax.ShapeDtypeStruct((B, T, ML_WIDTH), F32),
                   jax.ShapeDtypeStruct((B, 2, ML_HEADS * HEAD_PAD, HEAD_PAD), F32),
                   jax.ShapeDtypeStruct((B, 2, 2 * ML_HEADS, chunk), F32)],
        compiler_params=_cparams(("parallel", "arbitrary")),
        name="mlstm",
    )(qk, zml, gates, qk, zml, gates, p["f_bias"], c_init, m_init)


def _outproj_kernel(x_ref, g1_ref, sh2_ref, sc2_ref, ya_ref, yb_ref, h0_ref, h1_ref, o_ref, mg_ref, avg_ref,
                    wo_ref, l1g_ref, l1b_ref, wr_ref, x1_ref, hm_ref, aff_ref):
    hs = h0_ref[0] + h1_ref[0]
    mu = _dot2(hs, avg_ref[...])
    hc = hs - mu
    var = _dot2(hc * hc, avg_ref[...])
    yc = (1.0 / (1.0 + jnp.exp(-o_ref[0]))) * (hc * lax.rsqrt(var + LN_EPS) * mg_ref[...])
    y = _dot(jnp.concatenate([ya_ref[0], yb_ref[0], yc.astype(BF16)], axis=1), wo_ref[...])
    x1 = _plain_norm(ALPHA * x_ref[0] + g1_ref[0] * y) * l1g_ref[...] + l1b_ref[...]
    x1_ref[0] = x1
    hm = _plain_norm(x1) * (1.0 + sc2_ref[0]) + sh2_ref[0]
    hm_ref[0] = hm.astype(BF16)
    hi, lo = _split(hm)
    parts = _dot(jnp.concatenate([hi, lo], axis=1), wr_ref[...])
    logits = (parts[:, :LANE] + parts[:, LANE:]).T[:N_EXPERTS, :]
    ex = jnp.exp(logits - jnp.max(logits, axis=0, keepdims=True))
    aff_ref[0] = ex / jnp.sum(ex, axis=0, keepdims=True)


def _outproj(x, g1, sh2, sc2, ya, yb, h_fwd, h_bwd, zml, p, tm):
    B, T, D = x.shape
    const = lambda shape: pl.BlockSpec(shape, lambda b, i: (0,) * len(shape))
    vec = pl.BlockSpec((1, 1, D), lambda b, i: (b, 0, 0))
    return pl.pallas_call(
        _outproj_kernel,
        grid=(B, T // tm),
        in_specs=[pl.BlockSpec((1, tm, D), lambda b, i: (b, i, 0)), vec, vec, vec,
                  pl.BlockSpec((1, tm, SG_WIDTH), lambda b, i: (b, i, 0)),
                  pl.BlockSpec((1, tm, MLA_HEADS * MLA_VDIM), lambda b, i: (b, i, 0)),
                  pl.BlockSpec((1, tm, ML_WIDTH), lambda b, i: (b, i, 0)),
                  pl.BlockSpec((1, tm, ML_WIDTH), lambda b, i: (b, i, 0)),
                  pl.BlockSpec((1, tm, ML_WIDTH), lambda b, i: (b, i, 3)),
                  const((1, ML_WIDTH)), const((ML_WIDTH, ML_WIDTH)), const((D, D)), const((1, D)), const((1, D)),
                  const((2 * D, 2 * LANE))],
        out_specs=[pl.BlockSpec((1, tm, D), lambda b, i: (b, i, 0)),
                   pl.BlockSpec((1, tm, D), lambda b, i: (b, i, 0)),
                   pl.BlockSpec((1, N_EXPERTS, tm), lambda b, i: (b, 0, i))],
        out_shape=[jax.ShapeDtypeStruct((B, T, D), F32), jax.ShapeDtypeStruct((B, T, D), BF16),
                   jax.ShapeDtypeStruct((B, N_EXPERTS, T), F32)],
        compiler_params=_cparams(("parallel", "parallel")),
        name="outproj",
    )(x, g1, sh2, sc2, ya, yb, h_fwd, h_bwd, zml, p["ml_norm_g"], p["head_avg"], p["w_out"], p["ln1_g"], p["ln1_b"],
      p["w_router3"])


def _route_kernel(aff_ref, pos_ref, cnt_ref, *, cap, tw):
    aff = aff_ref[0]
    E, T = aff.shape
    bits = pltpu.bitcast(aff, jnp.int32)
    count = lambda m: jnp.sum(jnp.where(m, 1.0, 0.0), axis=1, keepdims=True)
    thr = jnp.zeros((E, 1), jnp.int32)
    for bit in range(30, -1, -1):
        cand = thr | (1 << bit)
        thr = jnp.where(count(bits >= cand) >= cap, cand, thr)
    above = bits > thr
    tied = bits == thr
    need = cap - count(above)
    idx = lax.broadcasted_iota(jnp.int32, (E, T), 1)
    cut = jnp.zeros((E, 1), jnp.int32)
    for bit in range(T.bit_length() - 1, -1, -1):
        cand = cut | (1 << bit)
        ok = (cand <= T) & (count(tied & (idx < cand)) <= need)
        cut = jnp.where(ok, cand, cut)
    sel = above | (tied & (idx < cut))
    r = lax.broadcasted_iota(jnp.int32, (tw, tw), 0)
    cidx = lax.broadcasted_iota(jnp.int32, (tw, tw), 1)
    before = jnp.where(r < cidx, 1.0, 0.0).astype(BF16)
    offset = jnp.zeros((E, 1), F32)
    lane = lax.broadcasted_iota(jnp.int32, (E, LANE), 1)
    table = jnp.zeros((E, LANE), F32)
    for blk in range(T // tw):
        cols = slice(blk * tw, (blk + 1) * tw)
        sb = jnp.where(sel[:, cols], 1.0, 0.0)
        slot = _dot(sb.astype(BF16), before) + offset
        pos_ref[0, :, cols] = jnp.where(sel[:, cols], slot, -1.0).astype(jnp.int32)
        offset = offset + jnp.sum(sb, axis=1, keepdims=True)
        table = jnp.where(lane == blk + 1, offset, table)
    cnt_ref[0] = table.astype(jnp.int32)


def _route(aff_t, cap):
    B, E, T = aff_t.shape
    nt = T // ROUTE_TILE
    pos, cnt = pl.pallas_call(
        functools.partial(_route_kernel, cap=cap, tw=ROUTE_TILE),
        grid=(B,),
        in_specs=[pl.BlockSpec((1, E, T), lambda b: (b, 0, 0))],
        out_specs=[pl.BlockSpec((1, E, T), lambda b: (b, 0, 0)), pl.BlockSpec((1, E, LANE), lambda b: (b, 0, 0))],
        out_shape=[jax.ShapeDtypeStruct((B, E, T), jnp.int32), jax.ShapeDtypeStruct((B, E, LANE), jnp.int32)],
        compiler_params=_cparams(("parallel",)),
        name="route",
    )(aff_t)
    return pos, cnt[:, :, :nt + 1].reshape(-1)


def _expert_kernel(cnt_ref, pos_ref, aff_ref, h_ref, wg_ref, wu_ref, wd_ref, y_ref, xe_ref, gate_ref,
                   wgb_ref, wub_ref, wdb_ref, *, n_win):
    nb, cap = y_ref.shape[0], y_ref.shape[2]
    nt = pos_ref.shape[2]
    rb = min(GATHER_ROWS, cap)
    e, j = pl.program_id(0), pl.program_id(1)

    @pl.when(j == 0)
    def _():
        wgb_ref[...] = wg_ref[0, 0].astype(BF16)
        wub_ref[...] = wu_ref[0, 0].astype(BF16)
        wdb_ref[...] = wd_ref[0, 0].astype(BF16)

    slot_iota = lax.broadcasted_iota(jnp.int32, (rb, ROUTE_TILE), 0)

    def gather_tile(bb, i, s0):
        hit = pos_ref[bb, 0, pl.ds(i, 1), :] == slot_iota + s0
        rows = pl.ds(pl.multiple_of(i * ROUTE_TILE, ROUTE_TILE), ROUTE_TILE)
        xe = _dot(jnp.where(hit, 1.0, 0.0).astype(BF16), h_ref[bb, rows, :])
        gate = jnp.sum(jnp.where(hit, aff_ref[bb, 0, pl.ds(i, 1), :], 0.0), axis=1, keepdims=True)
        return xe, gate

    for bb in range(nb):
        base = ((j * nb + bb) * N_EXPERTS + e) * (nt + 1)
        for r in range(cap // rb):
            s0 = r * rb
            first = sum((cnt_ref[base + i + 1] <= s0).astype(jnp.int32) for i in range(nt))
            end = sum((cnt_ref[base + i] < s0 + rb).astype(jnp.int32) for i in range(nt))
            start = jnp.minimum(first, nt - n_win)
            parts = [gather_tile(bb, start + w, s0) for w in range(n_win)]
            rows = slice(bb * cap + s0, bb * cap + s0 + rb)
            xe_ref[rows, :] = sum(pt[0] for pt in parts)
            gate_ref[rows, :] = sum(pt[1] for pt in parts)

            def overflow(i, carry, bb=bb, s0=s0, rows=rows):
                xe, gate = gather_tile(bb, i, s0)
                xe_ref[rows, :] += xe
                gate_ref[rows, :] += gate
                return carry

            lax.fori_loop(start + n_win, end, overflow, 0)
    xb = xe_ref[...].astype(BF16)
    hid = _silu(_dot(xb, wgb_ref[...])) * _dot(xb, wub_ref[...])
    y = (_dot(hid.astype(BF16), wdb_ref[...]) * gate_ref[...]).astype(BF16)
    for bb in range(nb):
        y_ref[bb, 0] = y[bb * cap:(bb + 1) * cap]


def _experts(pos, cnt, aff_t, hm, p, cap):
    B, T, D = hm.shape
    E = N_EXPERTS
    nt = T // ROUTE_TILE
    layer = p["layer"]
    nb = max(1, min(B, EXPERT_ROWS // cap))
    row = pl.BlockSpec((nb, 1, nt, ROUTE_TILE), lambda e, j, c: (j, e, 0, 0))
    wspec = lambda shape: pl.BlockSpec((1, 1) + shape, lambda e, j, c: (layer, e, 0, 0))
    return pl.pallas_call(
        functools.partial(_expert_kernel, n_win=min(GATHER_TILES, nt)),
        grid_spec=pltpu.PrefetchScalarGridSpec(
            num_scalar_prefetch=1,
            grid=(E, B // nb),
            in_specs=[row, row,
                      pl.BlockSpec((nb, T, D), lambda e, j, c: (j, 0, 0)),
                      wspec((D, EXPERT_FF)), wspec((D, EXPERT_FF)), wspec((EXPERT_FF, D))],
            out_specs=pl.BlockSpec((nb, 1, cap, D), lambda e, j, c: (j, e, 0, 0)),
            scratch_shapes=[pltpu.VMEM((nb * cap, D), F32), pltpu.VMEM((nb * cap, 1), F32),
                            pltpu.VMEM((D, EXPERT_FF), BF16), pltpu.VMEM((D, EXPERT_FF), BF16),
                            pltpu.VMEM((EXPERT_FF, D), BF16)]),
        out_shape=jax.ShapeDtypeStruct((B, E, cap, D), BF16),
        compiler_params=_cparams(("parallel", "arbitrary")),
        name="experts",
    )(cnt, pos.reshape(B, E, nt, ROUTE_TILE), aff_t.reshape(B, E, nt, ROUTE_TILE), hm,
      p["w_gate"], p["w_up"], p["w_down"])


def _combine_kernel(cnt_ref, pos_ref, y_ref, x_ref, g2_ref, lg_ref, lb_ref, o_ref, ycat_ref, acc_ref, *, nt):
    E, cap = y_ref.shape[1], y_ref.shape[2]
    tt = x_ref.shape[1]
    win = min(COMBINE_WIN, cap)
    per = tt // ROUTE_TILE
    b, i = pl.program_id(0), pl.program_id(1)
    slot_iota = lax.broadcasted_iota(jnp.int32, (win, tt), 0)

    def window(e, w):
        start = pl.multiple_of(jnp.minimum(w, cap - win), 16)
        slot = slot_iota + start
        hit = (pos_ref[0, e:e + 1, :] == slot) & (slot >= w)
        return jnp.where(hit, 1.0, 0.0).astype(BF16), y_ref[0, e, pl.ds(start, win), :]

    onehots, ranges = [], []
    for e in range(E):
        base = (b * E + e) * (nt + 1) + i * per
        lo, hi = cnt_ref[base], cnt_ref[base + per]
        w0 = jnp.bitwise_and(lo, ~15)
        onehot, rows = window(e, w0)
        ycat_ref[e * win:(e + 1) * win, :] = rows
        onehots.append(onehot)
        ranges.append((w0, hi))
    acc_ref[...] = _tn(jnp.concatenate(onehots, axis=0), ycat_ref[...])
    for e in range(E):
        w0, hi = ranges[e]

        def more(w, e=e):
            onehot, rows = window(e, w)
            acc_ref[...] += _tn(onehot, rows)
            return w + win

        lax.while_loop(lambda w, hi=hi: w < hi, more, w0 + win)
    o_ref[0] = _plain_norm(ALPHA * x_ref[0] + g2_ref[0] * acc_ref[...]) * lg_ref[...] + lb_ref[...]


def _combine(pos, cnt, ye, x1, g2, p, tt):
    B, T, D = x1.shape
    E, cap = ye.shape[1], ye.shape[2]
    win = min(COMBINE_WIN, cap)
    const = lambda shape: pl.BlockSpec(shape, lambda b, i, c: (0,) * len(shape))
    return pl.pallas_call(
        functools.partial(_combine_kernel, nt=T // ROUTE_TILE),
        grid_spec=pltpu.PrefetchScalarGridSpec(
            num_scalar_prefetch=1,
            grid=(B, T // tt),
            in_specs=[pl.BlockSpec((1, E, tt), lambda b, i, c: (b, 0, i)),
                      pl.BlockSpec((1, E, cap, D), lambda b, i, c: (b, 0, 0, 0)),
                      pl.BlockSpec((1, tt, D), lambda b, i, c: (b, i, 0)),
                      pl.BlockSpec((1, 1, D), lambda b, i, c: (b, 0, 0)),
                      const((1, D)), const((1, D))],
            out_specs=pl.BlockSpec((1, tt, D), lambda b, i, c: (b, i, 0)),
            scratch_shapes=[pltpu.VMEM((E * win, D), BF16), pltpu.VMEM((tt, D), F32)]),
        out_shape=jax.ShapeDtypeStruct((B, T, D), F32),
        compiler_params=_cparams(("parallel", "arbitrary")),
        name="combine",
    )(cnt, pos, ye, x1, g2, p["ln2_g"], p["ln2_b"])


def _pad_cols(a, before, total):
    pads = [(0, 0)] * (a.ndim - 1) + [(before, total - before - a.shape[-1])]
    return jnp.pad(a, pads)


def _prep_layer(l, w):
    D = D_MODEL
    wi, bi = w["w_in"][l], w["b_in"][l][None, :]
    n_sg, n_mla = 2 * SG_WIDTH, Q_LORA + KV_LORA + MLA_ROPE
    o_kr = n_sg + Q_LORA + KV_LORA
    o_ml = n_sg + n_mla
    o_g = o_ml + 4 * ML_WIDTH

    def cat(a):
        kr = a[:, o_kr:o_kr + MLA_ROPE]
        gates = a[:, o_g:o_g + 4 * ML_HEADS].reshape(-1, 2, 2, ML_HEADS).transpose(0, 2, 1, 3).reshape(-1, 4 * ML_HEADS)
        return jnp.concatenate([a[:, :o_kr], _pad_cols(kr, MLA_NOPE, HEAD_PAD), a[:, o_ml:o_g],
                                _pad_cols(gates, 0, LANE)], axis=1)

    wq = w["w_uq"][l].reshape(Q_LORA, MLA_HEADS, MLA_NOPE + MLA_ROPE)
    wq_main = _pad_cols(wq, 0, HEAD_PAD).reshape(Q_LORA, -1)
    wkv = w["w_ukv"][l].reshape(KV_LORA, MLA_HEADS, MLA_NOPE + MLA_VDIM)
    wk = _pad_cols(wkv[..., :MLA_NOPE], 0, HEAD_PAD).reshape(KV_LORA, -1)
    wv = _pad_cols(wkv[..., MLA_NOPE:], 0, HEAD_PAD).reshape(KV_LORA, -1)
    head = jnp.arange(ML_WIDTH) // ML_DIM
    wr = _pad_cols(w["w_router"][l], 0, LANE)
    wr_hi = wr.astype(BF16)
    return {
        "wcat": cat(wi).astype(BF16), "bcat": cat(bi),
        "q_norm_g": w["q_norm_g"][l][None, :], "kv_norm_g": w["kv_norm_g"][l][None, :],
        "wq": wq_main.astype(BF16),
        "wkv": jnp.concatenate([wk, wv], axis=1).astype(BF16),
        "sg_ln_g": w["sg_ln_g"][l][None, :], "sg_ln_b": w["sg_ln_b"][l][None, :],
        "sg_w": w["sg_w"][l].astype(BF16),
        "sg_bias": jnp.repeat(w["sg_b"][l].T, SG_GDIM, axis=1),
        "conv_w": jnp.pad(w["ml_conv_w"][l], ((0, 8 - ML_CONV), (0, 0))), "conv_b": w["ml_conv_b"][l][None, :],
        "f_bias": _pad_cols(w["ml_f_bias"][l].reshape(1, 2 * ML_HEADS), 0, LANE),
        "ml_norm_g": w["ml_norm_g"][l][None, :],
        "head_avg": ((head[:, None] == head[None, :]).astype(F32) / ML_DIM).astype(BF16),
        "w_out": w["w_out"][l].astype(BF16),
        "ln1_g": w["ln1_g"][l][None, :], "ln1_b": w["ln1_b"][l][None, :],
        "w_router3": jnp.concatenate([jnp.concatenate([wr_hi, (wr - wr_hi.astype(F32)).astype(BF16)], axis=1),
                                      jnp.concatenate([wr_hi, jnp.zeros_like(wr_hi)], axis=1)], axis=0),
        "layer": l, "w_gate": w["w_gate"], "w_up": w["w_up"], "w_down": w["w_down"],
        "ln2_g": w["ln2_g"][l][None, :], "ln2_b": w["ln2_b"][l][None, :],
    }


def _rope_tables(T):
    rows = T // GRID_W
    row = jnp.repeat(jnp.arange(rows, dtype=F32), GRID_W)
    colv = jnp.tile(jnp.arange(GRID_W, dtype=F32), rows)
    inv = ROPE_BASE ** (-jnp.arange(ROPE_AXIS // 2, dtype=F32) * 2.0 / ROPE_AXIS)
    ang = jnp.concatenate([row[:, None] * inv, colv[:, None] * inv], axis=-1)
    half = ROPE_AXIS // 2
    spread = lambda t: jnp.concatenate([t[:, :half], t[:, :half], t[:, half:], t[:, half:]], axis=1)
    cosT = jnp.concatenate([jnp.ones((T, MLA_NOPE), F32), spread(jnp.cos(ang)), jnp.zeros((T, 32), F32)], axis=1)
    sign = jnp.tile(jnp.repeat(jnp.asarray([-1.0, 1.0], F32), half), 2)
    sinS = _pad_cols(spread(jnp.sin(ang)) * sign, MLA_NOPE, HEAD_PAD)
    return cosT, sinS


def _moe(x1, hm, aff_t, g2, p, tt):
    cap = EC_CAPACITY * x1.shape[1] // N_EXPERTS
    pos, cnt = _route(aff_t, cap)
    ye = _experts(pos, cnt, aff_t, hm, p, cap)
    return _combine(pos, cnt, ye, x1, g2, p, tt)


def _forward(x, c, ctx, c_ctx, w):
    B, T, D = x.shape
    Tc = ctx.shape[1]
    L = w["w_ada"].shape[0]
    cc = jnp.zeros((16, D), F32).at[:B].set(c).at[B].set(c_ctx)
    ada = _ada(cc, w["w_ada"], w["b_ada"])
    cos_l, sin_l = _rope_tables(T)
    cos_c = _pad_cols(jnp.ones((Tc, MLA_NOPE + MLA_ROPE), F32), 0, HEAD_PAD)
    sin_c = jnp.zeros((Tc, HEAD_PAD), F32)
    zero_c = jnp.zeros((B, 2, ML_HEADS * HEAD_PAD, HEAD_PAD), F32)
    zero_m = jnp.zeros((B, 2, 2 * ML_HEADS, min(ML_CHUNK, Tc)), F32)
    tm = min(1024, T)
    xl, xc = x, ctx
    for l in range(L):
        p = _prep_layer(l, w)
        mods_l = [ada[l, :B, i * D:(i + 1) * D][:, None, :] for i in range(6)]
        mods_c = [jnp.broadcast_to(ada[l, B, i * D:(i + 1) * D][None, None, :], (B, 1, D)) for i in range(6)]
        update_ctx = l < L - 1
        ya_c, q_c, k_c, v_c, zml_c, g_c = _inproj(xc, mods_c[0], mods_c[1], cos_c, sin_c, p, min(tm, Tc))
        ya_l, q_l, k_l, v_l, zml_l, g_l = _inproj(xl, mods_l[0], mods_l[1], cos_l, sin_l, p, tm)
        hf_c, hb_c, cs_c, ms_c = _mlstm(_qkconv(zml_c, p, min(512, Tc)), zml_c, g_c, p, zero_c, zero_m)
        hf_l, hb_l, _, _ = _mlstm(_qkconv(zml_l, p, min(512, T)), zml_l, g_l, p, cs_c, ms_c)
        yb_l = _attention(q_l, [(k_l, v_l), (k_c, v_c)], tq=min(512, T))
        x1_l, hm_l, aff_l = _outproj(xl, mods_l[2], mods_l[3], mods_l[4], ya_l, yb_l, hf_l, hb_l, zml_l, p, tm)
        xl = _moe(x1_l, hm_l, aff_l, mods_l[5], p, tt=min(512, T))
        if update_ctx:
            yb_c = _attention(q_c, [(k_c, v_c)], tq=min(256, Tc))
            x1_c, hm_c, aff_c = _outproj(xc, mods_c[2], mods_c[3], mods_c[4], ya_c, yb_c, hf_c, hb_c, zml_c, p,
                                         min(tm, Tc))
            xc = _moe(x1_c, hm_c, aff_c, mods_c[5], p, tt=min(512, Tc))
    return xl


def kernel(x, c, ctx, c_ctx, w_ada, b_ada, w_in, b_in, sg_ln_g, sg_ln_b, sg_w, sg_b, q_norm_g, kv_norm_g, w_uq, w_ukv,
           ml_conv_w, ml_conv_b, ml_f_bias, ml_norm_g, w_out, ln1_g, ln1_b, w_router, w_gate, w_up, w_down, ln2_g, ln2_b):
    w = dict(w_ada=w_ada, b_ada=b_ada, w_in=w_in, b_in=b_in, sg_ln_g=sg_ln_g, sg_ln_b=sg_ln_b, sg_w=sg_w, sg_b=sg_b,
             q_norm_g=q_norm_g, kv_norm_g=kv_norm_g, w_uq=w_uq, w_ukv=w_ukv, ml_conv_w=ml_conv_w, ml_conv_b=ml_conv_b,
             ml_f_bias=ml_f_bias, ml_norm_g=ml_norm_g, w_out=w_out, ln1_g=ln1_g, ln1_b=ln1_b, w_router=w_router,
             w_gate=w_gate, w_up=w_up, w_down=w_down, ln2_g=ln2_g, ln2_b=ln2_b)
    return _forward(x, c, ctx, c_ctx, w)
```

```python
import functools

import jax
import jax.numpy as jnp
from jax import lax
from jax.experimental import pallas as pl
from jax.experimental.pallas import tpu as pltpu

F32 = jnp.float32
BF16 = jnp.bfloat16
HIGHEST = lax.Precision.HIGHEST

D_MODEL = 1024
DEPTH = 2
GRID_W = 64
SG_WIDTH = 256
SG_GROUPS = 4
SG_GDIM = 64
SG_CHUNK = 128
MLA_HEADS = 8
MLA_NOPE = 64
MLA_ROPE = 32
MLA_VDIM = 64
Q_LORA = 384
KV_LORA = 256
ROPE_AXIS = 16
ROPE_BASE = 10000.0
MLA_SCALE = (MLA_NOPE + MLA_ROPE) ** -0.5
ML_HEADS = 4
ML_DIM = 64
ML_WIDTH = 256
ML_CONV = 5
N_EXPERTS = 16
EXPERT_FF = 1024
EC_CAPACITY = 2
ALPHA = (2 * DEPTH) ** 0.25
LN_EPS = 1e-6
LOG2E = 1.4426950408889634

LANE = 128
HEAD_PAD = 128
ML_CHUNK = 256
ML_BLOCK = 512
OFF_SG, OFF_CQ, OFF_CKV, OFF_KR, OFF_ML, OFF_G, N_CAT = 0, 512, 896, 1152, 1280, 2304, 2432
VMEM_LIMIT = 56 * 1024 * 1024
ROUTE_TILE = 256
EXPERT_ROWS = 512
GATHER_ROWS = 128
GATHER_TILES = 6
COMBINE_WIN = 128
QK_LANES =ML_HEADS * HEAD_PAD + ML_WIDTH
ATTN_HEADS = 2
ATTN_KEYS = 4096


def _cparams(sem):
    return pltpu.CompilerParams(dimension_semantics=sem, vmem_limit_bytes=VMEM_LIMIT)


def _nt(a, b):
    return lax.dot_general(a, b, (((1,), (1,)), ((), ())), preferred_element_type=F32)


def _tn(a, b):
    return lax.dot_general(a, b, (((0,), (0,)), ((), ())), preferred_element_type=F32)


def _dot(a, b, precision=None):
    return jnp.dot(a, b, preferred_element_type=F32, precision=precision)


def _split(x):
    hi = x.astype(BF16)
    return hi, (x - hi.astype(F32)).astype(BF16)


def _dot2(x, w):
    hi, lo = _split(x)
    return _dot(jnp.concatenate([hi, lo], axis=1), jnp.concatenate([w, w], axis=0))


def _plain_norm(x):
    mu = jnp.mean(x, axis=-1, keepdims=True)
    xc = x - mu
    var = jnp.mean(xc * xc, axis=-1, keepdims=True)
    return xc * lax.rsqrt(var + LN_EPS)


def _rms(x):
    return x * lax.rsqrt(jnp.mean(x * x, axis=-1, keepdims=True) + LN_EPS)


def _silu(x):
    return x * (1.0 / (1.0 + jnp.exp(-x)))


def _gelu_tanh(x):
    return 0.5 * x * (1.0 + jnp.tanh(0.7978845608028654 * (x + 0.044715 * (x * x * x))))


def _log_sigmoid(x):
    return jnp.minimum(x, 0.0) - jnp.log(1.0 + jnp.exp(-jnp.abs(x)))


def _ada_kernel(c_ref, w_ref, b_ref, o_ref):
    o_ref[0] = _dot(_silu(c_ref[...]), w_ref[0], precision=HIGHEST) + b_ref[0]


def _ada(cc, w_ada, b_ada):
    L, D, N = w_ada.shape
    tn = 1536
    return pl.pallas_call(
        _ada_kernel,
        grid=(L, N // tn),
        in_specs=[pl.BlockSpec((16, D), lambda l, j: (0, 0)),
                  pl.BlockSpec((1, D, tn), lambda l, j: (l, 0, j)),
                  pl.BlockSpec((1, 1, tn), lambda l, j: (l, 0, j))],
        out_specs=pl.BlockSpec((1, 16, tn), lambda l, j: (l, 0, j)),
        out_shape=jax.ShapeDtypeStruct((L, 16, N), F32),
        compiler_params=_cparams(("parallel", "parallel")),
        name="ada",
    )(cc, w_ada, b_ada.reshape(L, 1, N))


def _inproj_kernel(x_ref, xp_ref, xn_ref, sh_ref, sc_ref, cos_ref, sin_ref, wcat_ref, bcat_ref, qg_ref, kvg_ref, wq_ref,
                   wkv_ref, sglg_ref, sglb_ref, sgw_ref, sgb_ref, cw_ref, cb_ref,
                   ya_ref, q_ref, k_ref, v_ref, qk_ref, vo_ref, g_ref):
    tm = x_ref.shape[1]
    modulated = lambda t: (_plain_norm(t) * (1.0 + sc_ref[0]) + sh_ref[0]).astype(BF16)
    z = _dot(modulated(x_ref[0]), wcat_ref[...]) + bcat_ref[...]
    i = pl.program_id(1)
    qk_cols = slice(OFF_ML, OFF_ML + 2 * ML_WIDTH)
    halo = _dot(modulated(jnp.concatenate([xp_ref[0], xn_ref[0]], axis=0)), wcat_ref[:, qk_cols]) + bcat_ref[:, qk_cols]
    prev = jnp.where(i > 0, halo[:8], 0.0)
    nxt = jnp.where(i < pl.num_programs(1) - 1, halo[8:], 0.0)
    big = jnp.concatenate([prev, z[:, qk_cols], nxt], axis=0)
    n_big = tm + 16
    pad = (ML_CONV - 1) // 2
    conv = cb_ref[...]
    for t in range(ML_CONV):
        s = t - pad
        shifted = big if s == 0 else pltpu.roll(big, (-s) % n_big, axis=0)
        conv = conv + shifted[8:8 + tm] * cw_ref[t:t + 1, :]
    qkc = _silu(conv)
    hlane = lax.broadcasted_iota(jnp.int32, (1, HEAD_PAD), 1)
    for hd in range(ML_HEADS):
        q_blk = qkc[:, (hd // 2) * HEAD_PAD:(hd // 2 + 1) * HEAD_PAD]
        in_half = (hlane >= ML_DIM) if hd % 2 else (hlane < ML_DIM)
        qk_ref[0, :, hd * HEAD_PAD:(hd + 1) * HEAD_PAD] = jnp.where(in_half, q_blk, 0.0).astype(BF16)
    qk_ref[0, :, ML_HEADS * HEAD_PAD:] = (qkc[:, ML_WIDTH:] * (ML_DIM ** -0.5)).astype(BF16)
    zs = _gelu_tanh(z[:, OFF_SG:OFF_SG + 2 * SG_WIDTH])
    u = zs[:, :SG_WIDTH]
    vn = _plain_norm(zs[:, SG_WIDTH:]) * sglg_ref[...] + sglb_ref[...]
    group = jnp.right_shift(lax.broadcasted_iota(jnp.int32, (SG_CHUNK, SG_WIDTH), 1), SG_GDIM.bit_length() - 1)
    for ch in range(tm // SG_CHUNK):
        rows = slice(ch * SG_CHUNK, (ch + 1) * SG_CHUNK)
        vc = vn[rows].astype(BF16)
        mixed = sgb_ref[...]
        for g in range(SG_GROUPS):
            mixed = mixed + jnp.where(group == g, _dot(sgw_ref[g], vc), 0.0)
        ya_ref[0, rows, :] = (u[rows] * mixed).astype(BF16)
    cosT = cos_ref[...]
    sinS = sin_ref[...]
    lane = lax.broadcasted_iota(jnp.int32, (1, HEAD_PAD), 1)
    first = jnp.bitwise_and(lane, ROPE_AXIS // 2) == 0

    def rope(t):
        partner = jnp.where(first, pltpu.roll(t, HEAD_PAD - ROPE_AXIS // 2, axis=1), pltpu.roll(t, ROPE_AXIS // 2, axis=1))
        return t * cosT + partner * sinS

    cqn = (_rms(z[:, OFF_CQ:OFF_CQ + Q_LORA]) * qg_ref[...]).astype(BF16)
    qq = _dot(cqn, wq_ref[...])
    ckvn = (_rms(z[:, OFF_CKV:OFF_CKV + KV_LORA]) * kvg_ref[...]).astype(BF16)
    kv = _dot(ckvn, wkv_ref[...])
    kr = rope(z[:, OFF_KR:OFF_KR + HEAD_PAD])
    one_lane = jnp.where(lane == MLA_VDIM, 1.0, 0.0)
    hw = MLA_HEADS * HEAD_PAD
    for hd in range(MLA_HEADS):
        cols = slice(hd * HEAD_PAD, (hd + 1) * HEAD_PAD)
        rcols = slice(hw + hd * HEAD_PAD, hw + (hd + 1) * HEAD_PAD)
        q_ref[0, hd] = (rope(qq[:, cols]) * (MLA_SCALE * LOG2E)).astype(BF16)
        k_ref[0, hd] = (kv[:, cols] + kr).astype(BF16)
        v_ref[0, hd] = (kv[:, rcols] + one_lane).astype(BF16)
    vo_ref[0] = z[:, OFF_ML + 2 * ML_WIDTH:OFF_ML + 4 * ML_WIDTH]
    g_ref[0] = z[:, OFF_G:OFF_G + LANE]


def _inproj(x, shift, scale, cosT, sinT, p, tm):
    B, T, D = x.shape
    hb = tm // 8
    const = lambda shape: pl.BlockSpec(shape, lambda b, i: (0,) * len(shape))
    heads = (B, MLA_HEADS, T, HEAD_PAD)
    return pl.pallas_call(
        _inproj_kernel,
        grid=(B, T // tm),
        in_specs=[pl.BlockSpec((1, tm, D), lambda b, i: (b, i, 0)),
                  pl.BlockSpec((1, 8, D), lambda b, i: (b, jnp.maximum(i * hb - 1, 0), 0)),
                  pl.BlockSpec((1, 8, D), lambda b, i: (b, jnp.minimum((i + 1) * hb, T // 8 - 1), 0)),
                  pl.BlockSpec((1, 1, D), lambda b, i: (b, 0, 0)),
                  pl.BlockSpec((1, 1, D), lambda b, i: (b, 0, 0)),
                  pl.BlockSpec((tm, HEAD_PAD), lambda b, i: (i, 0)),
                  pl.BlockSpec((tm, HEAD_PAD), lambda b, i: (i, 0)),
                  const((D, N_CAT)), const((1, N_CAT)), const((1, Q_LORA)), const((1, KV_LORA)),
                  const((Q_LORA, MLA_HEADS * HEAD_PAD)), const((KV_LORA, 2 * MLA_HEADS * HEAD_PAD)),
                  const((1, SG_WIDTH)), const((1, SG_WIDTH)), const((SG_GROUPS, SG_CHUNK, SG_CHUNK)),
                  const((SG_CHUNK, SG_WIDTH)), const((8, 2 * ML_WIDTH)), const((1, 2 * ML_WIDTH))],
        out_specs=[pl.BlockSpec((1, tm, SG_WIDTH), lambda b, i: (b, i, 0)),
                   pl.BlockSpec((1, MLA_HEADS, tm, HEAD_PAD), lambda b, i: (b, 0, i, 0)),
                   pl.BlockSpec((1, MLA_HEADS, tm, HEAD_PAD), lambda b, i: (b, 0, i, 0)),
                   pl.BlockSpec((1, MLA_HEADS, tm, HEAD_PAD), lambda b, i: (b, 0, i, 0)),
                   pl.BlockSpec((1, tm, QK_LANES), lambda b, i: (b, i, 0)),
                   pl.BlockSpec((1, tm, 2 * ML_WIDTH), lambda b, i: (b, i, 0)),
                   pl.BlockSpec((1, tm, LANE), lambda b, i: (b, i, 0))],
        out_shape=[jax.ShapeDtypeStruct((B, T, SG_WIDTH), BF16),
                   jax.ShapeDtypeStruct(heads, BF16), jax.ShapeDtypeStruct(heads, BF16),
                   jax.ShapeDtypeStruct(heads, BF16),
                   jax.ShapeDtypeStruct((B, T, QK_LANES), BF16),
                   jax.ShapeDtypeStruct((B, T, 2 * ML_WIDTH), F32),
                   jax.ShapeDtypeStruct((B, T, LANE), F32)],
        compiler_params=_cparams(("parallel", "parallel")),
        name="inproj",
    )(x, x, x, shift, scale, cosT, sinT, p["wcat"], p["bcat"], p["q_norm_g"], p["kv_norm_g"], p["wq"], p["wkv"],
      p["sg_ln_g"], p["sg_ln_b"], p["sg_w"], p["sg_bias"], p["conv_w"], p["conv_b"])


def _attn_kernel(*refs, n_src, tk):
    q_ref = refs[0]
    kv_refs = refs[1:1 + 2 * n_src]
    o_ref = refs[1 + 2 * n_src]
    nh, tq = q_ref.shape[1], q_ref.shape[2]
    qs = [q_ref[0, hh] for hh in range(nh)]
    ms = [jnp.full((tq, 1), -jnp.inf, F32) for _ in range(nh)]
    accs = [jnp.zeros((tq, HEAD_PAD), F32) for _ in range(nh)]
    for s in range(n_src):
        k_ref, v_ref = kv_refs[2 * s], kv_refs[2 * s + 1]
        ck = min(tk, k_ref.shape[2])
        for c in range(k_ref.shape[2] // ck):
            rows = slice(c * ck, (c + 1) * ck)
            for hh in range(nh):
                sc = _nt(qs[hh], k_ref[0, hh, rows, :])
                m_new = jnp.maximum(ms[hh], jnp.max(sc, axis=1, keepdims=True))
                pr = jnp.exp2(sc - m_new)
                accs[hh] = accs[hh] * jnp.exp2(ms[hh] - m_new) + _dot(pr.astype(BF16), v_ref[0, hh, rows, :])
                ms[hh] = m_new
    outs = [accs[hh] / accs[hh][:, MLA_VDIM:MLA_VDIM + 1] for hh in range(nh)]
    lane = lax.broadcasted_iota(jnp.int32, (tq, HEAD_PAD), 1)
    for pair in range(nh // 2):
        both = jnp.where(lane < MLA_VDIM, outs[2 * pair], pltpu.roll(outs[2 * pair + 1], MLA_VDIM, axis=1))
        o_ref[0, :, pair * HEAD_PAD:(pair + 1) * HEAD_PAD] = both.astype(BF16)


def _attention(q, kvs, tq, tk=ATTN_KEYS, nh=ATTN_HEADS):
    B, H, T, _ = q.shape
    in_specs = [pl.BlockSpec((1, nh, tq, HEAD_PAD), lambda b, hp, i: (b, hp, i, 0))]
    args = [q]
    for k, v in kvs:
        K = k.shape[2]
        in_specs += [pl.BlockSpec((1, nh, K, HEAD_PAD), lambda b, hp, i: (b, hp, 0, 0))] * 2
        args += [k, v]
    return pl.pallas_call(
        functools.partial(_attn_kernel, n_src=len(kvs), tk=tk),
        grid=(B, H // nh, T // tq),
        in_specs=in_specs,
        out_specs=pl.BlockSpec((1, tq, nh * MLA_VDIM), lambda b, hp, i: (b, i, hp)),
        out_shape=jax.ShapeDtypeStruct((B, T, H * MLA_VDIM), BF16),
        compiler_params=_cparams(("parallel", "parallel", "parallel")),
        name="attention",
    )(*args)


def _mlstm_direction(d, qk, vv, gl, fb, c_prev, m_prev):
    L = qk.shape[0]
    lf = _log_sigmoid(pltpu.roll(gl, LANE - 2 * ML_HEADS, axis=1) + fb)
    ks = lax.broadcasted_iota(jnp.int32, (L, L), 0)
    qt = lax.broadcasted_iota(jnp.int32, (L, L), 1)
    visible = (ks >= qt) if d else (ks <= qt)
    tri = jnp.where((qt >= ks) if d else (qt <= ks), 1.0, 0.0).astype(BF16)
    lf_hi, lf_lo = _split(lf)
    bc = _dot(tri, lf_hi) + _dot(tri, lf_lo)
    r_cols = gl - bc
    b8 = bc.T[0:2 * ML_HEADS]
    li8 = gl.T[0:2 * ML_HEADS]
    b_end = b8[:, 0:1] if d else b8[:, L - 1:L]
    gw = b_end - b8 + li8
    m_loc = jnp.max(gw, axis=1, keepdims=True)
    w8 = jnp.exp(gw - m_loc)
    m_next = jnp.maximum(b_end + m_prev, m_loc)
    a_sc = jnp.exp(b_end + m_prev - m_next)
    s_sc = jnp.exp(m_loc - m_next)
    a8 = b8 + m_prev

    srow = lax.broadcasted_iota(jnp.int32, (HEAD_PAD, L), 0)
    h_blocks, c_next = [], []
    for blk in range(ML_HEADS // 2):
        km = qk[:, (ML_HEADS + blk) * HEAD_PAD:(ML_HEADS + blk + 1) * HEAD_PAD]
        v_t = vv[:, blk * HEAD_PAD:(blk + 1) * HEAD_PAD].T
        hh = []
        for half in range(2):
            hd = 2 * blk + half
            g = d * ML_HEADS + hd
            in_half = (srow >= ML_DIM) if half else (srow < ML_DIM)
            den_row = 0 if half else ML_DIM
            qm = qk[:, hd * HEAD_PAD:(hd + 1) * HEAD_PAD]
            vaug_t = jnp.where(in_half, v_t, jnp.where(srow == den_row, 1.0, 0.0))
            dm = jnp.where(visible, r_cols[:, g:g + 1] + b8[g:g + 1, :], -jnp.inf)
            a_row = a8[g:g + 1, :]
            m_t = jnp.maximum(a_row, jnp.max(dm, axis=0, keepdims=True))
            pm = jnp.exp(dm - m_t) * _nt(km, qm)
            c0 = c_prev[hd]
            nd = (jnp.exp(a_row - m_t) * _nt(c0.astype(BF16), qm)
                  + _dot(vaug_t.astype(BF16), pm.astype(BF16)))
            den = nd[den_row:den_row + 1, :]
            hh.append(nd / jnp.maximum(jnp.abs(den), jnp.exp(-m_t)))
            c_loc = _dot((vaug_t * w8[g:g + 1, :]).astype(BF16), km)
            c_next.append(a_sc[g:g + 1, 0:HEAD_PAD] * c0 + s_sc[g:g + 1, 0:HEAD_PAD] * c_loc)
        h_blocks.append(jnp.where(srow < ML_DIM, hh[0], hh[1]).T)
    grow = lax.broadcasted_iota(jnp.int32, (2 * ML_HEADS, L), 0)
    mine = (grow >= d * ML_HEADS) & (grow < (d + 1) * ML_HEADS)
    return h_blocks, c_next, jnp.where(mine, m_next, m_prev)


def _mlstm_kernel(qkf_ref, vf_ref, gf_ref, qkb_ref, vb_ref, gb_ref, fb_ref, ci_ref, mi_ref,
                  hf_ref, hb_ref, cs_ref, ms_ref):
    @pl.when(pl.program_id(1) == 0)
    def _():
        cs_ref[...] = ci_ref[...]
        ms_ref[...] = mi_ref[...]

    rows = lambda hd: slice(hd * HEAD_PAD, (hd + 1) * HEAD_PAD)
    ins = ((qkf_ref, vf_ref, gf_ref), (qkb_ref, vb_ref, gb_ref))
    outs = (hf_ref, hb_ref)
    L = ms_ref.shape[3]
    n_sub = qkf_ref.shape[1] // L
    states = [([cs_ref[0, d, rows(hd), :] for hd in range(ML_HEADS)], ms_ref[0, d]) for d in range(2)]
    for d in range(2):
        qk_ref, v_ref, g_ref = ins[d]
        c_state, m_state = states[d]
        for s in (range(n_sub - 1, -1, -1) if d else range(n_sub)):
            tok = slice(s * L, (s + 1) * L)
            h_blocks, c_state, m_state = _mlstm_direction(d, qk_ref[0, tok, :], v_ref[0, tok, :], g_ref[0, tok, :],
                                                          fb_ref[...], c_state, m_state)
            for blk, hb in enumerate(h_blocks):
                outs[d][0, tok, blk * HEAD_PAD:(blk + 1) * HEAD_PAD] = hb
        states[d] = (c_state, m_state)
    for d in range(2):
        for hd in range(ML_HEADS):
            cs_ref[0, d, rows(hd), :] = states[d][0][hd]
        ms_ref[0, d] = states[d][1]


def _mlstm(qk, vo, gates, p, c_init, m_init):
    B, T, _ = vo.shape
    chunk = min(ML_CHUNK, T)
    L = min(ML_BLOCK, T)
    nc = T // L
    const = lambda shape: pl.BlockSpec(shape, lambda b, j: (0,) * len(shape))
    state = pl.BlockSpec((1, 2, ML_HEADS * HEAD_PAD, HEAD_PAD), lambda b, j: (b, 0, 0, 0))
    mstate = pl.BlockSpec((1, 2, 2 * ML_HEADS, chunk), lambda b, j: (b, 0, 0, 0))
    fwd = lambda b, j: (b, j, 0)
    bwd = lambda b, j: (b, nc - 1 - j, 0)
    return pl.pallas_call(
        _mlstm_kernel,
        grid=(B, nc),
        in_specs=[pl.BlockSpec((1, L, QK_LANES), fwd),
                  pl.BlockSpec((1, L, ML_WIDTH), fwd),
                  pl.BlockSpec((1, L, LANE), fwd),
                  pl.BlockSpec((1, L, QK_LANES), bwd),
                  pl.BlockSpec((1, L, ML_WIDTH), bwd),
                  pl.BlockSpec((1, L, LANE), bwd),
                  const((1, LANE)), state, mstate],
        out_specs=[pl.BlockSpec((1, L, ML_WIDTH), fwd), pl.BlockSpec((1, L, ML_WIDTH), bwd), state, mstate],
        out_shape=[jax.ShapeDtypeStruct((B, T, ML_WIDTH), F32), jax.ShapeDtypeStruct((B, T, ML_WIDTH), F32),
                   jax.ShapeDtypeStruct((B, 2, ML_HEADS * HEAD_PAD, HEAD_PAD), F32),
                   jax.ShapeDtypeStruct((B, 2, 2 * ML_HEADS, chunk), F32)],
        compiler_params=_cparams(("parallel", "arbitrary")),
        name="mlstm",
    )(qk, vo, gates, qk, vo, gates, p["f_bias"], c_init, m_init)


def _outproj_kernel(x_ref, g1_ref, sh2_ref, sc2_ref, ya_ref, yb_ref, h0_ref, h1_ref, o_ref, mg_ref, avg_ref,
                    wo_ref, l1g_ref, l1b_ref, wr_ref, x1_ref, hm_ref, aff_ref):
    hs = h0_ref[0] + h1_ref[0]
    mu = _dot2(hs, avg_ref[...])
    hc = hs - mu
    var = _dot2(hc * hc, avg_ref[...])
    yc = (1.0 / (1.0 + jnp.exp(-o_ref[0]))) * (hc * lax.rsqrt(var + LN_EPS) * mg_ref[...])
    y = _dot(jnp.concatenate([ya_ref[0], yb_ref[0], yc.astype(BF16)], axis=1), wo_ref[...])
    x1 = _plain_norm(ALPHA * x_ref[0] + g1_ref[0] * y) * l1g_ref[...] + l1b_ref[...]
    x1_ref[0] = x1
    hm = _plain_norm(x1) * (1.0 + sc2_ref[0]) + sh2_ref[0]
    hm_ref[0] = hm.astype(BF16)
    hi, lo = _split(hm)
    parts = _dot(jnp.concatenate([hi, lo], axis=1), wr_ref[...])
    logits = (parts[:, :LANE] + parts[:, LANE:]).T[:N_EXPERTS, :]
    ex = jnp.exp(logits - jnp.max(logits, axis=0, keepdims=True))
    aff_ref[0] = ex / jnp.sum(ex, axis=0, keepdims=True)


def _outproj(x, g1, sh2, sc2, ya, yb, h_fwd, h_bwd, vo, p, tm):
    B, T, D = x.shape
    const = lambda shape: pl.BlockSpec(shape, lambda b, i: (0,) * len(shape))
    vec = pl.BlockSpec((1, 1, D), lambda b, i: (b, 0, 0))
    return pl.pallas_call(
        _outproj_kernel,
        grid=(B, T // tm),
        in_specs=[pl.BlockSpec((1, tm, D), lambda b, i: (b, i, 0)), vec, vec, vec,
                  pl.BlockSpec((1, tm, SG_WIDTH), lambda b, i: (b, i, 0)),
                  pl.BlockSpec((1, tm, MLA_HEADS * MLA_VDIM), lambda b, i: (b, i, 0)),
                  pl.BlockSpec((1, tm, ML_WIDTH), lambda b, i: (b, i, 0)),
                  pl.BlockSpec((1, tm, ML_WIDTH), lambda b, i: (b, i, 0)),
                  pl.BlockSpec((1, tm, ML_WIDTH), lambda b, i: (b, i, 1)),
                  const((1, ML_WIDTH)), const((ML_WIDTH, ML_WIDTH)), const((D, D)), const((1, D)), const((1, D)),
                  const((2 * D, 2 * LANE))],
        out_specs=[pl.BlockSpec((1, tm, D), lambda b, i: (b, i, 0)),
                   pl.BlockSpec((1, tm, D), lambda b, i: (b, i, 0)),
                   pl.BlockSpec((1, N_EXPERTS, tm), lambda b, i: (b, 0, i))],
        out_shape=[jax.ShapeDtypeStruct((B, T, D), F32), jax.ShapeDtypeStruct((B, T, D), BF16),
                   jax.ShapeDtypeStruct((B, N_EXPERTS, T), F32)],
        compiler_params=_cparams(("parallel", "parallel")),
        name="outproj",
    )(x, g1, sh2, sc2, ya, yb, h_fwd, h_bwd, vo, p["ml_norm_g"], p["head_avg"], p["w_out"], p["ln1_g"], p["ln1_b"],
      p["w_router3"])


def _route_kernel(aff_ref, pos_ref, cnt_ref, *, cap, tw):
    aff = aff_ref[0]
    E, T = aff.shape
    bits = pltpu.bitcast(aff, jnp.int32)
    count = lambda m: jnp.sum(jnp.where(m, 1.0, 0.0), axis=1, keepdims=True)
    thr = jnp.zeros((E, 1), jnp.int32)
    for bit in range(30, -1, -1):
        cand = thr | (1 << bit)
        thr = jnp.where(count(bits >= cand) >= cap, cand, thr)
    above = bits > thr
    tied = bits == thr
    need = cap - count(above)
    idx = lax.broadcasted_iota(jnp.int32, (E, T), 1)
    cut = jnp.zeros((E, 1), jnp.int32)
    for bit in range(T.bit_length() - 1, -1, -1):
        cand = cut | (1 << bit)
        ok = (cand <= T) & (count(tied & (idx < cand)) <= need)
        cut = jnp.where(ok, cand, cut)
    sel = above | (tied & (idx < cut))
    r = lax.broadcasted_iota(jnp.int32, (tw, tw), 0)
    cidx = lax.broadcasted_iota(jnp.int32, (tw, tw), 1)
    before = jnp.where(r < cidx, 1.0, 0.0).astype(BF16)
    offset = jnp.zeros((E, 1), F32)
    lane = lax.broadcasted_iota(jnp.int32, (E, LANE), 1)
    table = jnp.zeros((E, LANE), F32)
    for blk in range(T // tw):
        cols = slice(blk * tw, (blk + 1) * tw)
        sb = jnp.where(sel[:, cols], 1.0, 0.0)
        slot = _dot(sb.astype(BF16), before) + offset
        pos_ref[0, :, cols] = jnp.where(sel[:, cols], slot, -1.0).astype(jnp.int32)
        offset = offset + jnp.sum(sb, axis=1, keepdims=True)
        table = jnp.where(lane == blk + 1, offset, table)
    cnt_ref[0] = table.astype(jnp.int32)


def _route(aff_t, cap):
    B, E, T = aff_t.shape
    nt = T // ROUTE_TILE
    pos, cnt = pl.pallas_call(
        functools.partial(_route_kernel, cap=cap, tw=ROUTE_TILE),
        grid=(B,),
        in_specs=[pl.BlockSpec((1, E, T), lambda b: (b, 0, 0))],
        out_specs=[pl.BlockSpec((1, E, T), lambda b: (b, 0, 0)), pl.BlockSpec((1, E, LANE), lambda b: (b, 0, 0))],
        out_shape=[jax.ShapeDtypeStruct((B, E, T), jnp.int32), jax.ShapeDtypeStruct((B, E, LANE), jnp.int32)],
        compiler_params=_cparams(("parallel",)),
        name="route",
    )(aff_t)
    return pos, cnt[:, :, :nt + 1].reshape(-1)


def _expert_kernel(cnt_ref, pos_ref, aff_ref, h_ref, wg_ref, wu_ref, wd_ref, y_ref, xe_ref, gate_ref,
                   wgb_ref, wub_ref, wdb_ref, *, n_win):
    nb, cap = y_ref.shape[0], y_ref.shape[2]
    nt = pos_ref.shape[2]
    rb = min(GATHER_ROWS, cap)
    e, j = pl.program_id(0), pl.program_id(1)

    @pl.when(j == 0)
    def _():
        wgb_ref[...] = wg_ref[0, 0].astype(BF16)
        wub_ref[...] = wu_ref[0, 0].astype(BF16)
        wdb_ref[...] = wd_ref[0, 0].astype(BF16)

    slot_iota = lax.broadcasted_iota(jnp.int32, (rb, ROUTE_TILE), 0)

    def gather_tile(bb, i, s0):
        hit = pos_ref[bb, 0, pl.ds(i, 1), :] == slot_iota + s0
        rows = pl.ds(pl.multiple_of(i * ROUTE_TILE, ROUTE_TILE), ROUTE_TILE)
        xe = _dot(jnp.where(hit, 1.0, 0.0).astype(BF16), h_ref[bb, rows, :])
        gate = jnp.sum(jnp.where(hit, aff_ref[bb, 0, pl.ds(i, 1), :], 0.0), axis=1, keepdims=True)
        return xe, gate

    for bb in range(nb):
        base = ((j * nb + bb) * N_EXPERTS + e) * (nt + 1)
        for r in range(cap // rb):
            s0 = r * rb
            first = sum((cnt_ref[base + i + 1] <= s0).astype(jnp.int32) for i in range(nt))
            end = sum((cnt_ref[base + i] < s0 + rb).astype(jnp.int32) for i in range(nt))
            start = jnp.minimum(first, nt - n_win)
            parts = [gather_tile(bb, start + w, s0) for w in range(n_win)]
            rows = slice(bb * cap + s0, bb * cap + s0 + rb)
            xe_ref[rows, :] = sum(pt[0] for pt in parts)
            gate_ref[rows, :] = sum(pt[1] for pt in parts)

            def overflow(i, carry, bb=bb, s0=s0, rows=rows):
                xe, gate = gather_tile(bb, i, s0)
                xe_ref[rows, :] += xe
                gate_ref[rows, :] += gate
                return carry

            lax.fori_loop(start + n_win, end, overflow, 0)
    xb = xe_ref[...].astype(BF16)
    hid = _silu(_dot(xb, wgb_ref[...])) * _dot(xb, wub_ref[...])
    y = (_dot(hid.astype(BF16), wdb_ref[...]) * gate_ref[...]).astype(BF16)
    for bb in range(nb):
        y_ref[bb, 0] = y[bb * cap:(bb + 1) * cap]


def _experts(pos, cnt, aff_t, hm, p, cap):
    B, T, D = hm.shape
    E = N_EXPERTS
    nt = T // ROUTE_TILE
    layer = p["layer"]
    nb = max(1, min(B, EXPERT_ROWS // cap))
    row = pl.BlockSpec((nb, 1, nt, ROUTE_TILE), lambda e, j, c: (j, e, 0, 0))
    wspec = lambda shape: pl.BlockSpec((1, 1) + shape, lambda e, j, c: (layer, e, 0, 0))
    return pl.pallas_call(
        functools.partial(_expert_kernel, n_win=min(GATHER_TILES, nt)),
        grid_spec=pltpu.PrefetchScalarGridSpec(
            num_scalar_prefetch=1,
            grid=(E, B // nb),
            in_specs=[row, row,
                      pl.BlockSpec((nb, T, D), lambda e, j, c: (j, 0, 0)),
                      wspec((D, EXPERT_FF)), wspec((D, EXPERT_FF)), wspec((EXPERT_FF, D))],
            out_specs=pl.BlockSpec((nb, 1, cap, D), lambda e, j, c: (j, e, 0, 0)),
            scratch_shapes=[pltpu.VMEM((nb * cap, D), F32), pltpu.VMEM((nb * cap, 1), F32),
                            pltpu.VMEM((D, EXPERT_FF), BF16), pltpu.VMEM((D, EXPERT_FF), BF16),
                            pltpu.VMEM((EXPERT_FF, D), BF16)]),
        out_shape=jax.ShapeDtypeStruct((B, E, cap, D), BF16),
        compiler_params=_cparams(("parallel", "arbitrary")),
        name="experts",
    )(cnt, pos.reshape(B, E, nt, ROUTE_TILE), aff_t.reshape(B, E, nt, ROUTE_TILE), hm,
      p["w_gate"], p["w_up"], p["w_down"])


def _combine_kernel(cnt_ref, pos_ref, y_ref, x_ref, g2_ref, lg_ref, lb_ref, o_ref, ycat_ref, acc_ref, *, nt):
    E, cap = y_ref.shape[1], y_ref.shape[2]
    tt = x_ref.shape[1]
    win = min(COMBINE_WIN, cap)
    per = tt // ROUTE_TILE
    b, i = pl.program_id(0), pl.program_id(1)
    slot_iota = lax.broadcasted_iota(jnp.int32, (win, tt), 0)

    def window(e, w):
        start = pl.multiple_of(jnp.minimum(w, cap - win), 16)
        slot = slot_iota + start
        hit = (pos_ref[0, e:e + 1, :] == slot) & (slot >= w)
        return jnp.where(hit, 1.0, 0.0).astype(BF16), y_ref[0, e, pl.ds(start, win), :]

    onehots, ranges = [], []
    for e in range(E):
        base = (b * E + e) * (nt + 1) + i * per
        lo, hi = cnt_ref[base], cnt_ref[base + per]
        w0 = jnp.bitwise_and(lo, ~15)
        onehot, rows = window(e, w0)
        ycat_ref[e * win:(e + 1) * win, :] = rows
        onehots.append(onehot)
        ranges.append((w0, hi))
    acc_ref[...] = _tn(jnp.concatenate(onehots, axis=0), ycat_ref[...])
    for e in range(E):
        w0, hi = ranges[e]

        def more(w, e=e):
            onehot, rows = window(e, w)
            acc_ref[...] += _tn(onehot, rows)
            return w + win

        lax.while_loop(lambda w, hi=hi: w < hi, more, w0 + win)
    o_ref[0] = _plain_norm(ALPHA * x_ref[0] + g2_ref[0] * acc_ref[...]) * lg_ref[...] + lb_ref[...]


def _combine(pos, cnt, ye, x1, g2, p, tt):
    B, T, D = x1.shape
    E, cap = ye.shape[1], ye.shape[2]
    win = min(COMBINE_WIN, cap)
    const = lambda shape: pl.BlockSpec(shape, lambda b, i, c: (0,) * len(shape))
    return pl.pallas_call(
        functools.partial(_combine_kernel, nt=T // ROUTE_TILE),
        grid_spec=pltpu.PrefetchScalarGridSpec(
            num_scalar_prefetch=1,
            grid=(B, T // tt),
            in_specs=[pl.BlockSpec((1, E, tt), lambda b, i, c: (b, 0, i)),
                      pl.BlockSpec((1, E, cap, D), lambda b, i, c: (b, 0, 0, 0)),
                      pl.BlockSpec((1, tt, D), lambda b, i, c: (b, i, 0)),
                      pl.BlockSpec((1, 1, D), lambda b, i, c: (b, 0, 0)),
                      const((1, D)), const((1, D))],
            out_specs=pl.BlockSpec((1, tt, D), lambda b, i, c: (b, i, 0)),
            scratch_shapes=[pltpu.VMEM((E * win, D), BF16), pltpu.VMEM((tt, D), F32)]),
        out_shape=jax.ShapeDtypeStruct((B, T, D), F32),
        compiler_params=_cparams(("parallel", "arbitrary")),
        name="combine",
    )(cnt, pos, ye, x1, g2, p["ln2_g"], p["ln2_b"])


def _pad_cols(a, before, total):
    pads = [(0, 0)] * (a.ndim - 1) + [(before, total - before - a.shape[-1])]
    return jnp.pad(a, pads)


def _prep_layer(l, w):
    D = D_MODEL
    wi, bi = w["w_in"][l], w["b_in"][l][None, :]
    n_sg, n_mla = 2 * SG_WIDTH, Q_LORA + KV_LORA + MLA_ROPE
    o_kr = n_sg + Q_LORA + KV_LORA
    o_ml = n_sg + n_mla
    o_g = o_ml + 4 * ML_WIDTH

    def cat(a):
        kr = a[:, o_kr:o_kr + MLA_ROPE]
        gates = a[:, o_g:o_g + 4 * ML_HEADS].reshape(-1, 2, 2, ML_HEADS).transpose(0, 2, 1, 3).reshape(-1, 4 * ML_HEADS)
        return jnp.concatenate([a[:, :o_kr], _pad_cols(kr, MLA_NOPE, HEAD_PAD), a[:, o_ml:o_g],
                                _pad_cols(gates, 0, LANE)], axis=1)

    wq = w["w_uq"][l].reshape(Q_LORA, MLA_HEADS, MLA_NOPE + MLA_ROPE)
    wq_main = _pad_cols(wq, 0, HEAD_PAD).reshape(Q_LORA, -1)
    wkv = w["w_ukv"][l].reshape(KV_LORA, MLA_HEADS, MLA_NOPE + MLA_VDIM)
    wk = _pad_cols(wkv[..., :MLA_NOPE], 0, HEAD_PAD).reshape(KV_LORA, -1)
    wv = _pad_cols(wkv[..., MLA_NOPE:], 0, HEAD_PAD).reshape(KV_LORA, -1)
    head = jnp.arange(ML_WIDTH) // ML_DIM
    wr = _pad_cols(w["w_router"][l], 0, LANE)
    wr_hi = wr.astype(BF16)
    return {
        "wcat": cat(wi).astype(BF16), "bcat": cat(bi),
        "q_norm_g": w["q_norm_g"][l][None, :], "kv_norm_g": w["kv_norm_g"][l][None, :],
        "wq": wq_main.astype(BF16),
        "wkv": jnp.concatenate([wk, wv], axis=1).astype(BF16),
        "sg_ln_g": w["sg_ln_g"][l][None, :], "sg_ln_b": w["sg_ln_b"][l][None, :],
        "sg_w": w["sg_w"][l].astype(BF16),
        "sg_bias": jnp.repeat(w["sg_b"][l].T, SG_GDIM, axis=1),
        "conv_w": jnp.pad(w["ml_conv_w"][l], ((0, 8 - ML_CONV), (0, 0))), "conv_b": w["ml_conv_b"][l][None, :],
        "f_bias": _pad_cols(w["ml_f_bias"][l].reshape(1, 2 * ML_HEADS), 0, LANE),
        "ml_norm_g": w["ml_norm_g"][l][None, :],
        "head_avg": ((head[:, None] == head[None, :]).astype(F32) / ML_DIM).astype(BF16),
        "w_out": w["w_out"][l].astype(BF16),
        "ln1_g": w["ln1_g"][l][None, :], "ln1_b": w["ln1_b"][l][None, :],
        "w_router3": jnp.concatenate([jnp.concatenate([wr_hi, (wr - wr_hi.astype(F32)).astype(BF16)], axis=1),
                                      jnp.concatenate([wr_hi, jnp.zeros_like(wr_hi)], axis=1)], axis=0),
        "layer": l, "w_gate": w["w_gate"], "w_up": w["w_up"], "w_down": w["w_down"],
        "ln2_g": w["ln2_g"][l][None, :], "ln2_b": w["ln2_b"][l][None, :],
    }


def _rope_tables(T):
    rows = T // GRID_W
    row = jnp.repeat(jnp.arange(rows, dtype=F32), GRID_W)
    colv = jnp.tile(jnp.arange(GRID_W, dtype=F32), rows)
    inv = ROPE_BASE ** (-jnp.arange(ROPE_AXIS // 2, dtype=F32) * 2.0 / ROPE_AXIS)
    ang = jnp.concatenate([row[:, None] * inv, colv[:, None] * inv], axis=-1)
    half = ROPE_AXIS // 2
    spread = lambda t: jnp.concatenate([t[:, :half], t[:, :half], t[:, half:], t[:, half:]], axis=1)
    cosT = jnp.concatenate([jnp.ones((T, MLA_NOPE), F32), spread(jnp.cos(ang)), jnp.zeros((T, 32), F32)], axis=1)
    sign = jnp.tile(jnp.repeat(jnp.asarray([-1.0, 1.0], F32), half), 2)
    sinS = _pad_cols(spread(jnp.sin(ang)) * sign, MLA_NOPE, HEAD_PAD)
    return cosT, sinS


def _moe(x1, hm, aff_t, g2, p, tt):
    cap = EC_CAPACITY * x1.shape[1] // N_EXPERTS
    pos, cnt = _route(aff_t, cap)
    ye = _experts(pos, cnt, aff_t, hm, p, cap)
    return _combine(pos, cnt, ye, x1, g2, p, tt)


def _forward(x, c, ctx, c_ctx, w):
    B, T, D = x.shape
    Tc = ctx.shape[1]
    L = w["w_ada"].shape[0]
    cc = jnp.zeros((16, D), F32).at[:B].set(c).at[B].set(c_ctx)
    ada = _ada(cc, w["w_ada"], w["b_ada"])
    cos_l, sin_l = _rope_tables(T)
    cos_c = _pad_cols(jnp.ones((Tc, MLA_NOPE + MLA_ROPE), F32), 0, HEAD_PAD)
    sin_c = jnp.zeros((Tc, HEAD_PAD), F32)
    zero_c = jnp.zeros((B, 2, ML_HEADS * HEAD_PAD, HEAD_PAD), F32)
    zero_m = jnp.zeros((B, 2, 2 * ML_HEADS, min(ML_CHUNK, Tc)), F32)
    tm = min(1024, T)
    xl, xc = x, ctx
    for l in range(L):
        p = _prep_layer(l, w)
        mods_l = [ada[l, :B, i * D:(i + 1) * D][:, None, :] for i in range(6)]
        mods_c = [jnp.broadcast_to(ada[l, B, i * D:(i + 1) * D][None, None, :], (B, 1, D)) for i in range(6)]
        update_ctx = l < L - 1
        ya_c, q_c, k_c, v_c, qk_c, vo_c, g_c = _inproj(xc, mods_c[0], mods_c[1], cos_c, sin_c, p, min(tm, Tc))
        ya_l, q_l, k_l, v_l, qk_l, vo_l, g_l = _inproj(xl, mods_l[0], mods_l[1], cos_l, sin_l, p, tm)
        hf_c, hb_c, cs_c, ms_c = _mlstm(qk_c, vo_c, g_c, p, zero_c, zero_m)
        hf_l, hb_l, _, _ = _mlstm(qk_l, vo_l, g_l, p, cs_c, ms_c)
        yb_l = _attention(q_l, [(k_l, v_l), (k_c, v_c)], tq=min(512, T))
        x1_l, hm_l, aff_l = _outproj(xl, mods_l[2], mods_l[3], mods_l[4], ya_l, yb_l, hf_l, hb_l, vo_l, p, tm)
        xl = _moe(x1_l, hm_l, aff_l, mods_l[5], p, tt=min(512, T))
        if update_ctx:
            yb_c = _attention(q_c, [(k_c, v_c)], tq=min(256, Tc))
            x1_c, hm_c, aff_c = _outproj(xc, mods_c[2], mods_c[3], mods_c[4], ya_c, yb_c, hf_c, hb_c, vo_c, p,
                                         min(tm, Tc))
            xc = _moe(x1_c, hm_c, aff_c, mods_c[5], p, tt=min(512, Tc))
    return xl


def kernel(x, c, ctx, c_ctx, w_ada, b_ada, w_in, b_in, sg_ln_g, sg_ln_b, sg_w, sg_b, q_norm_g, kv_norm_g, w_uq, w_ukv,
           ml_conv_w, ml_conv_b, ml_f_bias, ml_norm_g, w_out, ln1_g, ln1_b, w_router, w_gate, w_up, w_down, ln2_g, ln2_b):
    w = dict(w_ada=w_ada, b_ada=b_ada, w_in=w_in, b_in=b_in, sg_ln_g=sg_ln_g, sg_ln_b=sg_ln_b, sg_w=sg_w, sg_b=sg_b,
             q_norm_g=q_norm_g, kv_norm_g=kv_norm_g, w_uq=w_uq, w_ukv=w_ukv, ml_conv_w=ml_conv_w, ml_conv_b=ml_conv_b,
             ml_f_bias=ml_f_bias, ml_norm_g=ml_norm_g, w_out=w_out, ln1_g=ln1_g, ln1_b=ln1_b, w_router=w_router,
             w_gate=w_gate, w_up=w_up, w_down=w_down, ln2_g=ln2_g, ln2_b=ln2_b)
    return _forward(x, c, ctx, c_ctx, w)
```

```python
import functools

import jax
import jax.numpy as jnp
from jax import lax
from jax.experimental import pallas as pl
from jax.experimental.pallas import tpu as pltpu

F32 = jnp.float32
BF16 = jnp.bfloat16
HIGHEST = lax.Precision.HIGHEST

D_MODEL = 1024
DEPTH = 2
GRID_W = 64
SG_WIDTH = 256
SG_GROUPS = 4
SG_GDIM = 64
SG_CHUNK = 128
MLA_HEADS = 8
MLA_NOPE = 64
MLA_ROPE = 32
MLA_VDIM = 64
Q_LORA = 384
KV_LORA = 256
ROPE_AXIS = 16
ROPE_BASE = 10000.0
MLA_SCALE = (MLA_NOPE + MLA_ROPE) ** -0.5
ML_HEADS = 4
ML_DIM = 64
ML_WIDTH = 256
ML_CONV = 5
N_EXPERTS = 16
EXPERT_FF = 1024
EC_CAPACITY = 2
ALPHA = (2 * DEPTH) ** 0.25
LN_EPS = 1e-6
LOG2E = 1.4426950408889634

LANE = 128
VMEM_BYTES_V7X = 64 * 1024 * 1024
VMEM_LIMIT = VMEM_BYTES_V7X * 7 // 8

HEAD_PAD = LANE
QK_LANES = ML_HEADS * HEAD_PAD + ML_WIDTH
OFF_SG, OFF_CQ, OFF_CKV, OFF_KR, OFF_ML, OFF_G, N_CAT = 0, 512, 896, 1152, 1280, 2304, 2432

TOKEN_TILE = 1024
ATTN_QUERIES = 512
ATTN_HEADS = 2
ATTN_KEYS = 4096
ML_CHUNK = 256
ML_BLOCK = 1024
ROUTE_TILE = 256
EXPERT_ROWS = 512
GATHER_ROWS = 128
GATHER_TILES = 6
COMBINE_TILE = 512
COMBINE_WIN = 128


def _cparams(sem):
    return pltpu.CompilerParams(dimension_semantics=sem, vmem_limit_bytes=VMEM_LIMIT)


def _nt(a, b):
    return lax.dot_general(a, b, (((1,), (1,)), ((), ())), preferred_element_type=F32)


def _tn(a, b):
    return lax.dot_general(a, b, (((0,), (0,)), ((), ())), preferred_element_type=F32)


def _dot(a, b, precision=None):
    return jnp.dot(a, b, preferred_element_type=F32, precision=precision)


def _split(x):
    hi = x.astype(BF16)
    return hi, (x - hi.astype(F32)).astype(BF16)


def _dot2(x, w):
    hi, lo = _split(x)
    return _dot(jnp.concatenate([hi, lo], axis=1), jnp.concatenate([w, w], axis=0))


def _plain_norm(x):
    mu = jnp.mean(x, axis=-1, keepdims=True)
    xc = x - mu
    var = jnp.mean(xc * xc, axis=-1, keepdims=True)
    return xc * lax.rsqrt(var + LN_EPS)


def _rms(x):
    return x * lax.rsqrt(jnp.mean(x * x, axis=-1, keepdims=True) + LN_EPS)


def _silu(x):
    return x * (1.0 / (1.0 + jnp.exp(-x)))


def _gelu_tanh(x):
    return 0.5 * x * (1.0 + jnp.tanh(0.7978845608028654 * (x + 0.044715 * (x * x * x))))


def _log_sigmoid(x):
    return jnp.minimum(x, 0.0) - jnp.log(1.0 + jnp.exp(-jnp.abs(x)))


def _ada_kernel(c_ref, w_ref, b_ref, o_ref):
    o_ref[0] = _dot(_silu(c_ref[...]), w_ref[0], precision=HIGHEST) + b_ref[0]


def _ada(cc, w_ada, b_ada):
    L, D, N = w_ada.shape
    tn = 1536
    return pl.pallas_call(
        _ada_kernel,
        grid=(L, N // tn),
        in_specs=[pl.BlockSpec((16, D), lambda l, j: (0, 0)),
                  pl.BlockSpec((1, D, tn), lambda l, j: (l, 0, j)),
                  pl.BlockSpec((1, 1, tn), lambda l, j: (l, 0, j))],
        out_specs=pl.BlockSpec((1, 16, tn), lambda l, j: (l, 0, j)),
        out_shape=jax.ShapeDtypeStruct((L, 16, N), F32),
        compiler_params=_cparams(("parallel", "parallel")),
        name="ada",
    )(cc, w_ada, b_ada.reshape(L, 1, N))


def _inproj_kernel(x_ref, xp_ref, xn_ref, sh_ref, sc_ref, cos_ref, sin_ref, wcat_ref, bcat_ref, qg_ref, kvg_ref, wq_ref,
                   wkv_ref, sglg_ref, sglb_ref, sgw_ref, sgb_ref, cw_ref, cb_ref,
                   ya_ref, q_ref, k_ref, v_ref, qk_ref, vo_ref, g_ref):
    tm = x_ref.shape[1]
    modulated = lambda t: (_plain_norm(t) * (1.0 + sc_ref[0]) + sh_ref[0]).astype(BF16)
    z = _dot(modulated(x_ref[0]), wcat_ref[...]) + bcat_ref[...]
    i = pl.program_id(1)
    qk_cols = slice(OFF_ML, OFF_ML + 2 * ML_WIDTH)
    halo = _dot(modulated(jnp.concatenate([xp_ref[0], xn_ref[0]], axis=0)), wcat_ref[:, qk_cols]) + bcat_ref[:, qk_cols]
    prev = jnp.where(i > 0, halo[:8], 0.0)
    nxt = jnp.where(i < pl.num_programs(1) - 1, halo[8:], 0.0)
    big = jnp.concatenate([prev, z[:, qk_cols], nxt], axis=0)
    n_big = tm + 16
    pad = (ML_CONV - 1) // 2
    conv = cb_ref[...]
    for t in range(ML_CONV):
        s = t - pad
        shifted = big if s == 0 else pltpu.roll(big, (-s) % n_big, axis=0)
        conv = conv + shifted[8:8 + tm] * cw_ref[t:t + 1, :]
    qkc = _silu(conv)
    hlane = lax.broadcasted_iota(jnp.int32, (1, HEAD_PAD), 1)
    for hd in range(ML_HEADS):
        q_blk = qkc[:, (hd // 2) * HEAD_PAD:(hd // 2 + 1) * HEAD_PAD]
        in_half = (hlane >= ML_DIM) if hd % 2 else (hlane < ML_DIM)
        qk_ref[0, :, hd * HEAD_PAD:(hd + 1) * HEAD_PAD] = jnp.where(in_half, q_blk, 0.0).astype(BF16)
    qk_ref[0, :, ML_HEADS * HEAD_PAD:] = (qkc[:, ML_WIDTH:] * (ML_DIM ** -0.5)).astype(BF16)
    zs = _gelu_tanh(z[:, OFF_SG:OFF_SG + 2 * SG_WIDTH])
    u = zs[:, :SG_WIDTH]
    vn = _plain_norm(zs[:, SG_WIDTH:]) * sglg_ref[...] + sglb_ref[...]
    group = jnp.right_shift(lax.broadcasted_iota(jnp.int32, (SG_CHUNK, SG_WIDTH), 1), SG_GDIM.bit_length() - 1)
    for ch in range(tm // SG_CHUNK):
        rows = slice(ch * SG_CHUNK, (ch + 1) * SG_CHUNK)
        vc = vn[rows].astype(BF16)
        mixed = sgb_ref[...]
        for g in range(SG_GROUPS):
            mixed = mixed + jnp.where(group == g, _dot(sgw_ref[g], vc), 0.0)
        ya_ref[0, rows, :] = (u[rows] * mixed).astype(BF16)
    cosT = cos_ref[...]
    sinS = sin_ref[...]
    lane = lax.broadcasted_iota(jnp.int32, (1, HEAD_PAD), 1)
    first = jnp.bitwise_and(lane, ROPE_AXIS // 2) == 0

    def rope(t):
        partner = jnp.where(first, pltpu.roll(t, HEAD_PAD - ROPE_AXIS // 2, axis=1), pltpu.roll(t, ROPE_AXIS // 2, axis=1))
        return t * cosT + partner * sinS

    cqn = (_rms(z[:, OFF_CQ:OFF_CQ + Q_LORA]) * qg_ref[...]).astype(BF16)
    qq = _dot(cqn, wq_ref[...])
    ckvn = (_rms(z[:, OFF_CKV:OFF_CKV + KV_LORA]) * kvg_ref[...]).astype(BF16)
    kv = _dot(ckvn, wkv_ref[...])
    kr = rope(z[:, OFF_KR:OFF_KR + HEAD_PAD])
    one_lane = jnp.where(lane == MLA_VDIM, 1.0, 0.0)
    hw = MLA_HEADS * HEAD_PAD
    for hd in range(MLA_HEADS):
        cols = slice(hd * HEAD_PAD, (hd + 1) * HEAD_PAD)
        rcols = slice(hw + hd * HEAD_PAD, hw + (hd + 1) * HEAD_PAD)
        q_ref[0, hd] = (rope(qq[:, cols]) * (MLA_SCALE * LOG2E)).astype(BF16)
        k_ref[0, hd] = (kv[:, cols] + kr).astype(BF16)
        v_ref[0, hd] = (kv[:, rcols] + one_lane).astype(BF16)
    vo_ref[0] = z[:, OFF_ML + 2 * ML_WIDTH:OFF_ML + 4 * ML_WIDTH]
    g_ref[0] = z[:, OFF_G:OFF_G + LANE]


def _inproj(x, shift, scale, cosT, sinT, p):
    B, T, D = x.shape
    tm = min(TOKEN_TILE, T)
    hb = tm // 8
    const = lambda shape: pl.BlockSpec(shape, lambda b, i: (0,) * len(shape))
    heads = (B, MLA_HEADS, T, HEAD_PAD)
    return pl.pallas_call(
        _inproj_kernel,
        grid=(B, T // tm),
        in_specs=[pl.BlockSpec((1, tm, D), lambda b, i: (b, i, 0)),
                  pl.BlockSpec((1, 8, D), lambda b, i: (b, jnp.maximum(i * hb - 1, 0), 0)),
                  pl.BlockSpec((1, 8, D), lambda b, i: (b, jnp.minimum((i + 1) * hb, T // 8 - 1), 0)),
                  pl.BlockSpec((1, 1, D), lambda b, i: (b, 0, 0)),
                  pl.BlockSpec((1, 1, D), lambda b, i: (b, 0, 0)),
                  pl.BlockSpec((tm, HEAD_PAD), lambda b, i: (i, 0)),
                  pl.BlockSpec((tm, HEAD_PAD), lambda b, i: (i, 0)),
                  const((D, N_CAT)), const((1, N_CAT)), const((1, Q_LORA)), const((1, KV_LORA)),
                  const((Q_LORA, MLA_HEADS * HEAD_PAD)), const((KV_LORA, 2 * MLA_HEADS * HEAD_PAD)),
                  const((1, SG_WIDTH)), const((1, SG_WIDTH)), const((SG_GROUPS, SG_CHUNK, SG_CHUNK)),
                  const((SG_CHUNK, SG_WIDTH)), const((8, 2 * ML_WIDTH)), const((1, 2 * ML_WIDTH))],
        out_specs=[pl.BlockSpec((1, tm, SG_WIDTH), lambda b, i: (b, i, 0)),
                   pl.BlockSpec((1, MLA_HEADS, tm, HEAD_PAD), lambda b, i: (b, 0, i, 0)),
                   pl.BlockSpec((1, MLA_HEADS, tm, HEAD_PAD), lambda b, i: (b, 0, i, 0)),
                   pl.BlockSpec((1, MLA_HEADS, tm, HEAD_PAD), lambda b, i: (b, 0, i, 0)),
                   pl.BlockSpec((1, tm, QK_LANES), lambda b, i: (b, i, 0)),
                   pl.BlockSpec((1, tm, 2 * ML_WIDTH), lambda b, i: (b, i, 0)),
                   pl.BlockSpec((1, tm, LANE), lambda b, i: (b, i, 0))],
        out_shape=[jax.ShapeDtypeStruct((B, T, SG_WIDTH), BF16),
                   jax.ShapeDtypeStruct(heads, BF16), jax.ShapeDtypeStruct(heads, BF16),
                   jax.ShapeDtypeStruct(heads, BF16),
                   jax.ShapeDtypeStruct((B, T, QK_LANES), BF16),
                   jax.ShapeDtypeStruct((B, T, 2 * ML_WIDTH), F32),
                   jax.ShapeDtypeStruct((B, T, LANE), F32)],
        compiler_params=_cparams(("parallel", "parallel")),
        name="inproj",
    )(x, x, x, shift, scale, cosT, sinT, p["wcat"], p["bcat"], p["q_norm_g"], p["kv_norm_g"], p["wq"], p["wkv"],
      p["sg_ln_g"], p["sg_ln_b"], p["sg_w"], p["sg_bias"], p["conv_w"], p["conv_b"])


def _attn_kernel(*refs, n_src, tk):
    q_ref = refs[0]
    kv_refs = refs[1:1 + 2 * n_src]
    o_ref = refs[1 + 2 * n_src]
    nh, tq = q_ref.shape[1], q_ref.shape[2]
    qs = [q_ref[0, hh] for hh in range(nh)]
    ms = [jnp.full((tq, 1), -jnp.inf, F32) for _ in range(nh)]
    accs = [jnp.zeros((tq, HEAD_PAD), F32) for _ in range(nh)]
    for s in range(n_src):
        k_ref, v_ref = kv_refs[2 * s], kv_refs[2 * s + 1]
        ck = min(tk, k_ref.shape[2])
        for c in range(k_ref.shape[2] // ck):
            rows = slice(c * ck, (c + 1) * ck)
            for hh in range(nh):
                sc = _nt(qs[hh], k_ref[0, hh, rows, :])
                m_new = jnp.maximum(ms[hh], jnp.max(sc, axis=1, keepdims=True))
                pr = jnp.exp2(sc - m_new)
                accs[hh] = accs[hh] * jnp.exp2(ms[hh] - m_new) + _dot(pr.astype(BF16), v_ref[0, hh, rows, :])
                ms[hh] = m_new
    outs = [accs[hh] / accs[hh][:, MLA_VDIM:MLA_VDIM + 1] for hh in range(nh)]
    lane = lax.broadcasted_iota(jnp.int32, (tq, HEAD_PAD), 1)
    for pair in range(nh // 2):
        both = jnp.where(lane < MLA_VDIM, outs[2 * pair], pltpu.roll(outs[2 * pair + 1], MLA_VDIM, axis=1))
        o_ref[0, :, pair * HEAD_PAD:(pair + 1) * HEAD_PAD] = both.astype(BF16)


def _attention(q, kvs):
    B, H, T, _ = q.shape
    tq, tk, nh = min(ATTN_QUERIES, T), ATTN_KEYS, ATTN_HEADS
    in_specs = [pl.BlockSpec((1, nh, tq, HEAD_PAD), lambda b, hp, i: (b, hp, i, 0))]
    args = [q]
    for k, v in kvs:
        K = k.shape[2]
        in_specs += [pl.BlockSpec((1, nh, K, HEAD_PAD), lambda b, hp, i: (b, hp, 0, 0))] * 2
        args += [k, v]
    return pl.pallas_call(
        functools.partial(_attn_kernel, n_src=len(kvs), tk=tk),
        grid=(B, H // nh, T // tq),
        in_specs=in_specs,
        out_specs=pl.BlockSpec((1, tq, nh * MLA_VDIM), lambda b, hp, i: (b, i, hp)),
        out_shape=jax.ShapeDtypeStruct((B, T, H * MLA_VDIM), BF16),
        compiler_params=_cparams(("parallel", "parallel", "parallel")),
        name="attention",
    )(*args)


def _mlstm_direction(d, qk, vv, gl, fb, c_prev, m_prev):
    L = qk.shape[0]
    lf = _log_sigmoid(pltpu.roll(gl, LANE - 2 * ML_HEADS, axis=1) + fb)
    ks = lax.broadcasted_iota(jnp.int32, (L, L), 0)
    qt = lax.broadcasted_iota(jnp.int32, (L, L), 1)
    visible = (ks >= qt) if d else (ks <= qt)
    tri = jnp.where((qt >= ks) if d else (qt <= ks), 1.0, 0.0).astype(BF16)
    lf_hi, lf_lo = _split(lf)
    bc = _dot(tri, lf_hi) + _dot(tri, lf_lo)
    r_cols = gl - bc
    b8 = bc.T[0:2 * ML_HEADS]
    li8 = gl.T[0:2 * ML_HEADS]
    b_end = b8[:, 0:1] if d else b8[:, L - 1:L]
    gw = b_end - b8 + li8
    m_loc = jnp.max(gw, axis=1, keepdims=True)
    w8 = jnp.exp(gw - m_loc)
    m_next = jnp.maximum(b_end + m_prev, m_loc)
    a_sc = jnp.exp(b_end + m_prev - m_next)
    s_sc = jnp.exp(m_loc - m_next)
    a8 = b8 + m_prev

    srow = lax.broadcasted_iota(jnp.int32, (HEAD_PAD, L), 0)
    h_blocks, c_next = [], []
    for blk in range(ML_HEADS // 2):
        km = qk[:, (ML_HEADS + blk) * HEAD_PAD:(ML_HEADS + blk + 1) * HEAD_PAD]
        v_t = vv[:, blk * HEAD_PAD:(blk + 1) * HEAD_PAD].T
        hh = []
        for half in range(2):
            hd = 2 * blk + half
            g = d * ML_HEADS + hd
            in_half = (srow >= ML_DIM) if half else (srow < ML_DIM)
            den_row = 0 if half else ML_DIM
            qm = qk[:, hd * HEAD_PAD:(hd + 1) * HEAD_PAD]
            vaug_t = jnp.where(in_half, v_t, jnp.where(srow == den_row, 1.0, 0.0))
            dm = jnp.where(visible, r_cols[:, g:g + 1] + b8[g:g + 1, :], -jnp.inf)
            a_row = a8[g:g + 1, :]
            m_t = jnp.maximum(a_row, jnp.max(dm, axis=0, keepdims=True))
            pm = jnp.exp(dm - m_t) * _nt(km, qm)
            c0 = c_prev[hd]
            nd = (jnp.exp(a_row - m_t) * _nt(c0.astype(BF16), qm)
                  + _dot(vaug_t.astype(BF16), pm.astype(BF16)))
            den = nd[den_row:den_row + 1, :]
            hh.append(nd / jnp.maximum(jnp.abs(den), jnp.exp(-m_t)))
            c_loc = _dot((vaug_t * w8[g:g + 1, :]).astype(BF16), km)
            c_next.append(a_sc[g:g + 1, 0:HEAD_PAD] * c0 + s_sc[g:g + 1, 0:HEAD_PAD] * c_loc)
        h_blocks.append(jnp.where(srow < ML_DIM, hh[0], hh[1]).T)
    grow = lax.broadcasted_iota(jnp.int32, (2 * ML_HEADS, L), 0)
    mine = (grow >= d * ML_HEADS) & (grow < (d + 1) * ML_HEADS)
    return h_blocks, c_next, jnp.where(mine, m_next, m_prev)


def _mlstm_kernel(qkf_ref, vf_ref, gf_ref, qkb_ref, vb_ref, gb_ref, fb_ref, ci_ref, mi_ref,
                  hf_ref, hb_ref, cs_ref, ms_ref):
    @pl.when(pl.program_id(1) == 0)
    def _():
        cs_ref[...] = ci_ref[...]
        ms_ref[...] = mi_ref[...]

    rows = lambda hd: slice(hd * HEAD_PAD, (hd + 1) * HEAD_PAD)
    ins = ((qkf_ref, vf_ref, gf_ref), (qkb_ref, vb_ref, gb_ref))
    outs = (hf_ref, hb_ref)
    L = ms_ref.shape[3]
    n_sub = qkf_ref.shape[1] // L
    states = [([cs_ref[0, d, rows(hd), :] for hd in range(ML_HEADS)], ms_ref[0, d]) for d in range(2)]
    for d in range(2):
        qk_ref, v_ref, g_ref = ins[d]
        c_state, m_state = states[d]
        for s in (range(n_sub - 1, -1, -1) if d else range(n_sub)):
            tok = slice(s * L, (s + 1) * L)
            h_blocks, c_state, m_state = _mlstm_direction(d, qk_ref[0, tok, :], v_ref[0, tok, :], g_ref[0, tok, :],
                                                          fb_ref[...], c_state, m_state)
            for blk, hb in enumerate(h_blocks):
                outs[d][0, tok, blk * HEAD_PAD:(blk + 1) * HEAD_PAD] = hb
        states[d] = (c_state, m_state)
    for d in range(2):
        for hd in range(ML_HEADS):
            cs_ref[0, d, rows(hd), :] = states[d][0][hd]
        ms_ref[0, d] = states[d][1]


def _mlstm(qk, vo, gates, p, c_init, m_init):
    B, T, _ = vo.shape
    chunk = min(ML_CHUNK, T)
    L = min(ML_BLOCK, T)
    nc = T // L
    const = lambda shape: pl.BlockSpec(shape, lambda b, j: (0,) * len(shape))
    state = pl.BlockSpec((1, 2, ML_HEADS * HEAD_PAD, HEAD_PAD), lambda b, j: (b, 0, 0, 0))
    mstate = pl.BlockSpec((1, 2, 2 * ML_HEADS, chunk), lambda b, j: (b, 0, 0, 0))
    fwd = lambda b, j: (b, j, 0)
    bwd = lambda b, j: (b, nc - 1 - j, 0)
    return pl.pallas_call(
        _mlstm_kernel,
        grid=(B, nc),
        in_specs=[pl.BlockSpec((1, L, QK_LANES), fwd),
                  pl.BlockSpec((1, L, ML_WIDTH), fwd),
                  pl.BlockSpec((1, L, LANE), fwd),
                  pl.BlockSpec((1, L, QK_LANES), bwd),
                  pl.BlockSpec((1, L, ML_WIDTH), bwd),
                  pl.BlockSpec((1, L, LANE), bwd),
                  const((1, LANE)), state, mstate],
        out_specs=[pl.BlockSpec((1, L, ML_WIDTH), fwd), pl.BlockSpec((1, L, ML_WIDTH), bwd), state, mstate],
        out_shape=[jax.ShapeDtypeStruct((B, T, ML_WIDTH), F32), jax.ShapeDtypeStruct((B, T, ML_WIDTH), F32),
                   jax.ShapeDtypeStruct((B, 2, ML_HEADS * HEAD_PAD, HEAD_PAD), F32),
                   jax.ShapeDtypeStruct((B, 2, 2 * ML_HEADS, chunk), F32)],
        compiler_params=_cparams(("parallel", "arbitrary")),
        name="mlstm",
    )(qk, vo, gates, qk, vo, gates, p["f_bias"], c_init, m_init)


def _outproj_kernel(x_ref, g1_ref, sh2_ref, sc2_ref, ya_ref, yb_ref, h0_ref, h1_ref, o_ref, mg_ref, avg_ref,
                    wo_ref, l1g_ref, l1b_ref, wr_ref, x1_ref, hm_ref, aff_ref):
    hs = h0_ref[0] + h1_ref[0]
    mu = _dot2(hs, avg_ref[...])
    hc = hs - mu
    var = _dot2(hc * hc, avg_ref[...])
    yc = (1.0 / (1.0 + jnp.exp(-o_ref[0]))) * (hc * lax.rsqrt(var + LN_EPS) * mg_ref[...])
    y = _dot(jnp.concatenate([ya_ref[0], yb_ref[0], yc.astype(BF16)], axis=1), wo_ref[...])
    x1 = _plain_norm(ALPHA * x_ref[0] + g1_ref[0] * y) * l1g_ref[...] + l1b_ref[...]
    x1_ref[0] = x1
    hm = _plain_norm(x1) * (1.0 + sc2_ref[0]) + sh2_ref[0]
    hm_ref[0] = hm.astype(BF16)
    hi, lo = _split(hm)
    parts = _dot(jnp.concatenate([hi, lo], axis=1), wr_ref[...])
    logits = (parts[:, :LANE] + parts[:, LANE:]).T[:N_EXPERTS, :]
    ex = jnp.exp(logits - jnp.max(logits, axis=0, keepdims=True))
    aff_ref[0] = ex / jnp.sum(ex, axis=0, keepdims=True)


def _outproj(x, g1, sh2, sc2, ya, yb, h_fwd, h_bwd, vo, p):
    B, T, D = x.shape
    tm = min(TOKEN_TILE, T)
    const = lambda shape: pl.BlockSpec(shape, lambda b, i: (0,) * len(shape))
    vec = pl.BlockSpec((1, 1, D), lambda b, i: (b, 0, 0))
    return pl.pallas_call(
        _outproj_kernel,
        grid=(B, T // tm),
        in_specs=[pl.BlockSpec((1, tm, D), lambda b, i: (b, i, 0)), vec, vec, vec,
                  pl.BlockSpec((1, tm, SG_WIDTH), lambda b, i: (b, i, 0)),
                  pl.BlockSpec((1, tm, MLA_HEADS * MLA_VDIM), lambda b, i: (b, i, 0)),
                  pl.BlockSpec((1, tm, ML_WIDTH), lambda b, i: (b, i, 0)),
                  pl.BlockSpec((1, tm, ML_WIDTH), lambda b, i: (b, i, 0)),
                  pl.BlockSpec((1, tm, ML_WIDTH), lambda b, i: (b, i, 1)),
                  const((1, ML_WIDTH)), const((ML_WIDTH, ML_WIDTH)), const((D, D)), const((1, D)), const((1, D)),
                  const((2 * D, 2 * LANE))],
        out_specs=[pl.BlockSpec((1, tm, D), lambda b, i: (b, i, 0)),
                   pl.BlockSpec((1, tm, D), lambda b, i: (b, i, 0)),
                   pl.BlockSpec((1, N_EXPERTS, tm), lambda b, i: (b, 0, i))],
        out_shape=[jax.ShapeDtypeStruct((B, T, D), F32), jax.ShapeDtypeStruct((B, T, D), BF16),
                   jax.ShapeDtypeStruct((B, N_EXPERTS, T), F32)],
        compiler_params=_cparams(("parallel", "parallel")),
        name="outproj",
    )(x, g1, sh2, sc2, ya, yb, h_fwd, h_bwd, vo, p["ml_norm_g"], p["head_avg"], p["w_out"], p["ln1_g"], p["ln1_b"],
      p["w_router3"])


def _route_kernel(aff_ref, pos_ref, cnt_ref, *, cap, tw):
    aff = aff_ref[0]
    E, T = aff.shape
    bits = pltpu.bitcast(aff, jnp.int32)
    count = lambda m: jnp.sum(jnp.where(m, 1.0, 0.0), axis=1, keepdims=True)
    thr = jnp.zeros((E, 1), jnp.int32)
    for bit in range(30, -1, -1):
        cand = thr | (1 << bit)
        thr = jnp.where(count(bits >= cand) >= cap, cand, thr)
    above = bits > thr
    tied = bits == thr
    need = cap - count(above)
    idx = lax.broadcasted_iota(jnp.int32, (E, T), 1)
    cut = jnp.zeros((E, 1), jnp.int32)
    for bit in range(T.bit_length() - 1, -1, -1):
        cand = cut | (1 << bit)
        ok = (cand <= T) & (count(tied & (idx < cand)) <= need)
        cut = jnp.where(ok, cand, cut)
    sel = above | (tied & (idx < cut))
    r = lax.broadcasted_iota(jnp.int32, (tw, tw), 0)
    cidx = lax.broadcasted_iota(jnp.int32, (tw, tw), 1)
    before = jnp.where(r < cidx, 1.0, 0.0).astype(BF16)
    offset = jnp.zeros((E, 1), F32)
    lane = lax.broadcasted_iota(jnp.int32, (E, LANE), 1)
    table = jnp.zeros((E, LANE), F32)
    for blk in range(T // tw):
        cols = slice(blk * tw, (blk + 1) * tw)
        sb = jnp.where(sel[:, cols], 1.0, 0.0)
        slot = _dot(sb.astype(BF16), before) + offset
        pos_ref[0, :, cols] = jnp.where(sel[:, cols], slot, -1.0).astype(jnp.int32)
        offset = offset + jnp.sum(sb, axis=1, keepdims=True)
        table = jnp.where(lane == blk + 1, offset, table)
    cnt_ref[0] = table.astype(jnp.int32)


def _route(aff_t, cap):
    B, E, T = aff_t.shape
    nt = T // ROUTE_TILE
    pos, cnt = pl.pallas_call(
        functools.partial(_route_kernel, cap=cap, tw=ROUTE_TILE),
        grid=(B,),
        in_specs=[pl.BlockSpec((1, E, T), lambda b: (b, 0, 0))],
        out_specs=[pl.BlockSpec((1, E, T), lambda b: (b, 0, 0)), pl.BlockSpec((1, E, LANE), lambda b: (b, 0, 0))],
        out_shape=[jax.ShapeDtypeStruct((B, E, T), jnp.int32), jax.ShapeDtypeStruct((B, E, LANE), jnp.int32)],
        compiler_params=_cparams(("parallel",)),
        name="route",
    )(aff_t)
    return pos, cnt[:, :, :nt + 1].reshape(-1)


def _expert_kernel(cnt_ref, pos_ref, aff_ref, h_ref, wg_ref, wu_ref, wd_ref, y_ref, xe_ref, gate_ref,
                   wgb_ref, wub_ref, wdb_ref, *, n_win):
    nb, cap = y_ref.shape[0], y_ref.shape[2]
    nt = pos_ref.shape[2]
    rb = min(GATHER_ROWS, cap)
    e, j = pl.program_id(0), pl.program_id(1)

    @pl.when(j == 0)
    def _():
        wgb_ref[...] = wg_ref[0, 0].astype(BF16)
        wub_ref[...] = wu_ref[0, 0].astype(BF16)
        wdb_ref[...] = wd_ref[0, 0].astype(BF16)

    slot_iota = lax.broadcasted_iota(jnp.int32, (rb, ROUTE_TILE), 0)

    def gather(bb, i, n, s0):
        hits = [pos_ref[bb, 0, pl.ds(i + w, 1), :] == slot_iota + s0 for w in range(n)]
        onehot = jnp.concatenate([jnp.where(h, 1.0, 0.0).astype(BF16) for h in hits], axis=1)
        rows = pl.ds(pl.multiple_of(i * ROUTE_TILE, ROUTE_TILE), n * ROUTE_TILE)
        xe = _dot(onehot, h_ref[bb, rows, :])
        gate = sum(jnp.sum(jnp.where(h, aff_ref[bb, 0, pl.ds(i + w, 1), :], 0.0), axis=1, keepdims=True)
                   for w, h in enumerate(hits))
        return xe, gate

    for bb in range(nb):
        base = ((j * nb + bb) * N_EXPERTS + e) * (nt + 1)
        for r in range(cap // rb):
            s0 = r * rb
            first = sum((cnt_ref[base + i + 1] <= s0).astype(jnp.int32) for i in range(nt))
            end = sum((cnt_ref[base + i] < s0 + rb).astype(jnp.int32) for i in range(nt))
            start = jnp.minimum(first, nt - n_win)
            rows = slice(bb * cap + s0, bb * cap + s0 + rb)
            xe, gate = gather(bb, start, n_win, s0)
            xe_ref[rows, :] = xe.astype(BF16)
            gate_ref[rows, :] = gate

            def overflow(i, carry, bb=bb, s0=s0, rows=rows):
                xe, gate = gather(bb, i, 1, s0)
                xe_ref[rows, :] = (xe_ref[rows, :].astype(F32) + xe).astype(BF16)
                gate_ref[rows, :] += gate
                return carry

            lax.fori_loop(start + n_win, end, overflow, 0)
    xb = xe_ref[...]
    hid = _silu(_dot(xb, wgb_ref[...])) * _dot(xb, wub_ref[...])
    y = (_dot(hid.astype(BF16), wdb_ref[...]) * gate_ref[...]).astype(BF16)
    for bb in range(nb):
        y_ref[bb, 0] = y[bb * cap:(bb + 1) * cap]


def _experts(pos, cnt, aff_t, hm, p, cap):
    B, T, D = hm.shape
    E = N_EXPERTS
    nt = T // ROUTE_TILE
    layer = p["layer"]
    nb = max(1, min(B, EXPERT_ROWS // cap))
    row = pl.BlockSpec((nb, 1, nt, ROUTE_TILE), lambda e, j, c: (j, e, 0, 0))
    wspec = lambda shape: pl.BlockSpec((1, 1) + shape, lambda e, j, c: (layer, e, 0, 0))
    return pl.pallas_call(
        functools.partial(_expert_kernel, n_win=min(GATHER_TILES, nt)),
        grid_spec=pltpu.PrefetchScalarGridSpec(
            num_scalar_prefetch=1,
            grid=(E, B // nb),
            in_specs=[row, row,
                      pl.BlockSpec((nb, T, D), lambda e, j, c: (j, 0, 0)),
                      wspec((D, EXPERT_FF)), wspec((D, EXPERT_FF)), wspec((EXPERT_FF, D))],
            out_specs=pl.BlockSpec((nb, 1, cap, D), lambda e, j, c: (j, e, 0, 0)),
            scratch_shapes=[pltpu.VMEM((nb * cap, D), BF16), pltpu.VMEM((nb * cap, 1), F32),
                            pltpu.VMEM((D, EXPERT_FF), BF16), pltpu.VMEM((D, EXPERT_FF), BF16),
                            pltpu.VMEM((EXPERT_FF, D), BF16)]),
        out_shape=jax.ShapeDtypeStruct((B, E, cap, D), BF16),
        compiler_params=_cparams(("parallel", "arbitrary")),
        name="experts",
    )(cnt, pos.reshape(B, E, nt, ROUTE_TILE), aff_t.reshape(B, E, nt, ROUTE_TILE), hm,
      p["w_gate"], p["w_up"], p["w_down"])


def _combine_kernel(cnt_ref, pos_ref, y_ref, x_ref, g2_ref, lg_ref, lb_ref, o_ref, ycat_ref, acc_ref, *, nt):
    E, cap = y_ref.shape[1], y_ref.shape[2]
    tt = x_ref.shape[1]
    win = min(COMBINE_WIN, cap)
    per = tt // ROUTE_TILE
    b, i = pl.program_id(0), pl.program_id(1)
    slot_iota = lax.broadcasted_iota(jnp.int32, (win, tt), 0)

    def window(e, w):
        start = pl.multiple_of(jnp.minimum(w, cap - win), 16)
        slot = slot_iota + start
        hit = (pos_ref[0, e:e + 1, :] == slot) & (slot >= w)
        return jnp.where(hit, 1.0, 0.0).astype(BF16), y_ref[0, e, pl.ds(start, win), :]

    onehots, ranges = [], []
    for e in range(E):
        base = (b * E + e) * (nt + 1) + i * per
        lo, hi = cnt_ref[base], cnt_ref[base + per]
        w0 = jnp.bitwise_and(lo, ~15)
        onehot, rows = window(e, w0)
        ycat_ref[e * win:(e + 1) * win, :] = rows
        onehots.append(onehot)
        ranges.append((w0, hi))
    acc_ref[...] = _tn(jnp.concatenate(onehots, axis=0), ycat_ref[...])
    for e in range(E):
        w0, hi = ranges[e]

        def more(w, e=e):
            onehot, rows = window(e, w)
            acc_ref[...] += _tn(onehot, rows)
            return w + win

        lax.while_loop(lambda w, hi=hi: w < hi, more, w0 + win)
    o_ref[0] = _plain_norm(ALPHA * x_ref[0] + g2_ref[0] * acc_ref[...]) * lg_ref[...] + lb_ref[...]


def _combine(pos, cnt, ye, x1, g2, p):
    B, T, D = x1.shape
    tt = min(COMBINE_TILE, T)
    E, cap = ye.shape[1], ye.shape[2]
    win = min(COMBINE_WIN, cap)
    const = lambda shape: pl.BlockSpec(shape, lambda b, i, c: (0,) * len(shape))
    return pl.pallas_call(
        functools.partial(_combine_kernel, nt=T // ROUTE_TILE),
        grid_spec=pltpu.PrefetchScalarGridSpec(
            num_scalar_prefetch=1,
            grid=(B, T // tt),
            in_specs=[pl.BlockSpec((1, E, tt), lambda b, i, c: (b, 0, i)),
                      pl.BlockSpec((1, E, cap, D), lambda b, i, c: (b, 0, 0, 0)),
                      pl.BlockSpec((1, tt, D), lambda b, i, c: (b, i, 0)),
                      pl.BlockSpec((1, 1, D), lambda b, i, c: (b, 0, 0)),
                      const((1, D)), const((1, D))],
            out_specs=pl.BlockSpec((1, tt, D), lambda b, i, c: (b, i, 0)),
            scratch_shapes=[pltpu.VMEM((E * win, D), BF16), pltpu.VMEM((tt, D), F32)]),
        out_shape=jax.ShapeDtypeStruct((B, T, D), F32),
        compiler_params=_cparams(("parallel", "arbitrary")),
        name="combine",
    )(cnt, pos, ye, x1, g2, p["ln2_g"], p["ln2_b"])


def _pad_cols(a, before, total):
    pads = [(0, 0)] * (a.ndim - 1) + [(before, total - before - a.shape[-1])]
    return jnp.pad(a, pads)


def _prep_layer(l, w):
    D = D_MODEL
    wi, bi = w["w_in"][l], w["b_in"][l][None, :]
    n_sg, n_mla = 2 * SG_WIDTH, Q_LORA + KV_LORA + MLA_ROPE
    o_kr = n_sg + Q_LORA + KV_LORA
    o_ml = n_sg + n_mla
    o_g = o_ml + 4 * ML_WIDTH

    def cat(a):
        kr = a[:, o_kr:o_kr + MLA_ROPE]
        gates = a[:, o_g:o_g + 4 * ML_HEADS].reshape(-1, 2, 2, ML_HEADS).transpose(0, 2, 1, 3).reshape(-1, 4 * ML_HEADS)
        return jnp.concatenate([a[:, :o_kr], _pad_cols(kr, MLA_NOPE, HEAD_PAD), a[:, o_ml:o_g],
                                _pad_cols(gates, 0, LANE)], axis=1)

    wq = w["w_uq"][l].reshape(Q_LORA, MLA_HEADS, MLA_NOPE + MLA_ROPE)
    wq_main = _pad_cols(wq, 0, HEAD_PAD).reshape(Q_LORA, -1)
    wkv = w["w_ukv"][l].reshape(KV_LORA, MLA_HEADS, MLA_NOPE + MLA_VDIM)
    wk = _pad_cols(wkv[..., :MLA_NOPE], 0, HEAD_PAD).reshape(KV_LORA, -1)
    wv = _pad_cols(wkv[..., MLA_NOPE:], 0, HEAD_PAD).reshape(KV_LORA, -1)
    head = jnp.arange(ML_WIDTH) // ML_DIM
    wr = _pad_cols(w["w_router"][l], 0, LANE)
    wr_hi = wr.astype(BF16)
    return {
        "wcat": cat(wi).astype(BF16), "bcat": cat(bi),
        "q_norm_g": w["q_norm_g"][l][None, :], "kv_norm_g": w["kv_norm_g"][l][None, :],
        "wq": wq_main.astype(BF16),
        "wkv": jnp.concatenate([wk, wv], axis=1).astype(BF16),
        "sg_ln_g": w["sg_ln_g"][l][None, :], "sg_ln_b": w["sg_ln_b"][l][None, :],
        "sg_w": w["sg_w"][l].astype(BF16),
        "sg_bias": jnp.repeat(w["sg_b"][l].T, SG_GDIM, axis=1),
        "conv_w": jnp.pad(w["ml_conv_w"][l], ((0, 8 - ML_CONV), (0, 0))), "conv_b": w["ml_conv_b"][l][None, :],
        "f_bias": _pad_cols(w["ml_f_bias"][l].reshape(1, 2 * ML_HEADS), 0, LANE),
        "ml_norm_g": w["ml_norm_g"][l][None, :],
        "head_avg": ((head[:, None] == head[None, :]).astype(F32) / ML_DIM).astype(BF16),
        "w_out": w["w_out"][l].astype(BF16),
        "ln1_g": w["ln1_g"][l][None, :], "ln1_b": w["ln1_b"][l][None, :],
        "w_router3": jnp.concatenate([jnp.concatenate([wr_hi, (wr - wr_hi.astype(F32)).astype(BF16)], axis=1),
                                      jnp.concatenate([wr_hi, jnp.zeros_like(wr_hi)], axis=1)], axis=0),
        "layer": l, "w_gate": w["w_gate"], "w_up": w["w_up"], "w_down": w["w_down"],
        "ln2_g": w["ln2_g"][l][None, :], "ln2_b": w["ln2_b"][l][None, :],
    }


def _rope_tables(T):
    rows = T // GRID_W
    row = jnp.repeat(jnp.arange(rows, dtype=F32), GRID_W)
    colv = jnp.tile(jnp.arange(GRID_W, dtype=F32), rows)
    inv = ROPE_BASE ** (-jnp.arange(ROPE_AXIS // 2, dtype=F32) * 2.0 / ROPE_AXIS)
    ang = jnp.concatenate([row[:, None] * inv, colv[:, None] * inv], axis=-1)
    half = ROPE_AXIS // 2
    spread = lambda t: jnp.concatenate([t[:, :half], t[:, :half], t[:, half:], t[:, half:]], axis=1)
    cosT = jnp.concatenate([jnp.ones((T, MLA_NOPE), F32), spread(jnp.cos(ang)), jnp.zeros((T, 32), F32)], axis=1)
    sign = jnp.tile(jnp.repeat(jnp.asarray([-1.0, 1.0], F32), half), 2)
    sinS = _pad_cols(spread(jnp.sin(ang)) * sign, MLA_NOPE, HEAD_PAD)
    return cosT, sinS


def _moe(x1, hm, aff_t, g2, p):
    cap = EC_CAPACITY * x1.shape[1] // N_EXPERTS
    pos, cnt = _route(aff_t, cap)
    ye = _experts(pos, cnt, aff_t, hm, p, cap)
    return _combine(pos, cnt, ye, x1, g2, p)


def _forward(x, c, ctx, c_ctx, w):
    B, T, D = x.shape
    Tc = ctx.shape[1]
    L = w["w_ada"].shape[0]
    cc = jnp.zeros((16, D), F32).at[:B].set(c).at[B].set(c_ctx)
    ada = _ada(cc, w["w_ada"], w["b_ada"])
    cos_l, sin_l = _rope_tables(T)
    cos_c = _pad_cols(jnp.ones((Tc, MLA_NOPE + MLA_ROPE), F32), 0, HEAD_PAD)
    sin_c = jnp.zeros((Tc, HEAD_PAD), F32)
    zero_c = jnp.zeros((B, 2, ML_HEADS * HEAD_PAD, HEAD_PAD), F32)
    zero_m = jnp.zeros((B, 2, 2 * ML_HEADS, min(ML_CHUNK, Tc)), F32)
    xl, xc = x, ctx
    for l in range(L):
        p = _prep_layer(l, w)
        mods_l = [ada[l, :B, i * D:(i + 1) * D][:, None, :] for i in range(6)]
        mods_c = [jnp.broadcast_to(ada[l, B, i * D:(i + 1) * D][None, None, :], (B, 1, D)) for i in range(6)]
        ya_c, q_c, k_c, v_c, qk_c, vo_c, g_c = _inproj(xc, mods_c[0], mods_c[1], cos_c, sin_c, p)
        ya_l, q_l, k_l, v_l, qk_l, vo_l, g_l = _inproj(xl, mods_l[0], mods_l[1], cos_l, sin_l, p)
        hf_c, hb_c, cs_c, ms_c = _mlstm(qk_c, vo_c, g_c, p, zero_c, zero_m)
        hf_l, hb_l, _, _ = _mlstm(qk_l, vo_l, g_l, p, cs_c, ms_c)
        yb_l = _attention(q_l, [(k_l, v_l), (k_c, v_c)])
        x1_l, hm_l, aff_l = _outproj(xl, mods_l[2], mods_l[3], mods_l[4], ya_l, yb_l, hf_l, hb_l, vo_l, p)
        xl = _moe(x1_l, hm_l, aff_l, mods_l[5], p)
        if l < L - 1:
            yb_c = _attention(q_c, [(k_c, v_c)])
            x1_c, hm_c, aff_c = _outproj(xc, mods_c[2], mods_c[3], mods_c[4], ya_c, yb_c, hf_c, hb_c, vo_c, p)
            xc = _moe(x1_c, hm_c, aff_c, mods_c[5], p)
    return xl


def kernel(x, c, ctx, c_ctx, w_ada, b_ada, w_in, b_in, sg_ln_g, sg_ln_b, sg_w, sg_b, q_norm_g, kv_norm_g, w_uq, w_ukv,
           ml_conv_w, ml_conv_b, ml_f_bias, ml_norm_g, w_out, ln1_g, ln1_b, w_router, w_gate, w_up, w_down, ln2_g, ln2_b):
    w = dict(w_ada=w_ada, b_ada=b_ada, w_in=w_in, b_in=b_in, sg_ln_g=sg_ln_g, sg_ln_b=sg_ln_b, sg_w=sg_w, sg_b=sg_b,
             q_norm_g=q_norm_g, kv_norm_g=kv_norm_g, w_uq=w_uq, w_ukv=w_ukv, ml_conv_w=ml_conv_w, ml_conv_b=ml_conv_b,
             ml_f_bias=ml_f_bias, ml_norm_g=ml_norm_g, w_out=w_out, ln1_g=ln1_g, ln1_b=ln1_b, w_router=w_router,
             w_gate=w_gate, w_up=w_up, w_down=w_down, ln2_g=ln2_g, ln2_b=ln2_b)
    return _forward(x, c, ctx, c_ctx, w)
```

```python
import functools

import jax
import jax.numpy as jnp
from jax import lax
from jax.experimental import pallas as pl
from jax.experimental.pallas import tpu as pltpu

F32 = jnp.float32
BF16 = jnp.bfloat16
HIGHEST = lax.Precision.HIGHEST

D_MODEL = 1024
DEPTH = 2
GRID_W = 64
SG_WIDTH = 256
SG_GROUPS = 4
SG_GDIM = 64
SG_CHUNK = 128
MLA_HEADS = 8
MLA_NOPE = 64
MLA_ROPE = 32
MLA_VDIM = 64
Q_LORA = 384
KV_LORA = 256
ROPE_AXIS = 16
ROPE_BASE = 10000.0
MLA_SCALE = (MLA_NOPE + MLA_ROPE) ** -0.5
ML_HEADS = 4
ML_DIM = 64
ML_WIDTH = 256
ML_CONV = 5
N_EXPERTS = 16
EXPERT_FF = 1024
EC_CAPACITY = 2
ALPHA = (2 * DEPTH) ** 0.25
LN_EPS = 1e-6
LOG2E = 1.4426950408889634

LANE = 128
VMEM_BYTES_V7X = 64 * 1024 * 1024
VMEM_LIMIT = VMEM_BYTES_V7X * 7 // 8

HEAD_PAD = LANE
QK_LANES = ML_HEADS * HEAD_PAD + ML_WIDTH
OFF_SG, OFF_CQ, OFF_CKV, OFF_KR, OFF_ML, OFF_G, N_CAT = 0, 512, 896, 1152, 1280, 2304, 2432

TOKEN_TILE = 1024
ATTN_QUERIES = 512
ATTN_HEADS = 2
ATTN_KEYS = 4096
ML_CHUNK = 256
ML_BLOCK = 1024
ROUTE_TILE = 256
EXPERT_ROWS = 512
GATHER_ROWS = 128
GATHER_TILES = 6
COMBINE_TILE = 512
COMBINE_WIN = 128


def _cparams(sem):
    return pltpu.CompilerParams(dimension_semantics=sem, vmem_limit_bytes=VMEM_LIMIT)


def _nt(a, b):
    return lax.dot_general(a, b, (((1,), (1,)), ((), ())), preferred_element_type=F32)


def _tn(a, b):
    return lax.dot_general(a, b, (((0,), (0,)), ((), ())), preferred_element_type=F32)


def _dot(a, b, precision=None):
    return jnp.dot(a, b, preferred_element_type=F32, precision=precision)


def _split(x):
    hi = x.astype(BF16)
    return hi, (x - hi.astype(F32)).astype(BF16)


def _dot2(x, w):
    hi, lo = _split(x)
    return _dot(jnp.concatenate([hi, lo], axis=1), jnp.concatenate([w, w], axis=0))


def _plain_norm(x):
    mu = jnp.mean(x, axis=-1, keepdims=True)
    xc = x - mu
    var = jnp.mean(xc * xc, axis=-1, keepdims=True)
    return xc * lax.rsqrt(var + LN_EPS)


def _rms(x):
    return x * lax.rsqrt(jnp.mean(x * x, axis=-1, keepdims=True) + LN_EPS)


def _silu(x):
    return x * (1.0 / (1.0 + jnp.exp(-x)))


def _gelu_tanh(x):
    return 0.5 * x * (1.0 + jnp.tanh(0.7978845608028654 * (x + 0.044715 * (x * x * x))))


def _log_sigmoid(x):
    return jnp.minimum(x, 0.0) - jnp.log(1.0 + jnp.exp(-jnp.abs(x)))


def _ada_kernel(c_ref, w_ref, b_ref, o_ref):
    o_ref[0] = _dot(_silu(c_ref[...]), w_ref[0], precision=HIGHEST) + b_ref[0]


def _ada(cc, w_ada, b_ada):
    L, D, N = w_ada.shape
    tn = 1536
    return pl.pallas_call(
        _ada_kernel,
        grid=(L, N // tn),
        in_specs=[pl.BlockSpec((16, D), lambda l, j: (0, 0)),
                  pl.BlockSpec((1, D, tn), lambda l, j: (l, 0, j)),
                  pl.BlockSpec((1, 1, tn), lambda l, j: (l, 0, j))],
        out_specs=pl.BlockSpec((1, 16, tn), lambda l, j: (l, 0, j)),
        out_shape=jax.ShapeDtypeStruct((L, 16, N), F32),
        compiler_params=_cparams(("parallel", "parallel")),
        name="ada",
    )(cc, w_ada, b_ada.reshape(L, 1, N))


def _inproj_kernel(x_ref, xp_ref, xn_ref, sh_ref, sc_ref, cos_ref, sin_ref, wcat_ref, bcat_ref, qg_ref, kvg_ref, wq_ref,
                   wkv_ref, sglg_ref, sglb_ref, sgw_ref, sgb_ref, cw_ref, cb_ref,
                   ya_ref, q_ref, k_ref, v_ref, qk_ref, vo_ref, g_ref):
    tm = x_ref.shape[1]
    modulated = lambda t: (_plain_norm(t) * (1.0 + sc_ref[0]) + sh_ref[0]).astype(BF16)
    z = _dot(modulated(x_ref[0]), wcat_ref[...]) + bcat_ref[...]
    i = pl.program_id(1)
    qk_cols = slice(OFF_ML, OFF_ML + 2 * ML_WIDTH)
    halo = _dot(modulated(jnp.concatenate([xp_ref[0], xn_ref[0]], axis=0)), wcat_ref[:, qk_cols]) + bcat_ref[:, qk_cols]
    prev = jnp.where(i > 0, halo[:8], 0.0)
    nxt = jnp.where(i < pl.num_programs(1) - 1, halo[8:], 0.0)
    big = jnp.concatenate([prev, z[:, qk_cols], nxt], axis=0)
    n_big = tm + 16
    pad = (ML_CONV - 1) // 2
    conv = cb_ref[...]
    for t in range(ML_CONV):
        s = t - pad
        shifted = big if s == 0 else pltpu.roll(big, (-s) % n_big, axis=0)
        conv = conv + shifted[8:8 + tm] * cw_ref[t:t + 1, :]
    qkc = _silu(conv)
    hlane = lax.broadcasted_iota(jnp.int32, (1, HEAD_PAD), 1)
    for hd in range(ML_HEADS):
        q_blk = qkc[:, (hd // 2) * HEAD_PAD:(hd // 2 + 1) * HEAD_PAD]
        in_half = (hlane >= ML_DIM) if hd % 2 else (hlane < ML_DIM)
        qk_ref[0, :, hd * HEAD_PAD:(hd + 1) * HEAD_PAD] = jnp.where(in_half, q_blk, 0.0).astype(BF16)
    qk_ref[0, :, ML_HEADS * HEAD_PAD:] = (qkc[:, ML_WIDTH:] * (ML_DIM ** -0.5)).astype(BF16)
    zs = _gelu_tanh(z[:, OFF_SG:OFF_SG + 2 * SG_WIDTH])
    u = zs[:, :SG_WIDTH]
    vn = _plain_norm(zs[:, SG_WIDTH:]) * sglg_ref[...] + sglb_ref[...]
    group = jnp.right_shift(lax.broadcasted_iota(jnp.int32, (SG_CHUNK, SG_WIDTH), 1), SG_GDIM.bit_length() - 1)
    for ch in range(tm // SG_CHUNK):
        rows = slice(ch * SG_CHUNK, (ch + 1) * SG_CHUNK)
        vc = vn[rows].astype(BF16)
        mixed = sgb_ref[...]
        for g in range(SG_GROUPS):
            mixed = mixed + jnp.where(group == g, _dot(sgw_ref[g], vc), 0.0)
        ya_ref[0, rows, :] = (u[rows] * mixed).astype(BF16)
    cosT = cos_ref[...]
    sinS = sin_ref[...]
    lane = lax.broadcasted_iota(jnp.int32, (1, HEAD_PAD), 1)
    first = jnp.bitwise_and(lane, ROPE_AXIS // 2) == 0

    def rope(t):
        partner = jnp.where(first, pltpu.roll(t, HEAD_PAD - ROPE_AXIS // 2, axis=1), pltpu.roll(t, ROPE_AXIS // 2, axis=1))
        return t * cosT + partner * sinS

    cqn = (_rms(z[:, OFF_CQ:OFF_CQ + Q_LORA]) * qg_ref[...]).astype(BF16)
    qq = _dot(cqn, wq_ref[...])
    ckvn = (_rms(z[:, OFF_CKV:OFF_CKV + KV_LORA]) * kvg_ref[...]).astype(BF16)
    kv = _dot(ckvn, wkv_ref[...])
    kr = rope(z[:, OFF_KR:OFF_KR + HEAD_PAD])
    one_lane = jnp.where(lane == MLA_VDIM, 1.0, 0.0)
    hw = MLA_HEADS * HEAD_PAD
    for hd in range(MLA_HEADS):
        cols = slice(hd * HEAD_PAD, (hd + 1) * HEAD_PAD)
        rcols = slice(hw + hd * HEAD_PAD, hw + (hd + 1) * HEAD_PAD)
        q_ref[0, hd] = (rope(qq[:, cols]) * (MLA_SCALE * LOG2E)).astype(BF16)
        k_ref[0, hd] = (kv[:, cols] + kr).astype(BF16)
        v_ref[0, hd] = (kv[:, rcols] + one_lane).astype(BF16)
    vo_ref[0] = z[:, OFF_ML + 2 * ML_WIDTH:OFF_ML + 4 * ML_WIDTH]
    g_ref[0] = z[:, OFF_G:OFF_G + LANE]


def _inproj(x, shift, scale, cosT, sinT, p):
    B, T, D = x.shape
    tm = min(TOKEN_TILE, T)
    hb = tm // 8
    const = lambda shape: pl.BlockSpec(shape, lambda b, i: (0,) * len(shape))
    heads = (B, MLA_HEADS, T, HEAD_PAD)
    return pl.pallas_call(
        _inproj_kernel,
        grid=(B, T // tm),
        in_specs=[pl.BlockSpec((1, tm, D), lambda b, i: (b, i, 0)),
                  pl.BlockSpec((1, 8, D), lambda b, i: (b, jnp.maximum(i * hb - 1, 0), 0)),
                  pl.BlockSpec((1, 8, D), lambda b, i: (b, jnp.minimum((i + 1) * hb, T // 8 - 1), 0)),
                  pl.BlockSpec((1, 1, D), lambda b, i: (b, 0, 0)),
                  pl.BlockSpec((1, 1, D), lambda b, i: (b, 0, 0)),
                  pl.BlockSpec((tm, HEAD_PAD), lambda b, i: (i, 0)),
                  pl.BlockSpec((tm, HEAD_PAD), lambda b, i: (i, 0)),
                  const((D, N_CAT)), const((1, N_CAT)), const((1, Q_LORA)), const((1, KV_LORA)),
                  const((Q_LORA, MLA_HEADS * HEAD_PAD)), const((KV_LORA, 2 * MLA_HEADS * HEAD_PAD)),
                  const((1, SG_WIDTH)), const((1, SG_WIDTH)), const((SG_GROUPS, SG_CHUNK, SG_CHUNK)),
                  const((SG_CHUNK, SG_WIDTH)), const((8, 2 * ML_WIDTH)), const((1, 2 * ML_WIDTH))],
        out_specs=[pl.BlockSpec((1, tm, SG_WIDTH), lambda b, i: (b, i, 0)),
                   pl.BlockSpec((1, MLA_HEADS, tm, HEAD_PAD), lambda b, i: (b, 0, i, 0)),
                   pl.BlockSpec((1, MLA_HEADS, tm, HEAD_PAD), lambda b, i: (b, 0, i, 0)),
                   pl.BlockSpec((1, MLA_HEADS, tm, HEAD_PAD), lambda b, i: (b, 0, i, 0)),
                   pl.BlockSpec((1, tm, QK_LANES), lambda b, i: (b, i, 0)),
                   pl.BlockSpec((1, tm, 2 * ML_WIDTH), lambda b, i: (b, i, 0)),
                   pl.BlockSpec((1, tm, LANE), lambda b, i: (b, i, 0))],
        out_shape=[jax.ShapeDtypeStruct((B, T, SG_WIDTH), BF16),
                   jax.ShapeDtypeStruct(heads, BF16), jax.ShapeDtypeStruct(heads, BF16),
                   jax.ShapeDtypeStruct(heads, BF16),
                   jax.ShapeDtypeStruct((B, T, QK_LANES), BF16),
                   jax.ShapeDtypeStruct((B, T, 2 * ML_WIDTH), F32),
                   jax.ShapeDtypeStruct((B, T, LANE), F32)],
        compiler_params=_cparams(("parallel", "parallel")),
        name="inproj",
    )(x, x, x, shift, scale, cosT, sinT, p["wcat"], p["bcat"], p["q_norm_g"], p["kv_norm_g"], p["wq"], p["wkv"],
      p["sg_ln_g"], p["sg_ln_b"], p["sg_w"], p["sg_bias"], p["conv_w"], p["conv_b"])


def _attn_kernel(*refs, n_src, tk):
    q_ref = refs[0]
    kv_refs = refs[1:1 + 2 * n_src]
    o_ref = refs[1 + 2 * n_src]
    nh, tq = q_ref.shape[1], q_ref.shape[2]
    qs = [q_ref[0, hh] for hh in range(nh)]
    ms = [jnp.full((tq, 1), -jnp.inf, F32) for _ in range(nh)]
    accs = [jnp.zeros((tq, HEAD_PAD), F32) for _ in range(nh)]
    for s in range(n_src):
        k_ref, v_ref = kv_refs[2 * s], kv_refs[2 * s + 1]
        ck = min(tk, k_ref.shape[2])
        for c in range(k_ref.shape[2] // ck):
            rows = slice(c * ck, (c + 1) * ck)
            for hh in range(nh):
                sc = _nt(qs[hh], k_ref[0, hh, rows, :])
                m_new = jnp.maximum(ms[hh], jnp.max(sc, axis=1, keepdims=True))
                pr = jnp.exp2(sc - m_new)
                accs[hh] = accs[hh] * jnp.exp2(ms[hh] - m_new) + _dot(pr.astype(BF16), v_ref[0, hh, rows, :])
                ms[hh] = m_new
    outs = [accs[hh] / accs[hh][:, MLA_VDIM:MLA_VDIM + 1] for hh in range(nh)]
    lane = lax.broadcasted_iota(jnp.int32, (tq, HEAD_PAD), 1)
    for pair in range(nh // 2):
        both = jnp.where(lane < MLA_VDIM, outs[2 * pair], pltpu.roll(outs[2 * pair + 1], MLA_VDIM, axis=1))
        o_ref[0, :, pair * HEAD_PAD:(pair + 1) * HEAD_PAD] = both.astype(BF16)


def _attention(q, kvs):
    B, H, T, _ = q.shape
    tq, tk, nh = min(ATTN_QUERIES, T), ATTN_KEYS, ATTN_HEADS
    in_specs = [pl.BlockSpec((1, nh, tq, HEAD_PAD), lambda b, hp, i: (b, hp, i, 0))]
    args = [q]
    for k, v in kvs:
        K = k.shape[2]
        in_specs += [pl.BlockSpec((1, nh, K, HEAD_PAD), lambda b, hp, i: (b, hp, 0, 0))] * 2
        args += [k, v]
    return pl.pallas_call(
        functools.partial(_attn_kernel, n_src=len(kvs), tk=tk),
        grid=(B, H // nh, T // tq),
        in_specs=in_specs,
        out_specs=pl.BlockSpec((1, tq, nh * MLA_VDIM), lambda b, hp, i: (b, i, hp)),
        out_shape=jax.ShapeDtypeStruct((B, T, H * MLA_VDIM), BF16),
        compiler_params=_cparams(("parallel", "parallel", "parallel")),
        name="attention",
    )(*args)


def _mlstm_direction(d, qk, vv, gl, fb, c_prev, m_prev):
    L = qk.shape[0]
    lf = _log_sigmoid(pltpu.roll(gl, LANE - 2 * ML_HEADS, axis=1) + fb)
    ks = lax.broadcasted_iota(jnp.int32, (L, L), 0)
    qt = lax.broadcasted_iota(jnp.int32, (L, L), 1)
    visible = (ks >= qt) if d else (ks <= qt)
    tri = jnp.where((qt >= ks) if d else (qt <= ks), 1.0, 0.0).astype(BF16)
    lf_hi, lf_lo = _split(lf)
    bc = _dot(tri, lf_hi) + _dot(tri, lf_lo)
    r_cols = gl - bc
    b8 = bc.T[0:2 * ML_HEADS]
    li8 = gl.T[0:2 * ML_HEADS]
    b_end = b8[:, 0:1] if d else b8[:, L - 1:L]
    gw = b_end - b8 + li8
    m_loc = jnp.max(gw, axis=1, keepdims=True)
    w8 = jnp.exp(gw - m_loc)
    m_next = jnp.maximum(b_end + m_prev, m_loc)
    a_sc = jnp.exp(b_end + m_prev - m_next)
    s_sc = jnp.exp(m_loc - m_next)
    a8 = b8 + m_prev

    srow = lax.broadcasted_iota(jnp.int32, (HEAD_PAD, L), 0)
    h_blocks, c_next = [], []
    for blk in range(ML_HEADS // 2):
        km = qk[:, (ML_HEADS + blk) * HEAD_PAD:(ML_HEADS + blk + 1) * HEAD_PAD]
        v_t = vv[:, blk * HEAD_PAD:(blk + 1) * HEAD_PAD].T
        hh = []
        for half in range(2):
            hd = 2 * blk + half
            g = d * ML_HEADS + hd
            in_half = (srow >= ML_DIM) if half else (srow < ML_DIM)
            den_row = 0 if half else ML_DIM
            qm = qk[:, hd * HEAD_PAD:(hd + 1) * HEAD_PAD]
            vaug_t = jnp.where(in_half, v_t, jnp.where(srow == den_row, 1.0, 0.0))
            dm = jnp.where(visible, r_cols[:, g:g + 1] + b8[g:g + 1, :], -jnp.inf)
            a_row = a8[g:g + 1, :]
            m_t = jnp.maximum(a_row, jnp.max(dm, axis=0, keepdims=True))
            pm = jnp.exp(dm - m_t) * _nt(km, qm)
            c0 = c_prev[hd]
            nd = (jnp.exp(a_row - m_t) * _nt(c0.astype(BF16), qm)
                  + _dot(vaug_t.astype(BF16), pm.astype(BF16)))
            den = nd[den_row:den_row + 1, :]
            hh.append(nd / jnp.maximum(jnp.abs(den), jnp.exp(-m_t)))
            c_loc = _dot((vaug_t * w8[g:g + 1, :]).astype(BF16), km)
            c_next.append(a_sc[g:g + 1, 0:HEAD_PAD] * c0 + s_sc[g:g + 1, 0:HEAD_PAD] * c_loc)
        h_blocks.append(jnp.where(srow < ML_DIM, hh[0], hh[1]).T)
    grow = lax.broadcasted_iota(jnp.int32, (2 * ML_HEADS, L), 0)
    mine = (grow >= d * ML_HEADS) & (grow < (d + 1) * ML_HEADS)
    return h_blocks, c_next, jnp.where(mine, m_next, m_prev)


def _mlstm_kernel(qkf_ref, vf_ref, gf_ref, qkb_ref, vb_ref, gb_ref, fb_ref, ci_ref, mi_ref,
                  hf_ref, hb_ref, cs_ref, ms_ref):
    @pl.when(pl.program_id(1) == 0)
    def _():
        cs_ref[...] = ci_ref[...]
        ms_ref[...] = mi_ref[...]

    rows = lambda hd: slice(hd * HEAD_PAD, (hd + 1) * HEAD_PAD)
    ins = ((qkf_ref, vf_ref, gf_ref), (qkb_ref, vb_ref, gb_ref))
    outs = (hf_ref, hb_ref)
    L = ms_ref.shape[3]
    n_sub = qkf_ref.shape[1] // L
    states = [([cs_ref[0, d, rows(hd), :] for hd in range(ML_HEADS)], ms_ref[0, d]) for d in range(2)]
    for d in range(2):
        qk_ref, v_ref, g_ref = ins[d]
        c_state, m_state = states[d]
        for s in (range(n_sub - 1, -1, -1) if d else range(n_sub)):
            tok = slice(s * L, (s + 1) * L)
            h_blocks, c_state, m_state = _mlstm_direction(d, qk_ref[0, tok, :], v_ref[0, tok, :], g_ref[0, tok, :],
                                                          fb_ref[...], c_state, m_state)
            for blk, hb in enumerate(h_blocks):
                outs[d][0, tok, blk * HEAD_PAD:(blk + 1) * HEAD_PAD] = hb
        states[d] = (c_state, m_state)
    for d in range(2):
        for hd in range(ML_HEADS):
            cs_ref[0, d, rows(hd), :] = states[d][0][hd]
        ms_ref[0, d] = states[d][1]


def _mlstm(qk, vo, gates, p, c_init, m_init):
    B, T, _ = vo.shape
    chunk = min(ML_CHUNK, T)
    L = min(ML_BLOCK, T)
    nc = T // L
    const = lambda shape: pl.BlockSpec(shape, lambda b, j: (0,) * len(shape))
    state = pl.BlockSpec((1, 2, ML_HEADS * HEAD_PAD, HEAD_PAD), lambda b, j: (b, 0, 0, 0))
    mstate = pl.BlockSpec((1, 2, 2 * ML_HEADS, chunk), lambda b, j: (b, 0, 0, 0))
    fwd = lambda b, j: (b, j, 0)
    bwd = lambda b, j: (b, nc - 1 - j, 0)
    return pl.pallas_call(
        _mlstm_kernel,
        grid=(B, nc),
        in_specs=[pl.BlockSpec((1, L, QK_LANES), fwd),
                  pl.BlockSpec((1, L, ML_WIDTH), fwd),
                  pl.BlockSpec((1, L, LANE), fwd),
                  pl.BlockSpec((1, L, QK_LANES), bwd),
                  pl.BlockSpec((1, L, ML_WIDTH), bwd),
                  pl.BlockSpec((1, L, LANE), bwd),
                  const((1, LANE)), state, mstate],
        out_specs=[pl.BlockSpec((1, L, ML_WIDTH), fwd), pl.BlockSpec((1, L, ML_WIDTH), bwd), state, mstate],
        out_shape=[jax.ShapeDtypeStruct((B, T, ML_WIDTH), F32), jax.ShapeDtypeStruct((B, T, ML_WIDTH), F32),
                   jax.ShapeDtypeStruct((B, 2, ML_HEADS * HEAD_PAD, HEAD_PAD), F32),
                   jax.ShapeDtypeStruct((B, 2, 2 * ML_HEADS, chunk), F32)],
        compiler_params=_cparams(("parallel", "arbitrary")),
        name="mlstm",
    )(qk, vo, gates, qk, vo, gates, p["f_bias"], c_init, m_init)


def _outproj_kernel(x_ref, g1_ref, sh2_ref, sc2_ref, ya_ref, yb_ref, h0_ref, h1_ref, o_ref, mg_ref, avg_ref,
                    wo_ref, l1g_ref, l1b_ref, wr_ref, x1_ref, hm_ref, aff_ref):
    hs = h0_ref[0] + h1_ref[0]
    mu = _dot2(hs, avg_ref[...])
    hc = hs - mu
    var = _dot2(hc * hc, avg_ref[...])
    yc = (1.0 / (1.0 + jnp.exp(-o_ref[0]))) * (hc * lax.rsqrt(var + LN_EPS) * mg_ref[...])
    y = _dot(jnp.concatenate([ya_ref[0], yb_ref[0], yc.astype(BF16)], axis=1), wo_ref[...])
    x1 = _plain_norm(ALPHA * x_ref[0] + g1_ref[0] * y) * l1g_ref[...] + l1b_ref[...]
    x1_ref[0] = x1
    hm = _plain_norm(x1) * (1.0 + sc2_ref[0]) + sh2_ref[0]
    hm_ref[0] = hm.astype(BF16)
    hi, lo = _split(hm)
    parts = _dot(jnp.concatenate([hi, lo], axis=1), wr_ref[...])
    logits = (parts[:, :LANE] + parts[:, LANE:]).T[:N_EXPERTS, :]
    ex = jnp.exp(logits - jnp.max(logits, axis=0, keepdims=True))
    aff_ref[0] = ex / jnp.sum(ex, axis=0, keepdims=True)


def _outproj(x, g1, sh2, sc2, ya, yb, h_fwd, h_bwd, vo, p):
    B, T, D = x.shape
    tm = min(TOKEN_TILE, T)
    const = lambda shape: pl.BlockSpec(shape, lambda b, i: (0,) * len(shape))
    vec = pl.BlockSpec((1, 1, D), lambda b, i: (b, 0, 0))
    return pl.pallas_call(
        _outproj_kernel,
        grid=(B, T // tm),
        in_specs=[pl.BlockSpec((1, tm, D), lambda b, i: (b, i, 0)), vec, vec, vec,
                  pl.BlockSpec((1, tm, SG_WIDTH), lambda b, i: (b, i, 0)),
                  pl.BlockSpec((1, tm, MLA_HEADS * MLA_VDIM), lambda b, i: (b, i, 0)),
                  pl.BlockSpec((1, tm, ML_WIDTH), lambda b, i: (b, i, 0)),
                  pl.BlockSpec((1, tm, ML_WIDTH), lambda b, i: (b, i, 0)),
                  pl.BlockSpec((1, tm, ML_WIDTH), lambda b, i: (b, i, 1)),
                  const((1, ML_WIDTH)), const((ML_WIDTH, ML_WIDTH)), const((D, D)), const((1, D)), const((1, D)),
                  const((2 * D, 2 * LANE))],
        out_specs=[pl.BlockSpec((1, tm, D), lambda b, i: (b, i, 0)),
                   pl.BlockSpec((1, tm, D), lambda b, i: (b, i, 0)),
                   pl.BlockSpec((1, N_EXPERTS, tm), lambda b, i: (b, 0, i))],
        out_shape=[jax.ShapeDtypeStruct((B, T, D), F32), jax.ShapeDtypeStruct((B, T, D), BF16),
                   jax.ShapeDtypeStruct((B, N_EXPERTS, T), F32)],
        compiler_params=_cparams(("parallel", "parallel")),
        name="outproj",
    )(x, g1, sh2, sc2, ya, yb, h_fwd, h_bwd, vo, p["ml_norm_g"], p["head_avg"], p["w_out"], p["ln1_g"], p["ln1_b"],
      p["w_router3"])


def _route_kernel(aff_ref, pos_ref, cnt_ref, *, cap, tw):
    aff = aff_ref[0]
    E, T = aff.shape
    count = lambda m: jnp.sum(jnp.where(m, 1.0, 0.0), axis=1, keepdims=True)
    as_float = lambda b: pltpu.bitcast(b, F32)
    thr = jnp.zeros((E, 1), jnp.int32)
    for bit in range(30, -1, -1):
        cand = thr | (1 << bit)
        thr = jnp.where(count(aff >= as_float(cand)) >= cap, cand, thr)
    above = aff >= as_float(thr + 1)
    tied = (aff >= as_float(thr)) & jnp.logical_not(above)
    need = cap - count(above)
    idx = lax.broadcasted_iota(jnp.int32, (E, T), 1)
    cut = jnp.zeros((E, 1), jnp.int32)
    for bit in range(T.bit_length() - 1, -1, -1):
        cand = cut | (1 << bit)
        ok = (cand <= T) & (count(tied & (idx < cand)) <= need)
        cut = jnp.where(ok, cand, cut)
    sel = above | (tied & (idx < cut))
    r = lax.broadcasted_iota(jnp.int32, (tw, tw), 0)
    cidx = lax.broadcasted_iota(jnp.int32, (tw, tw), 1)
    before = jnp.where(r < cidx, 1.0, 0.0).astype(BF16)
    offset = jnp.zeros((E, 1), F32)
    lane = lax.broadcasted_iota(jnp.int32, (E, LANE), 1)
    table = jnp.zeros((E, LANE), F32)
    for blk in range(T // tw):
        cols = slice(blk * tw, (blk + 1) * tw)
        sb = jnp.where(sel[:, cols], 1.0, 0.0)
        slot = _dot(sb.astype(BF16), before) + offset
        pos_ref[0, :, cols] = jnp.where(sel[:, cols], slot, -1.0).astype(jnp.int32)
        offset = offset + jnp.sum(sb, axis=1, keepdims=True)
        table = jnp.where(lane == blk + 1, offset, table)
    cnt_ref[0] = table.astype(jnp.int32)


def _route(aff_t, cap):
    B, E, T = aff_t.shape
    nt = T // ROUTE_TILE
    pos, cnt = pl.pallas_call(
        functools.partial(_route_kernel, cap=cap, tw=ROUTE_TILE),
        grid=(B,),
        in_specs=[pl.BlockSpec((1, E, T), lambda b: (b, 0, 0))],
        out_specs=[pl.BlockSpec((1, E, T), lambda b: (b, 0, 0)), pl.BlockSpec((1, E, LANE), lambda b: (b, 0, 0))],
        out_shape=[jax.ShapeDtypeStruct((B, E, T), jnp.int32), jax.ShapeDtypeStruct((B, E, LANE), jnp.int32)],
        compiler_params=_cparams(("parallel",)),
        name="route",
    )(aff_t)
    return pos, cnt[:, :, :nt + 1].reshape(-1)


def _expert_kernel(cnt_ref, pos_ref, aff_ref, h_ref, wg_ref, wu_ref, wd_ref, y_ref, xe_ref, gate_ref,
                   wgb_ref, wub_ref, wdb_ref, *, n_win):
    nb, cap = y_ref.shape[0], y_ref.shape[2]
    nt = pos_ref.shape[2]
    rb = min(GATHER_ROWS, cap)
    e, j = pl.program_id(0), pl.program_id(1)

    @pl.when(j == 0)
    def _():
        wgb_ref[...] = wg_ref[0, 0].astype(BF16)
        wub_ref[...] = wu_ref[0, 0].astype(BF16)
        wdb_ref[...] = wd_ref[0, 0].astype(BF16)

    slot_iota = lax.broadcasted_iota(jnp.int32, (rb, ROUTE_TILE), 0)

    def gather(bb, i, n, s0):
        hits = [pos_ref[bb, 0, pl.ds(i + w, 1), :] == slot_iota + s0 for w in range(n)]
        onehot = jnp.concatenate([jnp.where(h, 1.0, 0.0).astype(BF16) for h in hits], axis=1)
        rows = pl.ds(pl.multiple_of(i * ROUTE_TILE, ROUTE_TILE), n * ROUTE_TILE)
        xe = _dot(onehot, h_ref[bb, rows, :])
        gate = sum(jnp.sum(jnp.where(h, aff_ref[bb, 0, pl.ds(i + w, 1), :], 0.0), axis=1, keepdims=True)
                   for w, h in enumerate(hits))
        return xe, gate

    for bb in range(nb):
        base = ((j * nb + bb) * N_EXPERTS + e) * (nt + 1)
        for r in range(cap // rb):
            s0 = r * rb
            first = sum((cnt_ref[base + i + 1] <= s0).astype(jnp.int32) for i in range(nt))
            end = sum((cnt_ref[base + i] < s0 + rb).astype(jnp.int32) for i in range(nt))
            start = jnp.minimum(first, nt - n_win)
            rows = slice(bb * cap + s0, bb * cap + s0 + rb)
            xe, gate = gather(bb, start, n_win, s0)
            xe_ref[rows, :] = xe.astype(BF16)
            gate_ref[rows, :] = gate

            def overflow(i, carry, bb=bb, s0=s0, rows=rows):
                xe, gate = gather(bb, i, 1, s0)
                xe_ref[rows, :] = (xe_ref[rows, :].astype(F32) + xe).astype(BF16)
                gate_ref[rows, :] += gate
                return carry

            lax.fori_loop(start + n_win, end, overflow, 0)
    xb = xe_ref[...]
    hid = _silu(_dot(xb, wgb_ref[...])) * _dot(xb, wub_ref[...])
    y = (_dot(hid.astype(BF16), wdb_ref[...]) * gate_ref[...]).astype(BF16)
    for bb in range(nb):
        y_ref[bb, 0] = y[bb * cap:(bb + 1) * cap]


def _experts(pos, cnt, aff_t, hm, p, cap):
    B, T, D = hm.shape
    E = N_EXPERTS
    nt = T // ROUTE_TILE
    layer = p["layer"]
    nb = max(1, min(B, EXPERT_ROWS // cap))
    row = pl.BlockSpec((nb, 1, nt, ROUTE_TILE), lambda e, j, c: (j, e, 0, 0))
    wspec = lambda shape: pl.BlockSpec((1, 1) + shape, lambda e, j, c: (layer, e, 0, 0))
    return pl.pallas_call(
        functools.partial(_expert_kernel, n_win=min(GATHER_TILES, nt)),
        grid_spec=pltpu.PrefetchScalarGridSpec(
            num_scalar_prefetch=1,
            grid=(E, B // nb),
            in_specs=[row, row,
                      pl.BlockSpec((nb, T, D), lambda e, j, c: (j, 0, 0)),
                      wspec((D, EXPERT_FF)), wspec((D, EXPERT_FF)), wspec((EXPERT_FF, D))],
            out_specs=pl.BlockSpec((nb, 1, cap, D), lambda e, j, c: (j, e, 0, 0)),
            scratch_shapes=[pltpu.VMEM((nb * cap, D), BF16), pltpu.VMEM((nb * cap, 1), F32),
                            pltpu.VMEM((D, EXPERT_FF), BF16), pltpu.VMEM((D, EXPERT_FF), BF16),
                            pltpu.VMEM((EXPERT_FF, D), BF16)]),
        out_shape=jax.ShapeDtypeStruct((B, E, cap, D), BF16),
        compiler_params=_cparams(("parallel", "arbitrary")),
        name="experts",
    )(cnt, pos.reshape(B, E, nt, ROUTE_TILE), aff_t.reshape(B, E, nt, ROUTE_TILE), hm,
      p["w_gate"], p["w_up"], p["w_down"])


def _combine_kernel(cnt_ref, pos_ref, y_ref, x_ref, g2_ref, lg_ref, lb_ref, o_ref, acc_ref, *, nt):
    E, cap = y_ref.shape[1], y_ref.shape[2]
    tt = x_ref.shape[1]
    win = min(COMBINE_WIN, cap)
    per = tt // ROUTE_TILE
    b, i = pl.program_id(0), pl.program_id(1)
    slot_iota = lax.broadcasted_iota(jnp.int32, (win, tt), 0)

    def window(e, w):
        start = pl.multiple_of(jnp.minimum(w, cap - win), 16)
        slot = slot_iota + start
        hit = (pos_ref[0, e:e + 1, :] == slot) & (slot >= w)
        return jnp.where(hit, 1.0, 0.0).astype(BF16), y_ref[0, e, pl.ds(start, win), :]

    onehots, windows, ranges = [], [], []
    for e in range(E):
        base = (b * E + e) * (nt + 1) + i * per
        lo, hi = cnt_ref[base], cnt_ref[base + per]
        w0 = jnp.bitwise_and(lo, ~15)
        onehot, rows = window(e, w0)
        windows.append(rows)
        onehots.append(onehot)
        ranges.append((w0, hi))
    acc_ref[...] = _tn(jnp.concatenate(onehots, axis=0), jnp.concatenate(windows, axis=0))
    for e in range(E):
        w0, hi = ranges[e]

        def more(w, e=e):
            onehot, rows = window(e, w)
            acc_ref[...] += _tn(onehot, rows)
            return w + win

        lax.while_loop(lambda w, hi=hi: w < hi, more, w0 + win)
    o_ref[0] = _plain_norm(ALPHA * x_ref[0] + g2_ref[0] * acc_ref[...]) * lg_ref[...] + lb_ref[...]


def _combine(pos, cnt, ye, x1, g2, p):
    B, T, D = x1.shape
    tt = min(COMBINE_TILE, T)
    E, cap = ye.shape[1], ye.shape[2]
    const = lambda shape: pl.BlockSpec(shape, lambda b, i, c: (0,) * len(shape))
    return pl.pallas_call(
        functools.partial(_combine_kernel, nt=T // ROUTE_TILE),
        grid_spec=pltpu.PrefetchScalarGridSpec(
            num_scalar_prefetch=1,
            grid=(B, T // tt),
            in_specs=[pl.BlockSpec((1, E, tt), lambda b, i, c: (b, 0, i)),
                      pl.BlockSpec((1, E, cap, D), lambda b, i, c: (b, 0, 0, 0)),
                      pl.BlockSpec((1, tt, D), lambda b, i, c: (b, i, 0)),
                      pl.BlockSpec((1, 1, D), lambda b, i, c: (b, 0, 0)),
                      const((1, D)), const((1, D))],
            out_specs=pl.BlockSpec((1, tt, D), lambda b, i, c: (b, i, 0)),
            scratch_shapes=[pltpu.VMEM((tt, D), F32)]),
        out_shape=jax.ShapeDtypeStruct((B, T, D), F32),
        compiler_params=_cparams(("parallel", "arbitrary")),
        name="combine",
    )(cnt, pos, ye, x1, g2, p["ln2_g"], p["ln2_b"])


def _pad_cols(a, before, total):
    pads = [(0, 0)] * (a.ndim - 1) + [(before, total - before - a.shape[-1])]
    return jnp.pad(a, pads)


def _prep_layer(l, w):
    D = D_MODEL
    wi, bi = w["w_in"][l], w["b_in"][l][None, :]
    n_sg, n_mla = 2 * SG_WIDTH, Q_LORA + KV_LORA + MLA_ROPE
    o_kr = n_sg + Q_LORA + KV_LORA
    o_ml = n_sg + n_mla
    o_g = o_ml + 4 * ML_WIDTH

    def cat(a):
        kr = a[:, o_kr:o_kr + MLA_ROPE]
        gates = a[:, o_g:o_g + 4 * ML_HEADS].reshape(-1, 2, 2, ML_HEADS).transpose(0, 2, 1, 3).reshape(-1, 4 * ML_HEADS)
        return jnp.concatenate([a[:, :o_kr], _pad_cols(kr, MLA_NOPE, HEAD_PAD), a[:, o_ml:o_g],
                                _pad_cols(gates, 0, LANE)], axis=1)

    wq = w["w_uq"][l].reshape(Q_LORA, MLA_HEADS, MLA_NOPE + MLA_ROPE)
    wq_main = _pad_cols(wq, 0, HEAD_PAD).reshape(Q_LORA, -1)
    wkv = w["w_ukv"][l].reshape(KV_LORA, MLA_HEADS, MLA_NOPE + MLA_VDIM)
    wk = _pad_cols(wkv[..., :MLA_NOPE], 0, HEAD_PAD).reshape(KV_LORA, -1)
    wv = _pad_cols(wkv[..., MLA_NOPE:], 0, HEAD_PAD).reshape(KV_LORA, -1)
    head = jnp.arange(ML_WIDTH) // ML_DIM
    wr = _pad_cols(w["w_router"][l], 0, LANE)
    wr_hi = wr.astype(BF16)
    return {
        "wcat": cat(wi).astype(BF16), "bcat": cat(bi),
        "q_norm_g": w["q_norm_g"][l][None, :], "kv_norm_g": w["kv_norm_g"][l][None, :],
        "wq": wq_main.astype(BF16),
        "wkv": jnp.concatenate([wk, wv], axis=1).astype(BF16),
        "sg_ln_g": w["sg_ln_g"][l][None, :], "sg_ln_b": w["sg_ln_b"][l][None, :],
        "sg_w": w["sg_w"][l].astype(BF16),
        "sg_bias": jnp.repeat(w["sg_b"][l].T, SG_GDIM, axis=1),
        "conv_w": jnp.pad(w["ml_conv_w"][l], ((0, 8 - ML_CONV), (0, 0))), "conv_b": w["ml_conv_b"][l][None, :],
        "f_bias": _pad_cols(w["ml_f_bias"][l].reshape(1, 2 * ML_HEADS), 0, LANE),
        "ml_norm_g": w["ml_norm_g"][l][None, :],
        "head_avg": ((head[:, None] == head[None, :]).astype(F32) / ML_DIM).astype(BF16),
        "w_out": w["w_out"][l].astype(BF16),
        "ln1_g": w["ln1_g"][l][None, :], "ln1_b": w["ln1_b"][l][None, :],
        "w_router3": jnp.concatenate([jnp.concatenate([wr_hi, (wr - wr_hi.astype(F32)).astype(BF16)], axis=1),
                                      jnp.concatenate([wr_hi, jnp.zeros_like(wr_hi)], axis=1)], axis=0),
        "layer": l, "w_gate": w["w_gate"], "w_up": w["w_up"], "w_down": w["w_down"],
        "ln2_g": w["ln2_g"][l][None, :], "ln2_b": w["ln2_b"][l][None, :],
    }


def _rope_tables(T):
    rows = T // GRID_W
    row = jnp.repeat(jnp.arange(rows, dtype=F32), GRID_W)
    colv = jnp.tile(jnp.arange(GRID_W, dtype=F32), rows)
    inv = ROPE_BASE ** (-jnp.arange(ROPE_AXIS // 2, dtype=F32) * 2.0 / ROPE_AXIS)
    ang = jnp.concatenate([row[:, None] * inv, colv[:, None] * inv], axis=-1)
    half = ROPE_AXIS // 2
    spread = lambda t: jnp.concatenate([t[:, :half], t[:, :half], t[:, half:], t[:, half:]], axis=1)
    cosT = jnp.concatenate([jnp.ones((T, MLA_NOPE), F32), spread(jnp.cos(ang)), jnp.zeros((T, 32), F32)], axis=1)
    sign = jnp.tile(jnp.repeat(jnp.asarray([-1.0, 1.0], F32), half), 2)
    sinS = _pad_cols(spread(jnp.sin(ang)) * sign, MLA_NOPE, HEAD_PAD)
    return cosT, sinS


def _moe(x1, hm, aff_t, g2, p):
    cap = EC_CAPACITY * x1.shape[1] // N_EXPERTS
    pos, cnt = _route(aff_t, cap)
    ye = _experts(pos, cnt, aff_t, hm, p, cap)
    return _combine(pos, cnt, ye, x1, g2, p)


def _forward(x, c, ctx, c_ctx, w):
    B, T, D = x.shape
    Tc = ctx.shape[1]
    L = w["w_ada"].shape[0]
    cc = jnp.zeros((16, D), F32).at[:B].set(c).at[B].set(c_ctx)
    ada = _ada(cc, w["w_ada"], w["b_ada"])
    cos_l, sin_l = _rope_tables(T)
    cos_c = _pad_cols(jnp.ones((Tc, MLA_NOPE + MLA_ROPE), F32), 0, HEAD_PAD)
    sin_c = jnp.zeros((Tc, HEAD_PAD), F32)
    zero_c = jnp.zeros((B, 2, ML_HEADS * HEAD_PAD, HEAD_PAD), F32)
    zero_m = jnp.zeros((B, 2, 2 * ML_HEADS, min(ML_CHUNK, Tc)), F32)
    xl, xc = x, ctx
    for l in range(L):
        p = _prep_layer(l, w)
        mods_l = [ada[l, :B, i * D:(i + 1) * D][:, None, :] for i in range(6)]
        mods_c = [jnp.broadcast_to(ada[l, B, i * D:(i + 1) * D][None, None, :], (B, 1, D)) for i in range(6)]
        ya_c, q_c, k_c, v_c, qk_c, vo_c, g_c = _inproj(xc, mods_c[0], mods_c[1], cos_c, sin_c, p)
        ya_l, q_l, k_l, v_l, qk_l, vo_l, g_l = _inproj(xl, mods_l[0], mods_l[1], cos_l, sin_l, p)
        hf_c, hb_c, cs_c, ms_c = _mlstm(qk_c, vo_c, g_c, p, zero_c, zero_m)
        hf_l, hb_l, _, _ = _mlstm(qk_l, vo_l, g_l, p, cs_c, ms_c)
        yb_l = _attention(q_l, [(k_l, v_l), (k_c, v_c)])
        x1_l, hm_l, aff_l = _outproj(xl, mods_l[2], mods_l[3], mods_l[4], ya_l, yb_l, hf_l, hb_l, vo_l, p)
        xl = _moe(x1_l, hm_l, aff_l, mods_l[5], p)
        if l < L - 1:
            yb_c = _attention(q_c, [(k_c, v_c)])
            x1_c, hm_c, aff_c = _outproj(xc, mods_c[2], mods_c[3], mods_c[4], ya_c, yb_c, hf_c, hb_c, vo_c, p)
            xc = _moe(x1_c, hm_c, aff_c, mods_c[5], p)
    return xl


def kernel(x, c, ctx, c_ctx, w_ada, b_ada, w_in, b_in, sg_ln_g, sg_ln_b, sg_w, sg_b, q_norm_g, kv_norm_g, w_uq, w_ukv,
           ml_conv_w, ml_conv_b, ml_f_bias, ml_norm_g, w_out, ln1_g, ln1_b, w_router, w_gate, w_up, w_down, ln2_g, ln2_b):
    w = dict(w_ada=w_ada, b_ada=b_ada, w_in=w_in, b_in=b_in, sg_ln_g=sg_ln_g, sg_ln_b=sg_ln_b, sg_w=sg_w, sg_b=sg_b,
             q_norm_g=q_norm_g, kv_norm_g=kv_norm_g, w_uq=w_uq, w_ukv=w_ukv, ml_conv_w=ml_conv_w, ml_conv_b=ml_conv_b,
             ml_f_bias=ml_f_bias, ml_norm_g=ml_norm_g, w_out=w_out, ln1_g=ln1_g, ln1_b=ln1_b, w_router=w_router,
             w_gate=w_gate, w_up=w_up, w_down=w_down, ln2_g=ln2_g, ln2_b=ln2_b)
    return _forward(x, c, ctx, c_ctx, w)
```

```python
import functools

import jax
import jax.numpy as jnp
from jax import lax
from jax.experimental import pallas as pl
from jax.experimental.pallas import tpu as pltpu

F32 = jnp.float32
BF16 = jnp.bfloat16
HIGHEST = lax.Precision.HIGHEST

D_MODEL = 1024
DEPTH = 2
GRID_W = 64
SG_WIDTH = 256
SG_GROUPS = 4
SG_GDIM = 64
SG_CHUNK = 128
MLA_HEADS = 8
MLA_NOPE = 64
MLA_ROPE = 32
MLA_VDIM = 64
Q_LORA = 384
KV_LORA = 256
ROPE_AXIS = 16
ROPE_BASE = 10000.0
MLA_SCALE = (MLA_NOPE + MLA_ROPE) ** -0.5
ML_HEADS = 4
ML_DIM = 64
ML_WIDTH = 256
ML_CONV = 5
N_EXPERTS = 16
EXPERT_FF = 1024
EC_CAPACITY = 2
ALPHA = (2 * DEPTH) ** 0.25
LN_EPS = 1e-6
LOG2E = 1.4426950408889634

LANE = 128
SUBLANE = 8
BF16_ROWS = 2 * SUBLANE
VMEM_BYTES_V7X = 64 * 1024 * 1024
VMEM_LIMIT = VMEM_BYTES_V7X * 7 // 8

HEAD_PAD = LANE
QK_LANES = ML_HEADS * HEAD_PAD + ML_WIDTH
OFF_SG, OFF_CQ, OFF_CKV, OFF_KR, OFF_ML, OFF_G, N_CAT = 0, 512, 896, 1152, 1280, 2304, 2432

ADA_COLS = 1536
TOKEN_TILE = 1024
ATTN_QUERIES = 512
ATTN_HEADS = 2
ATTN_KEYS = 4096
ML_CHUNK = 256
ML_BLOCK = 1024
ROUTE_TILE = 256
EXPERT_ROWS = 512
GATHER_ROWS = 128
GATHER_TILES = 6
COMBINE_TILE = 512
COMBINE_WIN = 128


def _cparams(sem):
    return pltpu.CompilerParams(dimension_semantics=sem, vmem_limit_bytes=VMEM_LIMIT)


def _nt(a, b):
    return lax.dot_general(a, b, (((1,), (1,)), ((), ())), preferred_element_type=F32)


def _tn(a, b):
    return lax.dot_general(a, b, (((0,), (0,)), ((), ())), preferred_element_type=F32)


def _dot(a, b, precision=None):
    return jnp.dot(a, b, preferred_element_type=F32, precision=precision)


def _split(x):
    hi = x.astype(BF16)
    return hi, (x - hi.astype(F32)).astype(BF16)


def _dot2(x, w):
    hi, lo = _split(x)
    return _dot(jnp.concatenate([hi, lo], axis=1), jnp.concatenate([w, w], axis=0))


def _plain_norm(x):
    mu = jnp.mean(x, axis=-1, keepdims=True)
    xc = x - mu
    var = jnp.mean(xc * xc, axis=-1, keepdims=True)
    return xc * lax.rsqrt(var + LN_EPS)


def _rms(x):
    return x * lax.rsqrt(jnp.mean(x * x, axis=-1, keepdims=True) + LN_EPS)


def _silu(x):
    return x * (1.0 / (1.0 + jnp.exp(-x)))


def _gelu_tanh(x):
    return 0.5 * x * (1.0 + jnp.tanh(0.7978845608028654 * (x + 0.044715 * (x * x * x))))


def _log_sigmoid(x):
    return jnp.minimum(x, 0.0) - jnp.log(1.0 + jnp.exp(-jnp.abs(x)))


def _ada_kernel(c_ref, w_ref, b_ref, o_ref):
    o_ref[0] = _dot(_silu(c_ref[...]), w_ref[0], precision=HIGHEST) + b_ref[0]


def _ada(cc, w_ada, b_ada):
    L, D, N = w_ada.shape
    rows = cc.shape[0]
    tn = ADA_COLS
    return pl.pallas_call(
        _ada_kernel,
        grid=(L, N // tn),
        in_specs=[pl.BlockSpec((rows, D), lambda l, j: (0, 0)),
                  pl.BlockSpec((1, D, tn), lambda l, j: (l, 0, j)),
                  pl.BlockSpec((1, 1, tn), lambda l, j: (l, 0, j))],
        out_specs=pl.BlockSpec((1, rows, tn), lambda l, j: (l, 0, j)),
        out_shape=jax.ShapeDtypeStruct((L, rows, N), F32),
        compiler_params=_cparams(("parallel", "parallel")),
        name="ada",
    )(cc, w_ada, b_ada.reshape(L, 1, N))


def _inproj_kernel(x_ref, xp_ref, xn_ref, sh_ref, sc_ref, cos_ref, sin_ref, wcat_ref, bcat_ref, qg_ref, kvg_ref, wq_ref,
                   wkv_ref, sglg_ref, sglb_ref, sgw_ref, sgb_ref, cw_ref, cb_ref,
                   ya_ref, q_ref, k_ref, v_ref, qk_ref, vo_ref, g_ref):
    tm = x_ref.shape[1]
    modulated = lambda t: (_plain_norm(t) * (1.0 + sc_ref[0]) + sh_ref[0]).astype(BF16)
    z = _dot(modulated(x_ref[0]), wcat_ref[...]) + bcat_ref[...]
    i = pl.program_id(1)
    qk_cols = slice(OFF_ML, OFF_ML + 2 * ML_WIDTH)
    halo = _dot(modulated(jnp.concatenate([xp_ref[0], xn_ref[0]], axis=0)), wcat_ref[:, qk_cols]) + bcat_ref[:, qk_cols]
    prev = jnp.where(i > 0, halo[:SUBLANE], 0.0)
    nxt = jnp.where(i < pl.num_programs(1) - 1, halo[SUBLANE:], 0.0)
    big = jnp.concatenate([prev, z[:, qk_cols], nxt], axis=0)
    n_big = tm + 2 * SUBLANE
    pad = (ML_CONV - 1) // 2
    conv = cb_ref[...]
    for t in range(ML_CONV):
        s = t - pad
        shifted = big if s == 0 else pltpu.roll(big, (-s) % n_big, axis=0)
        conv = conv + shifted[SUBLANE:SUBLANE + tm] * cw_ref[t:t + 1, :]
    qkc = _silu(conv)
    hlane = lax.broadcasted_iota(jnp.int32, (1, HEAD_PAD), 1)
    for hd in range(ML_HEADS):
        q_blk = qkc[:, (hd // 2) * HEAD_PAD:(hd // 2 + 1) * HEAD_PAD]
        in_half = (hlane >= ML_DIM) if hd % 2 else (hlane < ML_DIM)
        qk_ref[0, :, hd * HEAD_PAD:(hd + 1) * HEAD_PAD] = jnp.where(in_half, q_blk, 0.0).astype(BF16)
    qk_ref[0, :, ML_HEADS * HEAD_PAD:] = (qkc[:, ML_WIDTH:] * (ML_DIM ** -0.5)).astype(BF16)
    zs = _gelu_tanh(z[:, OFF_SG:OFF_SG + 2 * SG_WIDTH])
    u = zs[:, :SG_WIDTH]
    vn = _plain_norm(zs[:, SG_WIDTH:]) * sglg_ref[...] + sglb_ref[...]
    group = jnp.right_shift(lax.broadcasted_iota(jnp.int32, (SG_CHUNK, SG_WIDTH), 1), SG_GDIM.bit_length() - 1)
    for ch in range(tm // SG_CHUNK):
        rows = slice(ch * SG_CHUNK, (ch + 1) * SG_CHUNK)
        vc = vn[rows].astype(BF16)
        mixed = sgb_ref[...]
        for g in range(SG_GROUPS):
            mixed = mixed + jnp.where(group == g, _dot(sgw_ref[g], vc), 0.0)
        ya_ref[0, rows, :] = (u[rows] * mixed).astype(BF16)
    cosT = cos_ref[...]
    sinS = sin_ref[...]
    lane = lax.broadcasted_iota(jnp.int32, (1, HEAD_PAD), 1)
    first = jnp.bitwise_and(lane, ROPE_AXIS // 2) == 0

    def rope(t):
        partner = jnp.where(first, pltpu.roll(t, HEAD_PAD - ROPE_AXIS // 2, axis=1), pltpu.roll(t, ROPE_AXIS // 2, axis=1))
        return t * cosT + partner * sinS

    cqn = (_rms(z[:, OFF_CQ:OFF_CQ + Q_LORA]) * qg_ref[...]).astype(BF16)
    qq = _dot(cqn, wq_ref[...])
    ckvn = (_rms(z[:, OFF_CKV:OFF_CKV + KV_LORA]) * kvg_ref[...]).astype(BF16)
    kv = _dot(ckvn, wkv_ref[...])
    kr = rope(z[:, OFF_KR:OFF_KR + HEAD_PAD])
    one_lane = jnp.where(lane == MLA_VDIM, 1.0, 0.0)
    hw = MLA_HEADS * HEAD_PAD
    for hd in range(MLA_HEADS):
        cols = slice(hd * HEAD_PAD, (hd + 1) * HEAD_PAD)
        rcols = slice(hw + hd * HEAD_PAD, hw + (hd + 1) * HEAD_PAD)
        q_ref[0, hd] = (rope(qq[:, cols]) * (MLA_SCALE * LOG2E)).astype(BF16)
        k_ref[0, hd] = (kv[:, cols] + kr).astype(BF16)
        v_ref[0, hd] = (kv[:, rcols] + one_lane).astype(BF16)
    vo_ref[0] = z[:, OFF_ML + 2 * ML_WIDTH:OFF_ML + 4 * ML_WIDTH]
    g_ref[0] = z[:, OFF_G:OFF_G + LANE]


def _inproj(x, shift, scale, cosT, sinT, p):
    B, T, D = x.shape
    tm = min(TOKEN_TILE, T)
    hb = tm // SUBLANE
    const = lambda shape: pl.BlockSpec(shape, lambda b, i: (0,) * len(shape))
    heads = (B, MLA_HEADS, T, HEAD_PAD)
    return pl.pallas_call(
        _inproj_kernel,
        grid=(B, T // tm),
        in_specs=[pl.BlockSpec((1, tm, D), lambda b, i: (b, i, 0)),
                  pl.BlockSpec((1, SUBLANE, D), lambda b, i: (b, jnp.maximum(i * hb - 1, 0), 0)),
                  pl.BlockSpec((1, SUBLANE, D), lambda b, i: (b, jnp.minimum((i + 1) * hb, T // SUBLANE - 1), 0)),
                  pl.BlockSpec((1, 1, D), lambda b, i: (b, 0, 0)),
                  pl.BlockSpec((1, 1, D), lambda b, i: (b, 0, 0)),
                  pl.BlockSpec((tm, HEAD_PAD), lambda b, i: (i, 0)),
                  pl.BlockSpec((tm, HEAD_PAD), lambda b, i: (i, 0)),
                  const((D, N_CAT)), const((1, N_CAT)), const((1, Q_LORA)), const((1, KV_LORA)),
                  const((Q_LORA, MLA_HEADS * HEAD_PAD)), const((KV_LORA, 2 * MLA_HEADS * HEAD_PAD)),
                  const((1, SG_WIDTH)), const((1, SG_WIDTH)), const((SG_GROUPS, SG_CHUNK, SG_CHUNK)),
                  const((SG_CHUNK, SG_WIDTH)), const((SUBLANE, 2 * ML_WIDTH)), const((1, 2 * ML_WIDTH))],
        out_specs=[pl.BlockSpec((1, tm, SG_WIDTH), lambda b, i: (b, i, 0)),
                   pl.BlockSpec((1, MLA_HEADS, tm, HEAD_PAD), lambda b, i: (b, 0, i, 0)),
                   pl.BlockSpec((1, MLA_HEADS, tm, HEAD_PAD), lambda b, i: (b, 0, i, 0)),
                   pl.BlockSpec((1, MLA_HEADS, tm, HEAD_PAD), lambda b, i: (b, 0, i, 0)),
                   pl.BlockSpec((1, tm, QK_LANES), lambda b, i: (b, i, 0)),
                   pl.BlockSpec((1, tm, 2 * ML_WIDTH), lambda b, i: (b, i, 0)),
                   pl.BlockSpec((1, tm, LANE), lambda b, i: (b, i, 0))],
        out_shape=[jax.ShapeDtypeStruct((B, T, SG_WIDTH), BF16),
                   jax.ShapeDtypeStruct(heads, BF16), jax.ShapeDtypeStruct(heads, BF16),
                   jax.ShapeDtypeStruct(heads, BF16),
                   jax.ShapeDtypeStruct((B, T, QK_LANES), BF16),
                   jax.ShapeDtypeStruct((B, T, 2 * ML_WIDTH), F32),
                   jax.ShapeDtypeStruct((B, T, LANE), F32)],
        compiler_params=_cparams(("parallel", "parallel")),
        name="inproj",
    )(x, x, x, shift, scale, cosT, sinT, p["wcat"], p["bcat"], p["q_norm_g"], p["kv_norm_g"], p["wq"], p["wkv"],
      p["sg_ln_g"], p["sg_ln_b"], p["sg_w"], p["sg_bias"], p["conv_w"], p["conv_b"])


def _attn_kernel(*refs, n_src, tk):
    q_ref = refs[0]
    kv_refs = refs[1:1 + 2 * n_src]
    o_ref = refs[1 + 2 * n_src]
    nh, tq = q_ref.shape[1], q_ref.shape[2]
    qs = [q_ref[0, hh] for hh in range(nh)]
    ms = [jnp.full((tq, 1), -jnp.inf, F32) for _ in range(nh)]
    accs = [jnp.zeros((tq, HEAD_PAD), F32) for _ in range(nh)]
    for s in range(n_src):
        k_ref, v_ref = kv_refs[2 * s], kv_refs[2 * s + 1]
        ck = min(tk, k_ref.shape[2])
        for c in range(k_ref.shape[2] // ck):
            rows = slice(c * ck, (c + 1) * ck)
            for hh in range(nh):
                sc = _nt(qs[hh], k_ref[0, hh, rows, :])
                m_new = jnp.maximum(ms[hh], jnp.max(sc, axis=1, keepdims=True))
                pr = jnp.exp2(sc - m_new)
                accs[hh] = accs[hh] * jnp.exp2(ms[hh] - m_new) + _dot(pr.astype(BF16), v_ref[0, hh, rows, :])
                ms[hh] = m_new
    outs = [accs[hh] / accs[hh][:, MLA_VDIM:MLA_VDIM + 1] for hh in range(nh)]
    lane = lax.broadcasted_iota(jnp.int32, (tq, HEAD_PAD), 1)
    for pair in range(nh // 2):
        both = jnp.where(lane < MLA_VDIM, outs[2 * pair], pltpu.roll(outs[2 * pair + 1], MLA_VDIM, axis=1))
        o_ref[0, :, pair * HEAD_PAD:(pair + 1) * HEAD_PAD] = both.astype(BF16)


def _attention(q, kvs):
    B, H, T, _ = q.shape
    tq, tk, nh = min(ATTN_QUERIES, T), ATTN_KEYS, ATTN_HEADS
    in_specs = [pl.BlockSpec((1, nh, tq, HEAD_PAD), lambda b, hp, i: (b, hp, i, 0))]
    args = [q]
    for k, v in kvs:
        K = k.shape[2]
        in_specs += [pl.BlockSpec((1, nh, K, HEAD_PAD), lambda b, hp, i: (b, hp, 0, 0))] * 2
        args += [k, v]
    return pl.pallas_call(
        functools.partial(_attn_kernel, n_src=len(kvs), tk=tk),
        grid=(B, H // nh, T // tq),
        in_specs=in_specs,
        out_specs=pl.BlockSpec((1, tq, nh * MLA_VDIM), lambda b, hp, i: (b, i, hp)),
        out_shape=jax.ShapeDtypeStruct((B, T, H * MLA_VDIM), BF16),
        compiler_params=_cparams(("parallel", "parallel", "parallel")),
        name="attention",
    )(*args)


def _mlstm_direction(d, qk, vv, gl, fb, c_prev, m_prev):
    L = qk.shape[0]
    lf = _log_sigmoid(pltpu.roll(gl, LANE - 2 * ML_HEADS, axis=1) + fb)
    ks = lax.broadcasted_iota(jnp.int32, (L, L), 0)
    qt = lax.broadcasted_iota(jnp.int32, (L, L), 1)
    visible = (ks >= qt) if d else (ks <= qt)
    tri = jnp.where((qt >= ks) if d else (qt <= ks), 1.0, 0.0).astype(BF16)
    lf_hi, lf_lo = _split(lf)
    bc = _dot(tri, lf_hi) + _dot(tri, lf_lo)
    r_cols = gl - bc
    b8 = bc.T[0:2 * ML_HEADS]
    li8 = gl.T[0:2 * ML_HEADS]
    b_end = b8[:, 0:1] if d else b8[:, L - 1:L]
    gw = b_end - b8 + li8
    m_loc = jnp.max(gw, axis=1, keepdims=True)
    w8 = jnp.exp(gw - m_loc)
    m_next = jnp.maximum(b_end + m_prev, m_loc)
    a_sc = jnp.exp(b_end + m_prev - m_next)
    s_sc = jnp.exp(m_loc - m_next)
    a8 = b8 + m_prev

    srow = lax.broadcasted_iota(jnp.int32, (HEAD_PAD, L), 0)
    h_blocks, c_next = [], []
    for blk in range(ML_HEADS // 2):
        km = qk[:, (ML_HEADS + blk) * HEAD_PAD:(ML_HEADS + blk + 1) * HEAD_PAD]
        v_t = vv[:, blk * HEAD_PAD:(blk + 1) * HEAD_PAD].T
        hh = []
        for half in range(2):
            hd = 2 * blk + half
            g = d * ML_HEADS + hd
            in_half = (srow >= ML_DIM) if half else (srow < ML_DIM)
            den_row = 0 if half else ML_DIM
            qm = qk[:, hd * HEAD_PAD:(hd + 1) * HEAD_PAD]
            vaug_t = jnp.where(in_half, v_t, jnp.where(srow == den_row, 1.0, 0.0))
            dm = jnp.where(visible, r_cols[:, g:g + 1] + b8[g:g + 1, :], -jnp.inf)
            a_row = a8[g:g + 1, :]
            m_t = jnp.maximum(a_row, jnp.max(dm, axis=0, keepdims=True))
            pm = jnp.exp(dm - m_t) * _nt(km, qm)
            c0 = c_prev[hd]
            nd = (jnp.exp(a_row - m_t) * _nt(c0.astype(BF16), qm)
                  + _dot(vaug_t.astype(BF16), pm.astype(BF16)))
            den = nd[den_row:den_row + 1, :]
            hh.append(nd / jnp.maximum(jnp.abs(den), jnp.exp(-m_t)))
            c_loc = _dot((vaug_t * w8[g:g + 1, :]).astype(BF16), km)
            c_next.append(a_sc[g:g + 1, 0:HEAD_PAD] * c0 + s_sc[g:g + 1, 0:HEAD_PAD] * c_loc)
        h_blocks.append(jnp.where(srow < ML_DIM, hh[0], hh[1]).T)
    grow = lax.broadcasted_iota(jnp.int32, (2 * ML_HEADS, L), 0)
    mine = (grow >= d * ML_HEADS) & (grow < (d + 1) * ML_HEADS)
    return h_blocks, c_next, jnp.where(mine, m_next, m_prev)


def _mlstm_kernel(qkf_ref, vf_ref, gf_ref, qkb_ref, vb_ref, gb_ref, fb_ref, ci_ref, mi_ref,
                  hf_ref, hb_ref, cs_ref, ms_ref):
    @pl.when(pl.program_id(1) == 0)
    def _():
        cs_ref[...] = ci_ref[...]
        ms_ref[...] = mi_ref[...]

    rows = lambda hd: slice(hd * HEAD_PAD, (hd + 1) * HEAD_PAD)
    ins = ((qkf_ref, vf_ref, gf_ref), (qkb_ref, vb_ref, gb_ref))
    outs = (hf_ref, hb_ref)
    L = ms_ref.shape[3]
    n_sub = qkf_ref.shape[1] // L
    states = [([cs_ref[0, d, rows(hd), :] for hd in range(ML_HEADS)], ms_ref[0, d]) for d in range(2)]
    for d in range(2):
        qk_ref, v_ref, g_ref = ins[d]
        c_state, m_state = states[d]
        for s in (range(n_sub - 1, -1, -1) if d else range(n_sub)):
            tok = slice(s * L, (s + 1) * L)
            h_blocks, c_state, m_state = _mlstm_direction(d, qk_ref[0, tok, :], v_ref[0, tok, :], g_ref[0, tok, :],
                                                          fb_ref[...], c_state, m_state)
            for blk, hb in enumerate(h_blocks):
                outs[d][0, tok, blk * HEAD_PAD:(blk + 1) * HEAD_PAD] = hb
        states[d] = (c_state, m_state)
    for d in range(2):
        for hd in range(ML_HEADS):
            cs_ref[0, d, rows(hd), :] = states[d][0][hd]
        ms_ref[0, d] = states[d][1]


def _mlstm(qk, vo, gates, p, c_init, m_init):
    B, T, _ = vo.shape
    chunk = min(ML_CHUNK, T)
    L = min(ML_BLOCK, T)
    nc = T // L
    const = lambda shape: pl.BlockSpec(shape, lambda b, j: (0,) * len(shape))
    state = pl.BlockSpec((1, 2, ML_HEADS * HEAD_PAD, HEAD_PAD), lambda b, j: (b, 0, 0, 0))
    mstate = pl.BlockSpec((1, 2, 2 * ML_HEADS, chunk), lambda b, j: (b, 0, 0, 0))
    fwd = lambda b, j: (b, j, 0)
    bwd = lambda b, j: (b, nc - 1 - j, 0)
    return pl.pallas_call(
        _mlstm_kernel,
        grid=(B, nc),
        in_specs=[pl.BlockSpec((1, L, QK_LANES), fwd),
                  pl.BlockSpec((1, L, ML_WIDTH), fwd),
                  pl.BlockSpec((1, L, LANE), fwd),
                  pl.BlockSpec((1, L, QK_LANES), bwd),
                  pl.BlockSpec((1, L, ML_WIDTH), bwd),
                  pl.BlockSpec((1, L, LANE), bwd),
                  const((1, LANE)), state, mstate],
        out_specs=[pl.BlockSpec((1, L, ML_WIDTH), fwd), pl.BlockSpec((1, L, ML_WIDTH), bwd), state, mstate],
        out_shape=[jax.ShapeDtypeStruct((B, T, ML_WIDTH), F32), jax.ShapeDtypeStruct((B, T, ML_WIDTH), F32),
                   jax.ShapeDtypeStruct((B, 2, ML_HEADS * HEAD_PAD, HEAD_PAD), F32),
                   jax.ShapeDtypeStruct((B, 2, 2 * ML_HEADS, chunk), F32)],
        compiler_params=_cparams(("parallel", "arbitrary")),
        name="mlstm",
    )(qk, vo, gates, qk, vo, gates, p["f_bias"], c_init, m_init)


def _outproj_kernel(x_ref, g1_ref, sh2_ref, sc2_ref, ya_ref, yb_ref, h0_ref, h1_ref, o_ref, mg_ref, avg_ref,
                    wo_ref, l1g_ref, l1b_ref, wr_ref, x1_ref, hm_ref, aff_ref):
    hs = h0_ref[0] + h1_ref[0]
    mu = _dot2(hs, avg_ref[...])
    hc = hs - mu
    var = _dot2(hc * hc, avg_ref[...])
    yc = (1.0 / (1.0 + jnp.exp(-o_ref[0]))) * (hc * lax.rsqrt(var + LN_EPS) * mg_ref[...])
    y = _dot(jnp.concatenate([ya_ref[0], yb_ref[0], yc.astype(BF16)], axis=1), wo_ref[...])
    x1 = _plain_norm(ALPHA * x_ref[0] + g1_ref[0] * y) * l1g_ref[...] + l1b_ref[...]
    x1_ref[0] = x1
    hm = _plain_norm(x1) * (1.0 + sc2_ref[0]) + sh2_ref[0]
    hm_ref[0] = hm.astype(BF16)
    hi, lo = _split(hm)
    parts = _dot(jnp.concatenate([hi, lo], axis=1), wr_ref[...])
    logits = (parts[:, :LANE] + parts[:, LANE:]).T[:N_EXPERTS, :]
    ex = jnp.exp(logits - jnp.max(logits, axis=0, keepdims=True))
    aff_ref[0] = ex / jnp.sum(ex, axis=0, keepdims=True)


def _outproj(x, g1, sh2, sc2, ya, yb, h_fwd, h_bwd, vo, p):
    B, T, D = x.shape
    tm = min(TOKEN_TILE, T)
    const = lambda shape: pl.BlockSpec(shape, lambda b, i: (0,) * len(shape))
    vec = pl.BlockSpec((1, 1, D), lambda b, i: (b, 0, 0))
    return pl.pallas_call(
        _outproj_kernel,
        grid=(B, T // tm),
        in_specs=[pl.BlockSpec((1, tm, D), lambda b, i: (b, i, 0)), vec, vec, vec,
                  pl.BlockSpec((1, tm, SG_WIDTH), lambda b, i: (b, i, 0)),
                  pl.BlockSpec((1, tm, MLA_HEADS * MLA_VDIM), lambda b, i: (b, i, 0)),
                  pl.BlockSpec((1, tm, ML_WIDTH), lambda b, i: (b, i, 0)),
                  pl.BlockSpec((1, tm, ML_WIDTH), lambda b, i: (b, i, 0)),
                  pl.BlockSpec((1, tm, ML_WIDTH), lambda b, i: (b, i, 1)),
                  const((1, ML_WIDTH)), const((ML_WIDTH, ML_WIDTH)), const((D, D)), const((1, D)), const((1, D)),
                  const((2 * D, 2 * LANE))],
        out_specs=[pl.BlockSpec((1, tm, D), lambda b, i: (b, i, 0)),
                   pl.BlockSpec((1, tm, D), lambda b, i: (b, i, 0)),
                   pl.BlockSpec((1, N_EXPERTS, tm), lambda b, i: (b, 0, i))],
        out_shape=[jax.ShapeDtypeStruct((B, T, D), F32), jax.ShapeDtypeStruct((B, T, D), BF16),
                   jax.ShapeDtypeStruct((B, N_EXPERTS, T), F32)],
        compiler_params=_cparams(("parallel", "parallel")),
        name="outproj",
    )(x, g1, sh2, sc2, ya, yb, h_fwd, h_bwd, vo, p["ml_norm_g"], p["head_avg"], p["w_out"], p["ln1_g"], p["ln1_b"],
      p["w_router3"])


def _route_kernel(aff_ref, pos_ref, cnt_ref, *, cap, tw):
    aff = aff_ref[0]
    E, T = aff.shape
    count = lambda m: jnp.sum(jnp.where(m, 1.0, 0.0), axis=1, keepdims=True)
    as_float = lambda b: pltpu.bitcast(b, F32)
    thr = jnp.zeros((E, 1), jnp.int32)
    for bit in range(30, -1, -1):
        cand = thr | (1 << bit)
        thr = jnp.where(count(aff >= as_float(cand)) >= cap, cand, thr)
    above = aff >= as_float(thr + 1)
    tied = (aff >= as_float(thr)) & jnp.logical_not(above)
    need = cap - count(above)
    idx = lax.broadcasted_iota(jnp.int32, (E, T), 1)
    cut = jnp.zeros((E, 1), jnp.int32)
    for bit in range(T.bit_length() - 1, -1, -1):
        cand = cut | (1 << bit)
        ok = (cand <= T) & (count(tied & (idx < cand)) <= need)
        cut = jnp.where(ok, cand, cut)
    sel = above | (tied & (idx < cut))
    r = lax.broadcasted_iota(jnp.int32, (tw, tw), 0)
    cidx = lax.broadcasted_iota(jnp.int32, (tw, tw), 1)
    before = jnp.where(r < cidx, 1.0, 0.0).astype(BF16)
    offset = jnp.zeros((E, 1), F32)
    lane = lax.broadcasted_iota(jnp.int32, (E, LANE), 1)
    table = jnp.zeros((E, LANE), F32)
    for blk in range(T // tw):
        cols = slice(blk * tw, (blk + 1) * tw)
        sb = jnp.where(sel[:, cols], 1.0, 0.0)
        slot = _dot(sb.astype(BF16), before) + offset
        pos_ref[0, :, cols] = jnp.where(sel[:, cols], slot, -1.0).astype(jnp.int32)
        offset = offset + jnp.sum(sb, axis=1, keepdims=True)
        table = jnp.where(lane == blk + 1, offset, table)
    cnt_ref[0] = table.astype(jnp.int32)


def _route(aff_t, cap):
    B, E, T = aff_t.shape
    nt = T // ROUTE_TILE
    pos, cnt = pl.pallas_call(
        functools.partial(_route_kernel, cap=cap, tw=ROUTE_TILE),
        grid=(B,),
        in_specs=[pl.BlockSpec((1, E, T), lambda b: (b, 0, 0))],
        out_specs=[pl.BlockSpec((1, E, T), lambda b: (b, 0, 0)), pl.BlockSpec((1, E, LANE), lambda b: (b, 0, 0))],
        out_shape=[jax.ShapeDtypeStruct((B, E, T), jnp.int32), jax.ShapeDtypeStruct((B, E, LANE), jnp.int32)],
        compiler_params=_cparams(("parallel",)),
        name="route",
    )(aff_t)
    return pos, cnt[:, :, :nt + 1].reshape(-1)


def _expert_kernel(cnt_ref, pos_ref, aff_ref, h_ref, wg_ref, wu_ref, wd_ref, y_ref, xe_ref, gate_ref,
                   wgb_ref, wub_ref, wdb_ref, *, n_win):
    nb, cap = y_ref.shape[0], y_ref.shape[2]
    nt = pos_ref.shape[2]
    rb = min(GATHER_ROWS, cap)
    e, j = pl.program_id(0), pl.program_id(1)

    @pl.when(j == 0)
    def _():
        wgb_ref[...] = wg_ref[0, 0].astype(BF16)
        wub_ref[...] = wu_ref[0, 0].astype(BF16)
        wdb_ref[...] = wd_ref[0, 0].astype(BF16)

    slot_iota = lax.broadcasted_iota(jnp.int32, (rb, ROUTE_TILE), 0)

    def gather(bb, i, n, s0):
        hits = [pos_ref[bb, 0, pl.ds(i + w, 1), :] == slot_iota + s0 for w in range(n)]
        onehot = jnp.concatenate([jnp.where(h, 1.0, 0.0).astype(BF16) for h in hits], axis=1)
        rows = pl.ds(pl.multiple_of(i * ROUTE_TILE, ROUTE_TILE), n * ROUTE_TILE)
        xe = _dot(onehot, h_ref[bb, rows, :])
        gate = sum(jnp.sum(jnp.where(h, aff_ref[bb, 0, pl.ds(i + w, 1), :], 0.0), axis=1, keepdims=True)
                   for w, h in enumerate(hits))
        return xe, gate

    for bb in range(nb):
        base = ((j * nb + bb) * N_EXPERTS + e) * (nt + 1)
        for r in range(cap // rb):
            s0 = r * rb
            first = sum((cnt_ref[base + i + 1] <= s0).astype(jnp.int32) for i in range(nt))
            end = sum((cnt_ref[base + i] < s0 + rb).astype(jnp.int32) for i in range(nt))
            start = jnp.minimum(first, nt - n_win)
            rows = slice(bb * cap + s0, bb * cap + s0 + rb)
            xe, gate = gather(bb, start, n_win, s0)
            xe_ref[rows, :] = xe.astype(BF16)
            gate_ref[rows, :] = gate

            def overflow(i, carry, bb=bb, s0=s0, rows=rows):
                xe, gate = gather(bb, i, 1, s0)
                xe_ref[rows, :] = (xe_ref[rows, :].astype(F32) + xe).astype(BF16)
                gate_ref[rows, :] += gate
                return carry

            lax.fori_loop(start + n_win, end, overflow, 0)
    xb = xe_ref[...]
    hid = _silu(_dot(xb, wgb_ref[...])) * _dot(xb, wub_ref[...])
    y = (_dot(hid.astype(BF16), wdb_ref[...]) * gate_ref[...]).astype(BF16)
    for bb in range(nb):
        y_ref[bb, 0] = y[bb * cap:(bb + 1) * cap]


def _experts(pos, cnt, aff_t, hm, p, cap):
    B, T, D = hm.shape
    E = N_EXPERTS
    nt = T // ROUTE_TILE
    layer = p["layer"]
    nb = max(1, min(B, EXPERT_ROWS // cap))
    row = pl.BlockSpec((nb, 1, nt, ROUTE_TILE), lambda e, j, c: (j, e, 0, 0))
    wspec = lambda shape: pl.BlockSpec((1, 1) + shape, lambda e, j, c: (layer, e, 0, 0))
    return pl.pallas_call(
        functools.partial(_expert_kernel, n_win=min(GATHER_TILES, nt)),
        grid_spec=pltpu.PrefetchScalarGridSpec(
            num_scalar_prefetch=1,
            grid=(E, B // nb),
            in_specs=[row, row,
                      pl.BlockSpec((nb, T, D), lambda e, j, c: (j, 0, 0)),
                      wspec((D, EXPERT_FF)), wspec((D, EXPERT_FF)), wspec((EXPERT_FF, D))],
            out_specs=pl.BlockSpec((nb, 1, cap, D), lambda e, j, c: (j, e, 0, 0)),
            scratch_shapes=[pltpu.VMEM((nb * cap, D), BF16), pltpu.VMEM((nb * cap, 1), F32),
                            pltpu.VMEM((D, EXPERT_FF), BF16), pltpu.VMEM((D, EXPERT_FF), BF16),
                            pltpu.VMEM((EXPERT_FF, D), BF16)]),
        out_shape=jax.ShapeDtypeStruct((B, E, cap, D), BF16),
        compiler_params=_cparams(("parallel", "arbitrary")),
        name="experts",
    )(cnt, pos.reshape(B, E, nt, ROUTE_TILE), aff_t.reshape(B, E, nt, ROUTE_TILE), hm,
      p["w_gate"], p["w_up"], p["w_down"])


def _combine_kernel(cnt_ref, pos_ref, y_ref, x_ref, g2_ref, lg_ref, lb_ref, o_ref, acc_ref, *, nt):
    E, cap = y_ref.shape[1], y_ref.shape[2]
    tt = x_ref.shape[1]
    win = min(COMBINE_WIN, cap)
    per = tt // ROUTE_TILE
    b, i = pl.program_id(0), pl.program_id(1)
    slot_iota = lax.broadcasted_iota(jnp.int32, (win, tt), 0)

    def window(e, w):
        start = pl.multiple_of(jnp.minimum(w, cap - win), BF16_ROWS)
        slot = slot_iota + start
        hit = (pos_ref[0, e:e + 1, :] == slot) & (slot >= w)
        return jnp.where(hit, 1.0, 0.0).astype(BF16), y_ref[0, e, pl.ds(start, win), :]

    onehots, windows, ranges = [], [], []
    for e in range(E):
        base = (b * E + e) * (nt + 1) + i * per
        lo, hi = cnt_ref[base], cnt_ref[base + per]
        w0 = jnp.bitwise_and(lo, -BF16_ROWS)
        onehot, rows = window(e, w0)
        windows.append(rows)
        onehots.append(onehot)
        ranges.append((w0, hi))
    acc_ref[...] = _tn(jnp.concatenate(onehots, axis=0), jnp.concatenate(windows, axis=0))
    overflow = functools.reduce(jnp.logical_or, [hi > w0 + win for w0, hi in ranges])

    @pl.when(overflow)
    def _():
        for e in range(E):
            w0, hi = ranges[e]

            def more(w, e=e):
                onehot, rows = window(e, w)
                acc_ref[...] += _tn(onehot, rows)
                return w + win

            lax.while_loop(lambda w, hi=hi: w < hi, more, w0 + win)

    o_ref[0] = _plain_norm(ALPHA * x_ref[0] + g2_ref[0] * acc_ref[...]) * lg_ref[...] + lb_ref[...]


def _combine(pos, cnt, ye, x1, g2, p):
    B, T, D = x1.shape
    tt = min(COMBINE_TILE, T)
    E, cap = ye.shape[1], ye.shape[2]
    const = lambda shape: pl.BlockSpec(shape, lambda b, i, c: (0,) * len(shape))
    return pl.pallas_call(
        functools.partial(_combine_kernel, nt=T // ROUTE_TILE),
        grid_spec=pltpu.PrefetchScalarGridSpec(
            num_scalar_prefetch=1,
            grid=(B, T // tt),
            in_specs=[pl.BlockSpec((1, E, tt), lambda b, i, c: (b, 0, i)),
                      pl.BlockSpec((1, E, cap, D), lambda b, i, c: (b, 0, 0, 0)),
                      pl.BlockSpec((1, tt, D), lambda b, i, c: (b, i, 0)),
                      pl.BlockSpec((1, 1, D), lambda b, i, c: (b, 0, 0)),
                      const((1, D)), const((1, D))],
            out_specs=pl.BlockSpec((1, tt, D), lambda b, i, c: (b, i, 0)),
            scratch_shapes=[pltpu.VMEM((tt, D), F32)]),
        out_shape=jax.ShapeDtypeStruct((B, T, D), F32),
        compiler_params=_cparams(("parallel", "arbitrary")),
        name="combine",
    )(cnt, pos, ye, x1, g2, p["ln2_g"], p["ln2_b"])


def _pad_cols(a, before, total):
    pads = [(0, 0)] * (a.ndim - 1) + [(before, total - before - a.shape[-1])]
    return jnp.pad(a, pads)


def _prep_layer(l, w):
    D = D_MODEL
    wi, bi = w["w_in"][l], w["b_in"][l][None, :]
    n_sg, n_mla = 2 * SG_WIDTH, Q_LORA + KV_LORA + MLA_ROPE
    o_kr = n_sg + Q_LORA + KV_LORA
    o_ml = n_sg + n_mla
    o_g = o_ml + 4 * ML_WIDTH

    def cat(a):
        kr = a[:, o_kr:o_kr + MLA_ROPE]
        gates = a[:, o_g:o_g + 4 * ML_HEADS].reshape(-1, 2, 2, ML_HEADS).transpose(0, 2, 1, 3).reshape(-1, 4 * ML_HEADS)
        return jnp.concatenate([a[:, :o_kr], _pad_cols(kr, MLA_NOPE, HEAD_PAD), a[:, o_ml:o_g],
                                _pad_cols(gates, 0, LANE)], axis=1)

    wq = w["w_uq"][l].reshape(Q_LORA, MLA_HEADS, MLA_NOPE + MLA_ROPE)
    wq_main = _pad_cols(wq, 0, HEAD_PAD).reshape(Q_LORA, -1)
    wkv = w["w_ukv"][l].reshape(KV_LORA, MLA_HEADS, MLA_NOPE + MLA_VDIM)
    wk = _pad_cols(wkv[..., :MLA_NOPE], 0, HEAD_PAD).reshape(KV_LORA, -1)
    wv = _pad_cols(wkv[..., MLA_NOPE:], 0, HEAD_PAD).reshape(KV_LORA, -1)
    head = jnp.arange(ML_WIDTH) // ML_DIM
    wr = _pad_cols(w["w_router"][l], 0, LANE)
    wr_hi = wr.astype(BF16)
    return {
        "wcat": cat(wi).astype(BF16), "bcat": cat(bi),
        "q_norm_g": w["q_norm_g"][l][None, :], "kv_norm_g": w["kv_norm_g"][l][None, :],
        "wq": wq_main.astype(BF16),
        "wkv": jnp.concatenate([wk, wv], axis=1).astype(BF16),
        "sg_ln_g": w["sg_ln_g"][l][None, :], "sg_ln_b": w["sg_ln_b"][l][None, :],
        "sg_w": w["sg_w"][l].astype(BF16),
        "sg_bias": jnp.repeat(w["sg_b"][l].T, SG_GDIM, axis=1),
        "conv_w": jnp.pad(w["ml_conv_w"][l], ((0, SUBLANE - ML_CONV), (0, 0))), "conv_b": w["ml_conv_b"][l][None, :],
        "f_bias": _pad_cols(w["ml_f_bias"][l].reshape(1, 2 * ML_HEADS), 0, LANE),
        "ml_norm_g": w["ml_norm_g"][l][None, :],
        "head_avg": ((head[:, None] == head[None, :]).astype(F32) / ML_DIM).astype(BF16),
        "w_out": w["w_out"][l].astype(BF16),
        "ln1_g": w["ln1_g"][l][None, :], "ln1_b": w["ln1_b"][l][None, :],
        "w_router3": jnp.concatenate([jnp.concatenate([wr_hi, (wr - wr_hi.astype(F32)).astype(BF16)], axis=1),
                                      jnp.concatenate([wr_hi, jnp.zeros_like(wr_hi)], axis=1)], axis=0),
        "layer": l, "w_gate": w["w_gate"], "w_up": w["w_up"], "w_down": w["w_down"],
        "ln2_g": w["ln2_g"][l][None, :], "ln2_b": w["ln2_b"][l][None, :],
    }


def _rope_tables(T):
    rows = T // GRID_W
    row = jnp.repeat(jnp.arange(rows, dtype=F32), GRID_W)
    colv = jnp.tile(jnp.arange(GRID_W, dtype=F32), rows)
    inv = ROPE_BASE ** (-jnp.arange(ROPE_AXIS // 2, dtype=F32) * 2.0 / ROPE_AXIS)
    ang = jnp.concatenate([row[:, None] * inv, colv[:, None] * inv], axis=-1)
    half = ROPE_AXIS // 2
    spread = lambda t: jnp.concatenate([t[:, :half], t[:, :half], t[:, half:], t[:, half:]], axis=1)
    cosT = jnp.concatenate([jnp.ones((T, MLA_NOPE), F32), spread(jnp.cos(ang)), jnp.zeros((T, 32), F32)], axis=1)
    sign = jnp.tile(jnp.repeat(jnp.asarray([-1.0, 1.0], F32), half), 2)
    sinS = _pad_cols(spread(jnp.sin(ang)) * sign, MLA_NOPE, HEAD_PAD)
    return cosT, sinS


def _moe(x1, hm, aff_t, g2, p):
    cap = EC_CAPACITY * x1.shape[1] // N_EXPERTS
    pos, cnt = _route(aff_t, cap)
    ye = _experts(pos, cnt, aff_t, hm, p, cap)
    return _combine(pos, cnt, ye, x1, g2, p)


def _forward(x, c, ctx, c_ctx, w):
    B, T, D = x.shape
    Tc = ctx.shape[1]
    L = w["w_ada"].shape[0]
    rows = -(-(B + 1) // SUBLANE) * SUBLANE
    cc = jnp.zeros((rows, D), F32).at[:B].set(c).at[B].set(c_ctx)
    ada = _ada(cc, w["w_ada"], w["b_ada"])
    cos_l, sin_l = _rope_tables(T)
    cos_c = _pad_cols(jnp.ones((Tc, MLA_NOPE + MLA_ROPE), F32), 0, HEAD_PAD)
    sin_c = jnp.zeros((Tc, HEAD_PAD), F32)
    zero_c = jnp.zeros((B, 2, ML_HEADS * HEAD_PAD, HEAD_PAD), F32)
    zero_m = jnp.zeros((B, 2, 2 * ML_HEADS, min(ML_CHUNK, Tc)), F32)
    xl, xc = x, ctx
    for l in range(L):
        p = _prep_layer(l, w)
        mods_l = [ada[l, :B, i * D:(i + 1) * D][:, None, :] for i in range(6)]
        mods_c = [jnp.broadcast_to(ada[l, B, i * D:(i + 1) * D][None, None, :], (B, 1, D)) for i in range(6)]
        ya_c, q_c, k_c, v_c, qk_c, vo_c, g_c = _inproj(xc, mods_c[0], mods_c[1], cos_c, sin_c, p)
        ya_l, q_l, k_l, v_l, qk_l, vo_l, g_l = _inproj(xl, mods_l[0], mods_l[1], cos_l, sin_l, p)
        hf_c, hb_c, cs_c, ms_c = _mlstm(qk_c, vo_c, g_c, p, zero_c, zero_m)
        hf_l, hb_l, _, _ = _mlstm(qk_l, vo_l, g_l, p, cs_c, ms_c)
        yb_l = _attention(q_l, [(k_l, v_l), (k_c, v_c)])
        x1_l, hm_l, aff_l = _outproj(xl, mods_l[2], mods_l[3], mods_l[4], ya_l, yb_l, hf_l, hb_l, vo_l, p)
        xl = _moe(x1_l, hm_l, aff_l, mods_l[5], p)
        if l < L - 1:
            yb_c = _attention(q_c, [(k_c, v_c)])
            x1_c, hm_c, aff_c = _outproj(xc, mods_c[2], mods_c[3], mods_c[4], ya_c, yb_c, hf_c, hb_c, vo_c, p)
            xc = _moe(x1_c, hm_c, aff_c, mods_c[5], p)
    return xl


def kernel(x, c, ctx, c_ctx, w_ada, b_ada, w_in, b_in, sg_ln_g, sg_ln_b, sg_w, sg_b, q_norm_g, kv_norm_g, w_uq, w_ukv,
           ml_conv_w, ml_conv_b, ml_f_bias, ml_norm_g, w_out, ln1_g, ln1_b, w_router, w_gate, w_up, w_down, ln2_g, ln2_b):
    w = dict(w_ada=w_ada, b_ada=b_ada, w_in=w_in, b_in=b_in, sg_ln_g=sg_ln_g, sg_ln_b=sg_ln_b, sg_w=sg_w, sg_b=sg_b,
             q_norm_g=q_norm_g, kv_norm_g=kv_norm_g, w_uq=w_uq, w_ukv=w_ukv, ml_conv_w=ml_conv_w, ml_conv_b=ml_conv_b,
             ml_f_bias=ml_f_bias, ml_norm_g=ml_norm_g, w_out=w_out, ln1_g=ln1_g, ln1_b=ln1_b, w_router=w_router,
             w_gate=w_gate, w_up=w_up, w_down=w_down, ln2_g=ln2_g, ln2_b=ln2_b)
    return _forward(x, c, ctx, c_ctx, w)
```

```python
import functools

import jax
import jax.numpy as jnp
from jax import lax
from jax.experimental import pallas as pl
from jax.experimental.pallas import tpu as pltpu

F32 = jnp.float32
BF16 = jnp.bfloat16
HIGHEST = lax.Precision.HIGHEST

D_MODEL = 1024
DEPTH = 2
GRID_W = 64
SG_WIDTH = 256
SG_GROUPS = 4
SG_GDIM = 64
SG_CHUNK = 128
MLA_HEADS = 8
MLA_NOPE = 64
MLA_ROPE = 32
MLA_VDIM = 64
Q_LORA = 384
KV_LORA = 256
ROPE_AXIS = 16
ROPE_BASE = 10000.0
MLA_SCALE = (MLA_NOPE + MLA_ROPE) ** -0.5
ML_HEADS = 4
ML_DIM = 64
ML_WIDTH = 256
ML_CONV = 5
N_EXPERTS = 16
EXPERT_FF = 1024
EC_CAPACITY = 2
ALPHA = (2 * DEPTH) ** 0.25
LN_EPS = 1e-6
LOG2E = 1.4426950408889634

LANE = 128
SUBLANE = 8
BF16_ROWS = 2 * SUBLANE
VMEM_BYTES_V7X = 64 * 1024 * 1024
VMEM_LIMIT = VMEM_BYTES_V7X * 7 // 8

HEAD_PAD = LANE
QK_LANES = ML_HEADS * HEAD_PAD + ML_WIDTH
OFF_SG, OFF_CQ, OFF_CKV, OFF_KR, OFF_ML, OFF_G, N_CAT = 0, 512, 896, 1152, 1280, 2304, 2432

ADA_COLS = 1536
TOKEN_TILE = 1024
ATTN_QUERIES = 512
ATTN_HEADS = 2
ATTN_KEYS = 4096
ML_CHUNK = 256
ML_BLOCK = 1024
ROUTE_TILE = 256
EXPERT_ROWS = 512
GATHER_ROWS = 128
GATHER_TILES = 6
COMBINE_TILE = 512
COMBINE_WIN = 128


def _cparams(sem):
    return pltpu.CompilerParams(dimension_semantics=sem, vmem_limit_bytes=VMEM_LIMIT)


def _nt(a, b):
    return lax.dot_general(a, b, (((1,), (1,)), ((), ())), preferred_element_type=F32)


def _tn(a, b):
    return lax.dot_general(a, b, (((0,), (0,)), ((), ())), preferred_element_type=F32)


def _dot(a, b, precision=None):
    return jnp.dot(a, b, preferred_element_type=F32, precision=precision)


def _split(x):
    hi = x.astype(BF16)
    return hi, (x - hi.astype(F32)).astype(BF16)


def _dot2(x, w):
    hi, lo = _split(x)
    return _dot(jnp.concatenate([hi, lo], axis=1), jnp.concatenate([w, w], axis=0))


def _plain_norm(x):
    mu = jnp.mean(x, axis=-1, keepdims=True)
    xc = x - mu
    var = jnp.mean(xc * xc, axis=-1, keepdims=True)
    return xc * lax.rsqrt(var + LN_EPS)


def _rms(x):
    return x * lax.rsqrt(jnp.mean(x * x, axis=-1, keepdims=True) + LN_EPS)


def _silu(x):
    return x * (1.0 / (1.0 + jnp.exp(-x)))


def _gelu_tanh(x):
    return 0.5 * x * (1.0 + jnp.tanh(0.7978845608028654 * (x + 0.044715 * (x * x * x))))


def _log_sigmoid(x):
    return jnp.minimum(x, 0.0) - jnp.log(1.0 + jnp.exp(-jnp.abs(x)))


def _ada_kernel(c_ref, w_ref, b_ref, o_ref):
    o_ref[0] = _dot(_silu(c_ref[...]), w_ref[0], precision=HIGHEST) + b_ref[0]


def _ada(cc, w_ada, b_ada):
    L, D, N = w_ada.shape
    rows = cc.shape[0]
    tn = ADA_COLS
    return pl.pallas_call(
        _ada_kernel,
        grid=(L, N // tn),
        in_specs=[pl.BlockSpec((rows, D), lambda l, j: (0, 0)),
                  pl.BlockSpec((1, D, tn), lambda l, j: (l, 0, j)),
                  pl.BlockSpec((1, 1, tn), lambda l, j: (l, 0, j))],
        out_specs=pl.BlockSpec((1, rows, tn), lambda l, j: (l, 0, j)),
        out_shape=jax.ShapeDtypeStruct((L, rows, N), F32),
        compiler_params=_cparams(("parallel", "parallel")),
        name="ada",
    )(cc, w_ada, b_ada.reshape(L, 1, N))


def _inproj_kernel(x_ref, xp_ref, xn_ref, sh_ref, sc_ref, cos_ref, sin_ref, wcat_ref, bcat_ref, qg_ref, kvg_ref, wq_ref,
                   wkv_ref, sglg_ref, sglb_ref, sgw_ref, sgb_ref, cw_ref, cb_ref,
                   ya_ref, q_ref, k_ref, v_ref, qk_ref, vo_ref, g_ref):
    tm = x_ref.shape[1]
    modulated = lambda t: (_plain_norm(t) * (1.0 + sc_ref[0, 0, 0]) + sh_ref[0, 0, 0]).astype(BF16)
    z = _dot(modulated(x_ref[0]), wcat_ref[...]) + bcat_ref[...]
    i = pl.program_id(1)
    qk_cols = slice(OFF_ML, OFF_ML + 2 * ML_WIDTH)
    halo = _dot(modulated(jnp.concatenate([xp_ref[0], xn_ref[0]], axis=0)), wcat_ref[:, qk_cols]) + bcat_ref[:, qk_cols]
    prev = jnp.where(i > 0, halo[:SUBLANE], 0.0)
    nxt = jnp.where(i < pl.num_programs(1) - 1, halo[SUBLANE:], 0.0)
    big = jnp.concatenate([prev, z[:, qk_cols], nxt], axis=0)
    n_big = tm + 2 * SUBLANE
    pad = (ML_CONV - 1) // 2
    conv = cb_ref[...]
    for t in range(ML_CONV):
        s = t - pad
        shifted = big if s == 0 else pltpu.roll(big, (-s) % n_big, axis=0)
        conv = conv + shifted[SUBLANE:SUBLANE + tm] * cw_ref[t:t + 1, :]
    qkc = _silu(conv)
    hlane = lax.broadcasted_iota(jnp.int32, (1, HEAD_PAD), 1)
    for hd in range(ML_HEADS):
        q_blk = qkc[:, (hd // 2) * HEAD_PAD:(hd // 2 + 1) * HEAD_PAD]
        in_half = (hlane >= ML_DIM) if hd % 2 else (hlane < ML_DIM)
        qk_ref[0, :, hd * HEAD_PAD:(hd + 1) * HEAD_PAD] = jnp.where(in_half, q_blk, 0.0).astype(BF16)
    qk_ref[0, :, ML_HEADS * HEAD_PAD:] = (qkc[:, ML_WIDTH:] * (ML_DIM ** -0.5)).astype(BF16)
    zs = _gelu_tanh(z[:, OFF_SG:OFF_SG + 2 * SG_WIDTH])
    u = zs[:, :SG_WIDTH]
    vn = _plain_norm(zs[:, SG_WIDTH:]) * sglg_ref[...] + sglb_ref[...]
    group = jnp.right_shift(lax.broadcasted_iota(jnp.int32, (SG_CHUNK, SG_WIDTH), 1), SG_GDIM.bit_length() - 1)
    for ch in range(tm // SG_CHUNK):
        rows = slice(ch * SG_CHUNK, (ch + 1) * SG_CHUNK)
        vc = vn[rows].astype(BF16)
        mixed = sgb_ref[...]
        for g in range(SG_GROUPS):
            mixed = mixed + jnp.where(group == g, _dot(sgw_ref[g], vc), 0.0)
        ya_ref[0, rows, :] = (u[rows] * mixed).astype(BF16)
    cosT = cos_ref[...]
    sinS = sin_ref[...]
    lane = lax.broadcasted_iota(jnp.int32, (1, HEAD_PAD), 1)
    first = jnp.bitwise_and(lane, ROPE_AXIS // 2) == 0

    def rope(t):
        partner = jnp.where(first, pltpu.roll(t, HEAD_PAD - ROPE_AXIS // 2, axis=1), pltpu.roll(t, ROPE_AXIS // 2, axis=1))
        return t * cosT + partner * sinS

    cqn = (_rms(z[:, OFF_CQ:OFF_CQ + Q_LORA]) * qg_ref[...]).astype(BF16)
    qq = _dot(cqn, wq_ref[...])
    ckvn = (_rms(z[:, OFF_CKV:OFF_CKV + KV_LORA]) * kvg_ref[...]).astype(BF16)
    kv = _dot(ckvn, wkv_ref[...])
    kr = rope(z[:, OFF_KR:OFF_KR + HEAD_PAD])
    one_lane = jnp.where(lane == MLA_VDIM, 1.0, 0.0)
    hw = MLA_HEADS * HEAD_PAD
    for hd in range(MLA_HEADS):
        cols = slice(hd * HEAD_PAD, (hd + 1) * HEAD_PAD)
        rcols = slice(hw + hd * HEAD_PAD, hw + (hd + 1) * HEAD_PAD)
        q_ref[0, hd] = (rope(qq[:, cols]) * (MLA_SCALE * LOG2E)).astype(BF16)
        k_ref[0, hd] = (kv[:, cols] + kr).astype(BF16)
        v_ref[0, hd] = (kv[:, rcols] + one_lane).astype(BF16)
    vo_ref[0] = z[:, OFF_ML + 2 * ML_WIDTH:OFF_ML + 4 * ML_WIDTH]
    g_ref[0] = z[:, OFF_G:OFF_G + LANE]


def _mod_spec(mod, k):
    _, layer, row = mod
    return pl.BlockSpec((1, 1, 1, 1, D_MODEL), lambda b, *_: (layer, b if row is None else row, k, 0, 0))


def _inproj(x, mod, cosT, sinT, p):
    B, T, D = x.shape
    tm = min(TOKEN_TILE, T)
    hb = tm // SUBLANE
    const = lambda shape: pl.BlockSpec(shape, lambda b, i: (0,) * len(shape))
    heads = (B, MLA_HEADS, T, HEAD_PAD)
    return pl.pallas_call(
        _inproj_kernel,
        grid=(B, T // tm),
        in_specs=[pl.BlockSpec((1, tm, D), lambda b, i: (b, i, 0)),
                  pl.BlockSpec((1, SUBLANE, D), lambda b, i: (b, jnp.maximum(i * hb - 1, 0), 0)),
                  pl.BlockSpec((1, SUBLANE, D), lambda b, i: (b, jnp.minimum((i + 1) * hb, T // SUBLANE - 1), 0)),
                  _mod_spec(mod, 0), _mod_spec(mod, 1),
                  pl.BlockSpec((tm, HEAD_PAD), lambda b, i: (i, 0)),
                  pl.BlockSpec((tm, HEAD_PAD), lambda b, i: (i, 0)),
                  const((D, N_CAT)), const((1, N_CAT)), const((1, Q_LORA)), const((1, KV_LORA)),
                  const((Q_LORA, MLA_HEADS * HEAD_PAD)), const((KV_LORA, 2 * MLA_HEADS * HEAD_PAD)),
                  const((1, SG_WIDTH)), const((1, SG_WIDTH)), const((SG_GROUPS, SG_CHUNK, SG_CHUNK)),
                  const((SG_CHUNK, SG_WIDTH)), const((SUBLANE, 2 * ML_WIDTH)), const((1, 2 * ML_WIDTH))],
        out_specs=[pl.BlockSpec((1, tm, SG_WIDTH), lambda b, i: (b, i, 0)),
                   pl.BlockSpec((1, MLA_HEADS, tm, HEAD_PAD), lambda b, i: (b, 0, i, 0)),
                   pl.BlockSpec((1, MLA_HEADS, tm, HEAD_PAD), lambda b, i: (b, 0, i, 0)),
                   pl.BlockSpec((1, MLA_HEADS, tm, HEAD_PAD), lambda b, i: (b, 0, i, 0)),
                   pl.BlockSpec((1, tm, QK_LANES), lambda b, i: (b, i, 0)),
                   pl.BlockSpec((1, tm, 2 * ML_WIDTH), lambda b, i: (b, i, 0)),
                   pl.BlockSpec((1, tm, LANE), lambda b, i: (b, i, 0))],
        out_shape=[jax.ShapeDtypeStruct((B, T, SG_WIDTH), BF16),
                   jax.ShapeDtypeStruct(heads, BF16), jax.ShapeDtypeStruct(heads, BF16),
                   jax.ShapeDtypeStruct(heads, BF16),
                   jax.ShapeDtypeStruct((B, T, QK_LANES), BF16),
                   jax.ShapeDtypeStruct((B, T, 2 * ML_WIDTH), F32),
                   jax.ShapeDtypeStruct((B, T, LANE), F32)],
        compiler_params=_cparams(("parallel", "parallel")),
        name="inproj",
    )(x, x, x, mod[0], mod[0], cosT, sinT, p["wcat"], p["bcat"], p["q_norm_g"], p["kv_norm_g"], p["wq"], p["wkv"],
      p["sg_ln_g"], p["sg_ln_b"], p["sg_w"], p["sg_bias"], p["conv_w"], p["conv_b"])


def _attn_kernel(*refs, n_src, tk):
    q_ref = refs[0]
    kv_refs = refs[1:1 + 2 * n_src]
    o_ref = refs[1 + 2 * n_src]
    nh, tq = q_ref.shape[1], q_ref.shape[2]
    qs = [q_ref[0, hh] for hh in range(nh)]
    ms = [jnp.full((tq, 1), -jnp.inf, F32) for _ in range(nh)]
    accs = [jnp.zeros((tq, HEAD_PAD), F32) for _ in range(nh)]
    for s in range(n_src):
        k_ref, v_ref = kv_refs[2 * s], kv_refs[2 * s + 1]
        ck = min(tk, k_ref.shape[2])
        for c in range(k_ref.shape[2] // ck):
            rows = slice(c * ck, (c + 1) * ck)
            for hh in range(nh):
                sc = _nt(qs[hh], k_ref[0, hh, rows, :])
                m_new = jnp.maximum(ms[hh], jnp.max(sc, axis=1, keepdims=True))
                pr = jnp.exp2(sc - m_new)
                accs[hh] = accs[hh] * jnp.exp2(ms[hh] - m_new) + _dot(pr.astype(BF16), v_ref[0, hh, rows, :])
                ms[hh] = m_new
    outs = [accs[hh] / accs[hh][:, MLA_VDIM:MLA_VDIM + 1] for hh in range(nh)]
    lane = lax.broadcasted_iota(jnp.int32, (tq, HEAD_PAD), 1)
    for pair in range(nh // 2):
        both = jnp.where(lane < MLA_VDIM, outs[2 * pair], pltpu.roll(outs[2 * pair + 1], MLA_VDIM, axis=1))
        o_ref[0, :, pair * HEAD_PAD:(pair + 1) * HEAD_PAD] = both.astype(BF16)


def _attention(q, kvs):
    B, H, T, _ = q.shape
    tq, tk, nh = min(ATTN_QUERIES, T), ATTN_KEYS, ATTN_HEADS
    in_specs = [pl.BlockSpec((1, nh, tq, HEAD_PAD), lambda b, hp, i: (b, hp, i, 0))]
    args = [q]
    for k, v in kvs:
        K = k.shape[2]
        in_specs += [pl.BlockSpec((1, nh, K, HEAD_PAD), lambda b, hp, i: (b, hp, 0, 0))] * 2
        args += [k, v]
    return pl.pallas_call(
        functools.partial(_attn_kernel, n_src=len(kvs), tk=tk),
        grid=(B, H // nh, T // tq),
        in_specs=in_specs,
        out_specs=pl.BlockSpec((1, tq, nh * MLA_VDIM), lambda b, hp, i: (b, i, hp)),
        out_shape=jax.ShapeDtypeStruct((B, T, H * MLA_VDIM), BF16),
        compiler_params=_cparams(("parallel", "parallel", "parallel")),
        name="attention",
    )(*args)


def _mlstm_direction(d, qk, vv, gl, fb, c_prev, m_prev):
    L = qk.shape[0]
    lf = _log_sigmoid(pltpu.roll(gl, LANE - 2 * ML_HEADS, axis=1) + fb)
    ks = lax.broadcasted_iota(jnp.int32, (L, L), 0)
    qt = lax.broadcasted_iota(jnp.int32, (L, L), 1)
    visible = (ks >= qt) if d else (ks <= qt)
    tri = jnp.where((qt >= ks) if d else (qt <= ks), 1.0, 0.0).astype(BF16)
    lf_hi, lf_lo = _split(lf)
    bc = _dot(tri, lf_hi) + _dot(tri, lf_lo)
    r_cols = gl - bc
    b8 = bc.T[0:2 * ML_HEADS]
    li8 = gl.T[0:2 * ML_HEADS]
    b_end = b8[:, 0:1] if d else b8[:, L - 1:L]
    gw = b_end - b8 + li8
    m_loc = jnp.max(gw, axis=1, keepdims=True)
    w8 = jnp.exp(gw - m_loc)
    m_next = jnp.maximum(b_end + m_prev, m_loc)
    a_sc = jnp.exp(b_end + m_prev - m_next)
    s_sc = jnp.exp(m_loc - m_next)
    a8 = b8 + m_prev

    srow = lax.broadcasted_iota(jnp.int32, (HEAD_PAD, L), 0)
    h_blocks, c_next = [], []
    for blk in range(ML_HEADS // 2):
        km = qk[:, (ML_HEADS + blk) * HEAD_PAD:(ML_HEADS + blk + 1) * HEAD_PAD]
        v_t = vv[:, blk * HEAD_PAD:(blk + 1) * HEAD_PAD].T
        hh = []
        for half in range(2):
            hd = 2 * blk + half
            g = d * ML_HEADS + hd
            in_half = (srow >= ML_DIM) if half else (srow < ML_DIM)
            den_row = 0 if half else ML_DIM
            qm = qk[:, hd * HEAD_PAD:(hd + 1) * HEAD_PAD]
            vaug_t = jnp.where(in_half, v_t, jnp.where(srow == den_row, 1.0, 0.0))
            dm = jnp.where(visible, r_cols[:, g:g + 1] + b8[g:g + 1, :], -jnp.inf)
            a_row = a8[g:g + 1, :]
            m_t = jnp.maximum(a_row, jnp.max(dm, axis=0, keepdims=True))
            pm = jnp.exp(dm - m_t) * _nt(km, qm)
            c0 = c_prev[hd]
            nd = (jnp.exp(a_row - m_t) * _nt(c0.astype(BF16), qm)
                  + _dot(vaug_t.astype(BF16), pm.astype(BF16)))
            den = nd[den_row:den_row + 1, :]
            hh.append(nd / jnp.maximum(jnp.abs(den), jnp.exp(-m_t)))
            c_loc = _dot((vaug_t * w8[g:g + 1, :]).astype(BF16), km)
            c_next.append(a_sc[g:g + 1, 0:HEAD_PAD] * c0 + s_sc[g:g + 1, 0:HEAD_PAD] * c_loc)
        h_blocks.append(jnp.where(srow < ML_DIM, hh[0], hh[1]).T)
    grow = lax.broadcasted_iota(jnp.int32, (2 * ML_HEADS, L), 0)
    mine = (grow >= d * ML_HEADS) & (grow < (d + 1) * ML_HEADS)
    return h_blocks, c_next, jnp.where(mine, m_next, m_prev)


def _mlstm_kernel(qkf_ref, vf_ref, gf_ref, qkb_ref, vb_ref, gb_ref, fb_ref, ci_ref, mi_ref,
                  hf_ref, hb_ref, cs_ref, ms_ref):
    @pl.when(pl.program_id(1) == 0)
    def _():
        cs_ref[...] = ci_ref[...]
        ms_ref[...] = mi_ref[...]

    rows = lambda hd: slice(hd * HEAD_PAD, (hd + 1) * HEAD_PAD)
    ins = ((qkf_ref, vf_ref, gf_ref), (qkb_ref, vb_ref, gb_ref))
    outs = (hf_ref, hb_ref)
    L = ms_ref.shape[3]
    n_sub = qkf_ref.shape[1] // L
    states = [([cs_ref[0, d, rows(hd), :] for hd in range(ML_HEADS)], ms_ref[0, d]) for d in range(2)]
    for d in range(2):
        qk_ref, v_ref, g_ref = ins[d]
        c_state, m_state = states[d]
        for s in (range(n_sub - 1, -1, -1) if d else range(n_sub)):
            tok = slice(s * L, (s + 1) * L)
            h_blocks, c_state, m_state = _mlstm_direction(d, qk_ref[0, tok, :], v_ref[0, tok, :], g_ref[0, tok, :],
                                                          fb_ref[...], c_state, m_state)
            for blk, hb in enumerate(h_blocks):
                outs[d][0, tok, blk * HEAD_PAD:(blk + 1) * HEAD_PAD] = hb
        states[d] = (c_state, m_state)
    for d in range(2):
        for hd in range(ML_HEADS):
            cs_ref[0, d, rows(hd), :] = states[d][0][hd]
        ms_ref[0, d] = states[d][1]


def _mlstm(qk, vo, gates, p, c_init, m_init):
    B, T, _ = vo.shape
    chunk = min(ML_CHUNK, T)
    L = min(ML_BLOCK, T)
    nc = T // L
    const = lambda shape: pl.BlockSpec(shape, lambda b, j: (0,) * len(shape))
    state = pl.BlockSpec((1, 2, ML_HEADS * HEAD_PAD, HEAD_PAD), lambda b, j: (b, 0, 0, 0))
    mstate = pl.BlockSpec((1, 2, 2 * ML_HEADS, chunk), lambda b, j: (b, 0, 0, 0))
    fwd = lambda b, j: (b, j, 0)
    bwd = lambda b, j: (b, nc - 1 - j, 0)
    return pl.pallas_call(
        _mlstm_kernel,
        grid=(B, nc),
        in_specs=[pl.BlockSpec((1, L, QK_LANES), fwd),
                  pl.BlockSpec((1, L, ML_WIDTH), fwd),
                  pl.BlockSpec((1, L, LANE), fwd),
                  pl.BlockSpec((1, L, QK_LANES), bwd),
                  pl.BlockSpec((1, L, ML_WIDTH), bwd),
                  pl.BlockSpec((1, L, LANE), bwd),
                  const((1, LANE)), state, mstate],
        out_specs=[pl.BlockSpec((1, L, ML_WIDTH), fwd), pl.BlockSpec((1, L, ML_WIDTH), bwd), state, mstate],
        out_shape=[jax.ShapeDtypeStruct((B, T, ML_WIDTH), F32), jax.ShapeDtypeStruct((B, T, ML_WIDTH), F32),
                   jax.ShapeDtypeStruct((B, 2, ML_HEADS * HEAD_PAD, HEAD_PAD), F32),
                   jax.ShapeDtypeStruct((B, 2, 2 * ML_HEADS, chunk), F32)],
        compiler_params=_cparams(("parallel", "arbitrary")),
        name="mlstm",
    )(qk, vo, gates, qk, vo, gates, p["f_bias"], c_init, m_init)


def _outproj_kernel(x_ref, g1_ref, sh2_ref, sc2_ref, ya_ref, yb_ref, h0_ref, h1_ref, o_ref, mg_ref, avg_ref,
                    wo_ref, l1g_ref, l1b_ref, wr_ref, x1_ref, hm_ref, aff_ref):
    hs = h0_ref[0] + h1_ref[0]
    mu = _dot2(hs, avg_ref[...])
    hc = hs - mu
    var = _dot2(hc * hc, avg_ref[...])
    yc = (1.0 / (1.0 + jnp.exp(-o_ref[0]))) * (hc * lax.rsqrt(var + LN_EPS) * mg_ref[...])
    y = _dot(jnp.concatenate([ya_ref[0], yb_ref[0], yc.astype(BF16)], axis=1), wo_ref[...])
    x1 = _plain_norm(ALPHA * x_ref[0] + g1_ref[0, 0, 0] * y) * l1g_ref[...] + l1b_ref[...]
    x1_ref[0] = x1
    hm = _plain_norm(x1) * (1.0 + sc2_ref[0, 0, 0]) + sh2_ref[0, 0, 0]
    hm_ref[0] = hm.astype(BF16)
    hi, lo = _split(hm)
    parts = _dot(jnp.concatenate([hi, lo], axis=1), wr_ref[...])
    logits = (parts[:, :LANE] + parts[:, LANE:]).T[:N_EXPERTS, :]
    ex = jnp.exp(logits - jnp.max(logits, axis=0, keepdims=True))
    aff_ref[0] = ex / jnp.sum(ex, axis=0, keepdims=True)


def _outproj(x, mod, ya, yb, h_fwd, h_bwd, vo, p):
    B, T, D = x.shape
    tm = min(TOKEN_TILE, T)
    const = lambda shape: pl.BlockSpec(shape, lambda b, i: (0,) * len(shape))
    return pl.pallas_call(
        _outproj_kernel,
        grid=(B, T // tm),
        in_specs=[pl.BlockSpec((1, tm, D), lambda b, i: (b, i, 0)),
                  _mod_spec(mod, 2), _mod_spec(mod, 3), _mod_spec(mod, 4),
                  pl.BlockSpec((1, tm, SG_WIDTH), lambda b, i: (b, i, 0)),
                  pl.BlockSpec((1, tm, MLA_HEADS * MLA_VDIM), lambda b, i: (b, i, 0)),
                  pl.BlockSpec((1, tm, ML_WIDTH), lambda b, i: (b, i, 0)),
                  pl.BlockSpec((1, tm, ML_WIDTH), lambda b, i: (b, i, 0)),
                  pl.BlockSpec((1, tm, ML_WIDTH), lambda b, i: (b, i, 1)),
                  const((1, ML_WIDTH)), const((ML_WIDTH, ML_WIDTH)), const((D, D)), const((1, D)), const((1, D)),
                  const((2 * D, 2 * LANE))],
        out_specs=[pl.BlockSpec((1, tm, D), lambda b, i: (b, i, 0)),
                   pl.BlockSpec((1, tm, D), lambda b, i: (b, i, 0)),
                   pl.BlockSpec((1, N_EXPERTS, tm), lambda b, i: (b, 0, i))],
        out_shape=[jax.ShapeDtypeStruct((B, T, D), F32), jax.ShapeDtypeStruct((B, T, D), BF16),
                   jax.ShapeDtypeStruct((B, N_EXPERTS, T), F32)],
        compiler_params=_cparams(("parallel", "parallel")),
        name="outproj",
    )(x, mod[0], mod[0], mod[0], ya, yb, h_fwd, h_bwd, vo, p["ml_norm_g"], p["head_avg"], p["w_out"], p["ln1_g"], p["ln1_b"],
      p["w_router3"])


def _route_kernel(aff_ref, pos_ref, cnt_ref, *, cap, tw):
    aff = aff_ref[0]
    E, T = aff.shape
    count = lambda m: jnp.sum(jnp.where(m, 1.0, 0.0), axis=1, keepdims=True)
    as_float = lambda b: pltpu.bitcast(b, F32)
    thr = jnp.zeros((E, 1), jnp.int32)
    for bit in range(30, -1, -1):
        cand = thr | (1 << bit)
        thr = jnp.where(count(aff >= as_float(cand)) >= cap, cand, thr)
    above = aff >= as_float(thr + 1)
    tied = (aff >= as_float(thr)) & jnp.logical_not(above)
    need = cap - count(above)
    idx = lax.broadcasted_iota(jnp.int32, (E, T), 1)

    def cut_search():
        cut = jnp.zeros((E, 1), jnp.int32)
        for bit in range(T.bit_length() - 1, -1, -1):
            cand = cut | (1 << bit)
            ok = (cand <= T) & (count(tied & (idx < cand)) <= need)
            cut = jnp.where(ok, cand, cut)
        return cut

    surplus = jnp.max(jnp.where(count(tied) > need, 1.0, 0.0)) > 0.0
    cut = lax.cond(surplus, cut_search, lambda: jnp.full((E, 1), T, jnp.int32))
    sel = above | (tied & (idx < cut))
    r = lax.broadcasted_iota(jnp.int32, (tw, tw), 0)
    cidx = lax.broadcasted_iota(jnp.int32, (tw, tw), 1)
    before = jnp.where(r < cidx, 1.0, 0.0).astype(BF16)
    offset = jnp.zeros((E, 1), F32)
    lane = lax.broadcasted_iota(jnp.int32, (E, LANE), 1)
    table = jnp.zeros((E, LANE), F32)
    for blk in range(T // tw):
        cols = slice(blk * tw, (blk + 1) * tw)
        sb = jnp.where(sel[:, cols], 1.0, 0.0)
        slot = _dot(sb.astype(BF16), before) + offset
        pos_ref[0, :, cols] = jnp.where(sel[:, cols], slot, -1.0).astype(jnp.int32)
        offset = offset + jnp.sum(sb, axis=1, keepdims=True)
        table = jnp.where(lane == blk + 1, offset, table)
    cnt_ref[0] = table.astype(jnp.int32)


def _route(aff_t, cap):
    B, E, T = aff_t.shape
    nt = T // ROUTE_TILE
    pos, cnt = pl.pallas_call(
        functools.partial(_route_kernel, cap=cap, tw=ROUTE_TILE),
        grid=(B,),
        in_specs=[pl.BlockSpec((1, E, T), lambda b: (b, 0, 0))],
        out_specs=[pl.BlockSpec((1, E, T), lambda b: (b, 0, 0)), pl.BlockSpec((1, E, LANE), lambda b: (b, 0, 0))],
        out_shape=[jax.ShapeDtypeStruct((B, E, T), jnp.int32), jax.ShapeDtypeStruct((B, E, LANE), jnp.int32)],
        compiler_params=_cparams(("parallel",)),
        name="route",
    )(aff_t)
    return pos, cnt[:, :, :nt + 1].reshape(-1)


def _expert_kernel(cnt_ref, pos_ref, aff_ref, h_ref, wg_ref, wu_ref, wd_ref, y_ref, xe_ref, gate_ref,
                   wgb_ref, wub_ref, wdb_ref, *, n_win):
    nb, cap = y_ref.shape[0], y_ref.shape[2]
    nt = pos_ref.shape[2]
    rb = min(GATHER_ROWS, cap)
    e, j = pl.program_id(0), pl.program_id(1)

    @pl.when(j == 0)
    def _():
        wgb_ref[...] = wg_ref[0, 0].astype(BF16)
        wub_ref[...] = wu_ref[0, 0].astype(BF16)
        wdb_ref[...] = wd_ref[0, 0].astype(BF16)

    slot_iota = lax.broadcasted_iota(jnp.int32, (rb, ROUTE_TILE), 0)

    def gather(bb, i, n, s0):
        hits = [pos_ref[bb, 0, pl.ds(i + w, 1), :] == slot_iota + s0 for w in range(n)]
        onehot = jnp.concatenate([jnp.where(h, 1.0, 0.0).astype(BF16) for h in hits], axis=1)
        rows = pl.ds(pl.multiple_of(i * ROUTE_TILE, ROUTE_TILE), n * ROUTE_TILE)
        xe = _dot(onehot, h_ref[bb, rows, :])
        gate = sum(jnp.sum(jnp.where(h, aff_ref[bb, 0, pl.ds(i + w, 1), :], 0.0), axis=1, keepdims=True)
                   for w, h in enumerate(hits))
        return xe, gate

    for bb in range(nb):
        base = ((j * nb + bb) * N_EXPERTS + e) * (nt + 1)
        for r in range(cap // rb):
            s0 = r * rb
            first = sum((cnt_ref[base + i + 1] <= s0).astype(jnp.int32) for i in range(nt))
            end = sum((cnt_ref[base + i] < s0 + rb).astype(jnp.int32) for i in range(nt))
            start = jnp.minimum(first, nt - n_win)
            rows = slice(bb * cap + s0, bb * cap + s0 + rb)
            xe, gate = gather(bb, start, n_win, s0)
            xe_ref[rows, :] = xe.astype(BF16)
            gate_ref[rows, :] = gate

            def overflow(i, carry, bb=bb, s0=s0, rows=rows):
                xe, gate = gather(bb, i, 1, s0)
                xe_ref[rows, :] = (xe_ref[rows, :].astype(F32) + xe).astype(BF16)
                gate_ref[rows, :] += gate
                return carry

            lax.fori_loop(start + n_win, end, overflow, 0)
    xb = xe_ref[...]
    hid = _silu(_dot(xb, wgb_ref[...])) * _dot(xb, wub_ref[...])
    y = (_dot(hid.astype(BF16), wdb_ref[...]) * gate_ref[...]).astype(BF16)
    for bb in range(nb):
        y_ref[bb, 0] = y[bb * cap:(bb + 1) * cap]


def _experts(pos, cnt, aff_t, hm, p, cap):
    B, T, D = hm.shape
    E = N_EXPERTS
    nt = T // ROUTE_TILE
    layer = p["layer"]
    nb = max(1, min(B, EXPERT_ROWS // cap))
    row = pl.BlockSpec((nb, 1, nt, ROUTE_TILE), lambda e, j, c: (j, e, 0, 0))
    wspec = lambda shape: pl.BlockSpec((1, 1) + shape, lambda e, j, c: (layer, e, 0, 0))
    return pl.pallas_call(
        functools.partial(_expert_kernel, n_win=min(GATHER_TILES, nt)),
        grid_spec=pltpu.PrefetchScalarGridSpec(
            num_scalar_prefetch=1,
            grid=(E, B // nb),
            in_specs=[row, row,
                      pl.BlockSpec((nb, T, D), lambda e, j, c: (j, 0, 0)),
                      wspec((D, EXPERT_FF)), wspec((D, EXPERT_FF)), wspec((EXPERT_FF, D))],
            out_specs=pl.BlockSpec((nb, 1, cap, D), lambda e, j, c: (j, e, 0, 0)),
            scratch_shapes=[pltpu.VMEM((nb * cap, D), BF16), pltpu.VMEM((nb * cap, 1), F32),
                            pltpu.VMEM((D, EXPERT_FF), BF16), pltpu.VMEM((D, EXPERT_FF), BF16),
                            pltpu.VMEM((EXPERT_FF, D), BF16)]),
        out_shape=jax.ShapeDtypeStruct((B, E, cap, D), BF16),
        compiler_params=_cparams(("parallel", "arbitrary")),
        name="experts",
    )(cnt, pos.reshape(B, E, nt, ROUTE_TILE), aff_t.reshape(B, E, nt, ROUTE_TILE), hm,
      p["w_gate"], p["w_up"], p["w_down"])


def _combine_kernel(cnt_ref, pos_ref, y_ref, x_ref, g2_ref, lg_ref, lb_ref, o_ref, acc_ref, *, nt):
    E, cap = y_ref.shape[1], y_ref.shape[2]
    tt = x_ref.shape[1]
    win = min(COMBINE_WIN, cap)
    per = tt // ROUTE_TILE
    b, i = pl.program_id(0), pl.program_id(1)
    slot_iota = lax.broadcasted_iota(jnp.int32, (win, tt), 0)

    def window(e, w):
        start = pl.multiple_of(jnp.minimum(w, cap - win), BF16_ROWS)
        slot = slot_iota + start
        hit = (pos_ref[0, e:e + 1, :] == slot) & (slot >= w)
        return jnp.where(hit, 1.0, 0.0).astype(BF16), y_ref[0, e, pl.ds(start, win), :]

    onehots, windows, ranges = [], [], []
    for e in range(E):
        base = (b * E + e) * (nt + 1) + i * per
        lo, hi = cnt_ref[base], cnt_ref[base + per]
        w0 = jnp.bitwise_and(lo, -BF16_ROWS)
        onehot, rows = window(e, w0)
        windows.append(rows)
        onehots.append(onehot)
        ranges.append((w0, hi))
    acc_ref[...] = _tn(jnp.concatenate(onehots, axis=0), jnp.concatenate(windows, axis=0))
    overflow = functools.reduce(jnp.logical_or, [hi > w0 + win for w0, hi in ranges])

    @pl.when(overflow)
    def _():
        for e in range(E):
            w0, hi = ranges[e]

            def more(w, e=e):
                onehot, rows = window(e, w)
                acc_ref[...] += _tn(onehot, rows)
                return w + win

            lax.while_loop(lambda w, hi=hi: w < hi, more, w0 + win)

    o_ref[0] = _plain_norm(ALPHA * x_ref[0] + g2_ref[0, 0, 0] * acc_ref[...]) * lg_ref[...] + lb_ref[...]


def _combine(pos, cnt, ye, x1, mod, p):
    B, T, D = x1.shape
    tt = min(COMBINE_TILE, T)
    E, cap = ye.shape[1], ye.shape[2]
    const = lambda shape: pl.BlockSpec(shape, lambda b, i, c: (0,) * len(shape))
    return pl.pallas_call(
        functools.partial(_combine_kernel, nt=T // ROUTE_TILE),
        grid_spec=pltpu.PrefetchScalarGridSpec(
            num_scalar_prefetch=1,
            grid=(B, T // tt),
            in_specs=[pl.BlockSpec((1, E, tt), lambda b, i, c: (b, 0, i)),
                      pl.BlockSpec((1, E, cap, D), lambda b, i, c: (b, 0, 0, 0)),
                      pl.BlockSpec((1, tt, D), lambda b, i, c: (b, i, 0)),
                      _mod_spec(mod, 5), const((1, D)), const((1, D))],
            out_specs=pl.BlockSpec((1, tt, D), lambda b, i, c: (b, i, 0)),
            scratch_shapes=[pltpu.VMEM((tt, D), F32)]),
        out_shape=jax.ShapeDtypeStruct((B, T, D), F32),
        compiler_params=_cparams(("parallel", "arbitrary")),
        name="combine",
    )(cnt, pos, ye, x1, mod[0], p["ln2_g"], p["ln2_b"])


def _pad_cols(a, before, total):
    pads = [(0, 0)] * (a.ndim - 1) + [(before, total - before - a.shape[-1])]
    return jnp.pad(a, pads)


def _prep_layer(l, w):
    D = D_MODEL
    wi, bi = w["w_in"][l], w["b_in"][l][None, :]
    n_sg, n_mla = 2 * SG_WIDTH, Q_LORA + KV_LORA + MLA_ROPE
    o_kr = n_sg + Q_LORA + KV_LORA
    o_ml = n_sg + n_mla
    o_g = o_ml + 4 * ML_WIDTH

    def cat(a):
        kr = a[:, o_kr:o_kr + MLA_ROPE]
        gates = a[:, o_g:o_g + 4 * ML_HEADS].reshape(-1, 2, 2, ML_HEADS).transpose(0, 2, 1, 3).reshape(-1, 4 * ML_HEADS)
        return jnp.concatenate([a[:, :o_kr], _pad_cols(kr, MLA_NOPE, HEAD_PAD), a[:, o_ml:o_g],
                                _pad_cols(gates, 0, LANE)], axis=1)

    wq = w["w_uq"][l].reshape(Q_LORA, MLA_HEADS, MLA_NOPE + MLA_ROPE)
    wq_main = _pad_cols(wq, 0, HEAD_PAD).reshape(Q_LORA, -1)
    wkv = w["w_ukv"][l].reshape(KV_LORA, MLA_HEADS, MLA_NOPE + MLA_VDIM)
    wk = _pad_cols(wkv[..., :MLA_NOPE], 0, HEAD_PAD).reshape(KV_LORA, -1)
    wv = _pad_cols(wkv[..., MLA_NOPE:], 0, HEAD_PAD).reshape(KV_LORA, -1)
    head = jnp.arange(ML_WIDTH) // ML_DIM
    wr = _pad_cols(w["w_router"][l], 0, LANE)
    wr_hi = wr.astype(BF16)
    return {
        "wcat": cat(wi).astype(BF16), "bcat": cat(bi),
        "q_norm_g": w["q_norm_g"][l][None, :], "kv_norm_g": w["kv_norm_g"][l][None, :],
        "wq": wq_main.astype(BF16),
        "wkv": jnp.concatenate([wk, wv], axis=1).astype(BF16),
        "sg_ln_g": w["sg_ln_g"][l][None, :], "sg_ln_b": w["sg_ln_b"][l][None, :],
        "sg_w": w["sg_w"][l].astype(BF16),
        "sg_bias": jnp.repeat(w["sg_b"][l].T, SG_GDIM, axis=1),
        "conv_w": jnp.pad(w["ml_conv_w"][l], ((0, SUBLANE - ML_CONV), (0, 0))), "conv_b": w["ml_conv_b"][l][None, :],
        "f_bias": _pad_cols(w["ml_f_bias"][l].reshape(1, 2 * ML_HEADS), 0, LANE),
        "ml_norm_g": w["ml_norm_g"][l][None, :],
        "head_avg": ((head[:, None] == head[None, :]).astype(F32) / ML_DIM).astype(BF16),
        "w_out": w["w_out"][l].astype(BF16),
        "ln1_g": w["ln1_g"][l][None, :], "ln1_b": w["ln1_b"][l][None, :],
        "w_router3": jnp.concatenate([jnp.concatenate([wr_hi, (wr - wr_hi.astype(F32)).astype(BF16)], axis=1),
                                      jnp.concatenate([wr_hi, jnp.zeros_like(wr_hi)], axis=1)], axis=0),
        "layer": l, "w_gate": w["w_gate"], "w_up": w["w_up"], "w_down": w["w_down"],
        "ln2_g": w["ln2_g"][l][None, :], "ln2_b": w["ln2_b"][l][None, :],
    }


def _rope_tables(T):
    rows = T // GRID_W
    row = jnp.repeat(jnp.arange(rows, dtype=F32), GRID_W)
    colv = jnp.tile(jnp.arange(GRID_W, dtype=F32), rows)
    inv = ROPE_BASE ** (-jnp.arange(ROPE_AXIS // 2, dtype=F32) * 2.0 / ROPE_AXIS)
    ang = jnp.concatenate([row[:, None] * inv, colv[:, None] * inv], axis=-1)
    half = ROPE_AXIS // 2
    spread = lambda t: jnp.concatenate([t[:, :half], t[:, :half], t[:, half:], t[:, half:]], axis=1)
    cosT = jnp.concatenate([jnp.ones((T, MLA_NOPE), F32), spread(jnp.cos(ang)), jnp.zeros((T, 32), F32)], axis=1)
    sign = jnp.tile(jnp.repeat(jnp.asarray([-1.0, 1.0], F32), half), 2)
    sinS = _pad_cols(spread(jnp.sin(ang)) * sign, MLA_NOPE, HEAD_PAD)
    return cosT, sinS


def _moe(x1, hm, aff_t, mod, p):
    cap = EC_CAPACITY * x1.shape[1] // N_EXPERTS
    pos, cnt = _route(aff_t, cap)
    ye = _experts(pos, cnt, aff_t, hm, p, cap)
    return _combine(pos, cnt, ye, x1, mod, p)


def _forward(x, c, ctx, c_ctx, w):
    B, T, D = x.shape
    Tc = ctx.shape[1]
    L = w["w_ada"].shape[0]
    rows = -(-(B + 1) // SUBLANE) * SUBLANE
    cc = jnp.zeros((rows, D), F32).at[:B].set(c).at[B].set(c_ctx)
    ada = _ada(cc, w["w_ada"], w["b_ada"]).reshape(L, rows, 6, 1, D)
    cos_l, sin_l = _rope_tables(T)
    cos_c = _pad_cols(jnp.ones((Tc, MLA_NOPE + MLA_ROPE), F32), 0, HEAD_PAD)
    sin_c = jnp.zeros((Tc, HEAD_PAD), F32)
    zero_c = jnp.zeros((B, 2, ML_HEADS * HEAD_PAD, HEAD_PAD), F32)
    zero_m = jnp.zeros((B, 2, 2 * ML_HEADS, min(ML_CHUNK, Tc)), F32)
    xl, xc = x, ctx
    for l in range(L):
        p = _prep_layer(l, w)
        mod_l, mod_c = (ada, l, None), (ada, l, B)
        ya_c, q_c, k_c, v_c, qk_c, vo_c, g_c = _inproj(xc, mod_c, cos_c, sin_c, p)
        ya_l, q_l, k_l, v_l, qk_l, vo_l, g_l = _inproj(xl, mod_l, cos_l, sin_l, p)
        hf_c, hb_c, cs_c, ms_c = _mlstm(qk_c, vo_c, g_c, p, zero_c, zero_m)
        hf_l, hb_l, _, _ = _mlstm(qk_l, vo_l, g_l, p, cs_c, ms_c)
        yb_l = _attention(q_l, [(k_l, v_l), (k_c, v_c)])
        x1_l, hm_l, aff_l = _outproj(xl, mod_l, ya_l, yb_l, hf_l, hb_l, vo_l, p)
        xl = _moe(x1_l, hm_l, aff_l, mod_l, p)
        if l < L - 1:
            yb_c = _attention(q_c, [(k_c, v_c)])
            x1_c, hm_c, aff_c = _outproj(xc, mod_c, ya_c, yb_c, hf_c, hb_c, vo_c, p)
            xc = _moe(x1_c, hm_c, aff_c, mod_c, p)
    return xl


def kernel(x, c, ctx, c_ctx, w_ada, b_ada, w_in, b_in, sg_ln_g, sg_ln_b, sg_w, sg_b, q_norm_g, kv_norm_g, w_uq, w_ukv,
           ml_conv_w, ml_conv_b, ml_f_bias, ml_norm_g, w_out, ln1_g, ln1_b, w_router, w_gate, w_up, w_down, ln2_g, ln2_b):
    w = dict(w_ada=w_ada, b_ada=b_ada, w_in=w_in, b_in=b_in, sg_ln_g=sg_ln_g, sg_ln_b=sg_ln_b, sg_w=sg_w, sg_b=sg_b,
             q_norm_g=q_norm_g, kv_norm_g=kv_norm_g, w_uq=w_uq, w_ukv=w_ukv, ml_conv_w=ml_conv_w, ml_conv_b=ml_conv_b,
             ml_f_bias=ml_f_bias, ml_norm_g=ml_norm_g, w_out=w_out, ln1_g=ln1_g, ln1_b=ln1_b, w_router=w_router,
             w_gate=w_gate, w_up=w_up, w_down=w_down, ln2_g=ln2_g, ln2_b=ln2_b)
    return _forward(x, c, ctx, c_ctx, w)
```

```python
import functools

import jax
import jax.numpy as jnp
from jax import lax
from jax.experimental import pallas as pl
from jax.experimental.pallas import tpu as pltpu

F32 = jnp.float32
BF16 = jnp.bfloat16
HIGHEST = lax.Precision.HIGHEST

D_MODEL = 1024
DEPTH = 2
GRID_W = 64
SG_WIDTH = 256
SG_GROUPS = 4
SG_GDIM = 64
SG_CHUNK = 128
MLA_HEADS = 8
MLA_NOPE = 64
MLA_ROPE = 32
MLA_VDIM = 64
Q_LORA = 384
KV_LORA = 256
ROPE_AXIS = 16
ROPE_BASE = 10000.0
MLA_SCALE = (MLA_NOPE + MLA_ROPE) ** -0.5
ML_HEADS = 4
ML_DIM = 64
ML_WIDTH = 256
ML_CONV = 5
N_EXPERTS = 16
EXPERT_FF = 1024
EC_CAPACITY = 2
ALPHA = (2 * DEPTH) ** 0.25
LN_EPS = 1e-6
LOG2E = 1.4426950408889634

LANE = 128
SUBLANE = 8
BF16_ROWS = 2 * SUBLANE
VMEM_BYTES_V7X = 64 * 1024 * 1024
VMEM_LIMIT = VMEM_BYTES_V7X * 7 // 8

HEAD_PAD = LANE
QK_LANES = ML_HEADS * HEAD_PAD + ML_WIDTH
OFF_SG, OFF_CQ, OFF_CKV, OFF_KR, OFF_ML, OFF_G, N_CAT = 0, 512, 896, 1152, 1280, 2304, 2432

ADA_COLS = 1536
TOKEN_TILE = 1024
ROW_CHAIN = 512
ATTN_QUERIES = 512
ATTN_HEADS = 2
ATTN_KEYS = 4096
ML_CHUNK = 256
ML_BLOCK = 1024
ROUTE_TILE = 256
EXPERT_ROWS = 512
GATHER_ROWS = 128
GATHER_TILES = 6
COMBINE_TILE = 512
COMBINE_WIN = 128


def _cparams(sem):
    return pltpu.CompilerParams(dimension_semantics=sem, vmem_limit_bytes=VMEM_LIMIT)


def _nt(a, b):
    return lax.dot_general(a, b, (((1,), (1,)), ((), ())), preferred_element_type=F32)


def _tn(a, b):
    return lax.dot_general(a, b, (((0,), (0,)), ((), ())), preferred_element_type=F32)


def _dot(a, b, precision=None):
    return jnp.dot(a, b, preferred_element_type=F32, precision=precision)


def _split(x):
    hi = x.astype(BF16)
    return hi, (x - hi.astype(F32)).astype(BF16)


def _dot2(x, w):
    hi, lo = _split(x)
    return _dot(jnp.concatenate([hi, lo], axis=1), jnp.concatenate([w, w], axis=0))


def _plain_norm(x):
    mu = jnp.mean(x, axis=-1, keepdims=True)
    xc = x - mu
    var = jnp.mean(xc * xc, axis=-1, keepdims=True)
    return xc * lax.rsqrt(var + LN_EPS)


def _rms(x):
    return x * lax.rsqrt(jnp.mean(x * x, axis=-1, keepdims=True) + LN_EPS)


def _silu(x):
    return x * (1.0 / (1.0 + jnp.exp(-x)))


def _gelu_tanh(x):
    return 0.5 * x * (1.0 + jnp.tanh(0.7978845608028654 * (x + 0.044715 * (x * x * x))))


def _log_sigmoid(x):
    return jnp.minimum(x, 0.0) - jnp.log(1.0 + jnp.exp(-jnp.abs(x)))


def _ada_kernel(c_ref, w_ref, b_ref, o_ref):
    o_ref[0] = _dot(_silu(c_ref[...]), w_ref[0], precision=HIGHEST) + b_ref[0]


def _ada(cc, w_ada, b_ada):
    L, D, N = w_ada.shape
    rows = cc.shape[0]
    tn = ADA_COLS
    return pl.pallas_call(
        _ada_kernel,
        grid=(L, N // tn),
        in_specs=[pl.BlockSpec((rows, D), lambda l, j: (0, 0)),
                  pl.BlockSpec((1, D, tn), lambda l, j: (l, 0, j)),
                  pl.BlockSpec((1, 1, tn), lambda l, j: (l, 0, j))],
        out_specs=pl.BlockSpec((1, rows, tn), lambda l, j: (l, 0, j)),
        out_shape=jax.ShapeDtypeStruct((L, rows, N), F32),
        compiler_params=_cparams(("parallel", "parallel")),
        name="ada",
    )(cc, w_ada, b_ada.reshape(L, 1, N))


def _inproj_kernel(x_ref, xp_ref, xn_ref, sh_ref, sc_ref, cos_ref, sin_ref, wcat_ref, bcat_ref, qg_ref, kvg_ref, wq_ref,
                   wkv_ref, sglg_ref, sglb_ref, sgw_ref, sgb_ref, cw_ref, cb_ref,
                   ya_ref, q_ref, k_ref, v_ref, qk_ref, vo_ref, g_ref):
    tm = x_ref.shape[1]
    modulated = lambda t: (_plain_norm(t) * (1.0 + sc_ref[0, 0, 0]) + sh_ref[0, 0, 0]).astype(BF16)
    z = _dot(modulated(x_ref[0]), wcat_ref[...]) + bcat_ref[...]
    i = pl.program_id(1)
    qk_cols = slice(OFF_ML, OFF_ML + 2 * ML_WIDTH)
    halo = _dot(modulated(jnp.concatenate([xp_ref[0], xn_ref[0]], axis=0)), wcat_ref[:, qk_cols]) + bcat_ref[:, qk_cols]
    prev = jnp.where(i > 0, halo[:SUBLANE], 0.0)
    nxt = jnp.where(i < pl.num_programs(1) - 1, halo[SUBLANE:], 0.0)
    big = jnp.concatenate([prev, z[:, qk_cols], nxt], axis=0)
    n_big = tm + 2 * SUBLANE
    pad = (ML_CONV - 1) // 2
    conv = cb_ref[...]
    for t in range(ML_CONV):
        s = t - pad
        shifted = big if s == 0 else pltpu.roll(big, (-s) % n_big, axis=0)
        conv = conv + shifted[SUBLANE:SUBLANE + tm] * cw_ref[t:t + 1, :]
    qkc = _silu(conv)
    hlane = lax.broadcasted_iota(jnp.int32, (1, HEAD_PAD), 1)
    for hd in range(ML_HEADS):
        q_blk = qkc[:, (hd // 2) * HEAD_PAD:(hd // 2 + 1) * HEAD_PAD]
        in_half = (hlane >= ML_DIM) if hd % 2 else (hlane < ML_DIM)
        qk_ref[0, :, hd * HEAD_PAD:(hd + 1) * HEAD_PAD] = jnp.where(in_half, q_blk, 0.0).astype(BF16)
    qk_ref[0, :, ML_HEADS * HEAD_PAD:] = (qkc[:, ML_WIDTH:] * (ML_DIM ** -0.5)).astype(BF16)
    zs = _gelu_tanh(z[:, OFF_SG:OFF_SG + 2 * SG_WIDTH])
    u = zs[:, :SG_WIDTH]
    vn = _plain_norm(zs[:, SG_WIDTH:]) * sglg_ref[...] + sglb_ref[...]
    group = jnp.right_shift(lax.broadcasted_iota(jnp.int32, (SG_CHUNK, SG_WIDTH), 1), SG_GDIM.bit_length() - 1)
    for ch in range(tm // SG_CHUNK):
        rows = slice(ch * SG_CHUNK, (ch + 1) * SG_CHUNK)
        vc = vn[rows].astype(BF16)
        mixed = sgb_ref[...]
        for g in range(SG_GROUPS):
            mixed = mixed + jnp.where(group == g, _dot(sgw_ref[g], vc), 0.0)
        ya_ref[0, rows, :] = (u[rows] * mixed).astype(BF16)
    cosT = cos_ref[...]
    sinS = sin_ref[...]
    lane = lax.broadcasted_iota(jnp.int32, (1, HEAD_PAD), 1)
    first = jnp.bitwise_and(lane, ROPE_AXIS // 2) == 0

    def rope(t):
        partner = jnp.where(first, pltpu.roll(t, HEAD_PAD - ROPE_AXIS // 2, axis=1), pltpu.roll(t, ROPE_AXIS // 2, axis=1))
        return t * cosT + partner * sinS

    cqn = (_rms(z[:, OFF_CQ:OFF_CQ + Q_LORA]) * qg_ref[...]).astype(BF16)
    qq = _dot(cqn, wq_ref[...])
    ckvn = (_rms(z[:, OFF_CKV:OFF_CKV + KV_LORA]) * kvg_ref[...]).astype(BF16)
    kv = _dot(ckvn, wkv_ref[...])
    kr = rope(z[:, OFF_KR:OFF_KR + HEAD_PAD])
    one_lane = jnp.where(lane == MLA_VDIM, 1.0, 0.0)
    hw = MLA_HEADS * HEAD_PAD
    for hd in range(MLA_HEADS):
        cols = slice(hd * HEAD_PAD, (hd + 1) * HEAD_PAD)
        rcols = slice(hw + hd * HEAD_PAD, hw + (hd + 1) * HEAD_PAD)
        q_ref[0, hd] = (rope(qq[:, cols]) * (MLA_SCALE * LOG2E)).astype(BF16)
        k_ref[0, hd] = (kv[:, cols] + kr).astype(BF16)
        v_ref[0, hd] = (kv[:, rcols] + one_lane).astype(BF16)
    vo_ref[0] = z[:, OFF_ML + 2 * ML_WIDTH:OFF_ML + 4 * ML_WIDTH]
    g_ref[0] = z[:, OFF_G:OFF_G + LANE]


def _mod_spec(mod, k):
    _, layer, row = mod
    return pl.BlockSpec((1, 1, 1, 1, D_MODEL), lambda b, *_: (layer, b if row is None else row, k, 0, 0))


def _inproj(x, mod, cosT, sinT, p):
    B, T, D = x.shape
    tm = min(TOKEN_TILE, T)
    hb = tm // SUBLANE
    const = lambda shape: pl.BlockSpec(shape, lambda b, i: (0,) * len(shape))
    heads = (B, MLA_HEADS, T, HEAD_PAD)
    return pl.pallas_call(
        _inproj_kernel,
        grid=(B, T // tm),
        in_specs=[pl.BlockSpec((1, tm, D), lambda b, i: (b, i, 0)),
                  pl.BlockSpec((1, SUBLANE, D), lambda b, i: (b, jnp.maximum(i * hb - 1, 0), 0)),
                  pl.BlockSpec((1, SUBLANE, D), lambda b, i: (b, jnp.minimum((i + 1) * hb, T // SUBLANE - 1), 0)),
                  _mod_spec(mod, 0), _mod_spec(mod, 1),
                  pl.BlockSpec((tm, HEAD_PAD), lambda b, i: (i, 0)),
                  pl.BlockSpec((tm, HEAD_PAD), lambda b, i: (i, 0)),
                  const((D, N_CAT)), const((1, N_CAT)), const((1, Q_LORA)), const((1, KV_LORA)),
                  const((Q_LORA, MLA_HEADS * HEAD_PAD)), const((KV_LORA, 2 * MLA_HEADS * HEAD_PAD)),
                  const((1, SG_WIDTH)), const((1, SG_WIDTH)), const((SG_GROUPS, SG_CHUNK, SG_CHUNK)),
                  const((SG_CHUNK, SG_WIDTH)), const((SUBLANE, 2 * ML_WIDTH)), const((1, 2 * ML_WIDTH))],
        out_specs=[pl.BlockSpec((1, tm, SG_WIDTH), lambda b, i: (b, i, 0)),
                   pl.BlockSpec((1, MLA_HEADS, tm, HEAD_PAD), lambda b, i: (b, 0, i, 0)),
                   pl.BlockSpec((1, MLA_HEADS, tm, HEAD_PAD), lambda b, i: (b, 0, i, 0)),
                   pl.BlockSpec((1, MLA_HEADS, tm, HEAD_PAD), lambda b, i: (b, 0, i, 0)),
                   pl.BlockSpec((1, tm, QK_LANES), lambda b, i: (b, i, 0)),
                   pl.BlockSpec((1, tm, 2 * ML_WIDTH), lambda b, i: (b, i, 0)),
                   pl.BlockSpec((1, tm, LANE), lambda b, i: (b, i, 0))],
        out_shape=[jax.ShapeDtypeStruct((B, T, SG_WIDTH), BF16),
                   jax.ShapeDtypeStruct(heads, BF16), jax.ShapeDtypeStruct(heads, BF16),
                   jax.ShapeDtypeStruct(heads, BF16),
                   jax.ShapeDtypeStruct((B, T, QK_LANES), BF16),
                   jax.ShapeDtypeStruct((B, T, 2 * ML_WIDTH), F32),
                   jax.ShapeDtypeStruct((B, T, LANE), F32)],
        compiler_params=_cparams(("parallel", "parallel")),
        name="inproj",
    )(x, x, x, mod[0], mod[0], cosT, sinT, p["wcat"], p["bcat"], p["q_norm_g"], p["kv_norm_g"], p["wq"], p["wkv"],
      p["sg_ln_g"], p["sg_ln_b"], p["sg_w"], p["sg_bias"], p["conv_w"], p["conv_b"])


def _attn_kernel(*refs, n_src, tk):
    q_ref = refs[0]
    kv_refs = refs[1:1 + 2 * n_src]
    o_ref = refs[1 + 2 * n_src]
    nh, tq = q_ref.shape[1], q_ref.shape[2]
    qs = [q_ref[0, hh] for hh in range(nh)]
    ms = [jnp.full((tq, 1), -jnp.inf, F32) for _ in range(nh)]
    accs = [jnp.zeros((tq, HEAD_PAD), F32) for _ in range(nh)]
    for s in range(n_src):
        k_ref, v_ref = kv_refs[2 * s], kv_refs[2 * s + 1]
        ck = min(tk, k_ref.shape[2])
        for c in range(k_ref.shape[2] // ck):
            rows = slice(c * ck, (c + 1) * ck)
            for hh in range(nh):
                sc = _nt(qs[hh], k_ref[0, hh, rows, :])
                m_new = jnp.maximum(ms[hh], jnp.max(sc, axis=1, keepdims=True))
                pr = jnp.exp2(sc - m_new)
                accs[hh] = accs[hh] * jnp.exp2(ms[hh] - m_new) + _dot(pr.astype(BF16), v_ref[0, hh, rows, :])
                ms[hh] = m_new
    outs = [accs[hh] / accs[hh][:, MLA_VDIM:MLA_VDIM + 1] for hh in range(nh)]
    lane = lax.broadcasted_iota(jnp.int32, (tq, HEAD_PAD), 1)
    for pair in range(nh // 2):
        both = jnp.where(lane < MLA_VDIM, outs[2 * pair], pltpu.roll(outs[2 * pair + 1], MLA_VDIM, axis=1))
        o_ref[0, :, pair * HEAD_PAD:(pair + 1) * HEAD_PAD] = both.astype(BF16)


def _attention(q, kvs):
    B, H, T, _ = q.shape
    tq, tk, nh = min(ATTN_QUERIES, T), ATTN_KEYS, ATTN_HEADS
    in_specs = [pl.BlockSpec((1, nh, tq, HEAD_PAD), lambda b, hp, i: (b, hp, i, 0))]
    args = [q]
    for k, v in kvs:
        K = k.shape[2]
        in_specs += [pl.BlockSpec((1, nh, K, HEAD_PAD), lambda b, hp, i: (b, hp, 0, 0))] * 2
        args += [k, v]
    return pl.pallas_call(
        functools.partial(_attn_kernel, n_src=len(kvs), tk=tk),
        grid=(B, H // nh, T // tq),
        in_specs=in_specs,
        out_specs=pl.BlockSpec((1, tq, nh * MLA_VDIM), lambda b, hp, i: (b, i, hp)),
        out_shape=jax.ShapeDtypeStruct((B, T, H * MLA_VDIM), BF16),
        compiler_params=_cparams(("parallel", "parallel", "parallel")),
        name="attention",
    )(*args)


def _mlstm_direction(d, qk, vv, gl, fb, c_prev, m_prev):
    L = qk.shape[0]
    lf = _log_sigmoid(pltpu.roll(gl, LANE - 2 * ML_HEADS, axis=1) + fb)
    ks = lax.broadcasted_iota(jnp.int32, (L, L), 0)
    qt = lax.broadcasted_iota(jnp.int32, (L, L), 1)
    visible = (ks >= qt) if d else (ks <= qt)
    tri = jnp.where((qt >= ks) if d else (qt <= ks), 1.0, 0.0).astype(BF16)
    lf_hi, lf_lo = _split(lf)
    bc = _dot(tri, lf_hi) + _dot(tri, lf_lo)
    r_cols = gl - bc
    b8 = bc.T[0:2 * ML_HEADS]
    li8 = gl.T[0:2 * ML_HEADS]
    b_end = b8[:, 0:1] if d else b8[:, L - 1:L]
    gw = b_end - b8 + li8
    m_loc = jnp.max(gw, axis=1, keepdims=True)
    w8 = jnp.exp(gw - m_loc)
    m_next = jnp.maximum(b_end + m_prev, m_loc)
    a_sc = jnp.exp(b_end + m_prev - m_next)
    s_sc = jnp.exp(m_loc - m_next)
    a8 = b8 + m_prev

    srow = lax.broadcasted_iota(jnp.int32, (HEAD_PAD, L), 0)
    h_blocks, c_next = [], []
    for blk in range(ML_HEADS // 2):
        km = qk[:, (ML_HEADS + blk) * HEAD_PAD:(ML_HEADS + blk + 1) * HEAD_PAD]
        v_t = vv[:, blk * HEAD_PAD:(blk + 1) * HEAD_PAD].T
        hh = []
        for half in range(2):
            hd = 2 * blk + half
            g = d * ML_HEADS + hd
            in_half = (srow >= ML_DIM) if half else (srow < ML_DIM)
            den_row = 0 if half else ML_DIM
            qm = qk[:, hd * HEAD_PAD:(hd + 1) * HEAD_PAD]
            vaug_t = jnp.where(in_half, v_t, jnp.where(srow == den_row, 1.0, 0.0))
            dm = jnp.where(visible, r_cols[:, g:g + 1] + b8[g:g + 1, :], -jnp.inf)
            a_row = a8[g:g + 1, :]
            m_t = jnp.maximum(a_row, jnp.max(dm, axis=0, keepdims=True))
            pm = jnp.exp(dm - m_t) * _nt(km, qm)
            c0 = c_prev[hd]
            nd = (jnp.exp(a_row - m_t) * _nt(c0.astype(BF16), qm)
                  + _dot(vaug_t.astype(BF16), pm.astype(BF16)))
            den = nd[den_row:den_row + 1, :]
            hh.append(nd / jnp.maximum(jnp.abs(den), jnp.exp(-m_t)))
            c_loc = _dot((vaug_t * w8[g:g + 1, :]).astype(BF16), km)
            c_next.append(a_sc[g:g + 1, 0:HEAD_PAD] * c0 + s_sc[g:g + 1, 0:HEAD_PAD] * c_loc)
        h_blocks.append(jnp.where(srow < ML_DIM, hh[0], hh[1]).T)
    grow = lax.broadcasted_iota(jnp.int32, (2 * ML_HEADS, L), 0)
    mine = (grow >= d * ML_HEADS) & (grow < (d + 1) * ML_HEADS)
    return h_blocks, c_next, jnp.where(mine, m_next, m_prev)


def _mlstm_kernel(qkf_ref, vf_ref, gf_ref, qkb_ref, vb_ref, gb_ref, fb_ref, ci_ref, mi_ref,
                  hf_ref, hb_ref, cs_ref, ms_ref):
    @pl.when(pl.program_id(1) == 0)
    def _():
        cs_ref[...] = ci_ref[...]
        ms_ref[...] = mi_ref[...]

    rows = lambda hd: slice(hd * HEAD_PAD, (hd + 1) * HEAD_PAD)
    ins = ((qkf_ref, vf_ref, gf_ref), (qkb_ref, vb_ref, gb_ref))
    outs = (hf_ref, hb_ref)
    L = ms_ref.shape[3]
    n_sub = qkf_ref.shape[1] // L
    states = [([cs_ref[0, d, rows(hd), :] for hd in range(ML_HEADS)], ms_ref[0, d]) for d in range(2)]
    for d in range(2):
        qk_ref, v_ref, g_ref = ins[d]
        c_state, m_state = states[d]
        for s in (range(n_sub - 1, -1, -1) if d else range(n_sub)):
            tok = slice(s * L, (s + 1) * L)
            h_blocks, c_state, m_state = _mlstm_direction(d, qk_ref[0, tok, :], v_ref[0, tok, :], g_ref[0, tok, :],
                                                          fb_ref[...], c_state, m_state)
            for blk, hb in enumerate(h_blocks):
                outs[d][0, tok, blk * HEAD_PAD:(blk + 1) * HEAD_PAD] = hb
        states[d] = (c_state, m_state)
    for d in range(2):
        for hd in range(ML_HEADS):
            cs_ref[0, d, rows(hd), :] = states[d][0][hd]
        ms_ref[0, d] = states[d][1]


def _mlstm(qk, vo, gates, p, c_init, m_init):
    B, T, _ = vo.shape
    chunk = min(ML_CHUNK, T)
    L = min(ML_BLOCK, T)
    nc = T // L
    const = lambda shape: pl.BlockSpec(shape, lambda b, j: (0,) * len(shape))
    state = pl.BlockSpec((1, 2, ML_HEADS * HEAD_PAD, HEAD_PAD), lambda b, j: (b, 0, 0, 0))
    mstate = pl.BlockSpec((1, 2, 2 * ML_HEADS, chunk), lambda b, j: (b, 0, 0, 0))
    fwd = lambda b, j: (b, j, 0)
    bwd = lambda b, j: (b, nc - 1 - j, 0)
    return pl.pallas_call(
        _mlstm_kernel,
        grid=(B, nc),
        in_specs=[pl.BlockSpec((1, L, QK_LANES), fwd),
                  pl.BlockSpec((1, L, ML_WIDTH), fwd),
                  pl.BlockSpec((1, L, LANE), fwd),
                  pl.BlockSpec((1, L, QK_LANES), bwd),
                  pl.BlockSpec((1, L, ML_WIDTH), bwd),
                  pl.BlockSpec((1, L, LANE), bwd),
                  const((1, LANE)), state, mstate],
        out_specs=[pl.BlockSpec((1, L, ML_WIDTH), fwd), pl.BlockSpec((1, L, ML_WIDTH), bwd), state, mstate],
        out_shape=[jax.ShapeDtypeStruct((B, T, ML_WIDTH), F32), jax.ShapeDtypeStruct((B, T, ML_WIDTH), F32),
                   jax.ShapeDtypeStruct((B, 2, ML_HEADS * HEAD_PAD, HEAD_PAD), F32),
                   jax.ShapeDtypeStruct((B, 2, 2 * ML_HEADS, chunk), F32)],
        compiler_params=_cparams(("parallel", "arbitrary")),
        name="mlstm",
    )(qk, vo, gates, qk, vo, gates, p["f_bias"], c_init, m_init)


def _outproj_kernel(x_ref, g1_ref, sh2_ref, sc2_ref, ya_ref, yb_ref, h0_ref, h1_ref, o_ref, mg_ref, avg_ref,
                    wo_ref, l1g_ref, l1b_ref, wr_ref, x1_ref, hm_ref, aff_ref):
    tm = x_ref.shape[1]
    sub = min(ROW_CHAIN, tm)
    for r in range(tm // sub):
        rows = slice(r * sub, (r + 1) * sub)
        hs = h0_ref[0, rows, :] + h1_ref[0, rows, :]
        mu = _dot2(hs, avg_ref[...])
        hc = hs - mu
        var = _dot2(hc * hc, avg_ref[...])
        yc = (1.0 / (1.0 + jnp.exp(-o_ref[0, rows, :]))) * (hc * lax.rsqrt(var + LN_EPS) * mg_ref[...])
        y = _dot(jnp.concatenate([ya_ref[0, rows, :], yb_ref[0, rows, :], yc.astype(BF16)], axis=1), wo_ref[...])
        x1 = _plain_norm(ALPHA * x_ref[0, rows, :] + g1_ref[0, 0, 0] * y) * l1g_ref[...] + l1b_ref[...]
        x1_ref[0, rows, :] = x1
        hm = _plain_norm(x1) * (1.0 + sc2_ref[0, 0, 0]) + sh2_ref[0, 0, 0]
        hm_ref[0, rows, :] = hm.astype(BF16)
        hi, lo = _split(hm)
        parts = _dot(jnp.concatenate([hi, lo], axis=1), wr_ref[...])
        logits = (parts[:, :LANE] + parts[:, LANE:]).T[:N_EXPERTS, :]
        ex = jnp.exp(logits - jnp.max(logits, axis=0, keepdims=True))
        aff_ref[0, :, rows] = ex / jnp.sum(ex, axis=0, keepdims=True)


def _outproj(x, mod, ya, yb, h_fwd, h_bwd, vo, p):
    B, T, D = x.shape
    tm = min(TOKEN_TILE, T)
    const = lambda shape: pl.BlockSpec(shape, lambda b, i: (0,) * len(shape))
    return pl.pallas_call(
        _outproj_kernel,
        grid=(B, T // tm),
        in_specs=[pl.BlockSpec((1, tm, D), lambda b, i: (b, i, 0)),
                  _mod_spec(mod, 2), _mod_spec(mod, 3), _mod_spec(mod, 4),
                  pl.BlockSpec((1, tm, SG_WIDTH), lambda b, i: (b, i, 0)),
                  pl.BlockSpec((1, tm, MLA_HEADS * MLA_VDIM), lambda b, i: (b, i, 0)),
                  pl.BlockSpec((1, tm, ML_WIDTH), lambda b, i: (b, i, 0)),
                  pl.BlockSpec((1, tm, ML_WIDTH), lambda b, i: (b, i, 0)),
                  pl.BlockSpec((1, tm, ML_WIDTH), lambda b, i: (b, i, 1)),
                  const((1, ML_WIDTH)), const((ML_WIDTH, ML_WIDTH)), const((D, D)), const((1, D)), const((1, D)),
                  const((2 * D, 2 * LANE))],
        out_specs=[pl.BlockSpec((1, tm, D), lambda b, i: (b, i, 0)),
                   pl.BlockSpec((1, tm, D), lambda b, i: (b, i, 0)),
                   pl.BlockSpec((1, N_EXPERTS, tm), lambda b, i: (b, 0, i))],
        out_shape=[jax.ShapeDtypeStruct((B, T, D), F32), jax.ShapeDtypeStruct((B, T, D), BF16),
                   jax.ShapeDtypeStruct((B, N_EXPERTS, T), F32)],
        compiler_params=_cparams(("parallel", "parallel")),
        name="outproj",
    )(x, mod[0], mod[0], mod[0], ya, yb, h_fwd, h_bwd, vo, p["ml_norm_g"], p["head_avg"], p["w_out"], p["ln1_g"], p["ln1_b"],
      p["w_router3"])


def _route_kernel(aff_ref, pos_ref, cnt_ref, *, cap, tw):
    aff = aff_ref[0]
    E, T = aff.shape
    count = lambda m: jnp.sum(jnp.where(m, 1.0, 0.0), axis=1, keepdims=True)
    as_float = lambda b: pltpu.bitcast(b, F32)
    thr = jnp.zeros((E, 1), jnp.int32)
    for bit in range(30, -1, -1):
        cand = thr | (1 << bit)
        thr = jnp.where(count(aff >= as_float(cand)) >= cap, cand, thr)
    above = aff >= as_float(thr + 1)
    tied = (aff >= as_float(thr)) & jnp.logical_not(above)
    need = cap - count(above)
    idx = lax.broadcasted_iota(jnp.int32, (E, T), 1)

    def cut_search():
        cut = jnp.zeros((E, 1), jnp.int32)
        for bit in range(T.bit_length() - 1, -1, -1):
            cand = cut | (1 << bit)
            ok = (cand <= T) & (count(tied & (idx < cand)) <= need)
            cut = jnp.where(ok, cand, cut)
        return cut

    surplus = jnp.max(jnp.where(count(tied) > need, 1.0, 0.0)) > 0.0
    cut = lax.cond(surplus, cut_search, lambda: jnp.full((E, 1), T, jnp.int32))
    sel = above | (tied & (idx < cut))
    r = lax.broadcasted_iota(jnp.int32, (tw, tw), 0)
    cidx = lax.broadcasted_iota(jnp.int32, (tw, tw), 1)
    before = jnp.where(r < cidx, 1.0, 0.0).astype(BF16)
    offset = jnp.zeros((E, 1), F32)
    lane = lax.broadcasted_iota(jnp.int32, (E, LANE), 1)
    table = jnp.zeros((E, LANE), F32)
    for blk in range(T // tw):
        cols = slice(blk * tw, (blk + 1) * tw)
        sb = jnp.where(sel[:, cols], 1.0, 0.0)
        slot = _dot(sb.astype(BF16), before) + offset
        pos_ref[0, :, cols] = jnp.where(sel[:, cols], slot, -1.0).astype(jnp.int32)
        offset = offset + jnp.sum(sb, axis=1, keepdims=True)
        table = jnp.where(lane == blk + 1, offset, table)
    cnt_ref[0] = table.astype(jnp.int32)


def _route(aff_t, cap):
    B, E, T = aff_t.shape
    nt = T // ROUTE_TILE
    pos, cnt = pl.pallas_call(
        functools.partial(_route_kernel, cap=cap, tw=ROUTE_TILE),
        grid=(B,),
        in_specs=[pl.BlockSpec((1, E, T), lambda b: (b, 0, 0))],
        out_specs=[pl.BlockSpec((1, E, T), lambda b: (b, 0, 0)), pl.BlockSpec((1, E, LANE), lambda b: (b, 0, 0))],
        out_shape=[jax.ShapeDtypeStruct((B, E, T), jnp.int32), jax.ShapeDtypeStruct((B, E, LANE), jnp.int32)],
        compiler_params=_cparams(("parallel",)),
        name="route",
    )(aff_t)
    return pos, cnt[:, :, :nt + 1].reshape(-1)


def _expert_kernel(cnt_ref, pos_ref, aff_ref, h_ref, wg_ref, wu_ref, wd_ref, y_ref, xe_ref, gate_ref,
                   wgb_ref, wub_ref, wdb_ref, *, n_win):
    nb, cap = y_ref.shape[0], y_ref.shape[2]
    nt = pos_ref.shape[2]
    rb = min(GATHER_ROWS, cap)
    e, j = pl.program_id(0), pl.program_id(1)

    @pl.when(j == 0)
    def _():
        wgb_ref[...] = wg_ref[0, 0].astype(BF16)
        wub_ref[...] = wu_ref[0, 0].astype(BF16)
        wdb_ref[...] = wd_ref[0, 0].astype(BF16)

    slot_iota = lax.broadcasted_iota(jnp.int32, (rb, ROUTE_TILE), 0)

    def gather(bb, i, n, s0):
        hits = [pos_ref[bb, 0, pl.ds(i + w, 1), :] == slot_iota + s0 for w in range(n)]
        onehot = jnp.concatenate([jnp.where(h, 1.0, 0.0).astype(BF16) for h in hits], axis=1)
        rows = pl.ds(pl.multiple_of(i * ROUTE_TILE, ROUTE_TILE), n * ROUTE_TILE)
        xe = _dot(onehot, h_ref[bb, rows, :])
        gate = sum(jnp.sum(jnp.where(h, aff_ref[bb, 0, pl.ds(i + w, 1), :], 0.0), axis=1, keepdims=True)
                   for w, h in enumerate(hits))
        return xe, gate

    for bb in range(nb):
        base = ((j * nb + bb) * N_EXPERTS + e) * (nt + 1)
        for r in range(cap // rb):
            s0 = r * rb
            first = sum((cnt_ref[base + i + 1] <= s0).astype(jnp.int32) for i in range(nt))
            end = sum((cnt_ref[base + i] < s0 + rb).astype(jnp.int32) for i in range(nt))
            start = jnp.minimum(first, nt - n_win)
            rows = slice(bb * cap + s0, bb * cap + s0 + rb)
            xe, gate = gather(bb, start, n_win, s0)
            xe_ref[rows, :] = xe.astype(BF16)
            gate_ref[rows, :] = gate

            def overflow(i, carry, bb=bb, s0=s0, rows=rows):
                xe, gate = gather(bb, i, 1, s0)
                xe_ref[rows, :] = (xe_ref[rows, :].astype(F32) + xe).astype(BF16)
                gate_ref[rows, :] += gate
                return carry

            lax.fori_loop(start + n_win, end, overflow, 0)
    xb = xe_ref[...]
    hid = _silu(_dot(xb, wgb_ref[...])) * _dot(xb, wub_ref[...])
    y = (_dot(hid.astype(BF16), wdb_ref[...]) * gate_ref[...]).astype(BF16)
    for bb in range(nb):
        y_ref[bb, 0] = y[bb * cap:(bb + 1) * cap]


def _experts(pos, cnt, aff_t, hm, p, cap):
    B, T, D = hm.shape
    E = N_EXPERTS
    nt = T // ROUTE_TILE
    layer = p["layer"]
    nb = max(1, min(B, EXPERT_ROWS // cap))
    row = pl.BlockSpec((nb, 1, nt, ROUTE_TILE), lambda e, j, c: (j, e, 0, 0))
    wspec = lambda shape: pl.BlockSpec((1, 1) + shape, lambda e, j, c: (layer, e, 0, 0))
    return pl.pallas_call(
        functools.partial(_expert_kernel, n_win=min(GATHER_TILES, nt)),
        grid_spec=pltpu.PrefetchScalarGridSpec(
            num_scalar_prefetch=1,
            grid=(E, B // nb),
            in_specs=[row, row,
                      pl.BlockSpec((nb, T, D), lambda e, j, c: (j, 0, 0)),
                      wspec((D, EXPERT_FF)), wspec((D, EXPERT_FF)), wspec((EXPERT_FF, D))],
            out_specs=pl.BlockSpec((nb, 1, cap, D), lambda e, j, c: (j, e, 0, 0)),
            scratch_shapes=[pltpu.VMEM((nb * cap, D), BF16), pltpu.VMEM((nb * cap, 1), F32),
                            pltpu.VMEM((D, EXPERT_FF), BF16), pltpu.VMEM((D, EXPERT_FF), BF16),
                            pltpu.VMEM((EXPERT_FF, D), BF16)]),
        out_shape=jax.ShapeDtypeStruct((B, E, cap, D), BF16),
        compiler_params=_cparams(("parallel", "arbitrary")),
        name="experts",
    )(cnt, pos.reshape(B, E, nt, ROUTE_TILE), aff_t.reshape(B, E, nt, ROUTE_TILE), hm,
      p["w_gate"], p["w_up"], p["w_down"])


def _combine_kernel(cnt_ref, pos_ref, y_ref, x_ref, g2_ref, lg_ref, lb_ref, o_ref, acc_ref, *, nt):
    E, cap = y_ref.shape[1], y_ref.shape[2]
    tt = x_ref.shape[1]
    win = min(COMBINE_WIN, cap)
    per = tt // ROUTE_TILE
    b, i = pl.program_id(0), pl.program_id(1)
    slot_iota = lax.broadcasted_iota(jnp.int32, (win, tt), 0)

    def window(e, w):
        start = pl.multiple_of(jnp.minimum(w, cap - win), BF16_ROWS)
        slot = slot_iota + start
        hit = (pos_ref[0, e:e + 1, :] == slot) & (slot >= w)
        return jnp.where(hit, 1.0, 0.0).astype(BF16), y_ref[0, e, pl.ds(start, win), :]

    onehots, windows, ranges = [], [], []
    for e in range(E):
        base = (b * E + e) * (nt + 1) + i * per
        lo, hi = cnt_ref[base], cnt_ref[base + per]
        w0 = jnp.bitwise_and(lo, -BF16_ROWS)
        onehot, rows = window(e, w0)
        windows.append(rows)
        onehots.append(onehot)
        ranges.append((w0, hi))
    acc_ref[...] = _tn(jnp.concatenate(onehots, axis=0), jnp.concatenate(windows, axis=0))
    overflow = functools.reduce(jnp.logical_or, [hi > w0 + win for w0, hi in ranges])

    @pl.when(overflow)
    def _():
        for e in range(E):
            w0, hi = ranges[e]

            def more(w, e=e):
                onehot, rows = window(e, w)
                acc_ref[...] += _tn(onehot, rows)
                return w + win

            lax.while_loop(lambda w, hi=hi: w < hi, more, w0 + win)

    o_ref[0] = _plain_norm(ALPHA * x_ref[0] + g2_ref[0, 0, 0] * acc_ref[...]) * lg_ref[...] + lb_ref[...]


def _combine(pos, cnt, ye, x1, mod, p):
    B, T, D = x1.shape
    tt = min(COMBINE_TILE, T)
    E, cap = ye.shape[1], ye.shape[2]
    const = lambda shape: pl.BlockSpec(shape, lambda b, i, c: (0,) * len(shape))
    return pl.pallas_call(
        functools.partial(_combine_kernel, nt=T // ROUTE_TILE),
        grid_spec=pltpu.PrefetchScalarGridSpec(
            num_scalar_prefetch=1,
            grid=(B, T // tt),
            in_specs=[pl.BlockSpec((1, E, tt), lambda b, i, c: (b, 0, i)),
                      pl.BlockSpec((1, E, cap, D), lambda b, i, c: (b, 0, 0, 0)),
                      pl.BlockSpec((1, tt, D), lambda b, i, c: (b, i, 0)),
                      _mod_spec(mod, 5), const((1, D)), const((1, D))],
            out_specs=pl.BlockSpec((1, tt, D), lambda b, i, c: (b, i, 0)),
            scratch_shapes=[pltpu.VMEM((tt, D), F32)]),
        out_shape=jax.ShapeDtypeStruct((B, T, D), F32),
        compiler_params=_cparams(("parallel", "arbitrary")),
        name="combine",
    )(cnt, pos, ye, x1, mod[0], p["ln2_g"], p["ln2_b"])


def _pad_cols(a, before, total):
    pads = [(0, 0)] * (a.ndim - 1) + [(before, total - before - a.shape[-1])]
    return jnp.pad(a, pads)


def _prep_layer(l, w):
    D = D_MODEL
    wi, bi = w["w_in"][l], w["b_in"][l][None, :]
    n_sg, n_mla = 2 * SG_WIDTH, Q_LORA + KV_LORA + MLA_ROPE
    o_kr = n_sg + Q_LORA + KV_LORA
    o_ml = n_sg + n_mla
    o_g = o_ml + 4 * ML_WIDTH

    def cat(a):
        kr = a[:, o_kr:o_kr + MLA_ROPE]
        gates = a[:, o_g:o_g + 4 * ML_HEADS].reshape(-1, 2, 2, ML_HEADS).transpose(0, 2, 1, 3).reshape(-1, 4 * ML_HEADS)
        return jnp.concatenate([a[:, :o_kr], _pad_cols(kr, MLA_NOPE, HEAD_PAD), a[:, o_ml:o_g],
                                _pad_cols(gates, 0, LANE)], axis=1)

    wq = w["w_uq"][l].reshape(Q_LORA, MLA_HEADS, MLA_NOPE + MLA_ROPE)
    wq_main = _pad_cols(wq, 0, HEAD_PAD).reshape(Q_LORA, -1)
    wkv = w["w_ukv"][l].reshape(KV_LORA, MLA_HEADS, MLA_NOPE + MLA_VDIM)
    wk = _pad_cols(wkv[..., :MLA_NOPE], 0, HEAD_PAD).reshape(KV_LORA, -1)
    wv = _pad_cols(wkv[..., MLA_NOPE:], 0, HEAD_PAD).reshape(KV_LORA, -1)
    head = jnp.arange(ML_WIDTH) // ML_DIM
    wr = _pad_cols(w["w_router"][l], 0, LANE)
    wr_hi = wr.astype(BF16)
    return {
        "wcat": cat(wi).astype(BF16), "bcat": cat(bi),
        "q_norm_g": w["q_norm_g"][l][None, :], "kv_norm_g": w["kv_norm_g"][l][None, :],
        "wq": wq_main.astype(BF16),
        "wkv": jnp.concatenate([wk, wv], axis=1).astype(BF16),
        "sg_ln_g": w["sg_ln_g"][l][None, :], "sg_ln_b": w["sg_ln_b"][l][None, :],
        "sg_w": w["sg_w"][l].astype(BF16),
        "sg_bias": jnp.repeat(w["sg_b"][l].T, SG_GDIM, axis=1),
        "conv_w": jnp.pad(w["ml_conv_w"][l], ((0, SUBLANE - ML_CONV), (0, 0))), "conv_b": w["ml_conv_b"][l][None, :],
        "f_bias": _pad_cols(w["ml_f_bias"][l].reshape(1, 2 * ML_HEADS), 0, LANE),
        "ml_norm_g": w["ml_norm_g"][l][None, :],
        "head_avg": ((head[:, None] == head[None, :]).astype(F32) / ML_DIM).astype(BF16),
        "w_out": w["w_out"][l].astype(BF16),
        "ln1_g": w["ln1_g"][l][None, :], "ln1_b": w["ln1_b"][l][None, :],
        "w_router3": jnp.concatenate([jnp.concatenate([wr_hi, (wr - wr_hi.astype(F32)).astype(BF16)], axis=1),
                                      jnp.concatenate([wr_hi, jnp.zeros_like(wr_hi)], axis=1)], axis=0),
        "layer": l, "w_gate": w["w_gate"], "w_up": w["w_up"], "w_down": w["w_down"],
        "ln2_g": w["ln2_g"][l][None, :], "ln2_b": w["ln2_b"][l][None, :],
    }


def _rope_tables(T):
    rows = T // GRID_W
    row = jnp.repeat(jnp.arange(rows, dtype=F32), GRID_W)
    colv = jnp.tile(jnp.arange(GRID_W, dtype=F32), rows)
    inv = ROPE_BASE ** (-jnp.arange(ROPE_AXIS // 2, dtype=F32) * 2.0 / ROPE_AXIS)
    ang = jnp.concatenate([row[:, None] * inv, colv[:, None] * inv], axis=-1)
    half = ROPE_AXIS // 2
    spread = lambda t: jnp.concatenate([t[:, :half], t[:, :half], t[:, half:], t[:, half:]], axis=1)
    cosT = jnp.concatenate([jnp.ones((T, MLA_NOPE), F32), spread(jnp.cos(ang)), jnp.zeros((T, 32), F32)], axis=1)
    sign = jnp.tile(jnp.repeat(jnp.asarray([-1.0, 1.0], F32), half), 2)
    sinS = _pad_cols(spread(jnp.sin(ang)) * sign, MLA_NOPE, HEAD_PAD)
    return cosT, sinS


def _moe(x1, hm, aff_t, mod, p):
    cap = EC_CAPACITY * x1.shape[1] // N_EXPERTS
    pos, cnt = _route(aff_t, cap)
    ye = _experts(pos, cnt, aff_t, hm, p, cap)
    return _combine(pos, cnt, ye, x1, mod, p)


def _forward(x, c, ctx, c_ctx, w):
    B, T, D = x.shape
    Tc = ctx.shape[1]
    L = w["w_ada"].shape[0]
    rows = -(-(B + 1) // SUBLANE) * SUBLANE
    cc = jnp.zeros((rows, D), F32).at[:B].set(c).at[B].set(c_ctx)
    ada = _ada(cc, w["w_ada"], w["b_ada"]).reshape(L, rows, 6, 1, D)
    cos_l, sin_l = _rope_tables(T)
    cos_c = _pad_cols(jnp.ones((Tc, MLA_NOPE + MLA_ROPE), F32), 0, HEAD_PAD)
    sin_c = jnp.zeros((Tc, HEAD_PAD), F32)
    zero_c = jnp.zeros((B, 2, ML_HEADS * HEAD_PAD, HEAD_PAD), F32)
    zero_m = jnp.zeros((B, 2, 2 * ML_HEADS, min(ML_CHUNK, Tc)), F32)
    xl, xc = x, ctx
    for l in range(L):
        p = _prep_layer(l, w)
        mod_l, mod_c = (ada, l, None), (ada, l, B)
        ya_c, q_c, k_c, v_c, qk_c, vo_c, g_c = _inproj(xc, mod_c, cos_c, sin_c, p)
        ya_l, q_l, k_l, v_l, qk_l, vo_l, g_l = _inproj(xl, mod_l, cos_l, sin_l, p)
        hf_c, hb_c, cs_c, ms_c = _mlstm(qk_c, vo_c, g_c, p, zero_c, zero_m)
        hf_l, hb_l, _, _ = _mlstm(qk_l, vo_l, g_l, p, cs_c, ms_c)
        yb_l = _attention(q_l, [(k_l, v_l), (k_c, v_c)])
        x1_l, hm_l, aff_l = _outproj(xl, mod_l, ya_l, yb_l, hf_l, hb_l, vo_l, p)
        xl = _moe(x1_l, hm_l, aff_l, mod_l, p)
        if l < L - 1:
            yb_c = _attention(q_c, [(k_c, v_c)])
            x1_c, hm_c, aff_c = _outproj(xc, mod_c, ya_c, yb_c, hf_c, hb_c, vo_c, p)
            xc = _moe(x1_c, hm_c, aff_c, mod_c, p)
    return xl


def kernel(x, c, ctx, c_ctx, w_ada, b_ada, w_in, b_in, sg_ln_g, sg_ln_b, sg_w, sg_b, q_norm_g, kv_norm_g, w_uq, w_ukv,
           ml_conv_w, ml_conv_b, ml_f_bias, ml_norm_g, w_out, ln1_g, ln1_b, w_router, w_gate, w_up, w_down, ln2_g, ln2_b):
    w = dict(w_ada=w_ada, b_ada=b_ada, w_in=w_in, b_in=b_in, sg_ln_g=sg_ln_g, sg_ln_b=sg_ln_b, sg_w=sg_w, sg_b=sg_b,
             q_norm_g=q_norm_g, kv_norm_g=kv_norm_g, w_uq=w_uq, w_ukv=w_ukv, ml_conv_w=ml_conv_w, ml_conv_b=ml_conv_b,
             ml_f_bias=ml_f_bias, ml_norm_g=ml_norm_g, w_out=w_out, ln1_g=ln1_g, ln1_b=ln1_b, w_router=w_router,
             w_gate=w_gate, w_up=w_up, w_down=w_down, ln2_g=ln2_g, ln2_b=ln2_b)
    return _forward(x, c, ctx, c_ctx, w)
```

```python
import functools

import jax
import jax.numpy as jnp
from jax import lax
from jax.experimental import pallas as pl
from jax.experimental.pallas import tpu as pltpu

F32 = jnp.float32
BF16 = jnp.bfloat16
HIGHEST = lax.Precision.HIGHEST

D_MODEL = 1024
DEPTH = 2
GRID_W = 64
SG_WIDTH = 256
SG_GROUPS = 4
SG_GDIM = 64
SG_CHUNK = 128
MLA_HEADS = 8
MLA_NOPE = 64
MLA_ROPE = 32
MLA_VDIM = 64
Q_LORA = 384
KV_LORA = 256
ROPE_AXIS = 16
ROPE_BASE = 10000.0
MLA_SCALE = (MLA_NOPE + MLA_ROPE) ** -0.5
ML_HEADS = 4
ML_DIM = 64
ML_WIDTH = 256
ML_CONV = 5
N_EXPERTS = 16
EXPERT_FF = 1024
EC_CAPACITY = 2
ALPHA = (2 * DEPTH) ** 0.25
LN_EPS = 1e-6
LOG2E = 1.4426950408889634

LANE = 128
SUBLANE = 8
BF16_ROWS = 2 * SUBLANE
VMEM_BYTES_V7X = 64 * 1024 * 1024
VMEM_LIMIT = VMEM_BYTES_V7X * 7 // 8

HEAD_PAD = LANE
QK_LANES = ML_HEADS * HEAD_PAD + ML_WIDTH
OFF_SG, OFF_CQ, OFF_CKV, OFF_KR, OFF_ML, OFF_G, N_CAT = 0, 512, 896, 1152, 1280, 2304, 2432

ADA_COLS = 1536
TOKEN_TILE = 1024
ROW_CHAIN = 512
ATTN_QUERIES = 512
ATTN_HEADS = 2
ATTN_KEYS = 4096
ML_CHUNK = 256
ML_BLOCK = 1024
ROUTE_TILE = 256
EXPERT_ROWS = 512
GATHER_ROWS = 128
GATHER_TILES = 6
COMBINE_TILE = 512
COMBINE_WIN = 128


def _cparams(sem):
    return pltpu.CompilerParams(dimension_semantics=sem, vmem_limit_bytes=VMEM_LIMIT)


def _nt(a, b):
    return lax.dot_general(a, b, (((1,), (1,)), ((), ())), preferred_element_type=F32)


def _tn(a, b):
    return lax.dot_general(a, b, (((0,), (0,)), ((), ())), preferred_element_type=F32)


def _dot(a, b, precision=None):
    return jnp.dot(a, b, preferred_element_type=F32, precision=precision)


def _split(x):
    hi = x.astype(BF16)
    return hi, (x - hi.astype(F32)).astype(BF16)


def _dot2(x, w):
    hi, lo = _split(x)
    return _dot(jnp.concatenate([hi, lo], axis=1), jnp.concatenate([w, w], axis=0))


def _plain_norm(x):
    mu = jnp.mean(x, axis=-1, keepdims=True)
    xc = x - mu
    var = jnp.mean(xc * xc, axis=-1, keepdims=True)
    return xc * lax.rsqrt(var + LN_EPS)


def _rms(x):
    return x * lax.rsqrt(jnp.mean(x * x, axis=-1, keepdims=True) + LN_EPS)


def _silu(x):
    return x * (1.0 / (1.0 + jnp.exp(-x)))


def _gelu_tanh(x):
    return 0.5 * x * (1.0 + jnp.tanh(0.7978845608028654 * (x + 0.044715 * (x * x * x))))


def _log_sigmoid(x):
    return jnp.minimum(x, 0.0) - jnp.log(1.0 + jnp.exp(-jnp.abs(x)))


def _ada_kernel(c_ref, w_ref, b_ref, o_ref):
    o_ref[0] = _dot(_silu(c_ref[...]), w_ref[0], precision=HIGHEST) + b_ref[0]


def _ada(cc, w_ada, b_ada):
    L, D, N = w_ada.shape
    rows = cc.shape[0]
    tn = ADA_COLS
    return pl.pallas_call(
        _ada_kernel,
        grid=(L, N // tn),
        in_specs=[pl.BlockSpec((rows, D), lambda l, j: (0, 0)),
                  pl.BlockSpec((1, D, tn), lambda l, j: (l, 0, j)),
                  pl.BlockSpec((1, 1, tn), lambda l, j: (l, 0, j))],
        out_specs=pl.BlockSpec((1, rows, tn), lambda l, j: (l, 0, j)),
        out_shape=jax.ShapeDtypeStruct((L, rows, N), F32),
        compiler_params=_cparams(("parallel", "parallel")),
        name="ada",
    )(cc, w_ada, b_ada.reshape(L, 1, N))


def _inproj_kernel(x_ref, xp_ref, xn_ref, sh_ref, sc_ref, cos_ref, sin_ref, wcat_ref, bcat_ref, qg_ref, kvg_ref, wq_ref,
                   wkv_ref, sglg_ref, sglb_ref, sgw_ref, sgb_ref, cw_ref, cb_ref,
                   ya_ref, q_ref, k_ref, v_ref, qk_ref, vo_ref, g_ref):
    tm = x_ref.shape[1]
    modulated = lambda t: (_plain_norm(t) * (1.0 + sc_ref[0, 0, 0]) + sh_ref[0, 0, 0]).astype(BF16)
    z = _dot(modulated(x_ref[0]), wcat_ref[...]) + bcat_ref[...]
    i = pl.program_id(1)
    qk_cols = slice(OFF_ML, OFF_ML + 2 * ML_WIDTH)
    halo = _dot(modulated(jnp.concatenate([xp_ref[0], xn_ref[0]], axis=0)), wcat_ref[:, qk_cols]) + bcat_ref[:, qk_cols]
    prev = jnp.where(i > 0, halo[:SUBLANE], 0.0)
    nxt = jnp.where(i < pl.num_programs(1) - 1, halo[SUBLANE:], 0.0)
    big = jnp.concatenate([prev, z[:, qk_cols], nxt], axis=0)
    n_big = tm + 2 * SUBLANE
    pad = (ML_CONV - 1) // 2
    conv = cb_ref[...]
    for t in range(ML_CONV):
        s = t - pad
        shifted = big if s == 0 else pltpu.roll(big, (-s) % n_big, axis=0)
        conv = conv + shifted[SUBLANE:SUBLANE + tm] * cw_ref[t:t + 1, :]
    qkc = _silu(conv)
    hlane = lax.broadcasted_iota(jnp.int32, (1, HEAD_PAD), 1)
    for hd in range(ML_HEADS):
        q_blk = qkc[:, (hd // 2) * HEAD_PAD:(hd // 2 + 1) * HEAD_PAD]
        in_half = (hlane >= ML_DIM) if hd % 2 else (hlane < ML_DIM)
        qk_ref[0, :, hd * HEAD_PAD:(hd + 1) * HEAD_PAD] = jnp.where(in_half, q_blk, 0.0).astype(BF16)
    qk_ref[0, :, ML_HEADS * HEAD_PAD:] = (qkc[:, ML_WIDTH:] * (ML_DIM ** -0.5)).astype(BF16)
    zs = _gelu_tanh(z[:, OFF_SG:OFF_SG + 2 * SG_WIDTH])
    u = zs[:, :SG_WIDTH]
    vn = _plain_norm(zs[:, SG_WIDTH:]) * sglg_ref[...] + sglb_ref[...]
    group = jnp.right_shift(lax.broadcasted_iota(jnp.int32, (SG_CHUNK, SG_WIDTH), 1), SG_GDIM.bit_length() - 1)
    for ch in range(tm // SG_CHUNK):
        rows = slice(ch * SG_CHUNK, (ch + 1) * SG_CHUNK)
        vc = vn[rows].astype(BF16)
        mixed = sgb_ref[...]
        for g in range(SG_GROUPS):
            mixed = mixed + jnp.where(group == g, _dot(sgw_ref[g], vc), 0.0)
        ya_ref[0, rows, :] = (u[rows] * mixed).astype(BF16)
    cosT = cos_ref[...]
    sinS = sin_ref[...]
    lane = lax.broadcasted_iota(jnp.int32, (1, HEAD_PAD), 1)
    first = jnp.bitwise_and(lane, ROPE_AXIS // 2) == 0

    def rope(t):
        partner = jnp.where(first, pltpu.roll(t, HEAD_PAD - ROPE_AXIS // 2, axis=1), pltpu.roll(t, ROPE_AXIS // 2, axis=1))
        return t * cosT + partner * sinS

    cqn = (_rms(z[:, OFF_CQ:OFF_CQ + Q_LORA]) * qg_ref[...]).astype(BF16)
    qq = _dot(cqn, wq_ref[...])
    ckvn = (_rms(z[:, OFF_CKV:OFF_CKV + KV_LORA]) * kvg_ref[...]).astype(BF16)
    kv = _dot(ckvn, wkv_ref[...])
    kr = rope(z[:, OFF_KR:OFF_KR + HEAD_PAD])
    one_lane = jnp.where(lane == MLA_VDIM, 1.0, 0.0)
    hw = MLA_HEADS * HEAD_PAD
    for hd in range(MLA_HEADS):
        cols = slice(hd * HEAD_PAD, (hd + 1) * HEAD_PAD)
        rcols = slice(hw + hd * HEAD_PAD, hw + (hd + 1) * HEAD_PAD)
        q_ref[0, hd] = (rope(qq[:, cols]) * (MLA_SCALE * LOG2E)).astype(BF16)
        k_ref[0, hd] = (kv[:, cols] + kr).astype(BF16)
        v_ref[0, hd] = (kv[:, rcols] + one_lane).astype(BF16)
    vo_ref[0] = z[:, OFF_ML + 2 * ML_WIDTH:OFF_ML + 4 * ML_WIDTH]
    g_ref[0] = z[:, OFF_G:OFF_G + LANE]


def _mod_spec(mod, k):
    _, layer, row = mod
    return pl.BlockSpec((1, 1, 1, 1, D_MODEL), lambda b, *_: (layer, b if row is None else row, k, 0, 0))


def _inproj(x, mod, cosT, sinT, p):
    B, T, D = x.shape
    tm = min(TOKEN_TILE, T)
    hb = tm // SUBLANE
    const = lambda shape: pl.BlockSpec(shape, lambda b, i: (0,) * len(shape))
    heads = (B, MLA_HEADS, T, HEAD_PAD)
    return pl.pallas_call(
        _inproj_kernel,
        grid=(B, T // tm),
        in_specs=[pl.BlockSpec((1, tm, D), lambda b, i: (b, i, 0)),
                  pl.BlockSpec((1, SUBLANE, D), lambda b, i: (b, jnp.maximum(i * hb - 1, 0), 0)),
                  pl.BlockSpec((1, SUBLANE, D), lambda b, i: (b, jnp.minimum((i + 1) * hb, T // SUBLANE - 1), 0)),
                  _mod_spec(mod, 0), _mod_spec(mod, 1),
                  pl.BlockSpec((tm, HEAD_PAD), lambda b, i: (i, 0)),
                  pl.BlockSpec((tm, HEAD_PAD), lambda b, i: (i, 0)),
                  const((D, N_CAT)), const((1, N_CAT)), const((1, Q_LORA)), const((1, KV_LORA)),
                  const((Q_LORA, MLA_HEADS * HEAD_PAD)), const((KV_LORA, 2 * MLA_HEADS * HEAD_PAD)),
                  const((1, SG_WIDTH)), const((1, SG_WIDTH)), const((SG_GROUPS, SG_CHUNK, SG_CHUNK)),
                  const((SG_CHUNK, SG_WIDTH)), const((SUBLANE, 2 * ML_WIDTH)), const((1, 2 * ML_WIDTH))],
        out_specs=[pl.BlockSpec((1, tm, SG_WIDTH), lambda b, i: (b, i, 0)),
                   pl.BlockSpec((1, MLA_HEADS, tm, HEAD_PAD), lambda b, i: (b, 0, i, 0)),
                   pl.BlockSpec((1, MLA_HEADS, tm, HEAD_PAD), lambda b, i: (b, 0, i, 0)),
                   pl.BlockSpec((1, MLA_HEADS, tm, HEAD_PAD), lambda b, i: (b, 0, i, 0)),
                   pl.BlockSpec((1, tm, QK_LANES), lambda b, i: (b, i, 0)),
                   pl.BlockSpec((1, tm, 2 * ML_WIDTH), lambda b, i: (b, i, 0)),
                   pl.BlockSpec((1, tm, LANE), lambda b, i: (b, i, 0))],
        out_shape=[jax.ShapeDtypeStruct((B, T, SG_WIDTH), BF16),
                   jax.ShapeDtypeStruct(heads, BF16), jax.ShapeDtypeStruct(heads, BF16),
                   jax.ShapeDtypeStruct(heads, BF16),
                   jax.ShapeDtypeStruct((B, T, QK_LANES), BF16),
                   jax.ShapeDtypeStruct((B, T, 2 * ML_WIDTH), F32),
                   jax.ShapeDtypeStruct((B, T, LANE), F32)],
        compiler_params=_cparams(("parallel", "parallel")),
        name="inproj",
    )(x, x, x, mod[0], mod[0], cosT, sinT, p["wcat"], p["bcat"], p["q_norm_g"], p["kv_norm_g"], p["wq"], p["wkv"],
      p["sg_ln_g"], p["sg_ln_b"], p["sg_w"], p["sg_bias"], p["conv_w"], p["conv_b"])


def _attn_kernel(*refs, n_src, tk):
    q_ref = refs[0]
    kv_refs = refs[1:1 + 2 * n_src]
    o_ref = refs[1 + 2 * n_src]
    nh, tq = q_ref.shape[1], q_ref.shape[2]
    qs = [q_ref[0, hh] for hh in range(nh)]
    ms = [jnp.full((tq, 1), -jnp.inf, F32) for _ in range(nh)]
    accs = [jnp.zeros((tq, HEAD_PAD), F32) for _ in range(nh)]
    for s in range(n_src):
        k_ref, v_ref = kv_refs[2 * s], kv_refs[2 * s + 1]
        ck = min(tk, k_ref.shape[2])
        for c in range(k_ref.shape[2] // ck):
            rows = slice(c * ck, (c + 1) * ck)
            for hh in range(nh):
                sc = _nt(qs[hh], k_ref[0, hh, rows, :])
                m_new = jnp.maximum(ms[hh], jnp.max(sc, axis=1, keepdims=True))
                pr = jnp.exp2(sc - m_new)
                accs[hh] = accs[hh] * jnp.exp2(ms[hh] - m_new) + _dot(pr.astype(BF16), v_ref[0, hh, rows, :])
                ms[hh] = m_new
    outs = [accs[hh] / accs[hh][:, MLA_VDIM:MLA_VDIM + 1] for hh in range(nh)]
    lane = lax.broadcasted_iota(jnp.int32, (tq, HEAD_PAD), 1)
    for pair in range(nh // 2):
        both = jnp.where(lane < MLA_VDIM, outs[2 * pair], pltpu.roll(outs[2 * pair + 1], MLA_VDIM, axis=1))
        o_ref[0, :, pair * HEAD_PAD:(pair + 1) * HEAD_PAD] = both.astype(BF16)


def _attention(q, kvs):
    B, H, T, _ = q.shape
    tq, tk, nh = min(ATTN_QUERIES, T), ATTN_KEYS, ATTN_HEADS
    in_specs = [pl.BlockSpec((1, nh, tq, HEAD_PAD), lambda b, hp, i: (b, hp, i, 0))]
    args = [q]
    for k, v in kvs:
        K = k.shape[2]
        in_specs += [pl.BlockSpec((1, nh, K, HEAD_PAD), lambda b, hp, i: (b, hp, 0, 0))] * 2
        args += [k, v]
    return pl.pallas_call(
        functools.partial(_attn_kernel, n_src=len(kvs), tk=tk),
        grid=(B, H // nh, T // tq),
        in_specs=in_specs,
        out_specs=pl.BlockSpec((1, tq, nh * MLA_VDIM), lambda b, hp, i: (b, i, hp)),
        out_shape=jax.ShapeDtypeStruct((B, T, H * MLA_VDIM), BF16),
        compiler_params=_cparams(("parallel", "parallel", "parallel")),
        name="attention",
    )(*args)


def _mlstm_direction(d, qk, vv, gl, fb, c_prev, m_prev):
    L = qk.shape[0]
    lf = _log_sigmoid(pltpu.roll(gl, LANE - 2 * ML_HEADS, axis=1) + fb)
    ks = lax.broadcasted_iota(jnp.int32, (L, L), 0)
    qt = lax.broadcasted_iota(jnp.int32, (L, L), 1)
    visible = (ks >= qt) if d else (ks <= qt)
    tri = jnp.where((qt >= ks) if d else (qt <= ks), 1.0, 0.0).astype(BF16)
    lf_hi, lf_lo = _split(lf)
    bc = _dot(tri, lf_hi) + _dot(tri, lf_lo)
    r_cols = gl - bc
    b8 = bc.T[0:2 * ML_HEADS]
    li8 = gl.T[0:2 * ML_HEADS]
    b_end = b8[:, 0:1] if d else b8[:, L - 1:L]
    gw = b_end - b8 + li8
    m_loc = jnp.max(gw, axis=1, keepdims=True)
    w8 = jnp.exp(gw - m_loc)
    m_next = jnp.maximum(b_end + m_prev, m_loc)
    a_sc = jnp.exp(b_end + m_prev - m_next)
    s_sc = jnp.exp(m_loc - m_next)
    a8 = b8 + m_prev

    srow = lax.broadcasted_iota(jnp.int32, (HEAD_PAD, L), 0)
    h_blocks, c_next = [], []
    for blk in range(ML_HEADS // 2):
        km = qk[:, (ML_HEADS + blk) * HEAD_PAD:(ML_HEADS + blk + 1) * HEAD_PAD]
        v_t = vv[:, blk * HEAD_PAD:(blk + 1) * HEAD_PAD].T
        hh = []
        for half in range(2):
            hd = 2 * blk + half
            g = d * ML_HEADS + hd
            in_half = (srow >= ML_DIM) if half else (srow < ML_DIM)
            den_row = 0 if half else ML_DIM
            qm = qk[:, hd * HEAD_PAD:(hd + 1) * HEAD_PAD]
            vaug_t = jnp.where(in_half, v_t, jnp.where(srow == den_row, 1.0, 0.0))
            dm = jnp.where(visible, r_cols[:, g:g + 1] + b8[g:g + 1, :], -jnp.inf)
            a_row = a8[g:g + 1, :]
            m_t = jnp.maximum(a_row, jnp.max(dm, axis=0, keepdims=True))
            pm = jnp.exp(dm - m_t) * _nt(km, qm)
            c0 = c_prev[hd]
            nd = (jnp.exp(a_row - m_t) * _nt(c0.astype(BF16), qm)
                  + _dot(vaug_t.astype(BF16), pm.astype(BF16)))
            den = nd[den_row:den_row + 1, :]
            hh.append(nd / jnp.maximum(jnp.abs(den), jnp.exp(-m_t)))
            c_loc = _dot((vaug_t * w8[g:g + 1, :]).astype(BF16), km)
            c_next.append(a_sc[g:g + 1, 0:HEAD_PAD] * c0 + s_sc[g:g + 1, 0:HEAD_PAD] * c_loc)
        h_blocks.append(jnp.where(srow < ML_DIM, hh[0], hh[1]).T)
    grow = lax.broadcasted_iota(jnp.int32, (2 * ML_HEADS, L), 0)
    mine = (grow >= d * ML_HEADS) & (grow < (d + 1) * ML_HEADS)
    return h_blocks, c_next, jnp.where(mine, m_next, m_prev)


def _mlstm_kernel(qkf_ref, vf_ref, gf_ref, qkb_ref, vb_ref, gb_ref, fb_ref, ci_ref, mi_ref,
                  hf_ref, hb_ref, cs_ref, ms_ref):
    @pl.when(pl.program_id(1) == 0)
    def _():
        cs_ref[...] = ci_ref[...]
        ms_ref[...] = mi_ref[...]

    rows = lambda hd: slice(hd * HEAD_PAD, (hd + 1) * HEAD_PAD)
    ins = ((qkf_ref, vf_ref, gf_ref), (qkb_ref, vb_ref, gb_ref))
    outs = (hf_ref, hb_ref)
    L = ms_ref.shape[3]
    n_sub = qkf_ref.shape[1] // L
    states = [([cs_ref[0, d, rows(hd), :] for hd in range(ML_HEADS)], ms_ref[0, d]) for d in range(2)]
    for d in range(2):
        qk_ref, v_ref, g_ref = ins[d]
        c_state, m_state = states[d]
        for s in (range(n_sub - 1, -1, -1) if d else range(n_sub)):
            tok = slice(s * L, (s + 1) * L)
            h_blocks, c_state, m_state = _mlstm_direction(d, qk_ref[0, tok, :], v_ref[0, tok, :], g_ref[0, tok, :],
                                                          fb_ref[...], c_state, m_state)
            for blk, hb in enumerate(h_blocks):
                outs[d][0, tok, blk * HEAD_PAD:(blk + 1) * HEAD_PAD] = hb
        states[d] = (c_state, m_state)
    for d in range(2):
        for hd in range(ML_HEADS):
            cs_ref[0, d, rows(hd), :] = states[d][0][hd]
        ms_ref[0, d] = states[d][1]


def _mlstm(qk, vo, gates, p, c_init, m_init):
    B, T, _ = vo.shape
    chunk = min(ML_CHUNK, T)
    L = min(ML_BLOCK, T)
    nc = T // L
    const = lambda shape: pl.BlockSpec(shape, lambda b, j: (0,) * len(shape))
    state = pl.BlockSpec((1, 2, ML_HEADS * HEAD_PAD, HEAD_PAD), lambda b, j: (b, 0, 0, 0))
    mstate = pl.BlockSpec((1, 2, 2 * ML_HEADS, chunk), lambda b, j: (b, 0, 0, 0))
    fwd = lambda b, j: (b, j, 0)
    bwd = lambda b, j: (b, nc - 1 - j, 0)
    return pl.pallas_call(
        _mlstm_kernel,
        grid=(B, nc),
        in_specs=[pl.BlockSpec((1, L, QK_LANES), fwd),
                  pl.BlockSpec((1, L, ML_WIDTH), fwd),
                  pl.BlockSpec((1, L, LANE), fwd),
                  pl.BlockSpec((1, L, QK_LANES), bwd),
                  pl.BlockSpec((1, L, ML_WIDTH), bwd),
                  pl.BlockSpec((1, L, LANE), bwd),
                  const((1, LANE)), state, mstate],
        out_specs=[pl.BlockSpec((1, L, ML_WIDTH), fwd), pl.BlockSpec((1, L, ML_WIDTH), bwd), state, mstate],
        out_shape=[jax.ShapeDtypeStruct((B, T, ML_WIDTH), F32), jax.ShapeDtypeStruct((B, T, ML_WIDTH), F32),
                   jax.ShapeDtypeStruct((B, 2, ML_HEADS * HEAD_PAD, HEAD_PAD), F32),
                   jax.ShapeDtypeStruct((B, 2, 2 * ML_HEADS, chunk), F32)],
        compiler_params=_cparams(("parallel", "arbitrary")),
        name="mlstm",
    )(qk, vo, gates, qk, vo, gates, p["f_bias"], c_init, m_init)


def _outproj_kernel(x_ref, g1_ref, sh2_ref, sc2_ref, ya_ref, yb_ref, h0_ref, h1_ref, o_ref, mg_ref, avg_ref,
                    wo_ref, l1g_ref, l1b_ref, wr_ref, x1_ref, hm_ref, aff_ref):
    tm = x_ref.shape[1]
    sub = min(ROW_CHAIN, tm)
    for r in range(tm // sub):
        rows = slice(r * sub, (r + 1) * sub)
        hs = h0_ref[0, rows, :] + h1_ref[0, rows, :]
        mu = _dot2(hs, avg_ref[...])
        hc = hs - mu
        var = _dot2(hc * hc, avg_ref[...])
        yc = (1.0 / (1.0 + jnp.exp(-o_ref[0, rows, :]))) * (hc * lax.rsqrt(var + LN_EPS) * mg_ref[...])
        y = _dot(jnp.concatenate([ya_ref[0, rows, :], yb_ref[0, rows, :], yc.astype(BF16)], axis=1), wo_ref[...])
        x1 = _plain_norm(ALPHA * x_ref[0, rows, :] + g1_ref[0, 0, 0] * y) * l1g_ref[...] + l1b_ref[...]
        x1_ref[0, rows, :] = x1
        hm = _plain_norm(x1) * (1.0 + sc2_ref[0, 0, 0]) + sh2_ref[0, 0, 0]
        hm_ref[0, rows, :] = hm.astype(BF16)
        hi, lo = _split(hm)
        parts = _dot(jnp.concatenate([hi, lo], axis=1), wr_ref[...])
        logits = (parts[:, :LANE] + parts[:, LANE:]).T[:N_EXPERTS, :]
        ex = jnp.exp(logits - jnp.max(logits, axis=0, keepdims=True))
        aff_ref[0, :, rows] = ex / jnp.sum(ex, axis=0, keepdims=True)


def _outproj(x, mod, ya, yb, h_fwd, h_bwd, vo, p):
    B, T, D = x.shape
    tm = min(TOKEN_TILE, T)
    const = lambda shape: pl.BlockSpec(shape, lambda b, i: (0,) * len(shape))
    return pl.pallas_call(
        _outproj_kernel,
        grid=(B, T // tm),
        in_specs=[pl.BlockSpec((1, tm, D), lambda b, i: (b, i, 0)),
                  _mod_spec(mod, 2), _mod_spec(mod, 3), _mod_spec(mod, 4),
                  pl.BlockSpec((1, tm, SG_WIDTH), lambda b, i: (b, i, 0)),
                  pl.BlockSpec((1, tm, MLA_HEADS * MLA_VDIM), lambda b, i: (b, i, 0)),
                  pl.BlockSpec((1, tm, ML_WIDTH), lambda b, i: (b, i, 0)),
                  pl.BlockSpec((1, tm, ML_WIDTH), lambda b, i: (b, i, 0)),
                  pl.BlockSpec((1, tm, ML_WIDTH), lambda b, i: (b, i, 1)),
                  const((1, ML_WIDTH)), const((ML_WIDTH, ML_WIDTH)), const((D, D)), const((1, D)), const((1, D)),
                  const((2 * D, 2 * LANE))],
        out_specs=[pl.BlockSpec((1, tm, D), lambda b, i: (b, i, 0)),
                   pl.BlockSpec((1, tm, D), lambda b, i: (b, i, 0)),
                   pl.BlockSpec((1, N_EXPERTS, tm), lambda b, i: (b, 0, i))],
        out_shape=[jax.ShapeDtypeStruct((B, T, D), F32), jax.ShapeDtypeStruct((B, T, D), BF16),
                   jax.ShapeDtypeStruct((B, N_EXPERTS, T), F32)],
        compiler_params=_cparams(("parallel", "parallel")),
        name="outproj",
    )(x, mod[0], mod[0], mod[0], ya, yb, h_fwd, h_bwd, vo, p["ml_norm_g"], p["head_avg"], p["w_out"], p["ln1_g"], p["ln1_b"],
      p["w_router3"])


def _route_kernel(aff_ref, pos_ref, cnt_ref, *, cap, tw):
    aff = aff_ref[0]
    E, T = aff.shape
    count = lambda m: jnp.sum(jnp.where(m, 1.0, 0.0), axis=1, keepdims=True)
    as_float = lambda b: pltpu.bitcast(b, F32)
    thr = jnp.zeros((E, 1), jnp.int32)
    for bit in range(30, -1, -1):
        cand = thr | (1 << bit)
        thr = jnp.where(count(aff >= as_float(cand)) >= cap, cand, thr)
    above = aff >= as_float(thr + 1)
    tied = (aff >= as_float(thr)) & jnp.logical_not(above)
    need = cap - count(above)
    idx = lax.broadcasted_iota(jnp.int32, (E, T), 1)

    def cut_search():
        cut = jnp.zeros((E, 1), jnp.int32)
        for bit in range(T.bit_length() - 1, -1, -1):
            cand = cut | (1 << bit)
            ok = (cand <= T) & (count(tied & (idx < cand)) <= need)
            cut = jnp.where(ok, cand, cut)
        return cut

    surplus = jnp.max(jnp.where(count(tied) > need, 1.0, 0.0)) > 0.0
    cut = lax.cond(surplus, cut_search, lambda: jnp.full((E, 1), T, jnp.int32))
    sel = above | (tied & (idx < cut))
    r = lax.broadcasted_iota(jnp.int32, (tw, tw), 0)
    cidx = lax.broadcasted_iota(jnp.int32, (tw, tw), 1)
    before = jnp.where(r < cidx, 1.0, 0.0).astype(BF16)
    offset = jnp.zeros((E, 1), F32)
    lane = lax.broadcasted_iota(jnp.int32, (E, LANE), 1)
    table = jnp.zeros((E, LANE), F32)
    for blk in range(T // tw):
        cols = slice(blk * tw, (blk + 1) * tw)
        sb = jnp.where(sel[:, cols], 1.0, 0.0)
        slot = _dot(sb.astype(BF16), before) + offset
        pos_ref[0, :, cols] = jnp.where(sel[:, cols], slot, -1.0).astype(jnp.int32)
        offset = offset + jnp.sum(sb, axis=1, keepdims=True)
        table = jnp.where(lane == blk + 1, offset, table)
    cnt_ref[0] = table.astype(jnp.int32)


def _route(aff_t, cap):
    B, E, T = aff_t.shape
    nt = T // ROUTE_TILE
    pos, cnt = pl.pallas_call(
        functools.partial(_route_kernel, cap=cap, tw=ROUTE_TILE),
        grid=(B,),
        in_specs=[pl.BlockSpec((1, E, T), lambda b: (b, 0, 0))],
        out_specs=[pl.BlockSpec((1, E, T), lambda b: (b, 0, 0)), pl.BlockSpec((1, E, LANE), lambda b: (b, 0, 0))],
        out_shape=[jax.ShapeDtypeStruct((B, E, T), jnp.int32), jax.ShapeDtypeStruct((B, E, LANE), jnp.int32)],
        compiler_params=_cparams(("parallel",)),
        name="route",
    )(aff_t)
    return pos, cnt[:, :, :nt + 1].reshape(-1)


def _expert_kernel(cnt_ref, pos_ref, aff_ref, h_ref, wg_ref, wu_ref, wd_ref, y_ref, xe_ref, gate_ref,
                   wgb_ref, wub_ref, wdb_ref, *, n_win):
    nb, cap = y_ref.shape[0], y_ref.shape[2]
    nt = pos_ref.shape[2]
    rb = min(GATHER_ROWS, cap)
    e, j = pl.program_id(0), pl.program_id(1)

    @pl.when(j == 0)
    def _():
        wgb_ref[...] = wg_ref[0, 0].astype(BF16)
        wub_ref[...] = wu_ref[0, 0].astype(BF16)
        wdb_ref[...] = wd_ref[0, 0].astype(BF16)

    slot_iota = lax.broadcasted_iota(jnp.int32, (rb, ROUTE_TILE), 0)

    def gather(bb, i, n, s0):
        hits = [pos_ref[bb, 0, pl.ds(i + w, 1), :] == slot_iota + s0 for w in range(n)]
        onehot = jnp.concatenate([jnp.where(h, 1.0, 0.0).astype(BF16) for h in hits], axis=1)
        rows = pl.ds(pl.multiple_of(i * ROUTE_TILE, ROUTE_TILE), n * ROUTE_TILE)
        xe = _dot(onehot, h_ref[bb, rows, :])
        gate = sum(jnp.sum(jnp.where(h, aff_ref[bb, 0, pl.ds(i + w, 1), :], 0.0), axis=1, keepdims=True)
                   for w, h in enumerate(hits))
        return xe, gate

    blocks = []
    for bb in range(nb):
        base = ((j * nb + bb) * N_EXPERTS + e) * (nt + 1)
        for r in range(cap // rb):
            s0 = r * rb
            first = sum((cnt_ref[base + i + 1] <= s0).astype(jnp.int32) for i in range(nt))
            end = sum((cnt_ref[base + i] < s0 + rb).astype(jnp.int32) for i in range(nt))
            start = jnp.minimum(first, nt - n_win)
            rows = slice(bb * cap + s0, bb * cap + s0 + rb)
            xe, gate = gather(bb, start, n_win, s0)
            xe_ref[rows, :] = xe.astype(BF16)
            gate_ref[rows, :] = gate
            blocks.append((bb, s0, rows, start + n_win, end))

    @pl.when(functools.reduce(jnp.logical_or, [end > covered for _, _, _, covered, end in blocks]))
    def _():
        for bb, s0, rows, covered, end in blocks:
            def overflow(i, carry, bb=bb, s0=s0, rows=rows):
                xe, gate = gather(bb, i, 1, s0)
                xe_ref[rows, :] = (xe_ref[rows, :].astype(F32) + xe).astype(BF16)
                gate_ref[rows, :] += gate
                return carry

            lax.fori_loop(covered, end, overflow, 0)

    xb = xe_ref[...]
    hid = _silu(_dot(xb, wgb_ref[...])) * _dot(xb, wub_ref[...])
    y = (_dot(hid.astype(BF16), wdb_ref[...]) * gate_ref[...]).astype(BF16)
    for bb in range(nb):
        y_ref[bb, 0] = y[bb * cap:(bb + 1) * cap]


def _experts(pos, cnt, aff_t, hm, p, cap):
    B, T, D = hm.shape
    E = N_EXPERTS
    nt = T // ROUTE_TILE
    layer = p["layer"]
    nb = max(1, min(B, EXPERT_ROWS // cap))
    row = pl.BlockSpec((nb, 1, nt, ROUTE_TILE), lambda e, j, c: (j, e, 0, 0))
    wspec = lambda shape: pl.BlockSpec((1, 1) + shape, lambda e, j, c: (layer, e, 0, 0))
    return pl.pallas_call(
        functools.partial(_expert_kernel, n_win=min(GATHER_TILES, nt)),
        grid_spec=pltpu.PrefetchScalarGridSpec(
            num_scalar_prefetch=1,
            grid=(E, B // nb),
            in_specs=[row, row,
                      pl.BlockSpec((nb, T, D), lambda e, j, c: (j, 0, 0)),
                      wspec((D, EXPERT_FF)), wspec((D, EXPERT_FF)), wspec((EXPERT_FF, D))],
            out_specs=pl.BlockSpec((nb, 1, cap, D), lambda e, j, c: (j, e, 0, 0)),
            scratch_shapes=[pltpu.VMEM((nb * cap, D), BF16), pltpu.VMEM((nb * cap, 1), F32),
                            pltpu.VMEM((D, EXPERT_FF), BF16), pltpu.VMEM((D, EXPERT_FF), BF16),
                            pltpu.VMEM((EXPERT_FF, D), BF16)]),
        out_shape=jax.ShapeDtypeStruct((B, E, cap, D), BF16),
        compiler_params=_cparams(("parallel", "arbitrary")),
        name="experts",
    )(cnt, pos.reshape(B, E, nt, ROUTE_TILE), aff_t.reshape(B, E, nt, ROUTE_TILE), hm,
      p["w_gate"], p["w_up"], p["w_down"])


def _combine_kernel(cnt_ref, pos_ref, y_ref, x_ref, g2_ref, lg_ref, lb_ref, o_ref, acc_ref, *, nt):
    E, cap = y_ref.shape[1], y_ref.shape[2]
    tt = x_ref.shape[1]
    win = min(COMBINE_WIN, cap)
    per = tt // ROUTE_TILE
    b, i = pl.program_id(0), pl.program_id(1)
    slot_iota = lax.broadcasted_iota(jnp.int32, (win, tt), 0)

    def window(e, w):
        start = pl.multiple_of(jnp.minimum(w, cap - win), BF16_ROWS)
        slot = slot_iota + start
        hit = (pos_ref[0, e:e + 1, :] == slot) & (slot >= w)
        return jnp.where(hit, 1.0, 0.0).astype(BF16), y_ref[0, e, pl.ds(start, win), :]

    onehots, windows, ranges = [], [], []
    for e in range(E):
        base = (b * E + e) * (nt + 1) + i * per
        lo, hi = cnt_ref[base], cnt_ref[base + per]
        w0 = jnp.bitwise_and(lo, -BF16_ROWS)
        onehot, rows = window(e, w0)
        windows.append(rows)
        onehots.append(onehot)
        ranges.append((w0, hi))
    acc_ref[...] = _tn(jnp.concatenate(onehots, axis=0), jnp.concatenate(windows, axis=0))
    overflow = functools.reduce(jnp.logical_or, [hi > w0 + win for w0, hi in ranges])

    @pl.when(overflow)
    def _():
        for e in range(E):
            w0, hi = ranges[e]

            def more(w, e=e):
                onehot, rows = window(e, w)
                acc_ref[...] += _tn(onehot, rows)
                return w + win

            lax.while_loop(lambda w, hi=hi: w < hi, more, w0 + win)

    o_ref[0] = _plain_norm(ALPHA * x_ref[0] + g2_ref[0, 0, 0] * acc_ref[...]) * lg_ref[...] + lb_ref[...]


def _combine(pos, cnt, ye, x1, mod, p):
    B, T, D = x1.shape
    tt = min(COMBINE_TILE, T)
    E, cap = ye.shape[1], ye.shape[2]
    const = lambda shape: pl.BlockSpec(shape, lambda b, i, c: (0,) * len(shape))
    return pl.pallas_call(
        functools.partial(_combine_kernel, nt=T // ROUTE_TILE),
        grid_spec=pltpu.PrefetchScalarGridSpec(
            num_scalar_prefetch=1,
            grid=(B, T // tt),
            in_specs=[pl.BlockSpec((1, E, tt), lambda b, i, c: (b, 0, i)),
                      pl.BlockSpec((1, E, cap, D), lambda b, i, c: (b, 0, 0, 0)),
                      pl.BlockSpec((1, tt, D), lambda b, i, c: (b, i, 0)),
                      _mod_spec(mod, 5), const((1, D)), const((1, D))],
            out_specs=pl.BlockSpec((1, tt, D), lambda b, i, c: (b, i, 0)),
            scratch_shapes=[pltpu.VMEM((tt, D), F32)]),
        out_shape=jax.ShapeDtypeStruct((B, T, D), F32),
        compiler_params=_cparams(("parallel", "arbitrary")),
        name="combine",
    )(cnt, pos, ye, x1, mod[0], p["ln2_g"], p["ln2_b"])


def _pad_cols(a, before, total):
    pads = [(0, 0)] * (a.ndim - 1) + [(before, total - before - a.shape[-1])]
    return jnp.pad(a, pads)


def _prep_layer(l, w):
    D = D_MODEL
    wi, bi = w["w_in"][l], w["b_in"][l][None, :]
    n_sg, n_mla = 2 * SG_WIDTH, Q_LORA + KV_LORA + MLA_ROPE
    o_kr = n_sg + Q_LORA + KV_LORA
    o_ml = n_sg + n_mla
    o_g = o_ml + 4 * ML_WIDTH

    def cat(a):
        kr = a[:, o_kr:o_kr + MLA_ROPE]
        gates = a[:, o_g:o_g + 4 * ML_HEADS].reshape(-1, 2, 2, ML_HEADS).transpose(0, 2, 1, 3).reshape(-1, 4 * ML_HEADS)
        return jnp.concatenate([a[:, :o_kr], _pad_cols(kr, MLA_NOPE, HEAD_PAD), a[:, o_ml:o_g],
                                _pad_cols(gates, 0, LANE)], axis=1)

    wq = w["w_uq"][l].reshape(Q_LORA, MLA_HEADS, MLA_NOPE + MLA_ROPE)
    wq_main = _pad_cols(wq, 0, HEAD_PAD).reshape(Q_LORA, -1)
    wkv = w["w_ukv"][l].reshape(KV_LORA, MLA_HEADS, MLA_NOPE + MLA_VDIM)
    wk = _pad_cols(wkv[..., :MLA_NOPE], 0, HEAD_PAD).reshape(KV_LORA, -1)
    wv = _pad_cols(wkv[..., MLA_NOPE:], 0, HEAD_PAD).reshape(KV_LORA, -1)
    head = jnp.arange(ML_WIDTH) // ML_DIM
    wr = _pad_cols(w["w_router"][l], 0, LANE)
    wr_hi = wr.astype(BF16)
    return {
        "wcat": cat(wi).astype(BF16), "bcat": cat(bi),
        "q_norm_g": w["q_norm_g"][l][None, :], "kv_norm_g": w["kv_norm_g"][l][None, :],
        "wq": wq_main.astype(BF16),
        "wkv": jnp.concatenate([wk, wv], axis=1).astype(BF16),
        "sg_ln_g": w["sg_ln_g"][l][None, :], "sg_ln_b": w["sg_ln_b"][l][None, :],
        "sg_w": w["sg_w"][l].astype(BF16),
        "sg_bias": jnp.repeat(w["sg_b"][l].T, SG_GDIM, axis=1),
        "conv_w": jnp.pad(w["ml_conv_w"][l], ((0, SUBLANE - ML_CONV), (0, 0))), "conv_b": w["ml_conv_b"][l][None, :],
        "f_bias": _pad_cols(w["ml_f_bias"][l].reshape(1, 2 * ML_HEADS), 0, LANE),
        "ml_norm_g": w["ml_norm_g"][l][None, :],
        "head_avg": ((head[:, None] == head[None, :]).astype(F32) / ML_DIM).astype(BF16),
        "w_out": w["w_out"][l].astype(BF16),
        "ln1_g": w["ln1_g"][l][None, :], "ln1_b": w["ln1_b"][l][None, :],
        "w_router3": jnp.concatenate([jnp.concatenate([wr_hi, (wr - wr_hi.astype(F32)).astype(BF16)], axis=1),
                                      jnp.concatenate([wr_hi, jnp.zeros_like(wr_hi)], axis=1)], axis=0),
        "layer": l, "w_gate": w["w_gate"], "w_up": w["w_up"], "w_down": w["w_down"],
        "ln2_g": w["ln2_g"][l][None, :], "ln2_b": w["ln2_b"][l][None, :],
    }


def _rope_tables(T):
    rows = T // GRID_W
    row = jnp.repeat(jnp.arange(rows, dtype=F32), GRID_W)
    colv = jnp.tile(jnp.arange(GRID_W, dtype=F32), rows)
    inv = ROPE_BASE ** (-jnp.arange(ROPE_AXIS // 2, dtype=F32) * 2.0 / ROPE_AXIS)
    ang = jnp.concatenate([row[:, None] * inv, colv[:, None] * inv], axis=-1)
    half = ROPE_AXIS // 2
    spread = lambda t: jnp.concatenate([t[:, :half], t[:, :half], t[:, half:], t[:, half:]], axis=1)
    cosT = jnp.concatenate([jnp.ones((T, MLA_NOPE), F32), spread(jnp.cos(ang)), jnp.zeros((T, 32), F32)], axis=1)
    sign = jnp.tile(jnp.repeat(jnp.asarray([-1.0, 1.0], F32), half), 2)
    sinS = _pad_cols(spread(jnp.sin(ang)) * sign, MLA_NOPE, HEAD_PAD)
    return cosT, sinS


def _moe(x1, hm, aff_t, mod, p):
    cap = EC_CAPACITY * x1.shape[1] // N_EXPERTS
    pos, cnt = _route(aff_t, cap)
    ye = _experts(pos, cnt, aff_t, hm, p, cap)
    return _combine(pos, cnt, ye, x1, mod, p)


def _forward(x, c, ctx, c_ctx, w):
    B, T, D = x.shape
    Tc = ctx.shape[1]
    L = w["w_ada"].shape[0]
    rows = -(-(B + 1) // SUBLANE) * SUBLANE
    cc = jnp.zeros((rows, D), F32).at[:B].set(c).at[B].set(c_ctx)
    ada = _ada(cc, w["w_ada"], w["b_ada"]).reshape(L, rows, 6, 1, D)
    cos_l, sin_l = _rope_tables(T)
    cos_c = _pad_cols(jnp.ones((Tc, MLA_NOPE + MLA_ROPE), F32), 0, HEAD_PAD)
    sin_c = jnp.zeros((Tc, HEAD_PAD), F32)
    zero_c = jnp.zeros((B, 2, ML_HEADS * HEAD_PAD, HEAD_PAD), F32)
    zero_m = jnp.zeros((B, 2, 2 * ML_HEADS, min(ML_CHUNK, Tc)), F32)
    xl, xc = x, ctx
    for l in range(L):
        p = _prep_layer(l, w)
        mod_l, mod_c = (ada, l, None), (ada, l, B)
        ya_c, q_c, k_c, v_c, qk_c, vo_c, g_c = _inproj(xc, mod_c, cos_c, sin_c, p)
        ya_l, q_l, k_l, v_l, qk_l, vo_l, g_l = _inproj(xl, mod_l, cos_l, sin_l, p)
        hf_c, hb_c, cs_c, ms_c = _mlstm(qk_c, vo_c, g_c, p, zero_c, zero_m)
        hf_l, hb_l, _, _ = _mlstm(qk_l, vo_l, g_l, p, cs_c, ms_c)
        yb_l = _attention(q_l, [(k_l, v_l), (k_c, v_c)])
        x1_l, hm_l, aff_l = _outproj(xl, mod_l, ya_l, yb_l, hf_l, hb_l, vo_l, p)
        xl = _moe(x1_l, hm_l, aff_l, mod_l, p)
        if l < L - 1:
            yb_c = _attention(q_c, [(k_c, v_c)])
            x1_c, hm_c, aff_c = _outproj(xc, mod_c, ya_c, yb_c, hf_c, hb_c, vo_c, p)
            xc = _moe(x1_c, hm_c, aff_c, mod_c, p)
    return xl


def kernel(x, c, ctx, c_ctx, w_ada, b_ada, w_in, b_in, sg_ln_g, sg_ln_b, sg_w, sg_b, q_norm_g, kv_norm_g, w_uq, w_ukv,
           ml_conv_w, ml_conv_b, ml_f_bias, ml_norm_g, w_out, ln1_g, ln1_b, w_router, w_gate, w_up, w_down, ln2_g, ln2_b):
    w = dict(w_ada=w_ada, b_ada=b_ada, w_in=w_in, b_in=b_in, sg_ln_g=sg_ln_g, sg_ln_b=sg_ln_b, sg_w=sg_w, sg_b=sg_b,
             q_norm_g=q_norm_g, kv_norm_g=kv_norm_g, w_uq=w_uq, w_ukv=w_ukv, ml_conv_w=ml_conv_w, ml_conv_b=ml_conv_b,
             ml_f_bias=ml_f_bias, ml_norm_g=ml_norm_g, w_out=w_out, ln1_g=ln1_g, ln1_b=ln1_b, w_router=w_router,
             w_gate=w_gate, w_up=w_up, w_down=w_down, ln2_g=ln2_g, ln2_b=ln2_b)
    return _forward(x, c, ctx, c_ctx, w)
```

```python
import functools

import jax
import jax.numpy as jnp
from jax import lax
from jax.experimental import pallas as pl
from jax.experimental.pallas import tpu as pltpu

F32 = jnp.float32
BF16 = jnp.bfloat16
HIGHEST = lax.Precision.HIGHEST

D_MODEL = 1024
DEPTH = 2
GRID_W = 64
SG_WIDTH = 256
SG_GROUPS = 4
SG_GDIM = 64
SG_CHUNK = 128
MLA_HEADS = 8
MLA_NOPE = 64
MLA_ROPE = 32
MLA_VDIM = 64
Q_LORA = 384
KV_LORA = 256
ROPE_AXIS = 16
ROPE_BASE = 10000.0
MLA_SCALE = (MLA_NOPE + MLA_ROPE) ** -0.5
ML_HEADS = 4
ML_DIM = 64
ML_WIDTH = 256
ML_CONV = 5
N_EXPERTS = 16
EXPERT_FF = 1024
EC_CAPACITY = 2
ALPHA = (2 * DEPTH) ** 0.25
LN_EPS = 1e-6
LOG2E = 1.4426950408889634

LANE = 128
SUBLANE = 8
BF16_ROWS = 2 * SUBLANE
VMEM_BYTES_V7X = 64 * 1024 * 1024
VMEM_LIMIT = VMEM_BYTES_V7X * 7 // 8

HEAD_PAD = LANE
QK_LANES = ML_HEADS * HEAD_PAD + ML_WIDTH
OFF_SG, OFF_CQ, OFF_CKV, OFF_KR, OFF_ML, OFF_G, N_CAT = 0, 512, 896, 1152, 1280, 2304, 2432

ADA_COLS = 1536
TOKEN_TILE = 1024
ROW_CHAIN = 512
ATTN_QUERIES = 512
ATTN_HEADS = 2
ATTN_KEYS = 4096
ML_CHUNK = 256
ML_BLOCK = 2048
ROUTE_TILE = 256
EXPERT_ROWS = 512
GATHER_ROWS = 128
GATHER_TILES = 6
COMBINE_TILE = 512
COMBINE_WIN = 128


def _cparams(sem):
    return pltpu.CompilerParams(dimension_semantics=sem, vmem_limit_bytes=VMEM_LIMIT)


def _nt(a, b):
    return lax.dot_general(a, b, (((1,), (1,)), ((), ())), preferred_element_type=F32)


def _tn(a, b):
    return lax.dot_general(a, b, (((0,), (0,)), ((), ())), preferred_element_type=F32)


def _dot(a, b, precision=None):
    return jnp.dot(a, b, preferred_element_type=F32, precision=precision)


def _split(x):
    hi = x.astype(BF16)
    return hi, (x - hi.astype(F32)).astype(BF16)


def _dot2(x, w):
    hi, lo = _split(x)
    return _dot(jnp.concatenate([hi, lo], axis=1), jnp.concatenate([w, w], axis=0))


def _plain_norm(x):
    mu = jnp.mean(x, axis=-1, keepdims=True)
    xc = x - mu
    var = jnp.mean(xc * xc, axis=-1, keepdims=True)
    return xc * lax.rsqrt(var + LN_EPS)


def _rms(x):
    return x * lax.rsqrt(jnp.mean(x * x, axis=-1, keepdims=True) + LN_EPS)


def _silu(x):
    return x * (1.0 / (1.0 + jnp.exp(-x)))


def _gelu_tanh(x):
    return 0.5 * x * (1.0 + jnp.tanh(0.7978845608028654 * (x + 0.044715 * (x * x * x))))


def _log_sigmoid(x):
    return jnp.minimum(x, 0.0) - jnp.log(1.0 + jnp.exp(-jnp.abs(x)))


def _ada_kernel(c_ref, w_ref, b_ref, o_ref):
    o_ref[0] = _dot(_silu(c_ref[...]), w_ref[0], precision=HIGHEST) + b_ref[0]


def _ada(cc, w_ada, b_ada):
    L, D, N = w_ada.shape
    rows = cc.shape[0]
    tn = ADA_COLS
    return pl.pallas_call(
        _ada_kernel,
        grid=(L, N // tn),
        in_specs=[pl.BlockSpec((rows, D), lambda l, j: (0, 0)),
                  pl.BlockSpec((1, D, tn), lambda l, j: (l, 0, j)),
                  pl.BlockSpec((1, 1, tn), lambda l, j: (l, 0, j))],
        out_specs=pl.BlockSpec((1, rows, tn), lambda l, j: (l, 0, j)),
        out_shape=jax.ShapeDtypeStruct((L, rows, N), F32),
        compiler_params=_cparams(("parallel", "parallel")),
        name="ada",
    )(cc, w_ada, b_ada.reshape(L, 1, N))


def _inproj_kernel(x_ref, xp_ref, xn_ref, sh_ref, sc_ref, cos_ref, sin_ref, wcat_ref, bcat_ref, qg_ref, kvg_ref, wq_ref,
                   wkv_ref, sglg_ref, sglb_ref, sgw_ref, sgb_ref, cw_ref, cb_ref,
                   ya_ref, q_ref, k_ref, v_ref, qk_ref, vo_ref, g_ref):
    tm = x_ref.shape[1]
    modulated = lambda t: (_plain_norm(t) * (1.0 + sc_ref[0, 0, 0]) + sh_ref[0, 0, 0]).astype(BF16)
    z = _dot(modulated(x_ref[0]), wcat_ref[...]) + bcat_ref[...]
    i = pl.program_id(1)
    qk_cols = slice(OFF_ML, OFF_ML + 2 * ML_WIDTH)
    halo = _dot(modulated(jnp.concatenate([xp_ref[0], xn_ref[0]], axis=0)), wcat_ref[:, qk_cols]) + bcat_ref[:, qk_cols]
    prev = jnp.where(i > 0, halo[:SUBLANE], 0.0)
    nxt = jnp.where(i < pl.num_programs(1) - 1, halo[SUBLANE:], 0.0)
    big = jnp.concatenate([prev, z[:, qk_cols], nxt], axis=0)
    n_big = tm + 2 * SUBLANE
    pad = (ML_CONV - 1) // 2
    conv = cb_ref[...]
    for t in range(ML_CONV):
        s = t - pad
        shifted = big if s == 0 else pltpu.roll(big, (-s) % n_big, axis=0)
        conv = conv + shifted[SUBLANE:SUBLANE + tm] * cw_ref[t:t + 1, :]
    qkc = _silu(conv)
    hlane = lax.broadcasted_iota(jnp.int32, (1, HEAD_PAD), 1)
    for hd in range(ML_HEADS):
        q_blk = qkc[:, (hd // 2) * HEAD_PAD:(hd // 2 + 1) * HEAD_PAD]
        in_half = (hlane >= ML_DIM) if hd % 2 else (hlane < ML_DIM)
        qk_ref[0, :, hd * HEAD_PAD:(hd + 1) * HEAD_PAD] = jnp.where(in_half, q_blk, 0.0).astype(BF16)
    qk_ref[0, :, ML_HEADS * HEAD_PAD:] = (qkc[:, ML_WIDTH:] * (ML_DIM ** -0.5)).astype(BF16)
    zs = _gelu_tanh(z[:, OFF_SG:OFF_SG + 2 * SG_WIDTH])
    u = zs[:, :SG_WIDTH]
    vn = _plain_norm(zs[:, SG_WIDTH:]) * sglg_ref[...] + sglb_ref[...]
    group = jnp.right_shift(lax.broadcasted_iota(jnp.int32, (SG_CHUNK, SG_WIDTH), 1), SG_GDIM.bit_length() - 1)
    for ch in range(tm // SG_CHUNK):
        rows = slice(ch * SG_CHUNK, (ch + 1) * SG_CHUNK)
        vc = vn[rows].astype(BF16)
        mixed = sgb_ref[...]
        for g in range(SG_GROUPS):
            mixed = mixed + jnp.where(group == g, _dot(sgw_ref[g], vc), 0.0)
        ya_ref[0, rows, :] = (u[rows] * mixed).astype(BF16)
    cosT = cos_ref[...]
    sinS = sin_ref[...]
    lane = lax.broadcasted_iota(jnp.int32, (1, HEAD_PAD), 1)
    first = jnp.bitwise_and(lane, ROPE_AXIS // 2) == 0

    def rope(t):
        partner = jnp.where(first, pltpu.roll(t, HEAD_PAD - ROPE_AXIS // 2, axis=1), pltpu.roll(t, ROPE_AXIS // 2, axis=1))
        return t * cosT + partner * sinS

    cqn = (_rms(z[:, OFF_CQ:OFF_CQ + Q_LORA]) * qg_ref[...]).astype(BF16)
    qq = _dot(cqn, wq_ref[...])
    ckvn = (_rms(z[:, OFF_CKV:OFF_CKV + KV_LORA]) * kvg_ref[...]).astype(BF16)
    kv = _dot(ckvn, wkv_ref[...])
    kr = rope(z[:, OFF_KR:OFF_KR + HEAD_PAD])
    one_lane = jnp.where(lane == MLA_VDIM, 1.0, 0.0)
    hw = MLA_HEADS * HEAD_PAD
    for hd in range(MLA_HEADS):
        cols = slice(hd * HEAD_PAD, (hd + 1) * HEAD_PAD)
        rcols = slice(hw + hd * HEAD_PAD, hw + (hd + 1) * HEAD_PAD)
        q_ref[0, hd] = (rope(qq[:, cols]) * (MLA_SCALE * LOG2E)).astype(BF16)
        k_ref[0, hd] = (kv[:, cols] + kr).astype(BF16)
        v_ref[0, hd] = (kv[:, rcols] + one_lane).astype(BF16)
    vo_ref[0] = z[:, OFF_ML + 2 * ML_WIDTH:OFF_ML + 4 * ML_WIDTH]
    g_ref[0] = z[:, OFF_G:OFF_G + LANE]


def _mod_spec(mod, k):
    _, layer, row = mod
    return pl.BlockSpec((1, 1, 1, 1, D_MODEL), lambda b, *_: (layer, b if row is None else row, k, 0, 0))


def _inproj(x, mod, cosT, sinT, p):
    B, T, D = x.shape
    tm = min(TOKEN_TILE, T)
    hb = tm // SUBLANE
    const = lambda shape: pl.BlockSpec(shape, lambda b, i: (0,) * len(shape))
    heads = (B, MLA_HEADS, T, HEAD_PAD)
    return pl.pallas_call(
        _inproj_kernel,
        grid=(B, T // tm),
        in_specs=[pl.BlockSpec((1, tm, D), lambda b, i: (b, i, 0)),
                  pl.BlockSpec((1, SUBLANE, D), lambda b, i: (b, jnp.maximum(i * hb - 1, 0), 0)),
                  pl.BlockSpec((1, SUBLANE, D), lambda b, i: (b, jnp.minimum((i + 1) * hb, T // SUBLANE - 1), 0)),
                  _mod_spec(mod, 0), _mod_spec(mod, 1),
                  pl.BlockSpec((tm, HEAD_PAD), lambda b, i: (i, 0)),
                  pl.BlockSpec((tm, HEAD_PAD), lambda b, i: (i, 0)),
                  const((D, N_CAT)), const((1, N_CAT)), const((1, Q_LORA)), const((1, KV_LORA)),
                  const((Q_LORA, MLA_HEADS * HEAD_PAD)), const((KV_LORA, 2 * MLA_HEADS * HEAD_PAD)),
                  const((1, SG_WIDTH)), const((1, SG_WIDTH)), const((SG_GROUPS, SG_CHUNK, SG_CHUNK)),
                  const((SG_CHUNK, SG_WIDTH)), const((SUBLANE, 2 * ML_WIDTH)), const((1, 2 * ML_WIDTH))],
        out_specs=[pl.BlockSpec((1, tm, SG_WIDTH), lambda b, i: (b, i, 0)),
                   pl.BlockSpec((1, MLA_HEADS, tm, HEAD_PAD), lambda b, i: (b, 0, i, 0)),
                   pl.BlockSpec((1, MLA_HEADS, tm, HEAD_PAD), lambda b, i: (b, 0, i, 0)),
                   pl.BlockSpec((1, MLA_HEADS, tm, HEAD_PAD), lambda b, i: (b, 0, i, 0)),
                   pl.BlockSpec((1, tm, QK_LANES), lambda b, i: (b, i, 0)),
                   pl.BlockSpec((1, tm, 2 * ML_WIDTH), lambda b, i: (b, i, 0)),
                   pl.BlockSpec((1, tm, LANE), lambda b, i: (b, i, 0))],
        out_shape=[jax.ShapeDtypeStruct((B, T, SG_WIDTH), BF16),
                   jax.ShapeDtypeStruct(heads, BF16), jax.ShapeDtypeStruct(heads, BF16),
                   jax.ShapeDtypeStruct(heads, BF16),
                   jax.ShapeDtypeStruct((B, T, QK_LANES), BF16),
                   jax.ShapeDtypeStruct((B, T, 2 * ML_WIDTH), F32),
                   jax.ShapeDtypeStruct((B, T, LANE), F32)],
        compiler_params=_cparams(("parallel", "parallel")),
        name="inproj",
    )(x, x, x, mod[0], mod[0], cosT, sinT, p["wcat"], p["bcat"], p["q_norm_g"], p["kv_norm_g"], p["wq"], p["wkv"],
      p["sg_ln_g"], p["sg_ln_b"], p["sg_w"], p["sg_bias"], p["conv_w"], p["conv_b"])


def _attn_kernel(*refs, n_src, tk):
    q_ref = refs[0]
    kv_refs = refs[1:1 + 2 * n_src]
    o_ref = refs[1 + 2 * n_src]
    nh, tq = q_ref.shape[1], q_ref.shape[2]
    qs = [q_ref[0, hh] for hh in range(nh)]
    ms = [jnp.full((tq, 1), -jnp.inf, F32) for _ in range(nh)]
    accs = [jnp.zeros((tq, HEAD_PAD), F32) for _ in range(nh)]
    for s in range(n_src):
        k_ref, v_ref = kv_refs[2 * s], kv_refs[2 * s + 1]
        ck = min(tk, k_ref.shape[2])
        for c in range(k_ref.shape[2] // ck):
            rows = slice(c * ck, (c + 1) * ck)
            for hh in range(nh):
                sc = _nt(qs[hh], k_ref[0, hh, rows, :])
                m_new = jnp.maximum(ms[hh], jnp.max(sc, axis=1, keepdims=True))
                pr = jnp.exp2(sc - m_new)
                accs[hh] = accs[hh] * jnp.exp2(ms[hh] - m_new) + _dot(pr.astype(BF16), v_ref[0, hh, rows, :])
                ms[hh] = m_new
    outs = [accs[hh] / accs[hh][:, MLA_VDIM:MLA_VDIM + 1] for hh in range(nh)]
    lane = lax.broadcasted_iota(jnp.int32, (tq, HEAD_PAD), 1)
    for pair in range(nh // 2):
        both = jnp.where(lane < MLA_VDIM, outs[2 * pair], pltpu.roll(outs[2 * pair + 1], MLA_VDIM, axis=1))
        o_ref[0, :, pair * HEAD_PAD:(pair + 1) * HEAD_PAD] = both.astype(BF16)


def _attention(q, kvs):
    B, H, T, _ = q.shape
    tq, tk, nh = min(ATTN_QUERIES, T), ATTN_KEYS, ATTN_HEADS
    in_specs = [pl.BlockSpec((1, nh, tq, HEAD_PAD), lambda b, hp, i: (b, hp, i, 0))]
    args = [q]
    for k, v in kvs:
        K = k.shape[2]
        in_specs += [pl.BlockSpec((1, nh, K, HEAD_PAD), lambda b, hp, i: (b, hp, 0, 0))] * 2
        args += [k, v]
    return pl.pallas_call(
        functools.partial(_attn_kernel, n_src=len(kvs), tk=tk),
        grid=(B, H // nh, T // tq),
        in_specs=in_specs,
        out_specs=pl.BlockSpec((1, tq, nh * MLA_VDIM), lambda b, hp, i: (b, i, hp)),
        out_shape=jax.ShapeDtypeStruct((B, T, H * MLA_VDIM), BF16),
        compiler_params=_cparams(("parallel", "parallel", "parallel")),
        name="attention",
    )(*args)


def _mlstm_direction(d, qk, vv, gl, fb, c_prev, m_prev):
    L = qk.shape[0]
    lf = _log_sigmoid(pltpu.roll(gl, LANE - 2 * ML_HEADS, axis=1) + fb)
    ks = lax.broadcasted_iota(jnp.int32, (L, L), 0)
    qt = lax.broadcasted_iota(jnp.int32, (L, L), 1)
    visible = (ks >= qt) if d else (ks <= qt)
    tri = jnp.where((qt >= ks) if d else (qt <= ks), 1.0, 0.0).astype(BF16)
    lf_hi, lf_lo = _split(lf)
    bc = _dot(tri, lf_hi) + _dot(tri, lf_lo)
    r_cols = gl - bc
    b8 = bc.T[0:2 * ML_HEADS]
    li8 = gl.T[0:2 * ML_HEADS]
    b_end = b8[:, 0:1] if d else b8[:, L - 1:L]
    gw = b_end - b8 + li8
    m_loc = jnp.max(gw, axis=1, keepdims=True)
    w8 = jnp.exp(gw - m_loc)
    m_next = jnp.maximum(b_end + m_prev, m_loc)
    a_sc = jnp.exp(b_end + m_prev - m_next)
    s_sc = jnp.exp(m_loc - m_next)
    a8 = b8 + m_prev

    srow = lax.broadcasted_iota(jnp.int32, (HEAD_PAD, L), 0)
    h_blocks, c_next = [], []
    for blk in range(ML_HEADS // 2):
        km = qk[:, (ML_HEADS + blk) * HEAD_PAD:(ML_HEADS + blk + 1) * HEAD_PAD]
        v_t = vv[:, blk * HEAD_PAD:(blk + 1) * HEAD_PAD].T
        hh = []
        for half in range(2):
            hd = 2 * blk + half
            g = d * ML_HEADS + hd
            in_half = (srow >= ML_DIM) if half else (srow < ML_DIM)
            den_row = 0 if half else ML_DIM
            qm = qk[:, hd * HEAD_PAD:(hd + 1) * HEAD_PAD]
            vaug_t = jnp.where(in_half, v_t, jnp.where(srow == den_row, 1.0, 0.0))
            dm = jnp.where(visible, r_cols[:, g:g + 1] + b8[g:g + 1, :], -jnp.inf)
            a_row = a8[g:g + 1, :]
            m_t = jnp.maximum(a_row, jnp.max(dm, axis=0, keepdims=True))
            pm = jnp.exp(dm - m_t) * _nt(km, qm)
            c0 = c_prev[hd]
            nd = (jnp.exp(a_row - m_t) * _nt(c0.astype(BF16), qm)
                  + _dot(vaug_t.astype(BF16), pm.astype(BF16)))
            den = nd[den_row:den_row + 1, :]
            hh.append(nd / jnp.maximum(jnp.abs(den), jnp.exp(-m_t)))
            c_loc = _dot((vaug_t * w8[g:g + 1, :]).astype(BF16), km)
            c_next.append(a_sc[g:g + 1, 0:HEAD_PAD] * c0 + s_sc[g:g + 1, 0:HEAD_PAD] * c_loc)
        h_blocks.append(jnp.where(srow < ML_DIM, hh[0], hh[1]).T)
    grow = lax.broadcasted_iota(jnp.int32, (2 * ML_HEADS, L), 0)
    mine = (grow >= d * ML_HEADS) & (grow < (d + 1) * ML_HEADS)
    return h_blocks, c_next, jnp.where(mine, m_next, m_prev)


def _mlstm_kernel(qkf_ref, vf_ref, gf_ref, qkb_ref, vb_ref, gb_ref, fb_ref, ci_ref, mi_ref,
                  hf_ref, hb_ref, cs_ref, ms_ref):
    @pl.when(pl.program_id(1) == 0)
    def _():
        cs_ref[...] = ci_ref[...]
        ms_ref[...] = mi_ref[...]

    rows = lambda hd: slice(hd * HEAD_PAD, (hd + 1) * HEAD_PAD)
    ins = ((qkf_ref, vf_ref, gf_ref), (qkb_ref, vb_ref, gb_ref))
    outs = (hf_ref, hb_ref)
    L = ms_ref.shape[3]
    n_sub = qkf_ref.shape[1] // L
    states = [([cs_ref[0, d, rows(hd), :] for hd in range(ML_HEADS)], ms_ref[0, d]) for d in range(2)]
    for d in range(2):
        qk_ref, v_ref, g_ref = ins[d]
        c_state, m_state = states[d]
        for s in (range(n_sub - 1, -1, -1) if d else range(n_sub)):
            tok = slice(s * L, (s + 1) * L)
            h_blocks, c_state, m_state = _mlstm_direction(d, qk_ref[0, tok, :], v_ref[0, tok, :], g_ref[0, tok, :],
                                                          fb_ref[...], c_state, m_state)
            for blk, hb in enumerate(h_blocks):
                outs[d][0, tok, blk * HEAD_PAD:(blk + 1) * HEAD_PAD] = hb
        states[d] = (c_state, m_state)
    for d in range(2):
        for hd in range(ML_HEADS):
            cs_ref[0, d, rows(hd), :] = states[d][0][hd]
        ms_ref[0, d] = states[d][1]


def _mlstm(qk, vo, gates, p, c_init, m_init):
    B, T, _ = vo.shape
    chunk = min(ML_CHUNK, T)
    L = min(ML_BLOCK, T)
    nc = T // L
    const = lambda shape: pl.BlockSpec(shape, lambda b, j: (0,) * len(shape))
    state = pl.BlockSpec((1, 2, ML_HEADS * HEAD_PAD, HEAD_PAD), lambda b, j: (b, 0, 0, 0))
    mstate = pl.BlockSpec((1, 2, 2 * ML_HEADS, chunk), lambda b, j: (b, 0, 0, 0))
    fwd = lambda b, j: (b, j, 0)
    bwd = lambda b, j: (b, nc - 1 - j, 0)
    return pl.pallas_call(
        _mlstm_kernel,
        grid=(B, nc),
        in_specs=[pl.BlockSpec((1, L, QK_LANES), fwd),
                  pl.BlockSpec((1, L, ML_WIDTH), fwd),
                  pl.BlockSpec((1, L, LANE), fwd),
                  pl.BlockSpec((1, L, QK_LANES), bwd),
                  pl.BlockSpec((1, L, ML_WIDTH), bwd),
                  pl.BlockSpec((1, L, LANE), bwd),
                  const((1, LANE)), state, mstate],
        out_specs=[pl.BlockSpec((1, L, ML_WIDTH), fwd), pl.BlockSpec((1, L, ML_WIDTH), bwd), state, mstate],
        out_shape=[jax.ShapeDtypeStruct((B, T, ML_WIDTH), F32), jax.ShapeDtypeStruct((B, T, ML_WIDTH), F32),
                   jax.ShapeDtypeStruct((B, 2, ML_HEADS * HEAD_PAD, HEAD_PAD), F32),
                   jax.ShapeDtypeStruct((B, 2, 2 * ML_HEADS, chunk), F32)],
        compiler_params=_cparams(("parallel", "arbitrary")),
        name="mlstm",
    )(qk, vo, gates, qk, vo, gates, p["f_bias"], c_init, m_init)


def _outproj_kernel(x_ref, g1_ref, sh2_ref, sc2_ref, ya_ref, yb_ref, h0_ref, h1_ref, o_ref, mg_ref, avg_ref,
                    wo_ref, l1g_ref, l1b_ref, wr_ref, x1_ref, hm_ref, aff_ref):
    tm = x_ref.shape[1]
    sub = min(ROW_CHAIN, tm)
    for r in range(tm // sub):
        rows = slice(r * sub, (r + 1) * sub)
        hs = h0_ref[0, rows, :] + h1_ref[0, rows, :]
        mu = _dot2(hs, avg_ref[...])
        hc = hs - mu
        var = _dot2(hc * hc, avg_ref[...])
        yc = (1.0 / (1.0 + jnp.exp(-o_ref[0, rows, :]))) * (hc * lax.rsqrt(var + LN_EPS) * mg_ref[...])
        y = _dot(jnp.concatenate([ya_ref[0, rows, :], yb_ref[0, rows, :], yc.astype(BF16)], axis=1), wo_ref[...])
        x1 = _plain_norm(ALPHA * x_ref[0, rows, :] + g1_ref[0, 0, 0] * y) * l1g_ref[...] + l1b_ref[...]
        x1_ref[0, rows, :] = x1
        hm = _plain_norm(x1) * (1.0 + sc2_ref[0, 0, 0]) + sh2_ref[0, 0, 0]
        hm_ref[0, rows, :] = hm.astype(BF16)
        hi, lo = _split(hm)
        parts = _dot(jnp.concatenate([hi, lo], axis=1), wr_ref[...])
        logits = (parts[:, :LANE] + parts[:, LANE:]).T[:N_EXPERTS, :]
        ex = jnp.exp(logits - jnp.max(logits, axis=0, keepdims=True))
        aff_ref[0, :, rows] = ex / jnp.sum(ex, axis=0, keepdims=True)


def _outproj(x, mod, ya, yb, h_fwd, h_bwd, vo, p):
    B, T, D = x.shape
    tm = min(TOKEN_TILE, T)
    const = lambda shape: pl.BlockSpec(shape, lambda b, i: (0,) * len(shape))
    return pl.pallas_call(
        _outproj_kernel,
        grid=(B, T // tm),
        in_specs=[pl.BlockSpec((1, tm, D), lambda b, i: (b, i, 0)),
                  _mod_spec(mod, 2), _mod_spec(mod, 3), _mod_spec(mod, 4),
                  pl.BlockSpec((1, tm, SG_WIDTH), lambda b, i: (b, i, 0)),
                  pl.BlockSpec((1, tm, MLA_HEADS * MLA_VDIM), lambda b, i: (b, i, 0)),
                  pl.BlockSpec((1, tm, ML_WIDTH), lambda b, i: (b, i, 0)),
                  pl.BlockSpec((1, tm, ML_WIDTH), lambda b, i: (b, i, 0)),
                  pl.BlockSpec((1, tm, ML_WIDTH), lambda b, i: (b, i, 1)),
                  const((1, ML_WIDTH)), const((ML_WIDTH, ML_WIDTH)), const((D, D)), const((1, D)), const((1, D)),
                  const((2 * D, 2 * LANE))],
        out_specs=[pl.BlockSpec((1, tm, D), lambda b, i: (b, i, 0)),
                   pl.BlockSpec((1, tm, D), lambda b, i: (b, i, 0)),
                   pl.BlockSpec((1, N_EXPERTS, tm), lambda b, i: (b, 0, i))],
        out_shape=[jax.ShapeDtypeStruct((B, T, D), F32), jax.ShapeDtypeStruct((B, T, D), BF16),
                   jax.ShapeDtypeStruct((B, N_EXPERTS, T), F32)],
        compiler_params=_cparams(("parallel", "parallel")),
        name="outproj",
    )(x, mod[0], mod[0], mod[0], ya, yb, h_fwd, h_bwd, vo, p["ml_norm_g"], p["head_avg"], p["w_out"], p["ln1_g"], p["ln1_b"],
      p["w_router3"])


def _route_kernel(aff_ref, pos_ref, cnt_ref, *, cap, tw):
    aff = aff_ref[0]
    E, T = aff.shape
    count = lambda m: jnp.sum(jnp.where(m, 1.0, 0.0), axis=1, keepdims=True)
    as_float = lambda b: pltpu.bitcast(b, F32)
    thr = jnp.zeros((E, 1), jnp.int32)
    for bit in range(30, -1, -1):
        cand = thr | (1 << bit)
        thr = jnp.where(count(aff >= as_float(cand)) >= cap, cand, thr)
    above = aff >= as_float(thr + 1)
    tied = (aff >= as_float(thr)) & jnp.logical_not(above)
    need = cap - count(above)
    idx = lax.broadcasted_iota(jnp.int32, (E, T), 1)

    def cut_search():
        cut = jnp.zeros((E, 1), jnp.int32)
        for bit in range(T.bit_length() - 1, -1, -1):
            cand = cut | (1 << bit)
            ok = (cand <= T) & (count(tied & (idx < cand)) <= need)
            cut = jnp.where(ok, cand, cut)
        return cut

    surplus = jnp.max(jnp.where(count(tied) > need, 1.0, 0.0)) > 0.0
    cut = lax.cond(surplus, cut_search, lambda: jnp.full((E, 1), T, jnp.int32))
    sel = above | (tied & (idx < cut))
    r = lax.broadcasted_iota(jnp.int32, (tw, tw), 0)
    cidx = lax.broadcasted_iota(jnp.int32, (tw, tw), 1)
    before = jnp.where(r < cidx, 1.0, 0.0).astype(BF16)
    offset = jnp.zeros((E, 1), F32)
    lane = lax.broadcasted_iota(jnp.int32, (E, LANE), 1)
    table = jnp.zeros((E, LANE), F32)
    for blk in range(T // tw):
        cols = slice(blk * tw, (blk + 1) * tw)
        sb = jnp.where(sel[:, cols], 1.0, 0.0)
        slot = _dot(sb.astype(BF16), before) + offset
        pos_ref[0, :, cols] = jnp.where(sel[:, cols], slot, -1.0).astype(jnp.int32)
        offset = offset + jnp.sum(sb, axis=1, keepdims=True)
        table = jnp.where(lane == blk + 1, offset, table)
    cnt_ref[0] = table.astype(jnp.int32)


def _route(aff_t, cap):
    B, E, T = aff_t.shape
    nt = T // ROUTE_TILE
    pos, cnt = pl.pallas_call(
        functools.partial(_route_kernel, cap=cap, tw=ROUTE_TILE),
        grid=(B,),
        in_specs=[pl.BlockSpec((1, E, T), lambda b: (b, 0, 0))],
        out_specs=[pl.BlockSpec((1, E, T), lambda b: (b, 0, 0)), pl.BlockSpec((1, E, LANE), lambda b: (b, 0, 0))],
        out_shape=[jax.ShapeDtypeStruct((B, E, T), jnp.int32), jax.ShapeDtypeStruct((B, E, LANE), jnp.int32)],
        compiler_params=_cparams(("parallel",)),
        name="route",
    )(aff_t)
    return pos, cnt[:, :, :nt + 1].reshape(-1)


def _expert_kernel(cnt_ref, pos_ref, aff_ref, h_ref, wg_ref, wu_ref, wd_ref, y_ref, xe_ref, gate_ref,
                   wgb_ref, wub_ref, wdb_ref, *, n_win):
    nb, cap = y_ref.shape[0], y_ref.shape[2]
    nt = pos_ref.shape[2]
    rb = min(GATHER_ROWS, cap)
    e, j = pl.program_id(0), pl.program_id(1)

    @pl.when(j == 0)
    def _():
        wgb_ref[...] = wg_ref[0, 0].astype(BF16)
        wub_ref[...] = wu_ref[0, 0].astype(BF16)
        wdb_ref[...] = wd_ref[0, 0].astype(BF16)

    slot_iota = lax.broadcasted_iota(jnp.int32, (rb, ROUTE_TILE), 0)

    def gather(bb, i, n, s0):
        hits = [pos_ref[bb, 0, pl.ds(i + w, 1), :] == slot_iota + s0 for w in range(n)]
        onehot = jnp.concatenate([jnp.where(h, 1.0, 0.0).astype(BF16) for h in hits], axis=1)
        rows = pl.ds(pl.multiple_of(i * ROUTE_TILE, ROUTE_TILE), n * ROUTE_TILE)
        xe = _dot(onehot, h_ref[bb, rows, :])
        gate = sum(jnp.sum(jnp.where(h, aff_ref[bb, 0, pl.ds(i + w, 1), :], 0.0), axis=1, keepdims=True)
                   for w, h in enumerate(hits))
        return xe, gate

    blocks = []
    for bb in range(nb):
        base = ((j * nb + bb) * N_EXPERTS + e) * (nt + 1)
        for r in range(cap // rb):
            s0 = r * rb
            first = sum((cnt_ref[base + i + 1] <= s0).astype(jnp.int32) for i in range(nt))
            end = sum((cnt_ref[base + i] < s0 + rb).astype(jnp.int32) for i in range(nt))
            start = jnp.minimum(first, nt - n_win)
            rows = slice(bb * cap + s0, bb * cap + s0 + rb)
            xe, gate = gather(bb, start, n_win, s0)
            xe_ref[rows, :] = xe.astype(BF16)
            gate_ref[rows, :] = gate
            blocks.append((bb, s0, rows, start + n_win, end))

    @pl.when(functools.reduce(jnp.logical_or, [end > covered for _, _, _, covered, end in blocks]))
    def _():
        for bb, s0, rows, covered, end in blocks:
            def overflow(i, carry, bb=bb, s0=s0, rows=rows):
                xe, gate = gather(bb, i, 1, s0)
                xe_ref[rows, :] = (xe_ref[rows, :].astype(F32) + xe).astype(BF16)
                gate_ref[rows, :] += gate
                return carry

            lax.fori_loop(covered, end, overflow, 0)

    xb = xe_ref[...]
    hid = _silu(_dot(xb, wgb_ref[...])) * _dot(xb, wub_ref[...])
    y = (_dot(hid.astype(BF16), wdb_ref[...]) * gate_ref[...]).astype(BF16)
    for bb in range(nb):
        y_ref[bb, 0] = y[bb * cap:(bb + 1) * cap]


def _experts(pos, cnt, aff_t, hm, p, cap):
    B, T, D = hm.shape
    E = N_EXPERTS
    nt = T // ROUTE_TILE
    layer = p["layer"]
    nb = max(1, min(B, EXPERT_ROWS // cap))
    row = pl.BlockSpec((nb, 1, nt, ROUTE_TILE), lambda e, j, c: (j, e, 0, 0))
    wspec = lambda shape: pl.BlockSpec((1, 1) + shape, lambda e, j, c: (layer, e, 0, 0))
    return pl.pallas_call(
        functools.partial(_expert_kernel, n_win=min(GATHER_TILES, nt)),
        grid_spec=pltpu.PrefetchScalarGridSpec(
            num_scalar_prefetch=1,
            grid=(E, B // nb),
            in_specs=[row, row,
                      pl.BlockSpec((nb, T, D), lambda e, j, c: (j, 0, 0)),
                      wspec((D, EXPERT_FF)), wspec((D, EXPERT_FF)), wspec((EXPERT_FF, D))],
            out_specs=pl.BlockSpec((nb, 1, cap, D), lambda e, j, c: (j, e, 0, 0)),
            scratch_shapes=[pltpu.VMEM((nb * cap, D), BF16), pltpu.VMEM((nb * cap, 1), F32),
                            pltpu.VMEM((D, EXPERT_FF), BF16), pltpu.VMEM((D, EXPERT_FF), BF16),
                            pltpu.VMEM((EXPERT_FF, D), BF16)]),
        out_shape=jax.ShapeDtypeStruct((B, E, cap, D), BF16),
        compiler_params=_cparams(("parallel", "arbitrary")),
        name="experts",
    )(cnt, pos.reshape(B, E, nt, ROUTE_TILE), aff_t.reshape(B, E, nt, ROUTE_TILE), hm,
      p["w_gate"], p["w_up"], p["w_down"])


def _combine_kernel(cnt_ref, pos_ref, y_ref, x_ref, g2_ref, lg_ref, lb_ref, o_ref, acc_ref, *, nt):
    E, cap = y_ref.shape[1], y_ref.shape[2]
    tt = x_ref.shape[1]
    win = min(COMBINE_WIN, cap)
    per = tt // ROUTE_TILE
    b, i = pl.program_id(0), pl.program_id(1)
    slot_iota = lax.broadcasted_iota(jnp.int32, (win, tt), 0)

    def window(e, w):
        start = pl.multiple_of(jnp.minimum(w, cap - win), BF16_ROWS)
        slot = slot_iota + start
        hit = (pos_ref[0, e:e + 1, :] == slot) & (slot >= w)
        return jnp.where(hit, 1.0, 0.0).astype(BF16), y_ref[0, e, pl.ds(start, win), :]

    onehots, windows, ranges = [], [], []
    for e in range(E):
        base = (b * E + e) * (nt + 1) + i * per
        lo, hi = cnt_ref[base], cnt_ref[base + per]
        w0 = jnp.bitwise_and(lo, -BF16_ROWS)
        onehot, rows = window(e, w0)
        windows.append(rows)
        onehots.append(onehot)
        ranges.append((w0, hi))
    acc_ref[...] = _tn(jnp.concatenate(onehots, axis=0), jnp.concatenate(windows, axis=0))
    overflow = functools.reduce(jnp.logical_or, [hi > w0 + win for w0, hi in ranges])

    @pl.when(overflow)
    def _():
        for e in range(E):
            w0, hi = ranges[e]

            def more(w, e=e):
                onehot, rows = window(e, w)
                acc_ref[...] += _tn(onehot, rows)
                return w + win

            lax.while_loop(lambda w, hi=hi: w < hi, more, w0 + win)

    o_ref[0] = _plain_norm(ALPHA * x_ref[0] + g2_ref[0, 0, 0] * acc_ref[...]) * lg_ref[...] + lb_ref[...]


def _combine(pos, cnt, ye, x1, mod, p):
    B, T, D = x1.shape
    tt = min(COMBINE_TILE, T)
    E, cap = ye.shape[1], ye.shape[2]
    const = lambda shape: pl.BlockSpec(shape, lambda b, i, c: (0,) * len(shape))
    return pl.pallas_call(
        functools.partial(_combine_kernel, nt=T // ROUTE_TILE),
        grid_spec=pltpu.PrefetchScalarGridSpec(
            num_scalar_prefetch=1,
            grid=(B, T // tt),
            in_specs=[pl.BlockSpec((1, E, tt), lambda b, i, c: (b, 0, i)),
                      pl.BlockSpec((1, E, cap, D), lambda b, i, c: (b, 0, 0, 0)),
                      pl.BlockSpec((1, tt, D), lambda b, i, c: (b, i, 0)),
                      _mod_spec(mod, 5), const((1, D)), const((1, D))],
            out_specs=pl.BlockSpec((1, tt, D), lambda b, i, c: (b, i, 0)),
            scratch_shapes=[pltpu.VMEM((tt, D), F32)]),
        out_shape=jax.ShapeDtypeStruct((B, T, D), F32),
        compiler_params=_cparams(("parallel", "arbitrary")),
        name="combine",
    )(cnt, pos, ye, x1, mod[0], p["ln2_g"], p["ln2_b"])


def _pad_cols(a, before, total):
    pads = [(0, 0)] * (a.ndim - 1) + [(before, total - before - a.shape[-1])]
    return jnp.pad(a, pads)


def _prep_layer(l, w):
    D = D_MODEL
    wi, bi = w["w_in"][l], w["b_in"][l][None, :]
    n_sg, n_mla = 2 * SG_WIDTH, Q_LORA + KV_LORA + MLA_ROPE
    o_kr = n_sg + Q_LORA + KV_LORA
    o_ml = n_sg + n_mla
    o_g = o_ml + 4 * ML_WIDTH

    def cat(a):
        kr = a[:, o_kr:o_kr + MLA_ROPE]
        gates = a[:, o_g:o_g + 4 * ML_HEADS].reshape(-1, 2, 2, ML_HEADS).transpose(0, 2, 1, 3).reshape(-1, 4 * ML_HEADS)
        return jnp.concatenate([a[:, :o_kr], _pad_cols(kr, MLA_NOPE, HEAD_PAD), a[:, o_ml:o_g],
                                _pad_cols(gates, 0, LANE)], axis=1)

    wq = w["w_uq"][l].reshape(Q_LORA, MLA_HEADS, MLA_NOPE + MLA_ROPE)
    wq_main = _pad_cols(wq, 0, HEAD_PAD).reshape(Q_LORA, -1)
    wkv = w["w_ukv"][l].reshape(KV_LORA, MLA_HEADS, MLA_NOPE + MLA_VDIM)
    wk = _pad_cols(wkv[..., :MLA_NOPE], 0, HEAD_PAD).reshape(KV_LORA, -1)
    wv = _pad_cols(wkv[..., MLA_NOPE:], 0, HEAD_PAD).reshape(KV_LORA, -1)
    head = jnp.arange(ML_WIDTH) // ML_DIM
    wr = _pad_cols(w["w_router"][l], 0, LANE)
    wr_hi = wr.astype(BF16)
    return {
        "wcat": cat(wi).astype(BF16), "bcat": cat(bi),
        "q_norm_g": w["q_norm_g"][l][None, :], "kv_norm_g": w["kv_norm_g"][l][None, :],
        "wq": wq_main.astype(BF16),
        "wkv": jnp.concatenate([wk, wv], axis=1).astype(BF16),
        "sg_ln_g": w["sg_ln_g"][l][None, :], "sg_ln_b": w["sg_ln_b"][l][None, :],
        "sg_w": w["sg_w"][l].astype(BF16),
        "sg_bias": jnp.repeat(w["sg_b"][l].T, SG_GDIM, axis=1),
        "conv_w": jnp.pad(w["ml_conv_w"][l], ((0, SUBLANE - ML_CONV), (0, 0))), "conv_b": w["ml_conv_b"][l][None, :],
        "f_bias": _pad_cols(w["ml_f_bias"][l].reshape(1, 2 * ML_HEADS), 0, LANE),
        "ml_norm_g": w["ml_norm_g"][l][None, :],
        "head_avg": ((head[:, None] == head[None, :]).astype(F32) / ML_DIM).astype(BF16),
        "w_out": w["w_out"][l].astype(BF16),
        "ln1_g": w["ln1_g"][l][None, :], "ln1_b": w["ln1_b"][l][None, :],
        "w_router3": jnp.concatenate([jnp.concatenate([wr_hi, (wr - wr_hi.astype(F32)).astype(BF16)], axis=1),
                                      jnp.concatenate([wr_hi, jnp.zeros_like(wr_hi)], axis=1)], axis=0),
        "layer": l, "w_gate": w["w_gate"], "w_up": w["w_up"], "w_down": w["w_down"],
        "ln2_g": w["ln2_g"][l][None, :], "ln2_b": w["ln2_b"][l][None, :],
    }


def _rope_tables(T):
    rows = T // GRID_W
    row = jnp.repeat(jnp.arange(rows, dtype=F32), GRID_W)
    colv = jnp.tile(jnp.arange(GRID_W, dtype=F32), rows)
    inv = ROPE_BASE ** (-jnp.arange(ROPE_AXIS // 2, dtype=F32) * 2.0 / ROPE_AXIS)
    ang = jnp.concatenate([row[:, None] * inv, colv[:, None] * inv], axis=-1)
    half = ROPE_AXIS // 2
    spread = lambda t: jnp.concatenate([t[:, :half], t[:, :half], t[:, half:], t[:, half:]], axis=1)
    cosT = jnp.concatenate([jnp.ones((T, MLA_NOPE), F32), spread(jnp.cos(ang)), jnp.zeros((T, 32), F32)], axis=1)
    sign = jnp.tile(jnp.repeat(jnp.asarray([-1.0, 1.0], F32), half), 2)
    sinS = _pad_cols(spread(jnp.sin(ang)) * sign, MLA_NOPE, HEAD_PAD)
    return cosT, sinS


def _moe(x1, hm, aff_t, mod, p):
    cap = EC_CAPACITY * x1.shape[1] // N_EXPERTS
    pos, cnt = _route(aff_t, cap)
    ye = _experts(pos, cnt, aff_t, hm, p, cap)
    return _combine(pos, cnt, ye, x1, mod, p)


def _forward(x, c, ctx, c_ctx, w):
    B, T, D = x.shape
    Tc = ctx.shape[1]
    L = w["w_ada"].shape[0]
    rows = -(-(B + 1) // SUBLANE) * SUBLANE
    cc = jnp.zeros((rows, D), F32).at[:B].set(c).at[B].set(c_ctx)
    ada = _ada(cc, w["w_ada"], w["b_ada"]).reshape(L, rows, 6, 1, D)
    cos_l, sin_l = _rope_tables(T)
    cos_c = _pad_cols(jnp.ones((Tc, MLA_NOPE + MLA_ROPE), F32), 0, HEAD_PAD)
    sin_c = jnp.zeros((Tc, HEAD_PAD), F32)
    zero_c = jnp.zeros((B, 2, ML_HEADS * HEAD_PAD, HEAD_PAD), F32)
    zero_m = jnp.zeros((B, 2, 2 * ML_HEADS, min(ML_CHUNK, Tc)), F32)
    xl, xc = x, ctx
    for l in range(L):
        p = _prep_layer(l, w)
        mod_l, mod_c = (ada, l, None), (ada, l, B)
        ya_c, q_c, k_c, v_c, qk_c, vo_c, g_c = _inproj(xc, mod_c, cos_c, sin_c, p)
        ya_l, q_l, k_l, v_l, qk_l, vo_l, g_l = _inproj(xl, mod_l, cos_l, sin_l, p)
        hf_c, hb_c, cs_c, ms_c = _mlstm(qk_c, vo_c, g_c, p, zero_c, zero_m)
        hf_l, hb_l, _, _ = _mlstm(qk_l, vo_l, g_l, p, cs_c, ms_c)
        yb_l = _attention(q_l, [(k_l, v_l), (k_c, v_c)])
        x1_l, hm_l, aff_l = _outproj(xl, mod_l, ya_l, yb_l, hf_l, hb_l, vo_l, p)
        xl = _moe(x1_l, hm_l, aff_l, mod_l, p)
        if l < L - 1:
            yb_c = _attention(q_c, [(k_c, v_c)])
            x1_c, hm_c, aff_c = _outproj(xc, mod_c, ya_c, yb_c, hf_c, hb_c, vo_c, p)
            xc = _moe(x1_c, hm_c, aff_c, mod_c, p)
    return xl


def kernel(x, c, ctx, c_ctx, w_ada, b_ada, w_in, b_in, sg_ln_g, sg_ln_b, sg_w, sg_b, q_norm_g, kv_norm_g, w_uq, w_ukv,
           ml_conv_w, ml_conv_b, ml_f_bias, ml_norm_g, w_out, ln1_g, ln1_b, w_router, w_gate, w_up, w_down, ln2_g, ln2_b):
    w = dict(w_ada=w_ada, b_ada=b_ada, w_in=w_in, b_in=b_in, sg_ln_g=sg_ln_g, sg_ln_b=sg_ln_b, sg_w=sg_w, sg_b=sg_b,
             q_norm_g=q_norm_g, kv_norm_g=kv_norm_g, w_uq=w_uq, w_ukv=w_ukv, ml_conv_w=ml_conv_w, ml_conv_b=ml_conv_b,
             ml_f_bias=ml_f_bias, ml_norm_g=ml_norm_g, w_out=w_out, ln1_g=ln1_g, ln1_b=ln1_b, w_router=w_router,
             w_gate=w_gate, w_up=w_up, w_down=w_down, ln2_g=ln2_g, ln2_b=ln2_b)
    return _forward(x, c, ctx, c_ctx, w)
```
